```python
import math
import jax
import jax.numpy as jnp
from jax import lax
import numpy as np

D_MODEL = 1024
BATCH = 8
SEQ = 2048
DEPTH = 4

GRID_W = 64
CTX_LEN = 256
N_MOD = 6
SSM_WIDTH = 256
SSM_GROUP = 16
SSM_GROUPS = SSM_WIDTH // SSM_GROUP
SSM_STATE = 64
DT_MIN = 1e-3
DT_MAX = 1e-1
CONV_WIDTH = 256
CONV_K = 31
RET_HEADS = 4
RET_HEAD_DIM = 128
RET_WIDTH = RET_HEADS * RET_HEAD_DIM
RET_CHUNK = 128
ROPE_BASE = 10000.0
N_BRANCH = 3
IN_WIDTH = SSM_WIDTH + 2 * CONV_WIDTH + 4 * RET_WIDTH + N_BRANCH * D_MODEL
N_EXPERTS = 32
TOP_K = 4
EXPERT_FF = D_MODEL
SWIGLU_LIMIT = 7.0
SWIGLU_ALPHA = 1.702
MOE_BLOCK = 128
EPS = 1e-6

kernel_name = 'hybrid_s5_conformer_retention_moe_dit'


def _rms(x):
    xf = x.astype(jnp.float32)
    return xf * lax.rsqrt(jnp.mean(xf * xf, axis=-1, keepdims=True) + EPS)


def rms_norm(x, g):
    return (_rms(x) * g.astype(jnp.float32)).astype(x.dtype)


def layer_norm(x, g, b):
    xf = x.astype(jnp.float32)
    mu = jnp.mean(xf, axis=-1, keepdims=True)
    var = jnp.mean(jnp.square(xf - mu), axis=-1, keepdims=True)
    y = (xf - mu) * lax.rsqrt(var + EPS) * g.astype(jnp.float32) + b.astype(jnp.float32)
    return y.astype(x.dtype)


def modulate(h, shift, scale):
    return h * (1 + scale) + shift


def split_in_proj(z):
    widths = (SSM_WIDTH, 2 * CONV_WIDTH, RET_WIDTH, RET_WIDTH, RET_WIDTH, RET_WIDTH,
              D_MODEL, D_MODEL, D_MODEL)
    return jnp.split(z, np.cumsum(widths)[:-1].tolist(), axis=-1)


def _flip(t, rev):
    return jnp.flip(t, 1) if rev else t


def _linear_combine(e1, e2):
    a1, b1 = e1
    a2, b2 = e2
    return a2 * a1, a2 * b1 + b2


def s5_discretize(lam_re, lam_im, log_dt, b_re, b_im):
    lam = lax.complex(jnp.minimum(lam_re.astype(jnp.float32), -1e-4), lam_im.astype(jnp.float32))
    lam_dt = lam * jnp.exp(log_dt.astype(jnp.float32))[:, None]
    lam_bar = jnp.exp(lam_dt)
    b = lax.complex(b_re.astype(jnp.float32), b_im.astype(jnp.float32))
    b_bar = ((lam_bar - 1.0) / lam)[..., None] * b
    return lam_dt, lam_bar, b_bar


def s5_scan(ug, lam_bar, b_bar, rev):
    bu = jnp.einsum('blgm,gpm->blgp', ug.astype(jnp.complex64), b_bar)
    bu = _flip(bu, rev)
    _, s = lax.associative_scan(_linear_combine, (jnp.broadcast_to(lam_bar, bu.shape), bu), axis=1)
    return _flip(s, rev)


def s5_branch(uc, ul, lam_re, lam_im, log_dt, b_re, b_im, c_re, c_im, d_skip, w_glu, b_glu):
    bsz, n_ctx, _ = uc.shape
    n_lat = ul.shape[1]
    ucg = uc.astype(jnp.float32).reshape(bsz, n_ctx, SSM_GROUPS, SSM_GROUP)
    ulg = ul.astype(jnp.float32).reshape(bsz, n_lat, SSM_GROUPS, SSM_GROUP)
    d = d_skip.astype(jnp.float32).reshape(SSM_GROUPS, SSM_GROUP)
    yc = ucg * d
    yl = ulg * d
    for direction, rev in enumerate((False, True)):
        lam_dt, lam_bar, b_bar = s5_discretize(lam_re[direction], lam_im[direction], log_dt[direction],
                                              b_re[direction], b_im[direction])
        c_mat = lax.complex(c_re[direction].astype(jnp.float32), c_im[direction].astype(jnp.float32))
        s_c = s5_scan(ucg, lam_bar, b_bar, rev)
        s_c_last = s_c[:, 0] if rev else s_c[:, -1]
        steps = (jnp.arange(n_lat, 0, -1) if rev else jnp.arange(1, n_lat + 1)).astype(jnp.float32)
        carry = jnp.exp(lam_dt * steps[:, None, None]) * s_c_last[:, None]
        s_l = s5_scan(ulg, lam_bar, b_bar, rev) + carry
        yc = yc + jnp.einsum('blgp,gmp->blgm', s_c, c_mat).real
        yl = yl + jnp.einsum('blgp,gmp->blgm', s_l, c_mat).real

    def glu_out(y, like):
        y = jax.nn.gelu(y.reshape(y.shape[0], y.shape[1], SSM_WIDTH)).astype(like.dtype)
        a, g = jnp.split(y @ w_glu + b_glu, 2, axis=-1)
        return a * jax.nn.sigmoid(g)

    return glu_out(yc, uc), glu_out(yl, ul)


def conv_branch(v, w_dw, b_dw, ln_g, ln_b, w_pw, b_pw):
    a, g = jnp.split(v, 2, axis=-1)
    h = a * jax.nn.sigmoid(g)
    h = lax.conv_general_dilated(h, w_dw.astype(h.dtype), window_strides=(1,),
                                 padding=[(CONV_K // 2, CONV_K // 2)],
                                 dimension_numbers=('NWC', 'WIO', 'NWC'),
                                 feature_group_count=CONV_WIDTH) + b_dw
    h = layer_norm(h, ln_g, ln_b)
    return jax.nn.silu(h) @ w_pw + b_pw


def _rope_1d(t, pos):
    n = t.shape[-1] // 2
    inv_freq = ROPE_BASE ** (-jnp.arange(n, dtype=jnp.float32) / n)
    ang = pos.astype(jnp.float32)[:, None] * inv_freq
    cos = jnp.cos(ang)[None, :, None, :]
    sin = jnp.sin(ang)[None, :, None, :]
    t1, t2 = t[..., :n], t[..., n:]
    return jnp.concatenate([t1 * cos - t2 * sin, t1 * sin + t2 * cos], axis=-1)


def rope_2d(t, rows, cols):
    half = t.shape[-1] // 2
    return jnp.concatenate([_rope_1d(t[..., :half], rows), _rope_1d(t[..., half:], cols)], axis=-1)


def retention_chunkwise(q, k, v, log_g, s0):
    bsz, n_pos, n_h, dh = q.shape
    n_chunk = n_pos // RET_CHUNK
    qc = q.reshape(bsz, n_chunk, RET_CHUNK, n_h, dh)
    kc = k.reshape(bsz, n_chunk, RET_CHUNK, n_h, dh)
    vc = v.reshape(bsz, n_chunk, RET_CHUNK, n_h, dh)
    idx = jnp.arange(RET_CHUNK, dtype=jnp.float32)
    diff = idx[:, None] - idx[None, :]
    mask = jnp.where(diff >= 0, jnp.exp(log_g[:, None, None] * jnp.maximum(diff, 0.0)), 0.0)
    scores = jnp.einsum('bnihd,bnjhd->bnhij', qc, kc) * mask
    intra = jnp.einsum('bnhij,bnjhe->bnihe', scores, vc)
    zeta = jnp.exp(log_g[:, None] * (RET_CHUNK - 1 - idx))
    chunk_kv = jnp.einsum('bnjhd,hj,bnjhe->bnhde', kc, zeta, vc)
    g_chunk = jnp.exp(log_g * RET_CHUNK)[None, :, None, None]

    def step(s, kv):
        return g_chunk * s + kv, s

    s_final, s_prev = lax.scan(step, s0, jnp.moveaxis(chunk_kv, 1, 0))
    s_prev = jnp.moveaxis(s_prev, 0, 1)
    inter_decay = jnp.exp(log_g[:, None] * (idx + 1.0))
    inter = jnp.einsum('bnihd,hi,bnhde->bnihe', qc, inter_decay, s_prev)
    return (intra + inter).reshape(bsz, n_pos, n_h, dh), s_final


def retention_branch(qc, kc, vc, gc, ql, kl, vl, gl, rows, cols, log_g_fwd, log_g_bwd, w_o):
    def heads(t):
        return t.astype(jnp.float32).reshape(t.shape[0], t.shape[1], RET_HEADS, RET_HEAD_DIM)

    k_scale = RET_HEAD_DIM ** -0.5
    qc_, kc_, vc_ = heads(qc), heads(kc) * k_scale, heads(vc)
    ql_ = rope_2d(heads(ql), rows, cols)
    kl_ = rope_2d(heads(kl), rows, cols) * k_scale
    vl_ = heads(vl)
    s0 = jnp.zeros((qc.shape[0], RET_HEADS, RET_HEAD_DIM, RET_HEAD_DIM), jnp.float32)
    oc = jnp.zeros_like(vc_)
    ol = jnp.zeros_like(vl_)
    for log_g, rev in ((log_g_fwd, False), (log_g_bwd, True)):
        o_c, s_ctx = retention_chunkwise(_flip(qc_, rev), _flip(kc_, rev), _flip(vc_, rev), log_g, s0)
        o_l, _ = retention_chunkwise(_flip(ql_, rev), _flip(kl_, rev), _flip(vl_, rev), log_g, s_ctx)
        oc = oc + _flip(o_c, rev)
        ol = ol + _flip(o_l, rev)

    def out(o, g):
        o = _rms(o).reshape(o.shape[0], o.shape[1], RET_WIDTH).astype(g.dtype)
        return (o * jax.nn.silu(g)) @ w_o

    return out(oc, gc), out(ol, gl)


def token_mixer(hc, hl, rows, cols, log_g_fwd, log_g_bwd, w_in,
                lam_re, lam_im, log_dt, b_re, b_im, c_re, c_im, d_skip, w_glu, b_glu,
                w_dw, b_dw, ln_g, ln_b, w_pw, b_pw, w_o, w_out):
    uc, cvc, qc, kc, vc, gc, gsc, gcc, grc = split_in_proj(hc @ w_in)
    ul, cvl, ql, kl, vl, gl, gsl, gcl, grl = split_in_proj(hl @ w_in)
    ssm_c, ssm_l = s5_branch(uc, ul, lam_re, lam_im, log_dt, b_re, b_im, c_re, c_im, d_skip, w_glu, b_glu)
    conv_c = conv_branch(cvc, w_dw, b_dw, ln_g, ln_b, w_pw, b_pw)
    conv_l = conv_branch(cvl, w_dw, b_dw, ln_g, ln_b, w_pw, b_pw)
    ret_c, ret_l = retention_branch(qc, kc, vc, gc, ql, kl, vl, gl, rows, cols, log_g_fwd, log_g_bwd, w_o)

    def merge(ys, ycv, yr, gs, gcv, gr):
        return (jax.nn.sigmoid(gs) * ys + jax.nn.sigmoid(gcv) * ycv + jax.nn.sigmoid(gr) * yr) @ w_out

    return merge(ssm_c, conv_c, ret_c, gsc, gcc, grc), merge(ssm_l, conv_l, ret_l, gsl, gcl, grl)


def moe_ffn(h, w_router, b_router, w_gu, b_gu, w_down, b_down):
    n_tok, d_model = h.shape
    n_assign = n_tok * TOP_K
    n_blocks = (n_assign + N_EXPERTS * (MOE_BLOCK - 1)) // MOE_BLOCK
    logits = (h @ w_router + b_router).astype(jnp.float32)
    top_logit, top_idx = lax.top_k(logits, TOP_K)
    gates = jax.nn.softmax(top_logit, axis=-1).astype(h.dtype)
    flat_e = top_idx.reshape(-1)
    order = jnp.argsort(flat_e)
    sorted_e = flat_e[order]
    counts = jnp.bincount(flat_e, length=N_EXPERTS)
    blocks_per = (counts + MOE_BLOCK - 1) // MOE_BLOCK
    block_end = jnp.cumsum(blocks_per)
    padded_start = (block_end - blocks_per) * MOE_BLOCK
    start = jnp.cumsum(counts) - counts
    dest = padded_start[sorted_e] + jnp.arange(n_assign) - start[sorted_e]
    slot = jnp.zeros((n_assign,), jnp.int32).at[order].set(dest.astype(jnp.int32))
    rows = jnp.full((n_blocks * MOE_BLOCK,), n_tok, jnp.int32).at[slot].set(
        jnp.arange(n_assign, dtype=jnp.int32) // TOP_K)
    block_expert = jnp.minimum(jnp.searchsorted(block_end, jnp.arange(n_blocks), side='right'), N_EXPERTS - 1)
    h_pad = jnp.concatenate([h, jnp.zeros((1, d_model), h.dtype)], axis=0)
    xb = h_pad[rows].reshape(n_blocks, MOE_BLOCK, d_model)

    def expert_block(args):
        xblk, e = args
        gu = xblk @ w_gu[e] + b_gu[e]
        gate = jnp.minimum(gu[:, :EXPERT_FF], SWIGLU_LIMIT)
        up = jnp.clip(gu[:, EXPERT_FF:], -SWIGLU_LIMIT, SWIGLU_LIMIT)
        act = (up + 1) * gate * jax.nn.sigmoid(gate * SWIGLU_ALPHA)
        return act @ w_down[e] + b_down[e]

    yb = lax.map(expert_block, (xb, block_expert)).reshape(-1, d_model)
    y = yb[slot].reshape(n_tok, TOP_K, d_model)
    return jnp.einsum('tkd,tk->td', y, gates)


def setup_inputs(seed: int = 0) -> dict:
    key = jax.random.key(seed)
    ks = iter(jax.random.split(key, 40))
    f32 = jnp.float32

    def nrm(shape, scale):
        return jax.random.normal(next(ks), shape, f32) * scale

    def gain(shape):
        return 1.0 + nrm(shape, 0.01)

    G, P, M, E, F = SSM_GROUPS, SSM_STATE, SSM_GROUP, N_EXPERTS, EXPERT_FF
    d_in = D_MODEL ** -0.5
    return {
        'x': nrm((BATCH, SEQ, D_MODEL), 1.0),
        'c': nrm((BATCH, D_MODEL), 1.0),
        'ctx': nrm((BATCH, CTX_LEN, D_MODEL), 1.0),
        'c_ctx': nrm((D_MODEL,), 1.0),
        'ada_w': nrm((DEPTH, D_MODEL, N_MOD * D_MODEL), 0.5 * d_in),
        'ada_b': nrm((DEPTH, N_MOD * D_MODEL), 0.01),
        'norm1_g': gain((DEPTH, D_MODEL)),
        'w_in': nrm((DEPTH, D_MODEL, IN_WIDTH), d_in),
        'ssm_lam_re': -0.5 + nrm((DEPTH, 2, G, P), 0.01),
        'ssm_lam_im': jnp.pi * jnp.arange(P, dtype=f32) + nrm((DEPTH, 2, G, P), 0.01),
        'ssm_log_dt': jax.random.uniform(next(ks), (DEPTH, 2, G), f32, math.log(DT_MIN), math.log(DT_MAX)),
        'ssm_b_re': nrm((DEPTH, 2, G, P, M), (2 * M) ** -0.5),
        'ssm_b_im': nrm((DEPTH, 2, G, P, M), (2 * M) ** -0.5),
        'ssm_c_re': nrm((DEPTH, 2, G, M, P), P ** -0.5),
        'ssm_c_im': nrm((DEPTH, 2, G, M, P), P ** -0.5),
        'ssm_d': nrm((DEPTH, SSM_WIDTH), 1.0),
        'ssm_glu_w': nrm((DEPTH, SSM_WIDTH, 2 * D_MODEL), SSM_WIDTH ** -0.5),
        'ssm_glu_b': nrm((DEPTH, 2 * D_MODEL), 0.01),
        'conv_dw_w': nrm((DEPTH, CONV_K, 1, CONV_WIDTH), CONV_K ** -0.5),
        'conv_dw_b': nrm((DEPTH, CONV_WIDTH), 0.01),
        'conv_ln_g': gain((DEPTH, CONV_WIDTH)),
        'conv_ln_b': nrm((DEPTH, CONV_WIDTH), 0.01),
        'conv_pw_w': nrm((DEPTH, CONV_WIDTH, D_MODEL), CONV_WIDTH ** -0.5),
        'conv_pw_b': nrm((DEPTH, D_MODEL), 0.01),
        'ret_w_o': nrm((DEPTH, RET_WIDTH, D_MODEL), RET_WIDTH ** -0.5),
        'w_out': nrm((DEPTH, D_MODEL, D_MODEL), d_in),
        'norm2_g': gain((DEPTH, D_MODEL)),
        'router_w': nrm((DEPTH, D_MODEL, E), d_in),
        'router_b': nrm((DEPTH, E), 0.01),
        'exp_gu_w': nrm((DEPTH, E, D_MODEL, 2 * F), d_in),
        'exp_gu_b': nrm((DEPTH, E, 2 * F), 0.01),
        'exp_down_w': nrm((DEPTH, E, F, D_MODEL), F ** -0.5),
        'exp_down_b': nrm((DEPTH, E, D_MODEL), 0.01),
        'final_g': gain((D_MODEL,)),
    }


def reference(x, c, ctx, c_ctx, ada_w, ada_b, norm1_g, w_in, ssm_lam_re, ssm_lam_im, ssm_log_dt,
              ssm_b_re, ssm_b_im, ssm_c_re, ssm_c_im, ssm_d, ssm_glu_w, ssm_glu_b,
              conv_dw_w, conv_dw_b, conv_ln_g, conv_ln_b, conv_pw_w, conv_pw_b,
              ret_w_o, w_out, norm2_g, router_w, router_b, exp_gu_w, exp_gu_b,
              exp_down_w, exp_down_b, final_g):
    n_lat = x.shape[1]
    n_rows = n_lat // GRID_W
    rows = jnp.repeat(jnp.arange(n_rows, dtype=jnp.int32), GRID_W)
    cols = jnp.arange(n_lat, dtype=jnp.int32) % GRID_W
    log_g_fwd = jnp.log1p(-jnp.exp2(-5.0 - jnp.arange(RET_HEADS, dtype=jnp.float32)))
    log_g_bwd = log_g_fwd[::-1]
    s_lat = jax.nn.silu(c)
    s_ctx = jax.nn.silu(c_ctx)[None]
    xl, xc = x, ctx
    for i in range(DEPTH):
        last = i == DEPTH - 1
        mod_l = jnp.split((s_lat @ ada_w[i] + ada_b[i])[:, None, :], N_MOD, axis=-1)
        mod_c = jnp.split((s_ctx @ ada_w[i] + ada_b[i])[:, None, :], N_MOD, axis=-1)
        hc = modulate(rms_norm(xc, norm1_g[i]), mod_c[0], mod_c[1])
        hl = modulate(rms_norm(xl, norm1_g[i]), mod_l[0], mod_l[1])
        yc, yl = token_mixer(hc, hl, rows, cols, log_g_fwd, log_g_bwd, w_in[i],
                             ssm_lam_re[i], ssm_lam_im[i], ssm_log_dt[i], ssm_b_re[i], ssm_b_im[i],
                             ssm_c_re[i], ssm_c_im[i], ssm_d[i], ssm_glu_w[i], ssm_glu_b[i],
                             conv_dw_w[i], conv_dw_b[i], conv_ln_g[i], conv_ln_b[i], conv_pw_w[i], conv_pw_b[i],
                             ret_w_o[i], w_out[i])
        xl = xl + mod_l[2] * yl
        hl = modulate(rms_norm(xl, norm2_g[i]), mod_l[3], mod_l[4])
        moe_params = (router_w[i], router_b[i], exp_gu_w[i], exp_gu_b[i], exp_down_w[i], exp_down_b[i])
        if last:
            xl = xl + mod_l[5] * moe_ffn(hl.reshape(-1, D_MODEL), *moe_params).reshape(xl.shape)
        else:
            xc = xc + mod_c[2] * yc
            hc = modulate(rms_norm(xc, norm2_g[i]), mod_c[3], mod_c[4])
            n_ctx_tok = hc.shape[0] * hc.shape[1]
            y_all = moe_ffn(jnp.concatenate([hc.reshape(-1, D_MODEL), hl.reshape(-1, D_MODEL)], axis=0),
                            *moe_params)
            xc = xc + mod_c[5] * y_all[:n_ctx_tok].reshape(xc.shape)
            xl = xl + mod_l[5] * y_all[n_ctx_tok:].reshape(xl.shape)
    return rms_norm(xl, final_g)
```

```python
import functools
import math

import numpy as np
import jax
import jax.numpy as jnp
from jax import lax
from jax.experimental import pallas as pl
from jax.experimental.pallas import tpu as pltpu

F32 = jnp.float32
BF16 = jnp.bfloat16

D_MODEL = 1024
DEPTH = 4
GRID_W = 64
N_MOD = 6
SSM_WIDTH = 256
SSM_GROUP = 16
SSM_GROUPS = SSM_WIDTH // SSM_GROUP
SSM_STATE = 64
SSM_LANES = SSM_GROUPS * SSM_STATE
CONV_WIDTH = 256
CONV_K = 31
CONV_PAD = 16
RET_HEADS = 4
RET_HEAD_DIM = 128
RET_WIDTH = RET_HEADS * RET_HEAD_DIM
RET_CHUNK = 128
ROPE_BASE = 10000.0
IN_WIDTH = SSM_WIDTH + 2 * CONV_WIDTH + 4 * RET_WIDTH + 3 * D_MODEL
COL_CV = SSM_WIDTH
COL_QKVG = COL_CV + 2 * CONV_WIDTH
COL_GATES = COL_QKVG + 4 * RET_WIDTH
N_EXPERTS = 32
TOP_K = 4
EXPERT_FF = D_MODEL
SWIGLU_LIMIT = 7.0
SWIGLU_ALPHA = 1.702
EPS = 1e-6

ROW_TILE = 256
S5_STEPS = 64
S5_STRIP = 512
MOE_ROWS = 256
LOGIT_LANES = 128
VMEM_LIMIT = 56 * 1024 * 1024


def _cparams(sem):
    return pltpu.CompilerParams(dimension_semantics=sem, vmem_limit_bytes=VMEM_LIMIT)


def _rms(x):
    return x * lax.rsqrt(jnp.mean(x * x, axis=-1, keepdims=True) + EPS)


def _ada_kernel(s_ref, w_ref, b_ref, o_ref):
    s = s_ref[...]
    s = s * jax.nn.sigmoid(s)
    o_ref[0] = jnp.dot(s, w_ref[0], preferred_element_type=F32,
                       precision=lax.Precision.HIGHEST) + b_ref[0]


def _ada_mod(s_in, ada_w, ada_b):
    depth, d, n = ada_w.shape
    tn = 1536
    return pl.pallas_call(
        _ada_kernel,
        grid=(depth, n // tn),
        in_specs=[pl.BlockSpec((16, d), lambda i, j: (0, 0)),
                  pl.BlockSpec((1, d, tn), lambda i, j: (i, 0, j)),
                  pl.BlockSpec((1, 1, tn), lambda i, j: (i, 0, j))],
        out_specs=pl.BlockSpec((1, 16, tn), lambda i, j: (i, 0, j)),
        out_shape=jax.ShapeDtypeStruct((depth, 16, n), F32),
        compiler_params=_cparams(("arbitrary", "arbitrary")),
    )(s_in, ada_w, ada_b.reshape(depth, 1, n))


def _inproj_kernel(x_ref, mod_ref, g_ref, w_ref, u_ref, cv_ref, qkvg_ref, gates_ref):
    x = x_ref[0]
    mod = mod_ref[0, 0]
    h = (_rms(x) * g_ref[...]) * (1.0 + mod[1:2]) + mod[0:1]
    hb = h.astype(BF16)
    u_ref[...] = jnp.dot(hb, w_ref[:, 0:COL_CV], preferred_element_type=F32)
    cv_ref[0] = jnp.dot(hb, w_ref[:, COL_CV:COL_QKVG], preferred_element_type=F32)
    qkvg_ref[0] = jnp.dot(hb, w_ref[:, COL_QKVG:COL_GATES], preferred_element_type=F32)
    gates_ref[0] = jnp.dot(hb, w_ref[:, COL_GATES:IN_WIDTH], preferred_element_type=F32)


def _in_proj(x, modt, g1, w_in_bf, ctx_len):
    b, s_tot, d = x.shape
    nt = s_tot // ROW_TILE
    nctx = ctx_len // ROW_TILE
    seg = lambda j: jnp.where(j >= nctx, 1, 0)
    return pl.pallas_call(
        _inproj_kernel,
        grid=(b, nt),
        in_specs=[pl.BlockSpec((1, ROW_TILE, d), lambda i, j: (i, j, 0)),
                  pl.BlockSpec((1, 1, 8, d), lambda i, j: (i, seg(j), 0, 0)),
                  pl.BlockSpec((1, d), lambda i, j: (0, 0)),
                  pl.BlockSpec((d, IN_WIDTH), lambda i, j: (0, 0))],
        out_specs=[pl.BlockSpec((ROW_TILE, SSM_WIDTH), lambda i, j: (j, i)),
                   pl.BlockSpec((1, ROW_TILE, 2 * CONV_WIDTH), lambda i, j: (i, j, 0)),
                   pl.BlockSpec((1, ROW_TILE, 4 * RET_WIDTH), lambda i, j: (i, j, 0)),
                   pl.BlockSpec((1, ROW_TILE, 3 * D_MODEL), lambda i, j: (i, j, 0))],
        out_shape=[jax.ShapeDtypeStruct((s_tot, b * SSM_WIDTH), F32),
                   jax.ShapeDtypeStruct((b, s_tot, 2 * CONV_WIDTH), F32),
                   jax.ShapeDtypeStruct((b, s_tot, 4 * RET_WIDTH), F32),
                   jax.ShapeDtypeStruct((b, s_tot, 3 * D_MODEL), F32)],
        compiler_params=_cparams(("arbitrary", "arbitrary")),
    )(x, modt, g1.reshape(1, d), w_in_bf)


def _s5_params(lam_re, lam_im, log_dt, b_re, b_im, c_re, c_im):
    lam = lax.complex(jnp.minimum(lam_re.astype(F32), -1e-4), lam_im.astype(F32))
    lam_dt = lam * jnp.exp(log_dt.astype(F32))[..., None]
    lam_bar = jnp.exp(lam_dt)
    b = lax.complex(b_re.astype(F32), b_im.astype(F32))
    b_bar = ((lam_bar - 1.0) / lam)[..., None] * b
    eye = jnp.eye(SSM_GROUPS, dtype=F32)

    def in_mat(t):
        return jnp.einsum('dgpm,gh->dgmhp', t, eye).reshape(2, SSM_WIDTH, SSM_LANES)

    def out_mat(t):
        return jnp.einsum('dgmp,gh->dgphm', t, eye).reshape(2, SSM_LANES, SSM_WIDTH)

    bmat = jnp.concatenate([in_mat(jnp.real(b_bar)), in_mat(jnp.imag(b_bar))], axis=-1)
    cmat = jnp.concatenate([out_mat(c_re.astype(F32)), -out_mat(c_im.astype(F32))], axis=1)
    lvec = jnp.concatenate([jnp.real(lam_bar).reshape(2, 1, SSM_LANES),
                            jnp.imag(lam_bar).reshape(2, 1, SSM_LANES)], axis=-1)
    return bmat.astype(BF16), cmat.astype(BF16), lvec


def _s5_kernel(u_ref, bmat_ref, cmat_ref, lam_ref, dsk_ref, y_ref, st_ref, carry_ref):
    d = pl.program_id(0)
    c = pl.program_id(1)
    steps, nb, width = u_ref.shape

    @pl.when(c == 0)
    def _():
        carry_ref[...] = jnp.zeros_like(carry_ref)

    u = u_ref[...].reshape(steps * nb, width)
    bu = jnp.dot(u.astype(BF16), bmat_ref[0], preferred_element_type=F32)
    st_ref[...] = bu.reshape(steps, nb, 2 * SSM_LANES)

    for k in range(SSM_LANES // S5_STRIP):
        re = pl.ds(k * S5_STRIP, S5_STRIP)
        im = pl.ds(SSM_LANES + k * S5_STRIP, S5_STRIP)
        lr = jnp.broadcast_to(lam_ref[0, :, re], (nb, S5_STRIP))
        li = jnp.broadcast_to(lam_ref[0, :, im], (nb, S5_STRIP))

        def step(t, s):
            sr, si = s
            tt = jnp.where(d == 1, steps - 1 - t, t)
            nr = lr * sr - li * si + st_ref[tt, :, re]
            ni = lr * si + li * sr + st_ref[tt, :, im]
            st_ref[tt, :, re] = nr
            st_ref[tt, :, im] = ni
            return nr, ni

        sr, si = lax.fori_loop(0, steps, step, (carry_ref[:, re], carry_ref[:, im]), unroll=2)
        carry_ref[:, re] = sr
        carry_ref[:, im] = si

    st = st_ref[...].reshape(steps * nb, 2 * SSM_LANES)
    y = jnp.dot(st.astype(BF16), cmat_ref[0], preferred_element_type=F32)
    skip = jnp.where(d == 0, 1.0, 0.0) * dsk_ref[...]
    y_ref[0] = (y + u * skip).reshape(steps, nb, width)


def _s5_scan(u_tm, bmat, cmat, lvec, d_skip, ctx_len):
    s_tot, nb, width = u_tm.shape
    nch = s_tot // S5_STEPS
    nctx = ctx_len // S5_STEPS

    def chunk(d, c):
        back = jnp.where(c < nctx, nctx - 1 - c, nctx + nch - 1 - c)
        return jnp.where(d == 1, back, c)

    return pl.pallas_call(
        _s5_kernel,
        grid=(2, nch),
        in_specs=[pl.BlockSpec((S5_STEPS, nb, width), lambda d, c: (chunk(d, c), 0, 0)),
                  pl.BlockSpec((1, width, 2 * SSM_LANES), lambda d, c: (d, 0, 0)),
                  pl.BlockSpec((1, 2 * SSM_LANES, width), lambda d, c: (d, 0, 0)),
                  pl.BlockSpec((1, 1, 2 * SSM_LANES), lambda d, c: (d, 0, 0)),
                  pl.BlockSpec((1, width), lambda d, c: (0, 0))],
        out_specs=pl.BlockSpec((1, S5_STEPS, nb, width), lambda d, c: (d, chunk(d, c), 0, 0)),
        out_shape=jax.ShapeDtypeStruct((2, s_tot, nb, width), F32),
        scratch_shapes=[pltpu.VMEM((S5_STEPS, nb, 2 * SSM_LANES), F32),
                        pltpu.VMEM((nb, 2 * SSM_LANES), F32)],
        compiler_params=_cparams(("arbitrary", "arbitrary")),
    )(u_tm, bmat, cmat, lvec, d_skip.reshape(1, width))


def _conv_kernel(cv_ref, w_ref, b_ref, lg_ref, lb_ref, o_ref, hp_ref, win_ref, *, ctx_len):
    s_tot = cv_ref.shape[1]
    lat = s_tot - ctx_len
    rc = RET_CHUNK
    zeros = jnp.zeros((CONV_PAD, CONV_WIDTH), F32)
    a = cv_ref[0, :, 0:CONV_WIDTH]
    g = cv_ref[0, :, CONV_WIDTH:2 * CONV_WIDTH]
    h = a * jax.nn.sigmoid(g)
    hp_ref[0:CONV_PAD] = zeros
    hp_ref[CONV_PAD:CONV_PAD + ctx_len] = h[0:ctx_len]
    hp_ref[CONV_PAD + ctx_len:2 * CONV_PAD + ctx_len] = zeros
    hp_ref[2 * CONV_PAD + ctx_len:2 * CONV_PAD + s_tot] = h[ctx_len:s_tot]
    hp_ref[2 * CONV_PAD + s_tot:3 * CONV_PAD + s_tot] = zeros
    del lat

    def chunk(c, carry):
        r0 = pl.multiple_of(c * rc, rc)
        wbase = pl.multiple_of(r0 + jnp.where(r0 >= ctx_len, CONV_PAD, 0), 8)
        win_ref[...] = hp_ref[pl.ds(wbase, rc + 2 * CONV_PAD), :]
        acc = jnp.zeros((rc, CONV_WIDTH), F32) + b_ref[...]
        for k in range(CONV_K):
            off = k + CONV_PAD - CONV_K // 2
            acc = acc + w_ref[k:k + 1, :] * win_ref[off:off + rc, :]
        mu = jnp.mean(acc, axis=-1, keepdims=True)
        var = jnp.mean(jnp.square(acc - mu), axis=-1, keepdims=True)
        y = (acc - mu) * lax.rsqrt(var + EPS) * lg_ref[...] + lb_ref[...]
        o_ref[0, pl.ds(r0, rc), :] = y * jax.nn.sigmoid(y)
        return carry

    lax.fori_loop(0, s_tot // rc, chunk, 0)


def _conv_branch(cv, w_dw, b_dw, ln_g, ln_b, ctx_len):
    b, s_tot, _ = cv.shape
    w = jnp.concatenate([w_dw.reshape(CONV_K, CONV_WIDTH), jnp.zeros((1, CONV_WIDTH), F32)], axis=0)
    vec = lambda t: t.reshape(1, CONV_WIDTH)
    cst = lambda shape: pl.BlockSpec(shape, lambda i: (0,) * len(shape))
    return pl.pallas_call(
        functools.partial(_conv_kernel, ctx_len=ctx_len),
        grid=(b,),
        in_specs=[pl.BlockSpec((1, s_tot, 2 * CONV_WIDTH), lambda i: (i, 0, 0)),
                  cst((CONV_K + 1, CONV_WIDTH)), cst((1, CONV_WIDTH)),
                  cst((1, CONV_WIDTH)), cst((1, CONV_WIDTH))],
        out_specs=pl.BlockSpec((1, s_tot, CONV_WIDTH), lambda i: (i, 0, 0)),
        out_shape=jax.ShapeDtypeStruct((b, s_tot, CONV_WIDTH), F32),
        scratch_shapes=[pltpu.VMEM((s_tot + 3 * CONV_PAD, CONV_WIDTH), F32),
                        pltpu.VMEM((RET_CHUNK + 2 * CONV_PAD, CONV_WIDTH), F32)],
        compiler_params=_cparams(("arbitrary",)),
    )(cv, w, vec(b_dw), vec(ln_g), vec(ln_b))


def _ret_tables(ctx_len, seq):
    n = RET_HEAD_DIM // 4
    inv_freq = ROPE_BASE ** (-np.arange(n, dtype=np.float64) / n)
    pos = np.arange(seq)
    ang_r = (pos // GRID_W)[:, None] * inv_freq
    ang_c = (pos % GRID_W)[:, None] * inv_freq
    cos = np.concatenate([np.cos(ang_r), np.cos(ang_r), np.cos(ang_c), np.cos(ang_c)], axis=-1)
    sin = np.concatenate([-np.sin(ang_r), np.sin(ang_r), -np.sin(ang_c), np.sin(ang_c)], axis=-1)
    cos = np.concatenate([np.ones((ctx_len, RET_HEAD_DIM)), cos], axis=0)
    sin = np.concatenate([np.zeros((ctx_len, RET_HEAD_DIM)), sin], axis=0)
    log_g_fwd = np.log1p(-np.exp2(-5.0 - np.arange(RET_HEADS, dtype=np.float64)))
    idx = np.arange(RET_CHUNK, dtype=np.float64)
    diff = idx[:, None] - idx[None, :]
    tabs = np.zeros((2, RET_HEADS, 4, RET_CHUNK, RET_CHUNK))
    ones = np.ones((RET_CHUNK, RET_CHUNK))
    for d, log_g in enumerate((log_g_fwd, log_g_fwd[::-1])):
        for hd in range(RET_HEADS):
            lg = log_g[hd]
            if d == 0:
                mask = np.where(diff >= 0, np.exp(lg * np.maximum(diff, 0.0)), 0.0)
                dec = np.exp(lg * (idx + 1.0))
                zeta = np.exp(lg * (RET_CHUNK - 1 - idx))
            else:
                mask = np.where(diff <= 0, np.exp(lg * np.maximum(-diff, 0.0)), 0.0)
                dec = np.exp(lg * (RET_CHUNK - idx))
                zeta = np.exp(lg * idx)
            tabs[d, hd, 0] = mask
            tabs[d, hd, 1] = dec[:, None] * ones
            tabs[d, hd, 2] = zeta[:, None] * ones
            tabs[d, hd, 3] = np.exp(lg * RET_CHUNK) * ones
    return (jnp.asarray(cos, F32), jnp.asarray(sin, F32), jnp.asarray(tabs, F32))


def _ret_kernel(q_ref, k_ref, v_ref, g_ref, cos_ref, sin_ref, tab_ref, o_ref, acc_ref, *, ctx_len):
    s_tot = q_ref.shape[1]
    rc = RET_CHUNK
    nch = s_tot // rc
    nctx = ctx_len // rc
    k_scale = RET_HEAD_DIM ** -0.5
    lane = lax.broadcasted_iota(jnp.int32, (rc, RET_HEAD_DIM), 1)
    first_half = (lane % (RET_HEAD_DIM // 2)) < (RET_HEAD_DIM // 4)

    def rope(t, cs, sn):
        quarter = RET_HEAD_DIM // 4
        partner = jnp.where(first_half, pltpu.roll(t, RET_HEAD_DIM - quarter, 1), pltpu.roll(t, quarter, 1))
        return t * cs + partner * sn

    def chunk(c, state, d):
        r0 = pl.multiple_of(c * rc, rc)
        rows = pl.ds(r0, rc)
        cs = cos_ref[rows, :]
        sn = sin_ref[rows, :]
        q = rope(q_ref[0, rows, :], cs, sn)
        k = rope(k_ref[0, rows, :], cs, sn) * k_scale
        vb = v_ref[0, rows, :].astype(BF16)
        mask = tab_ref[d, 0, 0]
        dec = tab_ref[d, 0, 1]
        zeta = tab_ref[d, 0, 2]
        gch = tab_ref[d, 0, 3]
        scores = lax.dot_general(q.astype(BF16), k.astype(BF16), (((1,), (1,)), ((), ())),
                                 preferred_element_type=F32) * mask
        intra = jnp.dot(scores.astype(BF16), vb, preferred_element_type=F32)
        inter = jnp.dot((q * dec).astype(BF16), state.astype(BF16), preferred_element_type=F32)
        kv = lax.dot_general((k * zeta).astype(BF16), vb, (((0,), (0,)), ((), ())),
                             preferred_element_type=F32)
        return rows, intra + inter, gch * state + kv

    zero_state = jnp.zeros((RET_HEAD_DIM, RET_HEAD_DIM), F32)

    def fwd(c, state):
        rows, o, state = chunk(c, state, 0)
        acc_ref[rows, :] = o
        return state

    def bwd(j, state):
        c = jnp.where(j < nctx, nctx - 1 - j, nctx + nch - 1 - j)
        rows, o, state = chunk(c, state, 1)
        acc_ref[rows, :] += o
        return state

    lax.fori_loop(0, nch, fwd, zero_state)
    lax.fori_loop(0, nch, bwd, zero_state)
    g = g_ref[0]
    o_ref[0] = _rms(acc_ref[...]) * (g * jax.nn.sigmoid(g))


def _retention(qkvg, cos, sin, tabs, ctx_len):
    b, s_tot, _ = qkvg.shape
    hd = RET_HEAD_DIM
    part = lambda p: pl.BlockSpec((1, s_tot, hd), lambda i, h: (i, 0, p * RET_HEADS + h))
    return pl.pallas_call(
        functools.partial(_ret_kernel, ctx_len=ctx_len),
        grid=(b, RET_HEADS),
        in_specs=[part(0), part(1), part(2), part(3),
                  pl.BlockSpec((s_tot, hd), lambda i, h: (0, 0)),
                  pl.BlockSpec((s_tot, hd), lambda i, h: (0, 0)),
                  pl.BlockSpec((2, 1, 4, RET_CHUNK, RET_CHUNK), lambda i, h: (0, h, 0, 0, 0))],
        out_specs=pl.BlockSpec((1, s_tot, hd), lambda i, h: (i, 0, h)),
        out_shape=jax.ShapeDtypeStruct((b, s_tot, RET_WIDTH), F32),
        scratch_shapes=[pltpu.VMEM((s_tot, hd), F32)],
        compiler_params=_cparams(("arbitrary", "arbitrary")),
    )(qkvg, qkvg, qkvg, qkvg, cos, sin, tabs)


def _merge_kernel(x_ref, yf_ref, yb_ref, hc_ref, ro_ref, gates_ref, mod_ref, g2_ref,
                  wglu_ref, bglu_ref, wpw_ref, bpw_ref, wo_ref, wout_ref, wr_ref, br_ref,
                  x1_ref, h2_ref, lg_ref):
    d = D_MODEL
    mod = mod_ref[0, 0]
    ys = jax.nn.gelu(yf_ref[0] + yb_ref[0]).astype(BF16)
    z = jnp.dot(ys, wglu_ref[...], preferred_element_type=F32) + bglu_ref[...]
    y_ssm = z[:, 0:d] * jax.nn.sigmoid(z[:, d:2 * d])
    y_conv = jnp.dot(hc_ref[0].astype(BF16), wpw_ref[...], preferred_element_type=F32) + bpw_ref[...]
    y_ret = jnp.dot(ro_ref[0].astype(BF16), wo_ref[...], preferred_element_type=F32)
    m = (jax.nn.sigmoid(gates_ref[0, :, 0:d]) * y_ssm
         + jax.nn.sigmoid(gates_ref[0, :, d:2 * d]) * y_conv
         + jax.nn.sigmoid(gates_ref[0, :, 2 * d:3 * d]) * y_ret)
    y = jnp.dot(m.astype(BF16), wout_ref[...], preferred_element_type=F32)
    x1 = x_ref[0] + mod[2:3] * y
    x1_ref[0] = x1
    h2 = (_rms(x1) * g2_ref[...]) * (1.0 + mod[4:5]) + mod[3:4]
    h2_ref[0] = h2.astype(BF16)
    lg_ref[0] = jnp.dot(h2, wr_ref[...], preferred_element_type=F32,
                        precision=lax.Precision.HIGHEST) + br_ref[...]


def _merge(x, y_s5, hc, ro, gates, modt, g2, wglu, bglu, wpw, bpw, wo, wout, wr, br, ctx_len):
    b, s_tot, d = x.shape
    nt = s_tot // ROW_TILE
    nctx = ctx_len // ROW_TILE
    seg = lambda j: jnp.where(j >= nctx, 1, 0)
    row = lambda w: pl.BlockSpec((1, ROW_TILE, w), lambda i, j: (i, j, 0))
    cst = lambda shape: pl.BlockSpec(shape, lambda i, j: (0,) * len(shape))
    y2 = y_s5.reshape(2, s_tot, b * SSM_WIDTH)
    return pl.pallas_call(
        _merge_kernel,
        grid=(b, nt),
        in_specs=[row(d),
                  pl.BlockSpec((1, ROW_TILE, SSM_WIDTH), lambda i, j: (0, j, i)),
                  pl.BlockSpec((1, ROW_TILE, SSM_WIDTH), lambda i, j: (1, j, i)),
                  row(CONV_WIDTH), row(RET_WIDTH), row(3 * d),
                  pl.BlockSpec((1, 1, 8, d), lambda i, j: (i, seg(j), 0, 0)),
                  cst((1, d)),
                  cst((SSM_WIDTH, 2 * d)), cst((1, 2 * d)),
                  cst((CONV_WIDTH, d)), cst((1, d)),
                  cst((RET_WIDTH, d)), cst((d, d)),
                  cst((d, LOGIT_LANES)), cst((1, LOGIT_LANES))],
        out_specs=[row(d), row(d), row(LOGIT_LANES)],
        out_shape=[jax.ShapeDtypeStruct((b, s_tot, d), F32),
                   jax.ShapeDtypeStruct((b, s_tot, d), BF16),
                   jax.ShapeDtypeStruct((b, s_tot, LOGIT_LANES), F32)],
        compiler_params=_cparams(("arbitrary", "arbitrary")),
    )(x, y2, y2, hc, ro, gates, modt, g2.reshape(1, d), wglu, bglu.reshape(1, 2 * d),
      wpw, bpw.reshape(1, d), wo, wout, wr, br)


def _route(logits):
    n_tok = logits.shape[0]
    n_assign = n_tok * TOP_K
    n_blocks = (n_assign + N_EXPERTS * (MOE_ROWS - 1)) // MOE_ROWS
    top_logit, top_idx = lax.top_k(logits, TOP_K)
    gates = jax.nn.softmax(top_logit, axis=-1)
    flat_e = top_idx.reshape(-1)
    onehot = (flat_e[:, None] == jnp.arange(N_EXPERTS, dtype=flat_e.dtype)[None, :]).astype(jnp.int32)
    csum = jnp.cumsum(onehot, axis=0)
    rank = jnp.sum(csum * onehot, axis=1) - 1
    counts = csum[-1]
    blocks_per = (counts + MOE_ROWS - 1) // MOE_ROWS
    block_end = jnp.cumsum(blocks_per)
    padded_start = (block_end - blocks_per) * MOE_ROWS
    slot = (padded_start[flat_e] + rank).astype(jnp.int32)
    rows = jnp.zeros((n_blocks * MOE_ROWS,), jnp.int32).at[slot].set(
        jnp.arange(n_assign, dtype=jnp.int32) // TOP_K)
    block_expert = jnp.minimum(jnp.searchsorted(block_end, jnp.arange(n_blocks), side='right'),
                               N_EXPERTS - 1).astype(jnp.int32)
    return gates, slot, rows, block_expert, block_end[-1:].astype(jnp.int32)


def _expert_kernel(be_ref, nu_ref, x_ref, wgu_ref, bgu_ref, wd_ref, bd_ref, y_ref, wgu_bf, wd_bf):
    i = pl.program_id(0)
    prev = be_ref[jnp.maximum(i - 1, 0)]
    fresh = jnp.logical_or(i == 0, be_ref[i] != prev)
    used = i < nu_ref[0]

    @pl.when(jnp.logical_and(fresh, used))
    def _():
        wgu_bf[...] = wgu_ref[0, 0].astype(BF16)
        wd_bf[...] = wd_ref[0, 0].astype(BF16)

    @pl.when(used)
    def _():
        f = EXPERT_FF
        gu = jnp.dot(x_ref[...], wgu_bf[...], preferred_element_type=F32) + bgu_ref[0, 0]
        gate = jnp.minimum(gu[:, 0:f], SWIGLU_LIMIT)
        up = jnp.clip(gu[:, f:2 * f], -SWIGLU_LIMIT, SWIGLU_LIMIT)
        act = (up + 1.0) * gate * jax.nn.sigmoid(gate * SWIGLU_ALPHA)
        y_ref[...] = jnp.dot(act.astype(BF16), wd_bf[...], preferred_element_type=F32) + bd_ref[0, 0]

    @pl.when(jnp.logical_not(used))
    def _():
        y_ref[...] = jnp.zeros_like(y_ref)


def _experts(xb, block_expert, n_used, w_gu, b_gu, w_down, b_down, layer):
    n_pad, d = xb.shape
    n_blocks = n_pad // MOE_ROWS
    f = EXPERT_FF
    grid_spec = pltpu.PrefetchScalarGridSpec(
        num_scalar_prefetch=2,
        grid=(n_blocks,),
        in_specs=[pl.BlockSpec((MOE_ROWS, d), lambda i, be, nu: (i, 0)),
                  pl.BlockSpec((1, 1, d, 2 * f), lambda i, be, nu: (layer, be[i], 0, 0)),
                  pl.BlockSpec((1, 1, 1, 2 * f), lambda i, be, nu: (layer, be[i], 0, 0)),
                  pl.BlockSpec((1, 1, f, d), lambda i, be, nu: (layer, be[i], 0, 0)),
                  pl.BlockSpec((1, 1, 1, d), lambda i, be, nu: (layer, be[i], 0, 0))],
        out_specs=pl.BlockSpec((MOE_ROWS, d), lambda i, be, nu: (i, 0)),
        scratch_shapes=[pltpu.VMEM((d, 2 * f), BF16), pltpu.VMEM((f, d), BF16)],
    )
    return pl.pallas_call(
        _expert_kernel,
        grid_spec=grid_spec,
        out_shape=jax.ShapeDtypeStruct((n_pad, d), F32),
        compiler_params=_cparams(("arbitrary",)),
    )(block_expert, n_used, xb, w_gu, b_gu.reshape(DEPTH, N_EXPERTS, 1, 2 * f),
      w_down, b_down.reshape(DEPTH, N_EXPERTS, 1, d))


def _combine_kernel(x_ref, yg_ref, gt_ref, mod_ref, o_ref):
    mod = mod_ref[0, 0]
    gt = gt_ref[0]
    y = gt[:, 0:1] * yg_ref[0, :, 0, :]
    for k in range(1, TOP_K):
        y = y + gt[:, k:k + 1] * yg_ref[0, :, k, :]
    o_ref[0] = x_ref[0] + mod[5:6] * y


def _combine(x1, yg, gates, modt, ctx_len):
    b, s_tot, d = x1.shape
    nt = s_tot // ROW_TILE
    nctx = ctx_len // ROW_TILE
    seg = lambda j: jnp.where(j >= nctx, 1, 0)
    return pl.pallas_call(
        _combine_kernel,
        grid=(b, nt),
        in_specs=[pl.BlockSpec((1, ROW_TILE, d), lambda i, j: (i, j, 0)),
                  pl.BlockSpec((1, ROW_TILE, TOP_K, d), lambda i, j: (i, j, 0, 0)),
                  pl.BlockSpec((1, ROW_TILE, LOGIT_LANES), lambda i, j: (i, j, 0)),
                  pl.BlockSpec((1, 1, 8, d), lambda i, j: (i, seg(j), 0, 0))],
        out_specs=pl.BlockSpec((1, ROW_TILE, d), lambda i, j: (i, j, 0)),
        out_shape=jax.ShapeDtypeStruct((b, s_tot, d), F32),
        compiler_params=_cparams(("arbitrary", "arbitrary")),
    )(x1, yg, gates, modt)


def _moe(x1, h2, logits, modt, exp_gu_w, exp_gu_b, exp_down_w, exp_down_b, layer, ctx_len):
    b, s_tot, d = x1.shape
    n_tok = b * s_tot
    gates, slot, rows, block_expert, n_used = _route(logits.reshape(n_tok, LOGIT_LANES)[:, :N_EXPERTS])
    xb = jnp.take(h2.reshape(n_tok, d), rows, axis=0)
    yb = _experts(xb, block_expert, n_used, exp_gu_w, exp_gu_b, exp_down_w, exp_down_b, layer)
    yg = jnp.take(yb, slot, axis=0).reshape(b, s_tot, TOP_K, d)
    gates_p = jnp.pad(gates, ((0, 0), (0, LOGIT_LANES - TOP_K))).reshape(b, s_tot, LOGIT_LANES)
    return _combine(x1, yg, gates_p, modt, ctx_len)


def _final_kernel(x_ref, g_ref, o_ref):
    o_ref[0] = _rms(x_ref[0]) * g_ref[...]


def _final_norm(x, g, ctx_len):
    b, s_tot, d = x.shape
    seq = s_tot - ctx_len
    off = ctx_len // ROW_TILE
    return pl.pallas_call(
        _final_kernel,
        grid=(b, seq // ROW_TILE),
        in_specs=[pl.BlockSpec((1, ROW_TILE, d), lambda i, j: (i, j + off, 0)),
                  pl.BlockSpec((1, d), lambda i, j: (0, 0))],
        out_specs=pl.BlockSpec((1, ROW_TILE, d), lambda i, j: (i, j, 0)),
        out_shape=jax.ShapeDtypeStruct((b, seq, d), F32),
        compiler_params=_cparams(("arbitrary", "arbitrary")),
    )(x, g.reshape(1, d))


def _mixer_layer(x, modt, i, ctx_len, tables, norm1_g, w_in_bf, s5p, ssm_d, conv_p, merge_p):
    b, s_tot, d = x.shape
    u2, cv, qkvg, gates = _in_proj(x, modt, norm1_g, w_in_bf, ctx_len)
    bmat, cmat, lvec = s5p
    y_s5 = _s5_scan(u2.reshape(s_tot, b, SSM_WIDTH), bmat, cmat, lvec, ssm_d, ctx_len)
    hc = _conv_branch(cv, *conv_p, ctx_len)
    cos, sin, tabs = tables
    ro = _retention(qkvg, cos, sin, tabs, ctx_len)
    return _merge(x, y_s5, hc, ro, gates, modt, *merge_p, ctx_len)


def kernel(x, c, ctx, c_ctx, ada_w, ada_b, norm1_g, w_in, ssm_lam_re, ssm_lam_im, ssm_log_dt, ssm_b_re, ssm_b_im, ssm_c_re, ssm_c_im, ssm_d, ssm_glu_w, ssm_glu_b, conv_dw_w, conv_dw_b, conv_ln_g, conv_ln_b, conv_pw_w, conv_pw_b, ret_w_o, w_out, norm2_g, router_w, router_b, exp_gu_w, exp_gu_b, exp_down_w, exp_down_b, final_g):
    b, seq, d = x.shape
    ctx_len = ctx.shape[1]
    depth = ada_w.shape[0]
    xs = jnp.concatenate([ctx, x], axis=1)
    s_in = jnp.concatenate([c, c_ctx[None], jnp.zeros((16 - b - 1, d), F32)], axis=0)
    mod = _ada_mod(s_in, ada_w, ada_b).reshape(depth, 16, N_MOD, d)
    mod = jnp.pad(mod, ((0, 0), (0, 0), (0, 8 - N_MOD), (0, 0)))
    modt = jnp.stack([jnp.broadcast_to(mod[:, b][:, None], (depth, b, 8, d)), mod[:, :b]], axis=2)
    tables = _ret_tables(ctx_len, seq)
    wr = jnp.pad(router_w, ((0, 0), (0, 0), (0, LOGIT_LANES - N_EXPERTS)))
    br = jnp.pad(router_b, ((0, 0), (0, LOGIT_LANES - N_EXPERTS))).reshape(depth, 1, LOGIT_LANES)
    for i in range(depth):
        s5p = _s5_params(ssm_lam_re[i], ssm_lam_im[i], ssm_log_dt[i], ssm_b_re[i], ssm_b_im[i],
                         ssm_c_re[i], ssm_c_im[i])
        conv_p = (conv_dw_w[i], conv_dw_b[i], conv_ln_g[i], conv_ln_b[i])
        merge_p = (norm2_g[i], ssm_glu_w[i].astype(BF16), ssm_glu_b[i], conv_pw_w[i].astype(BF16),
                   conv_pw_b[i], ret_w_o[i].astype(BF16), w_out[i].astype(BF16), wr[i], br[i])
        x1, h2, logits = _mixer_layer(xs, modt[i], i, ctx_len, tables, norm1_g[i],
                                      w_in[i].astype(BF16), s5p, ssm_d[i], conv_p, merge_p)
        xs = _moe(x1, h2, logits, modt[i], exp_gu_w, exp_gu_b, exp_down_w, exp_down_b, i, ctx_len)
    return _final_norm(xs, final_g, ctx_len)
```

```python
import functools
import math

import numpy as np
import jax
import jax.numpy as jnp
from jax import lax
from jax.experimental import pallas as pl
from jax.experimental.pallas import tpu as pltpu

F32 = jnp.float32
BF16 = jnp.bfloat16

D_MODEL = 1024
DEPTH = 4
GRID_W = 64
N_MOD = 6
SSM_WIDTH = 256
SSM_GROUP = 16
SSM_GROUPS = SSM_WIDTH // SSM_GROUP
SSM_STATE = 64
SSM_LANES = SSM_GROUPS * SSM_STATE
CONV_WIDTH = 256
CONV_K = 31
CONV_PAD = 16
RET_HEADS = 4
RET_HEAD_DIM = 128
RET_WIDTH = RET_HEADS * RET_HEAD_DIM
RET_CHUNK = 128
ROPE_BASE = 10000.0
IN_WIDTH = SSM_WIDTH + 2 * CONV_WIDTH + 4 * RET_WIDTH + 3 * D_MODEL
COL_CV = SSM_WIDTH
COL_QKVG = COL_CV + 2 * CONV_WIDTH
COL_GATES = COL_QKVG + 4 * RET_WIDTH
N_EXPERTS = 32
TOP_K = 4
EXPERT_FF = D_MODEL
SWIGLU_LIMIT = 7.0
SWIGLU_ALPHA = 1.702
EPS = 1e-6

ROW_TILE = 256
S5_STEPS = 64
S5_STRIP = 512
MOE_ROWS = 256
LOGIT_LANES = 128
VMEM_LIMIT = 56 * 1024 * 1024


def _cparams(sem):
    return pltpu.CompilerParams(dimension_semantics=sem, vmem_limit_bytes=VMEM_LIMIT)


def _rms(x):
    return x * lax.rsqrt(jnp.mean(x * x, axis=-1, keepdims=True) + EPS)


def _ada_kernel(s_ref, w_ref, b_ref, o_ref):
    s = s_ref[...]
    s = s * jax.nn.sigmoid(s)
    o_ref[0] = jnp.dot(s, w_ref[0], preferred_element_type=F32,
                       precision=lax.Precision.HIGHEST) + b_ref[0]


def _ada_mod(s_in, ada_w, ada_b):
    depth, d, n = ada_w.shape
    tn = 1536
    return pl.pallas_call(
        _ada_kernel,
        grid=(depth, n // tn),
        in_specs=[pl.BlockSpec((16, d), lambda i, j: (0, 0)),
                  pl.BlockSpec((1, d, tn), lambda i, j: (i, 0, j)),
                  pl.BlockSpec((1, 1, tn), lambda i, j: (i, 0, j))],
        out_specs=pl.BlockSpec((1, 16, tn), lambda i, j: (i, 0, j)),
        out_shape=jax.ShapeDtypeStruct((depth, 16, n), F32),
        compiler_params=_cparams(("arbitrary", "arbitrary")),
        name="ada_mod",
    )(s_in, ada_w, ada_b.reshape(depth, 1, n))


def _inproj_kernel(x_ref, mod_ref, g_ref, w_ref, u_ref, cv_ref, qkvg_ref, gates_ref):
    x = x_ref[0]
    mod = mod_ref[0, 0]
    h = (_rms(x) * g_ref[...]) * (1.0 + mod[1:2]) + mod[0:1]
    hb = h.astype(BF16)
    u_ref[0] = jnp.dot(hb, w_ref[:, 0:COL_CV], preferred_element_type=F32)
    cv_ref[0] = jnp.dot(hb, w_ref[:, COL_CV:COL_QKVG], preferred_element_type=F32)
    qkvg_ref[0] = jnp.dot(hb, w_ref[:, COL_QKVG:COL_GATES], preferred_element_type=F32)
    gates_ref[0] = jnp.dot(hb, w_ref[:, COL_GATES:IN_WIDTH], preferred_element_type=F32)


def _in_proj(x, modt, g1, w_in_bf, ctx_len):
    b, s_tot, d = x.shape
    nt = s_tot // ROW_TILE
    nctx = ctx_len // ROW_TILE
    seg = lambda j: jnp.where(j >= nctx, 1, 0)
    return pl.pallas_call(
        _inproj_kernel,
        grid=(b, nt),
        in_specs=[pl.BlockSpec((1, ROW_TILE, d), lambda i, j: (i, j, 0)),
                  pl.BlockSpec((1, 1, 8, d), lambda i, j: (i, seg(j), 0, 0)),
                  pl.BlockSpec((1, d), lambda i, j: (0, 0)),
                  pl.BlockSpec((d, IN_WIDTH), lambda i, j: (0, 0))],
        out_specs=[pl.BlockSpec((1, ROW_TILE, SSM_WIDTH), lambda i, j: (i, j, 0)),
                   pl.BlockSpec((1, ROW_TILE, 2 * CONV_WIDTH), lambda i, j: (i, j, 0)),
                   pl.BlockSpec((1, ROW_TILE, 4 * RET_WIDTH), lambda i, j: (i, j, 0)),
                   pl.BlockSpec((1, ROW_TILE, 3 * D_MODEL), lambda i, j: (i, j, 0))],
        out_shape=[jax.ShapeDtypeStruct((b, s_tot, SSM_WIDTH), F32),
                   jax.ShapeDtypeStruct((b, s_tot, 2 * CONV_WIDTH), F32),
                   jax.ShapeDtypeStruct((b, s_tot, 4 * RET_WIDTH), F32),
                   jax.ShapeDtypeStruct((b, s_tot, 3 * D_MODEL), F32)],
        compiler_params=_cparams(("arbitrary", "arbitrary")),
        name="in_proj",
    )(x, modt, g1.reshape(1, d), w_in_bf)


def _s5_params(lam_re, lam_im, log_dt, b_re, b_im, c_re, c_im):
    lam = lax.complex(jnp.minimum(lam_re.astype(F32), -1e-4), lam_im.astype(F32))
    lam_dt = lam * jnp.exp(log_dt.astype(F32))[..., None]
    lam_bar = jnp.exp(lam_dt)
    b = lax.complex(b_re.astype(F32), b_im.astype(F32))
    b_bar = ((lam_bar - 1.0) / lam)[..., None] * b
    eye = jnp.eye(SSM_GROUPS, dtype=F32)

    def in_mat(t):
        return jnp.einsum('dgpm,gh->dgmhp', t, eye).reshape(2, SSM_WIDTH, SSM_LANES)

    def out_mat(t):
        return jnp.einsum('dgmp,gh->dgphm', t, eye).reshape(2, SSM_LANES, SSM_WIDTH)

    bmat = jnp.concatenate([in_mat(jnp.real(b_bar)), in_mat(jnp.imag(b_bar))], axis=-1)
    cmat = jnp.concatenate([out_mat(c_re.astype(F32)), -out_mat(c_im.astype(F32))], axis=1)
    lvec = jnp.concatenate([jnp.real(lam_bar).reshape(2, 1, SSM_LANES),
                            jnp.imag(lam_bar).reshape(2, 1, SSM_LANES)], axis=-1)
    return bmat.astype(BF16), cmat.astype(BF16), lvec


def _s5_kernel(u_ref, perm_ref, permt_ref, bmat_ref, cmat_ref, lam_ref, dsk_ref, y_ref, st_ref, carry_ref):
    d = pl.program_id(0)
    c = pl.program_id(1)
    nb, steps, width = u_ref.shape

    @pl.when(c == 0)
    def _():
        carry_ref[...] = jnp.zeros_like(carry_ref)

    u = u_ref[...].reshape(nb * steps, width)
    u_tm = jnp.dot(perm_ref[...], u.astype(BF16), preferred_element_type=F32).astype(BF16)
    bu = jnp.dot(u_tm, bmat_ref[0], preferred_element_type=F32)
    st_ref[...] = bu.reshape(steps, nb, 2 * SSM_LANES)

    for k in range(SSM_LANES // S5_STRIP):
        re = pl.ds(k * S5_STRIP, S5_STRIP)
        im = pl.ds(SSM_LANES + k * S5_STRIP, S5_STRIP)
        lr = jnp.broadcast_to(lam_ref[0, :, re], (nb, S5_STRIP))
        li = jnp.broadcast_to(lam_ref[0, :, im], (nb, S5_STRIP))

        def step(t, s):
            sr, si = s
            tt = jnp.where(d == 1, steps - 1 - t, t)
            nr = lr * sr - li * si + st_ref[tt, :, re]
            ni = lr * si + li * sr + st_ref[tt, :, im]
            st_ref[tt, :, re] = nr
            st_ref[tt, :, im] = ni
            return nr, ni

        sr, si = lax.fori_loop(0, steps, step, (carry_ref[:, re], carry_ref[:, im]), unroll=2)
        carry_ref[:, re] = sr
        carry_ref[:, im] = si

    st = st_ref[...].reshape(steps * nb, 2 * SSM_LANES)
    y_tm = jnp.dot(st.astype(BF16), cmat_ref[0], preferred_element_type=F32)
    y = jnp.dot(permt_ref[...], y_tm.astype(BF16), preferred_element_type=F32)
    skip = jnp.where(d == 0, 1.0, 0.0) * dsk_ref[...]
    y_ref[0] = (y + u * skip).reshape(nb, steps, width)


def _s5_scan(u, bmat, cmat, lvec, d_skip, ctx_len):
    nb, s_tot, width = u.shape
    nch = s_tot // S5_STEPS
    nctx = ctx_len // S5_STEPS
    rows = nb * S5_STEPS
    r = np.arange(rows)
    perm = np.zeros((rows, rows), np.float32)
    perm[r, (r % nb) * S5_STEPS + r // nb] = 1.0
    perm_tm = jnp.asarray(perm, BF16)
    perm_bm = jnp.asarray(perm.T, BF16)

    def chunk(d, c):
        back = jnp.where(c < nctx, nctx - 1 - c, nctx + nch - 1 - c)
        return jnp.where(d == 1, back, c)

    return pl.pallas_call(
        _s5_kernel,
        grid=(2, nch),
        in_specs=[pl.BlockSpec((nb, S5_STEPS, width), lambda d, c: (0, chunk(d, c), 0)),
                  pl.BlockSpec((rows, rows), lambda d, c: (0, 0)),
                  pl.BlockSpec((rows, rows), lambda d, c: (0, 0)),
                  pl.BlockSpec((1, width, 2 * SSM_LANES), lambda d, c: (d, 0, 0)),
                  pl.BlockSpec((1, 2 * SSM_LANES, width), lambda d, c: (d, 0, 0)),
                  pl.BlockSpec((1, 1, 2 * SSM_LANES), lambda d, c: (d, 0, 0)),
                  pl.BlockSpec((1, width), lambda d, c: (0, 0))],
        out_specs=pl.BlockSpec((1, nb, S5_STEPS, width), lambda d, c: (d, 0, chunk(d, c), 0)),
        out_shape=jax.ShapeDtypeStruct((2, nb, s_tot, width), F32),
        scratch_shapes=[pltpu.VMEM((S5_STEPS, nb, 2 * SSM_LANES), F32),
                        pltpu.VMEM((nb, 2 * SSM_LANES), F32)],
        compiler_params=_cparams(("arbitrary", "arbitrary")),
        name="s5_scan",
    )(u, perm_tm, perm_bm, bmat, cmat, lvec, d_skip.reshape(1, width))


def _conv_kernel(cv_ref, w_ref, b_ref, lg_ref, lb_ref, o_ref, hp_ref, win_ref, *, ctx_len):
    s_tot = cv_ref.shape[1]
    lat = s_tot - ctx_len
    rc = RET_CHUNK
    zeros = jnp.zeros((CONV_PAD, CONV_WIDTH), F32)
    a = cv_ref[0, :, 0:CONV_WIDTH]
    g = cv_ref[0, :, CONV_WIDTH:2 * CONV_WIDTH]
    h = a * jax.nn.sigmoid(g)
    hp_ref[0:CONV_PAD] = zeros
    hp_ref[CONV_PAD:CONV_PAD + ctx_len] = h[0:ctx_len]
    hp_ref[CONV_PAD + ctx_len:2 * CONV_PAD + ctx_len] = zeros
    hp_ref[2 * CONV_PAD + ctx_len:2 * CONV_PAD + s_tot] = h[ctx_len:s_tot]
    hp_ref[2 * CONV_PAD + s_tot:3 * CONV_PAD + s_tot] = zeros
    del lat

    def chunk(c, carry):
        r0 = pl.multiple_of(c * rc, rc)
        wbase = pl.multiple_of(r0 + jnp.where(r0 >= ctx_len, CONV_PAD, 0), 8)
        win_ref[...] = hp_ref[pl.ds(wbase, rc + 2 * CONV_PAD), :]
        acc = jnp.zeros((rc, CONV_WIDTH), F32) + b_ref[...]
        for k in range(CONV_K):
            off = k + CONV_PAD - CONV_K // 2
            acc = acc + w_ref[k:k + 1, :] * win_ref[off:off + rc, :]
        mu = jnp.mean(acc, axis=-1, keepdims=True)
        var = jnp.mean(jnp.square(acc - mu), axis=-1, keepdims=True)
        y = (acc - mu) * lax.rsqrt(var + EPS) * lg_ref[...] + lb_ref[...]
        o_ref[0, pl.ds(r0, rc), :] = y * jax.nn.sigmoid(y)
        return carry

    lax.fori_loop(0, s_tot // rc, chunk, 0)


def _conv_branch(cv, w_dw, b_dw, ln_g, ln_b, ctx_len):
    b, s_tot, _ = cv.shape
    w = jnp.concatenate([w_dw.reshape(CONV_K, CONV_WIDTH), jnp.zeros((1, CONV_WIDTH), F32)], axis=0)
    vec = lambda t: t.reshape(1, CONV_WIDTH)
    cst = lambda shape: pl.BlockSpec(shape, lambda i: (0,) * len(shape))
    return pl.pallas_call(
        functools.partial(_conv_kernel, ctx_len=ctx_len),
        grid=(b,),
        in_specs=[pl.BlockSpec((1, s_tot, 2 * CONV_WIDTH), lambda i: (i, 0, 0)),
                  cst((CONV_K + 1, CONV_WIDTH)), cst((1, CONV_WIDTH)),
                  cst((1, CONV_WIDTH)), cst((1, CONV_WIDTH))],
        out_specs=pl.BlockSpec((1, s_tot, CONV_WIDTH), lambda i: (i, 0, 0)),
        out_shape=jax.ShapeDtypeStruct((b, s_tot, CONV_WIDTH), F32),
        scratch_shapes=[pltpu.VMEM((s_tot + 3 * CONV_PAD, CONV_WIDTH), F32),
                        pltpu.VMEM((RET_CHUNK + 2 * CONV_PAD, CONV_WIDTH), F32)],
        compiler_params=_cparams(("arbitrary",)),
        name="conv_branch",
    )(cv, w, vec(b_dw), vec(ln_g), vec(ln_b))


def _ret_tables(ctx_len, seq):
    n = RET_HEAD_DIM // 4
    inv_freq = ROPE_BASE ** (-np.arange(n, dtype=np.float64) / n)
    pos = np.arange(seq)
    ang_r = (pos // GRID_W)[:, None] * inv_freq
    ang_c = (pos % GRID_W)[:, None] * inv_freq
    cos = np.concatenate([np.cos(ang_r), np.cos(ang_r), np.cos(ang_c), np.cos(ang_c)], axis=-1)
    sin = np.concatenate([-np.sin(ang_r), np.sin(ang_r), -np.sin(ang_c), np.sin(ang_c)], axis=-1)
    cos = np.concatenate([np.ones((ctx_len, RET_HEAD_DIM)), cos], axis=0)
    sin = np.concatenate([np.zeros((ctx_len, RET_HEAD_DIM)), sin], axis=0)
    log_g_fwd = np.log1p(-np.exp2(-5.0 - np.arange(RET_HEADS, dtype=np.float64)))
    idx = np.arange(RET_CHUNK, dtype=np.float64)
    diff = idx[:, None] - idx[None, :]
    tabs = np.zeros((2, RET_HEADS, 4, RET_CHUNK, RET_CHUNK))
    ones = np.ones((RET_CHUNK, RET_CHUNK))
    for d, log_g in enumerate((log_g_fwd, log_g_fwd[::-1])):
        for hd in range(RET_HEADS):
            lg = log_g[hd]
            if d == 0:
                mask = np.where(diff >= 0, np.exp(lg * np.maximum(diff, 0.0)), 0.0)
                dec = np.exp(lg * (idx + 1.0))
                zeta = np.exp(lg * (RET_CHUNK - 1 - idx))
            else:
                mask = np.where(diff <= 0, np.exp(lg * np.maximum(-diff, 0.0)), 0.0)
                dec = np.exp(lg * (RET_CHUNK - idx))
                zeta = np.exp(lg * idx)
            tabs[d, hd, 0] = mask
            tabs[d, hd, 1] = dec[:, None] * ones
            tabs[d, hd, 2] = zeta[:, None] * ones
            tabs[d, hd, 3] = np.exp(lg * RET_CHUNK) * ones
    return (jnp.asarray(cos, F32), jnp.asarray(sin, F32), jnp.asarray(tabs, F32))


def _ret_kernel(q_ref, k_ref, v_ref, g_ref, cos_ref, sin_ref, tab_ref, o_ref, acc_ref, *, ctx_len):
    s_tot = q_ref.shape[1]
    rc = RET_CHUNK
    nch = s_tot // rc
    nctx = ctx_len // rc
    k_scale = RET_HEAD_DIM ** -0.5
    lane = lax.broadcasted_iota(jnp.int32, (rc, RET_HEAD_DIM), 1)
    first_half = (lane % (RET_HEAD_DIM // 2)) < (RET_HEAD_DIM // 4)

    def rope(t, cs, sn):
        quarter = RET_HEAD_DIM // 4
        partner = jnp.where(first_half, pltpu.roll(t, RET_HEAD_DIM - quarter, 1), pltpu.roll(t, quarter, 1))
        return t * cs + partner * sn

    def chunk(c, state, d):
        r0 = pl.multiple_of(c * rc, rc)
        rows = pl.ds(r0, rc)
        cs = cos_ref[rows, :]
        sn = sin_ref[rows, :]
        q = rope(q_ref[0, rows, :], cs, sn)
        k = rope(k_ref[0, rows, :], cs, sn) * k_scale
        vb = v_ref[0, rows, :].astype(BF16)
        mask = tab_ref[d, 0, 0]
        dec = tab_ref[d, 0, 1]
        zeta = tab_ref[d, 0, 2]
        gch = tab_ref[d, 0, 3]
        scores = lax.dot_general(q.astype(BF16), k.astype(BF16), (((1,), (1,)), ((), ())),
                                 preferred_element_type=F32) * mask
        intra = jnp.dot(scores.astype(BF16), vb, preferred_element_type=F32)
        inter = jnp.dot((q * dec).astype(BF16), state.astype(BF16), preferred_element_type=F32)
        kv = lax.dot_general((k * zeta).astype(BF16), vb, (((0,), (0,)), ((), ())),
                             preferred_element_type=F32)
        return rows, intra + inter, gch * state + kv

    zero_state = jnp.zeros((RET_HEAD_DIM, RET_HEAD_DIM), F32)

    def fwd(c, state):
        rows, o, state = chunk(c, state, 0)
        acc_ref[rows, :] = o
        return state

    def bwd(j, state):
        c = jnp.where(j < nctx, nctx - 1 - j, nctx + nch - 1 - j)
        rows, o, state = chunk(c, state, 1)
        acc_ref[rows, :] += o
        return state

    lax.fori_loop(0, nch, fwd, zero_state)
    lax.fori_loop(0, nch, bwd, zero_state)
    g = g_ref[0]
    o_ref[0] = _rms(acc_ref[...]) * (g * jax.nn.sigmoid(g))


def _retention(qkvg, cos, sin, tabs, ctx_len):
    b, s_tot, _ = qkvg.shape
    hd = RET_HEAD_DIM
    part = lambda p: pl.BlockSpec((1, s_tot, hd), lambda i, h: (i, 0, p * RET_HEADS + h))
    return pl.pallas_call(
        functools.partial(_ret_kernel, ctx_len=ctx_len),
        grid=(b, RET_HEADS),
        in_specs=[part(0), part(1), part(2), part(3),
                  pl.BlockSpec((s_tot, hd), lambda i, h: (0, 0)),
                  pl.BlockSpec((s_tot, hd), lambda i, h: (0, 0)),
                  pl.BlockSpec((2, 1, 4, RET_CHUNK, RET_CHUNK), lambda i, h: (0, h, 0, 0, 0))],
        out_specs=pl.BlockSpec((1, s_tot, hd), lambda i, h: (i, 0, h)),
        out_shape=jax.ShapeDtypeStruct((b, s_tot, RET_WIDTH), F32),
        scratch_shapes=[pltpu.VMEM((s_tot, hd), F32)],
        compiler_params=_cparams(("arbitrary", "arbitrary")),
        name="retention",
    )(qkvg, qkvg, qkvg, qkvg, cos, sin, tabs)


def _merge_kernel(x_ref, yf_ref, yb_ref, hc_ref, ro_ref, gates_ref, mod_ref, g2_ref,
                  wglu_ref, bglu_ref, wpw_ref, bpw_ref, wo_ref, wout_ref, wr_ref, br_ref,
                  x1_ref, h2_ref, ti_ref, tg_ref):
    d = D_MODEL
    mod = mod_ref[0, 0]
    ys = jax.nn.gelu(yf_ref[0, 0] + yb_ref[0, 0]).astype(BF16)
    z = jnp.dot(ys, wglu_ref[...], preferred_element_type=F32) + bglu_ref[...]
    y_ssm = z[:, 0:d] * jax.nn.sigmoid(z[:, d:2 * d])
    y_conv = jnp.dot(hc_ref[0].astype(BF16), wpw_ref[...], preferred_element_type=F32) + bpw_ref[...]
    y_ret = jnp.dot(ro_ref[0].astype(BF16), wo_ref[...], preferred_element_type=F32)
    m = (jax.nn.sigmoid(gates_ref[0, :, 0:d]) * y_ssm
         + jax.nn.sigmoid(gates_ref[0, :, d:2 * d]) * y_conv
         + jax.nn.sigmoid(gates_ref[0, :, 2 * d:3 * d]) * y_ret)
    y = jnp.dot(m.astype(BF16), wout_ref[...], preferred_element_type=F32)
    x1 = x_ref[0] + mod[2:3] * y
    x1_ref[0] = x1
    h2 = (_rms(x1) * g2_ref[...]) * (1.0 + mod[4:5]) + mod[3:4]
    h2_ref[0] = h2.astype(BF16)
    logits = jnp.dot(h2, wr_ref[...], preferred_element_type=F32,
                     precision=lax.Precision.HIGHEST) + br_ref[...]
    lane = lax.broadcasted_iota(jnp.int32, logits.shape, 1)
    lane_f = lane.astype(F32)
    neg = jnp.float32(-jnp.inf)
    l = jnp.where(lane < N_EXPERTS, logits, neg)
    ti = jnp.zeros(logits.shape, F32)
    vals = []
    for k in range(TOP_K):
        m = jnp.max(l, axis=-1, keepdims=True)
        idx = jnp.min(jnp.where(l == m, lane_f, float(LOGIT_LANES)), axis=-1, keepdims=True)
        ti = jnp.where(lane == k, idx, ti)
        vals.append(m)
        l = jnp.where(lane_f == idx, neg, l)
    es = [jnp.exp(v - vals[0]) for v in vals]
    tot = es[0]
    for e in es[1:]:
        tot = tot + e
    tg = jnp.zeros(logits.shape, F32)
    for k in range(TOP_K):
        tg = jnp.where(lane == k, es[k] / tot, tg)
    ti_ref[0] = ti.astype(jnp.int32)
    tg_ref[0] = tg


def _merge(x, y_s5, hc, ro, gates, modt, g2, wglu, bglu, wpw, bpw, wo, wout, wr, br, ctx_len):
    b, s_tot, d = x.shape
    nt = s_tot // ROW_TILE
    nctx = ctx_len // ROW_TILE
    seg = lambda j: jnp.where(j >= nctx, 1, 0)
    row = lambda w: pl.BlockSpec((1, ROW_TILE, w), lambda i, j: (i, j, 0))
    cst = lambda shape: pl.BlockSpec(shape, lambda i, j: (0,) * len(shape))
    y2 = y_s5
    return pl.pallas_call(
        _merge_kernel,
        grid=(b, nt),
        in_specs=[row(d),
                  pl.BlockSpec((1, 1, ROW_TILE, SSM_WIDTH), lambda i, j: (0, i, j, 0)),
                  pl.BlockSpec((1, 1, ROW_TILE, SSM_WIDTH), lambda i, j: (1, i, j, 0)),
                  row(CONV_WIDTH), row(RET_WIDTH), row(3 * d),
                  pl.BlockSpec((1, 1, 8, d), lambda i, j: (i, seg(j), 0, 0)),
                  cst((1, d)),
                  cst((SSM_WIDTH, 2 * d)), cst((1, 2 * d)),
                  cst((CONV_WIDTH, d)), cst((1, d)),
                  cst((RET_WIDTH, d)), cst((d, d)),
                  cst((d, LOGIT_LANES)), cst((1, LOGIT_LANES))],
        out_specs=[row(d), row(d), row(LOGIT_LANES), row(LOGIT_LANES)],
        out_shape=[jax.ShapeDtypeStruct((b, s_tot, d), F32),
                   jax.ShapeDtypeStruct((b, s_tot, d), BF16),
                   jax.ShapeDtypeStruct((b, s_tot, LOGIT_LANES), jnp.int32),
                   jax.ShapeDtypeStruct((b, s_tot, LOGIT_LANES), F32)],
        compiler_params=_cparams(("arbitrary", "arbitrary")),
        name="merge_router",
    )(x, y2, y2, hc, ro, gates, modt, g2.reshape(1, d), wglu, bglu.reshape(1, 2 * d),
      wpw, bpw.reshape(1, d), wo, wout, wr, br)


def _route(top_idx):
    n_tok = top_idx.shape[0]
    n_assign = n_tok * TOP_K
    n_tiles = n_assign // MOE_ROWS
    n_visits = n_tiles + N_EXPERTS - 1
    i32 = jnp.int32
    flat_e = top_idx.reshape(-1)
    iota = jnp.arange(n_assign, dtype=i32)
    _, order = lax.sort((flat_e, iota), num_keys=1, is_stable=True)
    _, inv = lax.sort((order, iota), num_keys=1)
    experts = jnp.arange(N_EXPERTS, dtype=i32)
    counts = jnp.sum((flat_e[:, None] == experts[None, :]).astype(i32), axis=0)
    ends = jnp.cumsum(counts)
    starts = ends - counts
    first_tile = starts // MOE_ROWS
    tiles_per = jnp.where(counts > 0, (ends - 1) // MOE_ROWS - first_tile + 1, 0)
    vend = jnp.cumsum(tiles_per)
    vstart = vend - tiles_per
    total = vend[-1]
    v = jnp.arange(n_visits, dtype=i32)
    valid = v < total
    ve = jnp.minimum(jnp.sum((v[:, None] >= vend[None, :]).astype(i32), axis=1), N_EXPERTS - 1)
    ve = jnp.where(valid, ve, ve[jnp.maximum(total - 1, 0)])
    vt = jnp.where(valid, first_tile[ve] + v - vstart[ve], n_tiles - 1)
    lo = jnp.where(valid, jnp.clip(starts[ve] - vt * MOE_ROWS, 0, MOE_ROWS), 0)
    hi = jnp.where(valid, jnp.clip(ends[ve] - vt * MOE_ROWS, 0, MOE_ROWS), 0)
    return order // TOP_K, inv, (vt.astype(i32), ve.astype(i32), lo.astype(i32), hi.astype(i32))


def _expert_kernel(vt_ref, ve_ref, lo_ref, hi_ref, x_ref, wgu_ref, bgu_ref, wd_ref, bd_ref, y_ref,
                   wgu_bf, wd_bf):
    v = pl.program_id(0)
    prev = jnp.maximum(v - 1, 0)
    lo = lo_ref[v]
    hi = hi_ref[v]
    active = hi > lo
    fresh_weights = jnp.logical_or(v == 0, ve_ref[v] != ve_ref[prev])
    first_visit = jnp.logical_or(v == 0, vt_ref[v] != vt_ref[prev])

    @pl.when(jnp.logical_and(fresh_weights, active))
    def _():
        wgu_bf[...] = wgu_ref[0, 0].astype(BF16)
        wd_bf[...] = wd_ref[0, 0].astype(BF16)

    @pl.when(active)
    def _():
        f = EXPERT_FF
        gu = jnp.dot(x_ref[...], wgu_bf[...], preferred_element_type=F32) + bgu_ref[0, 0]
        gate = jnp.minimum(gu[:, 0:f], SWIGLU_LIMIT)
        up = jnp.clip(gu[:, f:2 * f], -SWIGLU_LIMIT, SWIGLU_LIMIT)
        act = (up + 1.0) * gate * jax.nn.sigmoid(gate * SWIGLU_ALPHA)
        y = jnp.dot(act.astype(BF16), wd_bf[...], preferred_element_type=F32) + bd_ref[0, 0]
        row = lax.broadcasted_iota(jnp.int32, (MOE_ROWS, 1), 0)
        mine = jnp.logical_and(row >= lo, row < hi)

        @pl.when(first_visit)
        def _():
            y_ref[...] = jnp.where(mine, y, 0.0).astype(y_ref.dtype)

        @pl.when(jnp.logical_not(first_visit))
        def _():
            y_ref[...] = jnp.where(mine, y.astype(y_ref.dtype), y_ref[...])


def _experts(xs, visits, w_gu, b_gu, w_down, b_down, layer):
    n_rows, d = xs.shape
    n_visits = visits[0].shape[0]
    f = EXPERT_FF
    wmap = lambda v, vt, ve, lo, hi: (layer, ve[v], 0, 0)
    tmap = lambda v, vt, ve, lo, hi: (vt[v], 0)
    grid_spec = pltpu.PrefetchScalarGridSpec(
        num_scalar_prefetch=4,
        grid=(n_visits,),
        in_specs=[pl.BlockSpec((MOE_ROWS, d), tmap),
                  pl.BlockSpec((1, 1, d, 2 * f), wmap),
                  pl.BlockSpec((1, 1, 1, 2 * f), wmap),
                  pl.BlockSpec((1, 1, f, d), wmap),
                  pl.BlockSpec((1, 1, 1, d), wmap)],
        out_specs=pl.BlockSpec((MOE_ROWS, d), tmap),
        scratch_shapes=[pltpu.VMEM((d, 2 * f), BF16), pltpu.VMEM((f, d), BF16)],
    )
    return pl.pallas_call(
        _expert_kernel,
        grid_spec=grid_spec,
        out_shape=jax.ShapeDtypeStruct((n_rows, d), BF16),
        compiler_params=_cparams(("arbitrary",)),
        name="experts",
    )(*visits, xs, w_gu, b_gu.reshape(DEPTH, N_EXPERTS, 1, 2 * f),
      w_down, b_down.reshape(DEPTH, N_EXPERTS, 1, d))


def _combine_kernel(x_ref, y0_ref, y1_ref, y2_ref, y3_ref, gt_ref, mod_ref, o_ref):
    mod = mod_ref[0, 0]
    gt = gt_ref[0]
    y = gt[:, 0:1] * y0_ref[0].astype(F32)
    for k, y_ref in ((1, y1_ref), (2, y2_ref), (3, y3_ref)):
        y = y + gt[:, k:k + 1] * y_ref[0].astype(F32)
    o_ref[0] = x_ref[0] + mod[5:6] * y


def _combine(x1, yg, gates, modt, ctx_len):
    b, s_tot, d = x1.shape
    nt = s_tot // ROW_TILE
    nctx = ctx_len // ROW_TILE
    seg = lambda j: jnp.where(j >= nctx, 1, 0)
    choice = lambda k: pl.BlockSpec((1, ROW_TILE, d), lambda i, j: (k, i * nt + j, 0))
    return pl.pallas_call(
        _combine_kernel,
        grid=(b, nt),
        in_specs=[pl.BlockSpec((1, ROW_TILE, d), lambda i, j: (i, j, 0)),
                  choice(0), choice(1), choice(2), choice(3),
                  pl.BlockSpec((1, ROW_TILE, LOGIT_LANES), lambda i, j: (i, j, 0)),
                  pl.BlockSpec((1, 1, 8, d), lambda i, j: (i, seg(j), 0, 0))],
        out_specs=pl.BlockSpec((1, ROW_TILE, d), lambda i, j: (i, j, 0)),
        out_shape=jax.ShapeDtypeStruct((b, s_tot, d), F32),
        compiler_params=_cparams(("arbitrary", "arbitrary")),
        name="moe_combine",
    )(x1, yg, yg, yg, yg, gates, modt)


def _moe(x1, h2, top_idx, top_gate, modt, exp_gu_w, exp_gu_b, exp_down_w, exp_down_b, layer, ctx_len):
    b, s_tot, d = x1.shape
    n_tok = b * s_tot
    rows, inv, visits = _route(top_idx.reshape(n_tok, LOGIT_LANES)[:, :TOP_K])
    xs = jnp.take(h2.reshape(n_tok, d), rows, axis=0)
    ys = _experts(xs, visits, exp_gu_w, exp_gu_b, exp_down_w, exp_down_b, layer)
    by_choice = inv.reshape(n_tok, TOP_K).T.reshape(-1)
    yg = jnp.take(ys, by_choice, axis=0).reshape(TOP_K, n_tok, d)
    return _combine(x1, yg, top_gate, modt, ctx_len)


def _final_kernel(x_ref, g_ref, o_ref):
    o_ref[0] = _rms(x_ref[0]) * g_ref[...]


def _final_norm(x, g, ctx_len):
    b, s_tot, d = x.shape
    seq = s_tot - ctx_len
    off = ctx_len // ROW_TILE
    return pl.pallas_call(
        _final_kernel,
        grid=(b, seq // ROW_TILE),
        in_specs=[pl.BlockSpec((1, ROW_TILE, d), lambda i, j: (i, j + off, 0)),
                  pl.BlockSpec((1, d), lambda i, j: (0, 0))],
        out_specs=pl.BlockSpec((1, ROW_TILE, d), lambda i, j: (i, j, 0)),
        out_shape=jax.ShapeDtypeStruct((b, seq, d), F32),
        compiler_params=_cparams(("arbitrary", "arbitrary")),
        name="final_norm",
    )(x, g.reshape(1, d))


def _mixer_layer(x, modt, i, ctx_len, tables, norm1_g, w_in_bf, s5p, ssm_d, conv_p, merge_p):
    b, s_tot, d = x.shape
    u2, cv, qkvg, gates = _in_proj(x, modt, norm1_g, w_in_bf, ctx_len)
    bmat, cmat, lvec = s5p
    y_s5 = _s5_scan(u2, bmat, cmat, lvec, ssm_d, ctx_len)
    hc = _conv_branch(cv, *conv_p, ctx_len)
    cos, sin, tabs = tables
    ro = _retention(qkvg, cos, sin, tabs, ctx_len)
    return _merge(x, y_s5, hc, ro, gates, modt, *merge_p, ctx_len)


def kernel(x, c, ctx, c_ctx, ada_w, ada_b, norm1_g, w_in, ssm_lam_re, ssm_lam_im, ssm_log_dt, ssm_b_re, ssm_b_im, ssm_c_re, ssm_c_im, ssm_d, ssm_glu_w, ssm_glu_b, conv_dw_w, conv_dw_b, conv_ln_g, conv_ln_b, conv_pw_w, conv_pw_b, ret_w_o, w_out, norm2_g, router_w, router_b, exp_gu_w, exp_gu_b, exp_down_w, exp_down_b, final_g):
    b, seq, d = x.shape
    ctx_len = ctx.shape[1]
    depth = ada_w.shape[0]
    xs = jnp.concatenate([ctx, x], axis=1)
    s_in = jnp.concatenate([c, c_ctx[None], jnp.zeros((16 - b - 1, d), F32)], axis=0)
    mod = _ada_mod(s_in, ada_w, ada_b).reshape(depth, 16, N_MOD, d)
    mod = jnp.pad(mod, ((0, 0), (0, 0), (0, 8 - N_MOD), (0, 0)))
    modt = jnp.stack([jnp.broadcast_to(mod[:, b][:, None], (depth, b, 8, d)), mod[:, :b]], axis=2)
    tables = _ret_tables(ctx_len, seq)
    wr = jnp.pad(router_w, ((0, 0), (0, 0), (0, LOGIT_LANES - N_EXPERTS)))
    br = jnp.pad(router_b, ((0, 0), (0, LOGIT_LANES - N_EXPERTS))).reshape(depth, 1, LOGIT_LANES)
    for i in range(depth):
        s5p = _s5_params(ssm_lam_re[i], ssm_lam_im[i], ssm_log_dt[i], ssm_b_re[i], ssm_b_im[i],
                         ssm_c_re[i], ssm_c_im[i])
        conv_p = (conv_dw_w[i], conv_dw_b[i], conv_ln_g[i], conv_ln_b[i])
        merge_p = (norm2_g[i], ssm_glu_w[i].astype(BF16), ssm_glu_b[i], conv_pw_w[i].astype(BF16),
                   conv_pw_b[i], ret_w_o[i].astype(BF16), w_out[i].astype(BF16), wr[i], br[i])
        x1, h2, top_idx, top_gate = _mixer_layer(xs, modt[i], i, ctx_len, tables, norm1_g[i],
                                                 w_in[i].astype(BF16), s5p, ssm_d[i], conv_p, merge_p)
        xs = _moe(x1, h2, top_idx, top_gate, modt[i], exp_gu_w, exp_gu_b, exp_down_w, exp_down_b,
                  i, ctx_len)
    return _final_norm(xs, final_g, ctx_len)
```

```python
import functools
import math

import numpy as np
import jax
import jax.numpy as jnp
from jax import lax
from jax.experimental import pallas as pl
from jax.experimental.pallas import tpu as pltpu

F32 = jnp.float32
BF16 = jnp.bfloat16

D_MODEL = 1024
DEPTH = 4
GRID_W = 64
N_MOD = 6
SSM_WIDTH = 256
SSM_GROUP = 16
SSM_GROUPS = SSM_WIDTH // SSM_GROUP
SSM_STATE = 64
SSM_LANES = SSM_GROUPS * SSM_STATE
CONV_WIDTH = 256
CONV_K = 31
CONV_PAD = 16
RET_HEADS = 4
RET_HEAD_DIM = 128
RET_WIDTH = RET_HEADS * RET_HEAD_DIM
RET_CHUNK = 128
ROPE_BASE = 10000.0
IN_WIDTH = SSM_WIDTH + 2 * CONV_WIDTH + 4 * RET_WIDTH + 3 * D_MODEL
COL_CV = SSM_WIDTH
COL_QKVG = COL_CV + 2 * CONV_WIDTH
COL_GATES = COL_QKVG + 4 * RET_WIDTH
N_EXPERTS = 32
TOP_K = 4
EXPERT_FF = D_MODEL
SWIGLU_LIMIT = 7.0
SWIGLU_ALPHA = 1.702
EPS = 1e-6

ROW_TILE = 256
S5_STEPS = 64
S5_STRIP = 512
MOE_ROWS = 256
FF_SLICE = 256
LOGIT_LANES = 128
VMEM_LIMIT = 56 * 1024 * 1024


def _cparams(sem):
    return pltpu.CompilerParams(dimension_semantics=sem, vmem_limit_bytes=VMEM_LIMIT)


def _rms(x):
    return x * lax.rsqrt(jnp.mean(x * x, axis=-1, keepdims=True) + EPS)


def _ada_kernel(s_ref, w_ref, b_ref, o_ref):
    s = s_ref[...]
    s = s * jax.nn.sigmoid(s)
    o_ref[0] = jnp.dot(s, w_ref[0], preferred_element_type=F32,
                       precision=lax.Precision.HIGHEST) + b_ref[0]


def _ada_mod(s_in, ada_w, ada_b):
    depth, d, n = ada_w.shape
    tn = 1536
    return pl.pallas_call(
        _ada_kernel,
        grid=(depth, n // tn),
        in_specs=[pl.BlockSpec((16, d), lambda i, j: (0, 0)),
                  pl.BlockSpec((1, d, tn), lambda i, j: (i, 0, j)),
                  pl.BlockSpec((1, 1, tn), lambda i, j: (i, 0, j))],
        out_specs=pl.BlockSpec((1, 16, tn), lambda i, j: (i, 0, j)),
        out_shape=jax.ShapeDtypeStruct((depth, 16, n), F32),
        compiler_params=_cparams(("arbitrary", "arbitrary")),
        name="ada_mod",
    )(s_in, ada_w, ada_b.reshape(depth, 1, n))


def _inproj_kernel(x_ref, mod_ref, g_ref, w_ref, u_ref, cv_ref, qkvg_ref, gates_ref):
    x = x_ref[0]
    mod = mod_ref[0, 0]
    h = (_rms(x) * g_ref[...]) * (1.0 + mod[1:2]) + mod[0:1]
    hb = h.astype(BF16)
    u_ref[0] = jnp.dot(hb, w_ref[:, 0:COL_CV], preferred_element_type=F32)
    cv_ref[0] = jnp.dot(hb, w_ref[:, COL_CV:COL_QKVG], preferred_element_type=F32)
    qkvg_ref[0] = jnp.dot(hb, w_ref[:, COL_QKVG:COL_GATES], preferred_element_type=F32).astype(BF16)
    gates = jnp.dot(hb, w_ref[:, COL_GATES:IN_WIDTH], preferred_element_type=F32)
    gates_ref[0] = jax.nn.sigmoid(gates).astype(BF16)


def _in_proj(x, modt, g1, w_in_bf, ctx_len):
    b, s_tot, d = x.shape
    nt = s_tot // ROW_TILE
    nctx = ctx_len // ROW_TILE
    seg = lambda j: jnp.where(j >= nctx, 1, 0)
    return pl.pallas_call(
        _inproj_kernel,
        grid=(b, nt),
        in_specs=[pl.BlockSpec((1, ROW_TILE, d), lambda i, j: (i, j, 0)),
                  pl.BlockSpec((1, 1, 8, d), lambda i, j: (i, seg(j), 0, 0)),
                  pl.BlockSpec((1, d), lambda i, j: (0, 0)),
                  pl.BlockSpec((d, IN_WIDTH), lambda i, j: (0, 0))],
        out_specs=[pl.BlockSpec((1, ROW_TILE, SSM_WIDTH), lambda i, j: (i, j, 0)),
                   pl.BlockSpec((1, ROW_TILE, 2 * CONV_WIDTH), lambda i, j: (i, j, 0)),
                   pl.BlockSpec((1, ROW_TILE, 4 * RET_WIDTH), lambda i, j: (i, j, 0)),
                   pl.BlockSpec((1, ROW_TILE, 3 * D_MODEL), lambda i, j: (i, j, 0))],
        out_shape=[jax.ShapeDtypeStruct((b, s_tot, SSM_WIDTH), F32),
                   jax.ShapeDtypeStruct((b, s_tot, 2 * CONV_WIDTH), F32),
                   jax.ShapeDtypeStruct((b, s_tot, 4 * RET_WIDTH), BF16),
                   jax.ShapeDtypeStruct((b, s_tot, 3 * D_MODEL), BF16)],
        compiler_params=_cparams(("arbitrary", "arbitrary")),
        name="in_proj",
    )(x, modt, g1.reshape(1, d), w_in_bf)


def _s5_params(lam_re, lam_im, log_dt, b_re, b_im, c_re, c_im):
    lam = lax.complex(jnp.minimum(lam_re.astype(F32), -1e-4), lam_im.astype(F32))
    lam_dt = lam * jnp.exp(log_dt.astype(F32))[..., None]
    lam_bar = jnp.exp(lam_dt)
    b = lax.complex(b_re.astype(F32), b_im.astype(F32))
    b_bar = ((lam_bar - 1.0) / lam)[..., None] * b
    eye = jnp.eye(SSM_GROUPS, dtype=F32)

    def in_mat(t):
        return jnp.einsum('dgpm,gh->dgmhp', t, eye).reshape(2, SSM_WIDTH, SSM_LANES)

    def out_mat(t):
        return jnp.einsum('dgmp,gh->dgphm', t, eye).reshape(2, SSM_LANES, SSM_WIDTH)

    bmat = jnp.concatenate([in_mat(jnp.real(b_bar)), in_mat(jnp.imag(b_bar))], axis=-1)
    cmat = jnp.concatenate([out_mat(c_re.astype(F32)), -out_mat(c_im.astype(F32))], axis=1)
    lvec = jnp.concatenate([jnp.real(lam_bar).reshape(2, 1, SSM_LANES),
                            jnp.imag(lam_bar).reshape(2, 1, SSM_LANES)], axis=-1)
    return bmat.astype(BF16), cmat.astype(BF16), lvec


def _s5_kernel(u_ref, perm_ref, permt_ref, bmat_ref, cmat_ref, lam_ref, dsk_ref, y_ref, st_ref, carry_ref):
    d = pl.program_id(0)
    c = pl.program_id(1)
    nb, steps, width = u_ref.shape

    @pl.when(c == 0)
    def _():
        carry_ref[...] = jnp.zeros_like(carry_ref)

    u = u_ref[...].reshape(nb * steps, width)
    u_tm = jnp.dot(perm_ref[...], u.astype(BF16), preferred_element_type=F32).astype(BF16)
    bu = jnp.dot(u_tm, bmat_ref[0], preferred_element_type=F32)
    st_ref[...] = bu.reshape(steps, nb, 2 * SSM_LANES)

    for k in range(SSM_LANES // S5_STRIP):
        re = pl.ds(k * S5_STRIP, S5_STRIP)
        im = pl.ds(SSM_LANES + k * S5_STRIP, S5_STRIP)
        lr = jnp.broadcast_to(lam_ref[0, :, re], (nb, S5_STRIP))
        li = jnp.broadcast_to(lam_ref[0, :, im], (nb, S5_STRIP))

        def step(t, s):
            sr, si = s
            tt = jnp.where(d == 1, steps - 1 - t, t)
            nr = lr * sr - li * si + st_ref[tt, :, re]
            ni = lr * si + li * sr + st_ref[tt, :, im]
            st_ref[tt, :, re] = nr
            st_ref[tt, :, im] = ni
            return nr, ni

        sr, si = lax.fori_loop(0, steps, step, (carry_ref[:, re], carry_ref[:, im]), unroll=2)
        carry_ref[:, re] = sr
        carry_ref[:, im] = si

    st = st_ref[...].reshape(steps * nb, 2 * SSM_LANES)
    y_tm = jnp.dot(st.astype(BF16), cmat_ref[0], preferred_element_type=F32)
    y = jnp.dot(permt_ref[...], y_tm.astype(BF16), preferred_element_type=F32)
    skip = jnp.where(d == 0, 1.0, 0.0) * dsk_ref[...]
    y_ref[0] = (y + u * skip).reshape(nb, steps, width)


def _s5_scan(u, bmat, cmat, lvec, d_skip, ctx_len):
    nb, s_tot, width = u.shape
    nch = s_tot // S5_STEPS
    nctx = ctx_len // S5_STEPS
    rows = nb * S5_STEPS
    r = np.arange(rows)
    perm = np.zeros((rows, rows), np.float32)
    perm[r, (r % nb) * S5_STEPS + r // nb] = 1.0
    perm_tm = jnp.asarray(perm, BF16)
    perm_bm = jnp.asarray(perm.T, BF16)

    def chunk(d, c):
        back = jnp.where(c < nctx, nctx - 1 - c, nctx + nch - 1 - c)
        return jnp.where(d == 1, back, c)

    return pl.pallas_call(
        _s5_kernel,
        grid=(2, nch),
        in_specs=[pl.BlockSpec((nb, S5_STEPS, width), lambda d, c: (0, chunk(d, c), 0)),
                  pl.BlockSpec((rows, rows), lambda d, c: (0, 0)),
                  pl.BlockSpec((rows, rows), lambda d, c: (0, 0)),
                  pl.BlockSpec((1, width, 2 * SSM_LANES), lambda d, c: (d, 0, 0)),
                  pl.BlockSpec((1, 2 * SSM_LANES, width), lambda d, c: (d, 0, 0)),
                  pl.BlockSpec((1, 1, 2 * SSM_LANES), lambda d, c: (d, 0, 0)),
                  pl.BlockSpec((1, width), lambda d, c: (0, 0))],
        out_specs=pl.BlockSpec((1, nb, S5_STEPS, width), lambda d, c: (d, 0, chunk(d, c), 0)),
        out_shape=jax.ShapeDtypeStruct((2, nb, s_tot, width), F32),
        scratch_shapes=[pltpu.VMEM((S5_STEPS, nb, 2 * SSM_LANES), F32),
                        pltpu.VMEM((nb, 2 * SSM_LANES), F32)],
        compiler_params=_cparams(("arbitrary", "arbitrary")),
        name="s5_scan",
    )(u, perm_tm, perm_bm, bmat, cmat, lvec, d_skip.reshape(1, width))


def _conv_kernel(cv_ref, w_ref, b_ref, lg_ref, lb_ref, o_ref, hp_ref, win_ref, *, ctx_len):
    s_tot = cv_ref.shape[1]
    lat = s_tot - ctx_len
    rc = RET_CHUNK
    zeros = jnp.zeros((CONV_PAD, CONV_WIDTH), F32)
    a = cv_ref[0, :, 0:CONV_WIDTH]
    g = cv_ref[0, :, CONV_WIDTH:2 * CONV_WIDTH]
    h = a * jax.nn.sigmoid(g)
    hp_ref[0:CONV_PAD] = zeros
    hp_ref[CONV_PAD:CONV_PAD + ctx_len] = h[0:ctx_len]
    hp_ref[CONV_PAD + ctx_len:2 * CONV_PAD + ctx_len] = zeros
    hp_ref[2 * CONV_PAD + ctx_len:2 * CONV_PAD + s_tot] = h[ctx_len:s_tot]
    hp_ref[2 * CONV_PAD + s_tot:3 * CONV_PAD + s_tot] = zeros
    del lat

    def chunk(c, carry):
        r0 = pl.multiple_of(c * rc, rc)
        wbase = pl.multiple_of(r0 + jnp.where(r0 >= ctx_len, CONV_PAD, 0), 8)
        win_ref[...] = hp_ref[pl.ds(wbase, rc + 2 * CONV_PAD), :]
        acc = jnp.zeros((rc, CONV_WIDTH), F32) + b_ref[...]
        for k in range(CONV_K):
            off = k + CONV_PAD - CONV_K // 2
            acc = acc + w_ref[k:k + 1, :] * win_ref[off:off + rc, :]
        mu = jnp.mean(acc, axis=-1, keepdims=True)
        var = jnp.mean(jnp.square(acc - mu), axis=-1, keepdims=True)
        y = (acc - mu) * lax.rsqrt(var + EPS) * lg_ref[...] + lb_ref[...]
        o_ref[0, pl.ds(r0, rc), :] = y * jax.nn.sigmoid(y)
        return carry

    lax.fori_loop(0, s_tot // rc, chunk, 0)


def _conv_branch(cv, w_dw, b_dw, ln_g, ln_b, ctx_len):
    b, s_tot, _ = cv.shape
    w = jnp.concatenate([w_dw.reshape(CONV_K, CONV_WIDTH), jnp.zeros((1, CONV_WIDTH), F32)], axis=0)
    vec = lambda t: t.reshape(1, CONV_WIDTH)
    cst = lambda shape: pl.BlockSpec(shape, lambda i: (0,) * len(shape))
    return pl.pallas_call(
        functools.partial(_conv_kernel, ctx_len=ctx_len),
        grid=(b,),
        in_specs=[pl.BlockSpec((1, s_tot, 2 * CONV_WIDTH), lambda i: (i, 0, 0)),
                  cst((CONV_K + 1, CONV_WIDTH)), cst((1, CONV_WIDTH)),
                  cst((1, CONV_WIDTH)), cst((1, CONV_WIDTH))],
        out_specs=pl.BlockSpec((1, s_tot, CONV_WIDTH), lambda i: (i, 0, 0)),
        out_shape=jax.ShapeDtypeStruct((b, s_tot, CONV_WIDTH), F32),
        scratch_shapes=[pltpu.VMEM((s_tot + 3 * CONV_PAD, CONV_WIDTH), F32),
                        pltpu.VMEM((RET_CHUNK + 2 * CONV_PAD, CONV_WIDTH), F32)],
        compiler_params=_cparams(("arbitrary",)),
        name="conv_branch",
    )(cv, w, vec(b_dw), vec(ln_g), vec(ln_b))


N_TAB = 5


def _ret_tables(ctx_len, seq):
    n = RET_HEAD_DIM // 4
    inv_freq = ROPE_BASE ** (-np.arange(n, dtype=np.float64) / n)
    pos = np.arange(seq)
    ang_r = (pos // GRID_W)[:, None] * inv_freq
    ang_c = (pos % GRID_W)[:, None] * inv_freq
    cos = np.concatenate([np.cos(ang_r), np.cos(ang_r), np.cos(ang_c), np.cos(ang_c)], axis=-1)
    sin = np.concatenate([-np.sin(ang_r), np.sin(ang_r), -np.sin(ang_c), np.sin(ang_c)], axis=-1)
    cos = np.concatenate([np.ones((ctx_len, RET_HEAD_DIM)), cos], axis=0)
    sin = np.concatenate([np.zeros((ctx_len, RET_HEAD_DIM)), sin], axis=0)
    log_g_fwd = np.log1p(-np.exp2(-5.0 - np.arange(RET_HEADS, dtype=np.float64)))
    idx = np.arange(RET_CHUNK, dtype=np.float64)
    diff = idx[:, None] - idx[None, :]
    tabs = np.zeros((2, RET_HEADS, N_TAB, RET_CHUNK, RET_CHUNK))
    ones = np.ones((RET_CHUNK, RET_CHUNK))
    for d, log_g in enumerate((log_g_fwd, log_g_fwd[::-1])):
        for hd in range(RET_HEADS):
            lg = log_g[hd]
            if d == 0:
                mask = np.where(diff >= 0, np.exp(lg * np.maximum(diff, 0.0)), 0.0)
                dec = np.exp(lg * (idx + 1.0))
                zeta = np.exp(lg * (RET_CHUNK - 1 - idx))
            else:
                mask = np.where(diff <= 0, np.exp(lg * np.maximum(-diff, 0.0)), 0.0)
                dec = np.exp(lg * (RET_CHUNK - idx))
                zeta = np.exp(lg * idx)
            tabs[d, hd, 0] = mask
            tabs[d, hd, 1] = dec[:, None] * ones
            tabs[d, hd, 2] = zeta[:, None] * ones
            tabs[d, hd, 3] = np.exp(lg * RET_CHUNK) * ones
            tabs[d, hd, 4] = zeta[None, :] * ones
    return (jnp.asarray(cos, F32), jnp.asarray(sin, F32), jnp.asarray(tabs, F32))


def _ret_kernel(q_ref, k_ref, v_ref, g_ref, cos_ref, sin_ref, tab_ref, o_ref,
                qs_ref, qd_ref, kt_ref, kz_ref, out_ref, state_ref, *, ctx_len):
    s_tot = q_ref.shape[1]
    rc = RET_CHUNK
    nch = s_tot // rc
    nctx = ctx_len // rc
    k_scale = RET_HEAD_DIM ** -0.5
    lane = lax.broadcasted_iota(jnp.int32, (rc, RET_HEAD_DIM), 1)
    first_half = (lane % (RET_HEAD_DIM // 2)) < (RET_HEAD_DIM // 4)

    def rope(t, cs, sn):
        quarter = RET_HEAD_DIM // 4
        partner = jnp.where(first_half, pltpu.roll(t, RET_HEAD_DIM - quarter, 1), pltpu.roll(t, quarter, 1))
        return t * cs + partner * sn

    def prepare(c, carry):
        rows = pl.ds(pl.multiple_of(c * rc, rc), rc)
        cs = cos_ref[rows, :]
        sn = sin_ref[rows, :]
        q = rope(q_ref[0, rows, :].astype(F32), cs, sn)
        kt = (rope(k_ref[0, rows, :].astype(F32), cs, sn) * k_scale).T
        qs_ref[c] = q.astype(BF16)
        kt_ref[c] = kt.astype(BF16)
        for d in range(2):
            qd_ref[d, c] = (q * tab_ref[d, 0, 1]).astype(BF16)
            kz_ref[d, c] = (kt * tab_ref[d, 0, 4]).astype(BF16)
        return carry

    lax.fori_loop(0, nch, prepare, 0)
    state_ref[...] = jnp.zeros_like(state_ref)

    def chunk(c, d):
        rows = pl.ds(pl.multiple_of(c * rc, rc), rc)
        vb = v_ref[0, rows, :]
        state = state_ref[d]
        scores = jnp.dot(qs_ref[c], kt_ref[c], preferred_element_type=F32) * tab_ref[d, 0, 0]
        o = (jnp.dot(scores.astype(BF16), vb, preferred_element_type=F32)
             + jnp.dot(qd_ref[d, c], state.astype(BF16), preferred_element_type=F32))
        kv = jnp.dot(kz_ref[d, c], vb, preferred_element_type=F32)
        state_ref[d] = tab_ref[d, 0, 3] * state + kv
        out_ref[d, rows, :] = o

    def step(j, carry):
        chunk(j, 0)
        chunk(jnp.where(j < nctx, nctx - 1 - j, nctx + nch - 1 - j), 1)
        return carry

    lax.fori_loop(0, nch, step, 0)
    g = g_ref[0].astype(F32)
    o_ref[0] = (_rms(out_ref[0] + out_ref[1]) * (g * jax.nn.sigmoid(g))).astype(o_ref.dtype)


def _retention(qkvg, cos, sin, tabs, ctx_len):
    b, s_tot, _ = qkvg.shape
    hd = RET_HEAD_DIM
    nch = s_tot // RET_CHUNK
    part = lambda p: pl.BlockSpec((1, s_tot, hd), lambda i, h: (i, 0, p * RET_HEADS + h))
    return pl.pallas_call(
        functools.partial(_ret_kernel, ctx_len=ctx_len),
        grid=(b, RET_HEADS),
        in_specs=[part(0), part(1), part(2), part(3),
                  pl.BlockSpec((s_tot, hd), lambda i, h: (0, 0)),
                  pl.BlockSpec((s_tot, hd), lambda i, h: (0, 0)),
                  pl.BlockSpec((2, 1, N_TAB, RET_CHUNK, RET_CHUNK), lambda i, h: (0, h, 0, 0, 0))],
        out_specs=pl.BlockSpec((1, s_tot, hd), lambda i, h: (i, 0, h)),
        out_shape=jax.ShapeDtypeStruct((b, s_tot, RET_WIDTH), BF16),
        scratch_shapes=[pltpu.VMEM((nch, RET_CHUNK, hd), BF16),
                        pltpu.VMEM((2, nch, RET_CHUNK, hd), BF16),
                        pltpu.VMEM((nch, hd, RET_CHUNK), BF16),
                        pltpu.VMEM((2, nch, hd, RET_CHUNK), BF16),
                        pltpu.VMEM((2, s_tot, hd), F32),
                        pltpu.VMEM((2, hd, hd), F32)],
        compiler_params=_cparams(("arbitrary", "arbitrary")),
        name="retention",
    )(qkvg, qkvg, qkvg, qkvg, cos, sin, tabs)


def _merge_kernel(x_ref, yf_ref, yb_ref, hc_ref, ro_ref, gates_ref, mod_ref, g2_ref,
                  wglu_ref, bglu_ref, wpw_ref, bpw_ref, wo_ref, wout_ref, wr_ref, br_ref,
                  x1_ref, h2_ref, ti_ref, tg_ref):
    d = D_MODEL
    mod = mod_ref[0, 0]
    ys = jax.nn.gelu(yf_ref[0, 0] + yb_ref[0, 0]).astype(BF16)
    z = jnp.dot(ys, wglu_ref[...], preferred_element_type=F32) + bglu_ref[...]
    y_ssm = z[:, 0:d] * jax.nn.sigmoid(z[:, d:2 * d])
    y_conv = jnp.dot(hc_ref[0].astype(BF16), wpw_ref[...], preferred_element_type=F32) + bpw_ref[...]
    y_ret = jnp.dot(ro_ref[0].astype(BF16), wo_ref[...], preferred_element_type=F32)
    m = (gates_ref[0, :, 0:d].astype(F32) * y_ssm
         + gates_ref[0, :, d:2 * d].astype(F32) * y_conv
         + gates_ref[0, :, 2 * d:3 * d].astype(F32) * y_ret)
    y = jnp.dot(m.astype(BF16), wout_ref[...], preferred_element_type=F32)
    x1 = x_ref[0] + mod[2:3] * y
    x1_ref[0] = x1
    h2 = (_rms(x1) * g2_ref[...]) * (1.0 + mod[4:5]) + mod[3:4]
    h2_ref[0] = h2
    logits = jnp.dot(h2, wr_ref[...], preferred_element_type=F32,
                     precision=lax.Precision.HIGHEST) + br_ref[...]
    lane = lax.broadcasted_iota(jnp.int32, logits.shape, 1)
    lane_f = lane.astype(F32)
    neg = jnp.float32(-jnp.inf)
    l = jnp.where(lane < N_EXPERTS, logits, neg)
    ti = jnp.zeros(logits.shape, F32)
    vals = []
    for k in range(TOP_K):
        m = jnp.max(l, axis=-1, keepdims=True)
        idx = jnp.min(jnp.where(l == m, lane_f, float(LOGIT_LANES)), axis=-1, keepdims=True)
        ti = jnp.where(lane == k, idx, ti)
        vals.append(m)
        l = jnp.where(lane_f == idx, neg, l)
    es = [jnp.exp(v - vals[0]) for v in vals]
    tot = es[0]
    for e in es[1:]:
        tot = tot + e
    tg = jnp.zeros(logits.shape, F32)
    for k in range(TOP_K):
        tg = jnp.where(lane == k, es[k] / tot, tg)
    ti_ref[0] = ti.astype(jnp.int32)
    tg_ref[0] = tg


def _merge(x, y_s5, hc, ro, gates, modt, g2, wglu, bglu, wpw, bpw, wo, wout, wr, br, ctx_len):
    b, s_tot, d = x.shape
    nt = s_tot // ROW_TILE
    nctx = ctx_len // ROW_TILE
    seg = lambda j: jnp.where(j >= nctx, 1, 0)
    row = lambda w: pl.BlockSpec((1, ROW_TILE, w), lambda i, j: (i, j, 0))
    cst = lambda shape: pl.BlockSpec(shape, lambda i, j: (0,) * len(shape))
    y2 = y_s5
    return pl.pallas_call(
        _merge_kernel,
        grid=(b, nt),
        in_specs=[row(d),
                  pl.BlockSpec((1, 1, ROW_TILE, SSM_WIDTH), lambda i, j: (0, i, j, 0)),
                  pl.BlockSpec((1, 1, ROW_TILE, SSM_WIDTH), lambda i, j: (1, i, j, 0)),
                  row(CONV_WIDTH), row(RET_WIDTH), row(3 * d),
                  pl.BlockSpec((1, 1, 8, d), lambda i, j: (i, seg(j), 0, 0)),
                  cst((1, d)),
                  cst((SSM_WIDTH, 2 * d)), cst((1, 2 * d)),
                  cst((CONV_WIDTH, d)), cst((1, d)),
                  cst((RET_WIDTH, d)), cst((d, d)),
                  cst((d, LOGIT_LANES)), cst((1, LOGIT_LANES))],
        out_specs=[row(d), row(d), row(LOGIT_LANES), row(LOGIT_LANES)],
        out_shape=[jax.ShapeDtypeStruct((b, s_tot, d), F32),
                   jax.ShapeDtypeStruct((b, s_tot, d), F32),
                   jax.ShapeDtypeStruct((b, s_tot, LOGIT_LANES), jnp.int32),
                   jax.ShapeDtypeStruct((b, s_tot, LOGIT_LANES), F32)],
        compiler_params=_cparams(("arbitrary", "arbitrary")),
        name="merge_router",
    )(x, y2, y2, hc, ro, gates, modt, g2.reshape(1, d), wglu, bglu.reshape(1, 2 * d),
      wpw, bpw.reshape(1, d), wo, wout, wr, br)


def _route(top_idx):
    n_tok = top_idx.shape[0]
    n_assign = n_tok * TOP_K
    n_tiles = n_assign // MOE_ROWS
    n_visits = n_tiles + N_EXPERTS - 1
    i32 = jnp.int32
    flat_e = top_idx.reshape(-1)
    iota = jnp.arange(n_assign, dtype=i32)
    _, order = lax.sort((flat_e, iota), num_keys=1, is_stable=True)
    _, inv = lax.sort((order, iota), num_keys=1)
    experts = jnp.arange(N_EXPERTS, dtype=i32)
    counts = jnp.sum((flat_e[:, None] == experts[None, :]).astype(i32), axis=0)
    ends = jnp.cumsum(counts)
    starts = ends - counts
    first_tile = starts // MOE_ROWS
    tiles_per = jnp.where(counts > 0, (ends - 1) // MOE_ROWS - first_tile + 1, 0)
    vend = jnp.cumsum(tiles_per)
    vstart = vend - tiles_per
    total = vend[-1]
    v = jnp.arange(n_visits, dtype=i32)
    valid = v < total
    ve = jnp.minimum(jnp.sum((v[:, None] >= vend[None, :]).astype(i32), axis=1), N_EXPERTS - 1)
    ve = jnp.where(valid, ve, ve[jnp.maximum(total - 1, 0)])
    vt = jnp.where(valid, first_tile[ve] + v - vstart[ve], n_tiles - 1)
    lo = jnp.where(valid, jnp.clip(starts[ve] - vt * MOE_ROWS, 0, MOE_ROWS), 0)
    hi = jnp.where(valid, jnp.clip(ends[ve] - vt * MOE_ROWS, 0, MOE_ROWS), 0)
    return order // TOP_K, inv, (vt.astype(i32), ve.astype(i32), lo.astype(i32), hi.astype(i32))


def _expert_kernel(vt_ref, ve_ref, lo_ref, hi_ref, x_ref, wgu_ref, bgu_ref, wd_ref, bd_ref, y_ref,
                   wgu_bf, wd_bf):
    v = pl.program_id(0)
    prev = jnp.maximum(v - 1, 0)
    lo = lo_ref[v]
    hi = hi_ref[v]
    active = hi > lo
    fresh_weights = jnp.logical_or(v == 0, ve_ref[v] != ve_ref[prev])
    first_visit = jnp.logical_or(v == 0, vt_ref[v] != vt_ref[prev])

    @pl.when(jnp.logical_and(fresh_weights, active))
    def _():
        wgu_bf[...] = wgu_ref[0, 0].astype(BF16)
        wd_bf[...] = wd_ref[0, 0].astype(BF16)

    @pl.when(active)
    def _():
        f = EXPERT_FF
        x = x_ref[...].astype(BF16)
        y = bd_ref[0, 0]
        for c in range(f // FF_SLICE):
            g_cols = pl.ds(c * FF_SLICE, FF_SLICE)
            u_cols = pl.ds(f + c * FF_SLICE, FF_SLICE)
            gate = jnp.dot(x, wgu_bf[:, g_cols], preferred_element_type=F32) + bgu_ref[0, 0, :, g_cols]
            up = jnp.dot(x, wgu_bf[:, u_cols], preferred_element_type=F32) + bgu_ref[0, 0, :, u_cols]
            gate = jnp.minimum(gate, SWIGLU_LIMIT)
            up = jnp.clip(up, -SWIGLU_LIMIT, SWIGLU_LIMIT)
            act = (up + 1.0) * gate * jax.nn.sigmoid(gate * SWIGLU_ALPHA)
            y = y + jnp.dot(act.astype(BF16), wd_bf[g_cols, :], preferred_element_type=F32)
        row = lax.broadcasted_iota(jnp.int32, (MOE_ROWS, 1), 0)
        mine = jnp.logical_and(row >= lo, row < hi)

        @pl.when(first_visit)
        def _():
            y_ref[...] = jnp.where(mine, y, 0.0).astype(y_ref.dtype)

        @pl.when(jnp.logical_not(first_visit))
        def _():
            y_ref[...] = jnp.where(mine, y.astype(y_ref.dtype), y_ref[...])


def _experts(xs, visits, w_gu, b_gu, w_down, b_down, layer):
    n_rows, d = xs.shape
    n_visits = visits[0].shape[0]
    f = EXPERT_FF
    wmap = lambda v, vt, ve, lo, hi: (layer, ve[v], 0, 0)
    tmap = lambda v, vt, ve, lo, hi: (vt[v], 0)
    grid_spec = pltpu.PrefetchScalarGridSpec(
        num_scalar_prefetch=4,
        grid=(n_visits,),
        in_specs=[pl.BlockSpec((MOE_ROWS, d), tmap),
                  pl.BlockSpec((1, 1, d, 2 * f), wmap),
                  pl.BlockSpec((1, 1, 1, 2 * f), wmap),
                  pl.BlockSpec((1, 1, f, d), wmap),
                  pl.BlockSpec((1, 1, 1, d), wmap)],
        out_specs=pl.BlockSpec((MOE_ROWS, d), tmap),
        scratch_shapes=[pltpu.VMEM((d, 2 * f), BF16), pltpu.VMEM((f, d), BF16)],
    )
    return pl.pallas_call(
        _expert_kernel,
        grid_spec=grid_spec,
        out_shape=jax.ShapeDtypeStruct((n_rows, d), BF16),
        compiler_params=_cparams(("arbitrary",)),
        name="experts",
    )(*visits, xs, w_gu, b_gu.reshape(DEPTH, N_EXPERTS, 1, 2 * f),
      w_down, b_down.reshape(DEPTH, N_EXPERTS, 1, d))


def _combine_kernel(x_ref, y0_ref, y1_ref, y2_ref, y3_ref, gt_ref, mod_ref, o_ref):
    mod = mod_ref[0, 0]
    gt = gt_ref[0]
    y = gt[:, 0:1] * y0_ref[0].astype(F32)
    for k, y_ref in ((1, y1_ref), (2, y2_ref), (3, y3_ref)):
        y = y + gt[:, k:k + 1] * y_ref[0].astype(F32)
    o_ref[0] = x_ref[0] + mod[5:6] * y


def _combine(x1, yg, gates, modt, ctx_len):
    b, s_tot, d = x1.shape
    nt = s_tot // ROW_TILE
    nctx = ctx_len // ROW_TILE
    seg = lambda j: jnp.where(j >= nctx, 1, 0)
    choice = lambda k: pl.BlockSpec((1, ROW_TILE, d), lambda i, j: (k, i * nt + j, 0))
    return pl.pallas_call(
        _combine_kernel,
        grid=(b, nt),
        in_specs=[pl.BlockSpec((1, ROW_TILE, d), lambda i, j: (i, j, 0)),
                  choice(0), choice(1), choice(2), choice(3),
                  pl.BlockSpec((1, ROW_TILE, LOGIT_LANES), lambda i, j: (i, j, 0)),
                  pl.BlockSpec((1, 1, 8, d), lambda i, j: (i, seg(j), 0, 0))],
        out_specs=pl.BlockSpec((1, ROW_TILE, d), lambda i, j: (i, j, 0)),
        out_shape=jax.ShapeDtypeStruct((b, s_tot, d), F32),
        compiler_params=_cparams(("arbitrary", "arbitrary")),
        name="moe_combine",
    )(x1, yg, yg, yg, yg, gates, modt)


def _moe(x1, h2, top_idx, top_gate, modt, exp_gu_w, exp_gu_b, exp_down_w, exp_down_b, layer, ctx_len):
    b, s_tot, d = x1.shape
    n_tok = b * s_tot
    rows, inv, visits = _route(top_idx.reshape(n_tok, LOGIT_LANES)[:, :TOP_K])
    xs = h2.reshape(n_tok, d).at[rows].get(mode="promise_in_bounds")
    ys = _experts(xs, visits, exp_gu_w, exp_gu_b, exp_down_w, exp_down_b, layer)
    by_choice = inv.reshape(n_tok, TOP_K).T.reshape(-1)
    yg = ys.at[by_choice].get(mode="promise_in_bounds").reshape(TOP_K, n_tok, d)
    return _combine(x1, yg, top_gate, modt, ctx_len)


def _final_kernel(x_ref, g_ref, o_ref):
    o_ref[0] = _rms(x_ref[0]) * g_ref[...]


def _final_norm(x, g, ctx_len):
    b, s_tot, d = x.shape
    seq = s_tot - ctx_len
    off = ctx_len // ROW_TILE
    return pl.pallas_call(
        _final_kernel,
        grid=(b, seq // ROW_TILE),
        in_specs=[pl.BlockSpec((1, ROW_TILE, d), lambda i, j: (i, j + off, 0)),
                  pl.BlockSpec((1, d), lambda i, j: (0, 0))],
        out_specs=pl.BlockSpec((1, ROW_TILE, d), lambda i, j: (i, j, 0)),
        out_shape=jax.ShapeDtypeStruct((b, seq, d), F32),
        compiler_params=_cparams(("arbitrary", "arbitrary")),
        name="final_norm",
    )(x, g.reshape(1, d))


def _mixer_layer(x, modt, i, ctx_len, tables, norm1_g, w_in_bf, s5p, ssm_d, conv_p, merge_p):
    b, s_tot, d = x.shape
    u2, cv, qkvg, gates = _in_proj(x, modt, norm1_g, w_in_bf, ctx_len)
    bmat, cmat, lvec = s5p
    y_s5 = _s5_scan(u2, bmat, cmat, lvec, ssm_d, ctx_len)
    hc = _conv_branch(cv, *conv_p, ctx_len)
    cos, sin, tabs = tables
    ro = _retention(qkvg, cos, sin, tabs, ctx_len)
    return _merge(x, y_s5, hc, ro, gates, modt, *merge_p, ctx_len)


def kernel(x, c, ctx, c_ctx, ada_w, ada_b, norm1_g, w_in, ssm_lam_re, ssm_lam_im, ssm_log_dt, ssm_b_re, ssm_b_im, ssm_c_re, ssm_c_im, ssm_d, ssm_glu_w, ssm_glu_b, conv_dw_w, conv_dw_b, conv_ln_g, conv_ln_b, conv_pw_w, conv_pw_b, ret_w_o, w_out, norm2_g, router_w, router_b, exp_gu_w, exp_gu_b, exp_down_w, exp_down_b, final_g):
    b, seq, d = x.shape
    ctx_len = ctx.shape[1]
    depth = ada_w.shape[0]
    xs = jnp.concatenate([ctx, x], axis=1)
    s_in = jnp.concatenate([c, c_ctx[None], jnp.zeros((16 - b - 1, d), F32)], axis=0)
    mod = _ada_mod(s_in, ada_w, ada_b).reshape(depth, 16, N_MOD, d)
    mod = jnp.pad(mod, ((0, 0), (0, 0), (0, 8 - N_MOD), (0, 0)))
    modt = jnp.stack([jnp.broadcast_to(mod[:, b][:, None], (depth, b, 8, d)), mod[:, :b]], axis=2)
    tables = _ret_tables(ctx_len, seq)
    wr = jnp.pad(router_w, ((0, 0), (0, 0), (0, LOGIT_LANES - N_EXPERTS)))
    br = jnp.pad(router_b, ((0, 0), (0, LOGIT_LANES - N_EXPERTS))).reshape(depth, 1, LOGIT_LANES)
    for i in range(depth):
        s5p = _s5_params(ssm_lam_re[i], ssm_lam_im[i], ssm_log_dt[i], ssm_b_re[i], ssm_b_im[i],
                         ssm_c_re[i], ssm_c_im[i])
        conv_p = (conv_dw_w[i], conv_dw_b[i], conv_ln_g[i], conv_ln_b[i])
        merge_p = (norm2_g[i], ssm_glu_w[i].astype(BF16), ssm_glu_b[i], conv_pw_w[i].astype(BF16),
                   conv_pw_b[i], ret_w_o[i].astype(BF16), w_out[i].astype(BF16), wr[i], br[i])
        x1, h2, top_idx, top_gate = _mixer_layer(xs, modt[i], i, ctx_len, tables, norm1_g[i],
                                                 w_in[i].astype(BF16), s5p, ssm_d[i], conv_p, merge_p)
        xs = _moe(x1, h2, top_idx, top_gate, modt[i], exp_gu_w, exp_gu_b, exp_down_w, exp_down_b,
                  i, ctx_len)
    return _final_norm(xs, final_g, ctx_len)
```

```python
import functools
import math

import numpy as np
import jax
import jax.numpy as jnp
from jax import lax
from jax.experimental import pallas as pl
from jax.experimental.pallas import tpu as pltpu

F32 = jnp.float32
BF16 = jnp.bfloat16

D_MODEL = 1024
DEPTH = 4
GRID_W = 64
N_MOD = 6
SSM_WIDTH = 256
SSM_GROUP = 16
SSM_GROUPS = SSM_WIDTH // SSM_GROUP
SSM_STATE = 64
SSM_LANES = SSM_GROUPS * SSM_STATE
CONV_WIDTH = 256
CONV_K = 31
CONV_PAD = 16
RET_HEADS = 4
RET_HEAD_DIM = 128
RET_WIDTH = RET_HEADS * RET_HEAD_DIM
RET_CHUNK = 128
ROPE_BASE = 10000.0
IN_WIDTH = SSM_WIDTH + 2 * CONV_WIDTH + 4 * RET_WIDTH + 3 * D_MODEL
COL_CV = SSM_WIDTH
COL_QKVG = COL_CV + 2 * CONV_WIDTH
COL_GATES = COL_QKVG + 4 * RET_WIDTH
N_EXPERTS = 32
TOP_K = 4
EXPERT_FF = D_MODEL
SWIGLU_LIMIT = 7.0
SWIGLU_ALPHA = 1.702
EPS = 1e-6

ROW_TILE = 256
HALF_TILE = 128
S5_STEPS = 64
S5_STRIP = 512
MOE_ROWS = 512
MOE_PART = 256
LOGIT_LANES = 128
VMEM_LIMIT = 56 * 1024 * 1024


def _cparams(sem):
    return pltpu.CompilerParams(dimension_semantics=sem, vmem_limit_bytes=VMEM_LIMIT)


def _rms(x):
    return x * lax.rsqrt(jnp.mean(x * x, axis=-1, keepdims=True) + EPS)


def _ada_kernel(s_ref, w_ref, b_ref, o_ref):
    s = s_ref[...]
    s = s * jax.nn.sigmoid(s)
    o_ref[0] = jnp.dot(s, w_ref[0], preferred_element_type=F32,
                       precision=lax.Precision.HIGHEST) + b_ref[0]


def _ada_mod(s_in, ada_w, ada_b):
    depth, d, n = ada_w.shape
    tn = 1536
    return pl.pallas_call(
        _ada_kernel,
        grid=(depth, n // tn),
        in_specs=[pl.BlockSpec((16, d), lambda i, j: (0, 0)),
                  pl.BlockSpec((1, d, tn), lambda i, j: (i, 0, j)),
                  pl.BlockSpec((1, 1, tn), lambda i, j: (i, 0, j))],
        out_specs=pl.BlockSpec((1, 16, tn), lambda i, j: (i, 0, j)),
        out_shape=jax.ShapeDtypeStruct((depth, 16, n), F32),
        compiler_params=_cparams(("arbitrary", "arbitrary")),
        name="ada_mod",
    )(s_in, ada_w, ada_b.reshape(depth, 1, n))


def _inproj_kernel(x_ref, mod_ref, g_ref, w_ref, u_ref, cv_ref, qkvg_ref, gates_ref):
    x = x_ref[0]
    mod = mod_ref[0, 0]
    h = (_rms(x) * g_ref[...]) * (1.0 + mod[1:2]) + mod[0:1]
    hb = h.astype(BF16)
    u_ref[0] = jnp.dot(hb, w_ref[:, 0:COL_CV], preferred_element_type=F32)
    cv_ref[0] = jnp.dot(hb, w_ref[:, COL_CV:COL_QKVG], preferred_element_type=F32)
    qkvg_ref[0] = jnp.dot(hb, w_ref[:, COL_QKVG:COL_GATES], preferred_element_type=F32).astype(BF16)
    gates = jnp.dot(hb, w_ref[:, COL_GATES:IN_WIDTH], preferred_element_type=F32)
    gates_ref[0] = jax.nn.sigmoid(gates).astype(BF16)


def _in_proj(x, modt, g1, w_in_bf, ctx_len):
    b, s_tot, d = x.shape
    nt = s_tot // ROW_TILE
    nctx = ctx_len // ROW_TILE
    seg = lambda j: jnp.where(j >= nctx, 1, 0)
    return pl.pallas_call(
        _inproj_kernel,
        grid=(b, nt),
        in_specs=[pl.BlockSpec((1, ROW_TILE, d), lambda i, j: (i, j, 0)),
                  pl.BlockSpec((1, 1, 8, d), lambda i, j: (i, seg(j), 0, 0)),
                  pl.BlockSpec((1, d), lambda i, j: (0, 0)),
                  pl.BlockSpec((d, IN_WIDTH), lambda i, j: (0, 0))],
        out_specs=[pl.BlockSpec((1, ROW_TILE, SSM_WIDTH), lambda i, j: (i, j, 0)),
                   pl.BlockSpec((1, ROW_TILE, 2 * CONV_WIDTH), lambda i, j: (i, j, 0)),
                   pl.BlockSpec((1, ROW_TILE, 4 * RET_WIDTH), lambda i, j: (i, j, 0)),
                   pl.BlockSpec((1, ROW_TILE, 3 * D_MODEL), lambda i, j: (i, j, 0))],
        out_shape=[jax.ShapeDtypeStruct((b, s_tot, SSM_WIDTH), F32),
                   jax.ShapeDtypeStruct((b, s_tot, 2 * CONV_WIDTH), F32),
                   jax.ShapeDtypeStruct((b, s_tot, 4 * RET_WIDTH), BF16),
                   jax.ShapeDtypeStruct((b, s_tot, 3 * D_MODEL), BF16)],
        compiler_params=_cparams(("arbitrary", "arbitrary")),
        name="in_proj",
    )(x, modt, g1.reshape(1, d), w_in_bf)


def _s5_params(lam_re, lam_im, log_dt, b_re, b_im, c_re, c_im):
    lam = lax.complex(jnp.minimum(lam_re.astype(F32), -1e-4), lam_im.astype(F32))
    lam_dt = lam * jnp.exp(log_dt.astype(F32))[..., None]
    lam_bar = jnp.exp(lam_dt)
    b = lax.complex(b_re.astype(F32), b_im.astype(F32))
    b_bar = ((lam_bar - 1.0) / lam)[..., None] * b
    eye = jnp.eye(SSM_GROUPS, dtype=F32)

    def in_mat(t):
        return jnp.einsum('dgpm,gh->dgmhp', t, eye).reshape(2, SSM_WIDTH, SSM_LANES)

    def out_mat(t):
        return jnp.einsum('dgmp,gh->dgphm', t, eye).reshape(2, SSM_LANES, SSM_WIDTH)

    bmat = jnp.concatenate([in_mat(jnp.real(b_bar)), in_mat(jnp.imag(b_bar))], axis=-1)
    cmat = jnp.concatenate([out_mat(c_re.astype(F32)), -out_mat(c_im.astype(F32))], axis=1)
    lvec = jnp.concatenate([jnp.real(lam_bar).reshape(2, 1, SSM_LANES),
                            jnp.imag(lam_bar).reshape(2, 1, SSM_LANES)], axis=-1)
    return bmat.astype(BF16), cmat.astype(BF16), lvec


def _s5_kernel(u_ref, perm_ref, permt_ref, bmat_ref, cmat_ref, lam_ref, dsk_ref, y_ref, st_ref, carry_ref):
    d = pl.program_id(0)
    c = pl.program_id(1)
    nb, steps, width = u_ref.shape

    @pl.when(c == 0)
    def _():
        carry_ref[...] = jnp.zeros_like(carry_ref)

    u = u_ref[...].reshape(nb * steps, width)
    u_tm = jnp.dot(perm_ref[...], u.astype(BF16), preferred_element_type=F32).astype(BF16)
    bu = jnp.dot(u_tm, bmat_ref[0], preferred_element_type=F32)
    st_ref[...] = bu.reshape(steps, nb, 2 * SSM_LANES)

    for k in range(SSM_LANES // S5_STRIP):
        re = pl.ds(k * S5_STRIP, S5_STRIP)
        im = pl.ds(SSM_LANES + k * S5_STRIP, S5_STRIP)
        lr = jnp.broadcast_to(lam_ref[0, :, re], (nb, S5_STRIP))
        li = jnp.broadcast_to(lam_ref[0, :, im], (nb, S5_STRIP))

        def step(t, s):
            sr, si = s
            tt = jnp.where(d == 1, steps - 1 - t, t)
            nr = lr * sr - li * si + st_ref[tt, :, re]
            ni = lr * si + li * sr + st_ref[tt, :, im]
            st_ref[tt, :, re] = nr
            st_ref[tt, :, im] = ni
            return nr, ni

        sr, si = lax.fori_loop(0, steps, step, (carry_ref[:, re], carry_ref[:, im]), unroll=2)
        carry_ref[:, re] = sr
        carry_ref[:, im] = si

    st = st_ref[...].reshape(steps * nb, 2 * SSM_LANES)
    y_tm = jnp.dot(st.astype(BF16), cmat_ref[0], preferred_element_type=F32)
    y = jnp.dot(permt_ref[...], y_tm.astype(BF16), preferred_element_type=F32)
    skip = jnp.where(d == 0, 1.0, 0.0) * dsk_ref[...]
    y_ref[0] = (y + u * skip).reshape(nb, steps, width)


def _s5_scan(u, bmat, cmat, lvec, d_skip, ctx_len):
    nb, s_tot, width = u.shape
    nch = s_tot // S5_STEPS
    nctx = ctx_len // S5_STEPS
    rows = nb * S5_STEPS
    r = np.arange(rows)
    perm = np.zeros((rows, rows), np.float32)
    perm[r, (r % nb) * S5_STEPS + r // nb] = 1.0
    perm_tm = jnp.asarray(perm, BF16)
    perm_bm = jnp.asarray(perm.T, BF16)

    def chunk(d, c):
        back = jnp.where(c < nctx, nctx - 1 - c, nctx + nch - 1 - c)
        return jnp.where(d == 1, back, c)

    return pl.pallas_call(
        _s5_kernel,
        grid=(2, nch),
        in_specs=[pl.BlockSpec((nb, S5_STEPS, width), lambda d, c: (0, chunk(d, c), 0)),
                  pl.BlockSpec((rows, rows), lambda d, c: (0, 0)),
                  pl.BlockSpec((rows, rows), lambda d, c: (0, 0)),
                  pl.BlockSpec((1, width, 2 * SSM_LANES), lambda d, c: (d, 0, 0)),
                  pl.BlockSpec((1, 2 * SSM_LANES, width), lambda d, c: (d, 0, 0)),
                  pl.BlockSpec((1, 1, 2 * SSM_LANES), lambda d, c: (d, 0, 0)),
                  pl.BlockSpec((1, width), lambda d, c: (0, 0))],
        out_specs=pl.BlockSpec((1, nb, S5_STEPS, width), lambda d, c: (d, 0, chunk(d, c), 0)),
        out_shape=jax.ShapeDtypeStruct((2, nb, s_tot, width), F32),
        scratch_shapes=[pltpu.VMEM((S5_STEPS, nb, 2 * SSM_LANES), F32),
                        pltpu.VMEM((nb, 2 * SSM_LANES), F32)],
        compiler_params=_cparams(("arbitrary", "arbitrary")),
        name="s5_scan",
    )(u, perm_tm, perm_bm, bmat, cmat, lvec, d_skip.reshape(1, width))


def _conv_kernel(cv_ref, w_ref, b_ref, lg_ref, lb_ref, o_ref, hp_ref, win_ref, *, ctx_len):
    s_tot = cv_ref.shape[1]
    lat = s_tot - ctx_len
    rc = RET_CHUNK
    zeros = jnp.zeros((CONV_PAD, CONV_WIDTH), F32)
    a = cv_ref[0, :, 0:CONV_WIDTH]
    g = cv_ref[0, :, CONV_WIDTH:2 * CONV_WIDTH]
    h = a * jax.nn.sigmoid(g)
    hp_ref[0:CONV_PAD] = zeros
    hp_ref[CONV_PAD:CONV_PAD + ctx_len] = h[0:ctx_len]
    hp_ref[CONV_PAD + ctx_len:2 * CONV_PAD + ctx_len] = zeros
    hp_ref[2 * CONV_PAD + ctx_len:2 * CONV_PAD + s_tot] = h[ctx_len:s_tot]
    hp_ref[2 * CONV_PAD + s_tot:3 * CONV_PAD + s_tot] = zeros
    del lat

    def chunk(c, carry):
        r0 = pl.multiple_of(c * rc, rc)
        wbase = pl.multiple_of(r0 + jnp.where(r0 >= ctx_len, CONV_PAD, 0), 8)
        win_ref[...] = hp_ref[pl.ds(wbase, rc + 2 * CONV_PAD), :]
        acc = jnp.zeros((rc, CONV_WIDTH), F32) + b_ref[...]
        for k in range(CONV_K):
            off = k + CONV_PAD - CONV_K // 2
            acc = acc + w_ref[k:k + 1, :] * win_ref[off:off + rc, :]
        mu = jnp.mean(acc, axis=-1, keepdims=True)
        var = jnp.mean(jnp.square(acc - mu), axis=-1, keepdims=True)
        y = (acc - mu) * lax.rsqrt(var + EPS) * lg_ref[...] + lb_ref[...]
        o_ref[0, pl.ds(r0, rc), :] = y * jax.nn.sigmoid(y)
        return carry

    lax.fori_loop(0, s_tot // rc, chunk, 0)


def _conv_branch(cv, w_dw, b_dw, ln_g, ln_b, ctx_len):
    b, s_tot, _ = cv.shape
    w = jnp.concatenate([w_dw.reshape(CONV_K, CONV_WIDTH), jnp.zeros((1, CONV_WIDTH), F32)], axis=0)
    vec = lambda t: t.reshape(1, CONV_WIDTH)
    cst = lambda shape: pl.BlockSpec(shape, lambda i: (0,) * len(shape))
    return pl.pallas_call(
        functools.partial(_conv_kernel, ctx_len=ctx_len),
        grid=(b,),
        in_specs=[pl.BlockSpec((1, s_tot, 2 * CONV_WIDTH), lambda i: (i, 0, 0)),
                  cst((CONV_K + 1, CONV_WIDTH)), cst((1, CONV_WIDTH)),
                  cst((1, CONV_WIDTH)), cst((1, CONV_WIDTH))],
        out_specs=pl.BlockSpec((1, s_tot, CONV_WIDTH), lambda i: (i, 0, 0)),
        out_shape=jax.ShapeDtypeStruct((b, s_tot, CONV_WIDTH), F32),
        scratch_shapes=[pltpu.VMEM((s_tot + 3 * CONV_PAD, CONV_WIDTH), F32),
                        pltpu.VMEM((RET_CHUNK + 2 * CONV_PAD, CONV_WIDTH), F32)],
        compiler_params=_cparams(("arbitrary",)),
        name="conv_branch",
    )(cv, w, vec(b_dw), vec(ln_g), vec(ln_b))


N_TAB = 5


def _ret_tables(ctx_len, seq):
    n = RET_HEAD_DIM // 4
    inv_freq = ROPE_BASE ** (-np.arange(n, dtype=np.float64) / n)
    pos = np.arange(seq)
    ang_r = (pos // GRID_W)[:, None] * inv_freq
    ang_c = (pos % GRID_W)[:, None] * inv_freq
    cos = np.concatenate([np.cos(ang_r), np.cos(ang_r), np.cos(ang_c), np.cos(ang_c)], axis=-1)
    sin = np.concatenate([-np.sin(ang_r), np.sin(ang_r), -np.sin(ang_c), np.sin(ang_c)], axis=-1)
    cos = np.concatenate([np.ones((ctx_len, RET_HEAD_DIM)), cos], axis=0)
    sin = np.concatenate([np.zeros((ctx_len, RET_HEAD_DIM)), sin], axis=0)
    log_g_fwd = np.log1p(-np.exp2(-5.0 - np.arange(RET_HEADS, dtype=np.float64)))
    idx = np.arange(RET_CHUNK, dtype=np.float64)
    diff = idx[:, None] - idx[None, :]
    tabs = np.zeros((2, RET_HEADS, N_TAB, RET_CHUNK, RET_CHUNK))
    ones = np.ones((RET_CHUNK, RET_CHUNK))
    for d, log_g in enumerate((log_g_fwd, log_g_fwd[::-1])):
        for hd in range(RET_HEADS):
            lg = log_g[hd]
            if d == 0:
                mask = np.where(diff >= 0, np.exp(lg * np.maximum(diff, 0.0)), 0.0)
                dec = np.exp(lg * (idx + 1.0))
                zeta = np.exp(lg * (RET_CHUNK - 1 - idx))
            else:
                mask = np.where(diff <= 0, np.exp(lg * np.maximum(-diff, 0.0)), 0.0)
                dec = np.exp(lg * (RET_CHUNK - idx))
                zeta = np.exp(lg * idx)
            tabs[d, hd, 0] = mask
            tabs[d, hd, 1] = dec[:, None] * ones
            tabs[d, hd, 2] = zeta[:, None] * ones
            tabs[d, hd, 3] = np.exp(lg * RET_CHUNK) * ones
            tabs[d, hd, 4] = zeta[None, :] * ones
    return (jnp.asarray(cos, F32), jnp.asarray(sin, F32), jnp.asarray(tabs, F32))


def _ret_kernel(q_ref, k_ref, v_ref, g_ref, cos_ref, sin_ref, tab_ref, o_ref,
                qs_ref, qd_ref, kt_ref, kz_ref, out_ref, state_ref, *, ctx_len):
    s_tot = q_ref.shape[1]
    rc = RET_CHUNK
    nch = s_tot // rc
    nctx = ctx_len // rc
    k_scale = RET_HEAD_DIM ** -0.5
    lane = lax.broadcasted_iota(jnp.int32, (rc, RET_HEAD_DIM), 1)
    first_half = (lane % (RET_HEAD_DIM // 2)) < (RET_HEAD_DIM // 4)

    def rope(t, cs, sn):
        quarter = RET_HEAD_DIM // 4
        partner = jnp.where(first_half, pltpu.roll(t, RET_HEAD_DIM - quarter, 1), pltpu.roll(t, quarter, 1))
        return t * cs + partner * sn

    def prepare(c, carry):
        rows = pl.ds(pl.multiple_of(c * rc, rc), rc)
        cs = cos_ref[rows, :]
        sn = sin_ref[rows, :]
        q = rope(q_ref[0, rows, :].astype(F32), cs, sn)
        kt = (rope(k_ref[0, rows, :].astype(F32), cs, sn) * k_scale).T
        qs_ref[c] = q.astype(BF16)
        kt_ref[c] = kt.astype(BF16)
        for d in range(2):
            qd_ref[d, c] = (q * tab_ref[d, 0, 1]).astype(BF16)
            kz_ref[d, c] = (kt * tab_ref[d, 0, 4]).astype(BF16)
        return carry

    lax.fori_loop(0, nch, prepare, 0)
    state_ref[...] = jnp.zeros_like(state_ref)

    def chunk(c, d):
        rows = pl.ds(pl.multiple_of(c * rc, rc), rc)
        vb = v_ref[0, rows, :]
        state = state_ref[d]
        scores = jnp.dot(qs_ref[c], kt_ref[c], preferred_element_type=F32) * tab_ref[d, 0, 0]
        o = (jnp.dot(scores.astype(BF16), vb, preferred_element_type=F32)
             + jnp.dot(qd_ref[d, c], state.astype(BF16), preferred_element_type=F32))
        kv = jnp.dot(kz_ref[d, c], vb, preferred_element_type=F32)
        state_ref[d] = tab_ref[d, 0, 3] * state + kv
        out_ref[d, rows, :] = o

    def step(j, carry):
        chunk(j, 0)
        chunk(jnp.where(j < nctx, nctx - 1 - j, nctx + nch - 1 - j), 1)
        return carry

    lax.fori_loop(0, nch, step, 0)
    g = g_ref[0].astype(F32)
    o_ref[0] = (_rms(out_ref[0] + out_ref[1]) * (g * jax.nn.sigmoid(g))).astype(o_ref.dtype)


def _retention(qkvg, cos, sin, tabs, ctx_len):
    b, s_tot, _ = qkvg.shape
    hd = RET_HEAD_DIM
    nch = s_tot // RET_CHUNK
    part = lambda p: pl.BlockSpec((1, s_tot, hd), lambda i, h: (i, 0, p * RET_HEADS + h))
    return pl.pallas_call(
        functools.partial(_ret_kernel, ctx_len=ctx_len),
        grid=(b, RET_HEADS),
        in_specs=[part(0), part(1), part(2), part(3),
                  pl.BlockSpec((s_tot, hd), lambda i, h: (0, 0)),
                  pl.BlockSpec((s_tot, hd), lambda i, h: (0, 0)),
                  pl.BlockSpec((2, 1, N_TAB, RET_CHUNK, RET_CHUNK), lambda i, h: (0, h, 0, 0, 0))],
        out_specs=pl.BlockSpec((1, s_tot, hd), lambda i, h: (i, 0, h)),
        out_shape=jax.ShapeDtypeStruct((b, s_tot, RET_WIDTH), BF16),
        scratch_shapes=[pltpu.VMEM((nch, RET_CHUNK, hd), BF16),
                        pltpu.VMEM((2, nch, RET_CHUNK, hd), BF16),
                        pltpu.VMEM((nch, hd, RET_CHUNK), BF16),
                        pltpu.VMEM((2, nch, hd, RET_CHUNK), BF16),
                        pltpu.VMEM((2, s_tot, hd), F32),
                        pltpu.VMEM((2, hd, hd), F32)],
        compiler_params=_cparams(("arbitrary", "arbitrary")),
        name="retention",
    )(qkvg, qkvg, qkvg, qkvg, cos, sin, tabs)


def _merge_kernel(x_ref, yf_ref, yb_ref, hc_ref, ro_ref, gates_ref, mod_ref, g2_ref,
                  wglu_ref, bglu_ref, wpw_ref, bpw_ref, wo_ref, wout_ref, wr_ref, br_ref,
                  x1_ref, h2_ref, ti_ref, tg_ref):
    d = D_MODEL
    mod = mod_ref[0, 0]
    for half in range(ROW_TILE // HALF_TILE):
        rows = pl.ds(half * HALF_TILE, HALF_TILE)
        ys = jax.nn.gelu(yf_ref[0, 0, rows, :] + yb_ref[0, 0, rows, :]).astype(BF16)
        z = jnp.dot(ys, wglu_ref[...], preferred_element_type=F32) + bglu_ref[...]
        y_ssm = z[:, 0:d] * jax.nn.sigmoid(z[:, d:2 * d])
        y_conv = jnp.dot(hc_ref[0, rows, :].astype(BF16), wpw_ref[...],
                         preferred_element_type=F32) + bpw_ref[...]
        y_ret = jnp.dot(ro_ref[0, rows, :].astype(BF16), wo_ref[...], preferred_element_type=F32)
        m = (gates_ref[0, rows, 0:d].astype(F32) * y_ssm
             + gates_ref[0, rows, d:2 * d].astype(F32) * y_conv
             + gates_ref[0, rows, 2 * d:3 * d].astype(F32) * y_ret)
        y = jnp.dot(m.astype(BF16), wout_ref[...], preferred_element_type=F32)
        x1 = x_ref[0, rows, :] + mod[2:3] * y
        x1_ref[0, rows, :] = x1
        h2 = (_rms(x1) * g2_ref[...]) * (1.0 + mod[4:5]) + mod[3:4]
        h2_ref[0, rows, :] = h2
        h_hi = h2.astype(BF16)
        h_lo = (h2 - h_hi.astype(F32)).astype(BF16)
        logits = (jnp.dot(h_hi, wr_ref[0], preferred_element_type=F32)
                  + jnp.dot(h_lo, wr_ref[0], preferred_element_type=F32)
                  + jnp.dot(h_hi, wr_ref[1], preferred_element_type=F32)) + br_ref[...]
        lane = lax.broadcasted_iota(jnp.int32, logits.shape, 1)
        lane_f = lane.astype(F32)
        neg = jnp.float32(-jnp.inf)
        l = jnp.where(lane < N_EXPERTS, logits, neg)
        ti = jnp.zeros(logits.shape, F32)
        vals = []
        for k in range(TOP_K):
            top = jnp.max(l, axis=-1, keepdims=True)
            idx = jnp.min(jnp.where(l == top, lane_f, float(LOGIT_LANES)), axis=-1, keepdims=True)
            ti = jnp.where(lane == k, idx, ti)
            vals.append(top)
            l = jnp.where(lane_f == idx, neg, l)
        es = [jnp.exp(v - vals[0]) for v in vals]
        tot = es[0]
        for e in es[1:]:
            tot = tot + e
        tg = jnp.zeros(logits.shape, F32)
        for k in range(TOP_K):
            tg = jnp.where(lane == k, es[k] / tot, tg)
        ti_ref[0, rows, :] = ti.astype(jnp.int32)
        tg_ref[0, rows, :] = tg


def _merge(x, y_s5, hc, ro, gates, modt, g2, wglu, bglu, wpw, bpw, wo, wout, wr, br, ctx_len):
    b, s_tot, d = x.shape
    nt = s_tot // ROW_TILE
    nctx = ctx_len // ROW_TILE
    seg = lambda j: jnp.where(j >= nctx, 1, 0)
    row = lambda w: pl.BlockSpec((1, ROW_TILE, w), lambda i, j: (i, j, 0))
    cst = lambda shape: pl.BlockSpec(shape, lambda i, j: (0,) * len(shape))
    y2 = y_s5
    return pl.pallas_call(
        _merge_kernel,
        grid=(b, nt),
        in_specs=[row(d),
                  pl.BlockSpec((1, 1, ROW_TILE, SSM_WIDTH), lambda i, j: (0, i, j, 0)),
                  pl.BlockSpec((1, 1, ROW_TILE, SSM_WIDTH), lambda i, j: (1, i, j, 0)),
                  row(CONV_WIDTH), row(RET_WIDTH), row(3 * d),
                  pl.BlockSpec((1, 1, 8, d), lambda i, j: (i, seg(j), 0, 0)),
                  cst((1, d)),
                  cst((SSM_WIDTH, 2 * d)), cst((1, 2 * d)),
                  cst((CONV_WIDTH, d)), cst((1, d)),
                  cst((RET_WIDTH, d)), cst((d, d)),
                  cst((2, d, LOGIT_LANES)), cst((1, LOGIT_LANES))],
        out_specs=[row(d), row(d), row(LOGIT_LANES), row(LOGIT_LANES)],
        out_shape=[jax.ShapeDtypeStruct((b, s_tot, d), F32),
                   jax.ShapeDtypeStruct((b, s_tot, d), F32),
                   jax.ShapeDtypeStruct((b, s_tot, LOGIT_LANES), jnp.int32),
                   jax.ShapeDtypeStruct((b, s_tot, LOGIT_LANES), F32)],
        compiler_params=_cparams(("arbitrary", "arbitrary")),
        name="merge_router",
    )(x, y2, y2, hc, ro, gates, modt, g2.reshape(1, d), wglu, bglu.reshape(1, 2 * d),
      wpw, bpw.reshape(1, d), wo, wout, wr, br)


def _route(top_idx):
    n_tok = top_idx.shape[0]
    n_assign = n_tok * TOP_K
    n_tiles = n_assign // MOE_ROWS
    n_visits = n_tiles + N_EXPERTS - 1
    i32 = jnp.int32
    flat_e = top_idx.reshape(-1)
    iota = jnp.arange(n_assign, dtype=i32)
    _, order = lax.sort((flat_e, iota), num_keys=1, is_stable=True)
    _, inv = lax.sort((order, iota), num_keys=1)
    experts = jnp.arange(N_EXPERTS, dtype=i32)
    counts = jnp.sum((flat_e[:, None] == experts[None, :]).astype(i32), axis=0)
    ends = jnp.cumsum(counts)
    starts = ends - counts
    first_tile = starts // MOE_ROWS
    tiles_per = jnp.where(counts > 0, (ends - 1) // MOE_ROWS - first_tile + 1, 0)
    vend = jnp.cumsum(tiles_per)
    vstart = vend - tiles_per
    total = vend[-1]
    v = jnp.arange(n_visits, dtype=i32)
    valid = v < total
    ve = jnp.minimum(jnp.sum((v[:, None] >= vend[None, :]).astype(i32), axis=1), N_EXPERTS - 1)
    ve = jnp.where(valid, ve, ve[jnp.maximum(total - 1, 0)])
    vt = jnp.where(valid, first_tile[ve] + v - vstart[ve], n_tiles - 1)
    lo = jnp.where(valid, jnp.clip(starts[ve] - vt * MOE_ROWS, 0, MOE_ROWS), 0)
    hi = jnp.where(valid, jnp.clip(ends[ve] - vt * MOE_ROWS, 0, MOE_ROWS), 0)
    return order // TOP_K, inv, (vt.astype(i32), ve.astype(i32), lo.astype(i32), hi.astype(i32))


def _expert_kernel(vt_ref, ve_ref, lo_ref, hi_ref, x_ref, wgu_ref, bgu_ref, wd_ref, bd_ref, y_ref,
                   wgu_bf, wd_bf):
    v = pl.program_id(0)
    prev = jnp.maximum(v - 1, 0)
    lo = lo_ref[v]
    hi = hi_ref[v]
    active = hi > lo
    fresh_weights = jnp.logical_or(v == 0, ve_ref[v] != ve_ref[prev])
    first_visit = jnp.logical_or(v == 0, vt_ref[v] != vt_ref[prev])

    @pl.when(jnp.logical_and(fresh_weights, active))
    def _():
        wgu_bf[...] = wgu_ref[0, 0].astype(BF16)
        wd_bf[...] = wd_ref[0, 0].astype(BF16)

    @pl.when(first_visit)
    def _():
        y_ref[...] = jnp.zeros_like(y_ref)

    @pl.when(active)
    def _():
        f = EXPERT_FF
        for part in range(MOE_ROWS // MOE_PART):
            rows = pl.ds(part * MOE_PART, MOE_PART)
            gu = jnp.dot(x_ref[rows, :].astype(BF16), wgu_bf[...],
                         preferred_element_type=F32) + bgu_ref[0, 0]
            gate = jnp.minimum(gu[:, 0:f], SWIGLU_LIMIT)
            up = jnp.clip(gu[:, f:2 * f], -SWIGLU_LIMIT, SWIGLU_LIMIT)
            act = (up + 1.0) * gate * jax.nn.sigmoid(gate * SWIGLU_ALPHA)
            y = jnp.dot(act.astype(BF16), wd_bf[...], preferred_element_type=F32) + bd_ref[0, 0]
            row = part * MOE_PART + lax.broadcasted_iota(jnp.int32, (MOE_PART, 1), 0)
            mine = jnp.logical_and(row >= lo, row < hi)
            y_ref[rows, :] = jnp.where(mine, y.astype(y_ref.dtype), y_ref[rows, :])


def _experts(xs, visits, w_gu, b_gu, w_down, b_down, layer):
    n_rows, d = xs.shape
    n_visits = visits[0].shape[0]
    f = EXPERT_FF
    wmap = lambda v, vt, ve, lo, hi: (layer, ve[v], 0, 0)
    tmap = lambda v, vt, ve, lo, hi: (vt[v], 0)
    grid_spec = pltpu.PrefetchScalarGridSpec(
        num_scalar_prefetch=4,
        grid=(n_visits,),
        in_specs=[pl.BlockSpec((MOE_ROWS, d), tmap),
                  pl.BlockSpec((1, 1, d, 2 * f), wmap),
                  pl.BlockSpec((1, 1, 1, 2 * f), wmap),
                  pl.BlockSpec((1, 1, f, d), wmap),
                  pl.BlockSpec((1, 1, 1, d), wmap)],
        out_specs=pl.BlockSpec((MOE_ROWS, d), tmap),
        scratch_shapes=[pltpu.VMEM((d, 2 * f), BF16), pltpu.VMEM((f, d), BF16)],
    )
    return pl.pallas_call(
        _expert_kernel,
        grid_spec=grid_spec,
        out_shape=jax.ShapeDtypeStruct((n_rows, d), BF16),
        compiler_params=_cparams(("arbitrary",)),
        name="experts",
    )(*visits, xs, w_gu, b_gu.reshape(DEPTH, N_EXPERTS, 1, 2 * f),
      w_down, b_down.reshape(DEPTH, N_EXPERTS, 1, d))


def _combine_kernel(x_ref, y0_ref, y1_ref, y2_ref, y3_ref, gt_ref, mod_ref, o_ref):
    mod = mod_ref[0, 0]
    gt = gt_ref[0]
    y = gt[:, 0:1] * y0_ref[0].astype(F32)
    for k, y_ref in ((1, y1_ref), (2, y2_ref), (3, y3_ref)):
        y = y + gt[:, k:k + 1] * y_ref[0].astype(F32)
    o_ref[0] = x_ref[0] + mod[5:6] * y


def _combine(x1, yg, gates, modt, ctx_len):
    b, s_tot, d = x1.shape
    nt = s_tot // ROW_TILE
    nctx = ctx_len // ROW_TILE
    seg = lambda j: jnp.where(j >= nctx, 1, 0)
    choice = lambda k: pl.BlockSpec((1, ROW_TILE, d), lambda i, j: (k, i * nt + j, 0))
    return pl.pallas_call(
        _combine_kernel,
        grid=(b, nt),
        in_specs=[pl.BlockSpec((1, ROW_TILE, d), lambda i, j: (i, j, 0)),
                  choice(0), choice(1), choice(2), choice(3),
                  pl.BlockSpec((1, ROW_TILE, LOGIT_LANES), lambda i, j: (i, j, 0)),
                  pl.BlockSpec((1, 1, 8, d), lambda i, j: (i, seg(j), 0, 0))],
        out_specs=pl.BlockSpec((1, ROW_TILE, d), lambda i, j: (i, j, 0)),
        out_shape=jax.ShapeDtypeStruct((b, s_tot, d), F32),
        compiler_params=_cparams(("arbitrary", "arbitrary")),
        name="moe_combine",
    )(x1, yg, yg, yg, yg, gates, modt)


def _moe(x1, h2, top_idx, top_gate, modt, exp_gu_w, exp_gu_b, exp_down_w, exp_down_b, layer, ctx_len):
    b, s_tot, d = x1.shape
    n_tok = b * s_tot
    rows, inv, visits = _route(top_idx.reshape(n_tok, LOGIT_LANES)[:, :TOP_K])
    xs = h2.reshape(n_tok, d).at[rows].get(mode="promise_in_bounds")
    ys = _experts(xs, visits, exp_gu_w, exp_gu_b, exp_down_w, exp_down_b, layer)
    by_choice = inv.reshape(n_tok, TOP_K).T.reshape(-1)
    yg = ys.at[by_choice].get(mode="promise_in_bounds").reshape(TOP_K, n_tok, d)
    return _combine(x1, yg, top_gate, modt, ctx_len)


def _final_kernel(x_ref, g_ref, o_ref):
    o_ref[0] = _rms(x_ref[0]) * g_ref[...]


def _final_norm(x, g, ctx_len):
    b, s_tot, d = x.shape
    seq = s_tot - ctx_len
    off = ctx_len // ROW_TILE
    return pl.pallas_call(
        _final_kernel,
        grid=(b, seq // ROW_TILE),
        in_specs=[pl.BlockSpec((1, ROW_TILE, d), lambda i, j: (i, j + off, 0)),
                  pl.BlockSpec((1, d), lambda i, j: (0, 0))],
        out_specs=pl.BlockSpec((1, ROW_TILE, d), lambda i, j: (i, j, 0)),
        out_shape=jax.ShapeDtypeStruct((b, seq, d), F32),
        compiler_params=_cparams(("arbitrary", "arbitrary")),
        name="final_norm",
    )(x, g.reshape(1, d))


def _mixer_layer(x, modt, i, ctx_len, tables, norm1_g, w_in_bf, s5p, ssm_d, conv_p, merge_p):
    b, s_tot, d = x.shape
    u2, cv, qkvg, gates = _in_proj(x, modt, norm1_g, w_in_bf, ctx_len)
    bmat, cmat, lvec = s5p
    y_s5 = _s5_scan(u2, bmat, cmat, lvec, ssm_d, ctx_len)
    hc = _conv_branch(cv, *conv_p, ctx_len)
    cos, sin, tabs = tables
    ro = _retention(qkvg, cos, sin, tabs, ctx_len)
    return _merge(x, y_s5, hc, ro, gates, modt, *merge_p, ctx_len)


def kernel(x, c, ctx, c_ctx, ada_w, ada_b, norm1_g, w_in, ssm_lam_re, ssm_lam_im, ssm_log_dt, ssm_b_re, ssm_b_im, ssm_c_re, ssm_c_im, ssm_d, ssm_glu_w, ssm_glu_b, conv_dw_w, conv_dw_b, conv_ln_g, conv_ln_b, conv_pw_w, conv_pw_b, ret_w_o, w_out, norm2_g, router_w, router_b, exp_gu_w, exp_gu_b, exp_down_w, exp_down_b, final_g):
    b, seq, d = x.shape
    ctx_len = ctx.shape[1]
    depth = ada_w.shape[0]
    xs = jnp.concatenate([ctx, x], axis=1)
    s_in = jnp.concatenate([c, c_ctx[None], jnp.zeros((16 - b - 1, d), F32)], axis=0)
    mod = _ada_mod(s_in, ada_w, ada_b).reshape(depth, 16, N_MOD, d)
    mod = jnp.pad(mod, ((0, 0), (0, 0), (0, 8 - N_MOD), (0, 0)))
    modt = jnp.stack([jnp.broadcast_to(mod[:, b][:, None], (depth, b, 8, d)), mod[:, :b]], axis=2)
    tables = _ret_tables(ctx_len, seq)
    wr = jnp.pad(router_w, ((0, 0), (0, 0), (0, LOGIT_LANES - N_EXPERTS)))
    wr_hi = wr.astype(BF16)
    wr = jnp.stack([wr_hi, (wr - wr_hi.astype(F32)).astype(BF16)], axis=1)
    br = jnp.pad(router_b, ((0, 0), (0, LOGIT_LANES - N_EXPERTS))).reshape(depth, 1, LOGIT_LANES)
    for i in range(depth):
        s5p = _s5_params(ssm_lam_re[i], ssm_lam_im[i], ssm_log_dt[i], ssm_b_re[i], ssm_b_im[i],
                         ssm_c_re[i], ssm_c_im[i])
        conv_p = (conv_dw_w[i], conv_dw_b[i], conv_ln_g[i], conv_ln_b[i])
        merge_p = (norm2_g[i], ssm_glu_w[i].astype(BF16), ssm_glu_b[i], conv_pw_w[i].astype(BF16),
                   conv_pw_b[i], ret_w_o[i].astype(BF16), w_out[i].astype(BF16), wr[i], br[i])
        x1, h2, top_idx, top_gate = _mixer_layer(xs, modt[i], i, ctx_len, tables, norm1_g[i],
                                                 w_in[i].astype(BF16), s5p, ssm_d[i], conv_p, merge_p)
        xs = _moe(x1, h2, top_idx, top_gate, modt[i], exp_gu_w, exp_gu_b, exp_down_w, exp_down_b,
                  i, ctx_len)
    return _final_norm(xs, final_g, ctx_len)
```

```python
import functools
import math

import numpy as np
import jax
import jax.numpy as jnp
from jax import lax
from jax.experimental import pallas as pl
from jax.experimental.pallas import tpu as pltpu

F32 = jnp.float32
BF16 = jnp.bfloat16

D_MODEL = 1024
DEPTH = 4
GRID_W = 64
N_MOD = 6
SSM_WIDTH = 256
SSM_GROUP = 16
SSM_GROUPS = SSM_WIDTH // SSM_GROUP
SSM_STATE = 64
SSM_LANES = SSM_GROUPS * SSM_STATE
CONV_WIDTH = 256
CONV_K = 31
CONV_PAD = 16
RET_HEADS = 4
RET_HEAD_DIM = 128
RET_WIDTH = RET_HEADS * RET_HEAD_DIM
RET_CHUNK = 128
ROPE_BASE = 10000.0
IN_WIDTH = SSM_WIDTH + 2 * CONV_WIDTH + 4 * RET_WIDTH + 3 * D_MODEL
COL_CV = SSM_WIDTH
COL_QKVG = COL_CV + 2 * CONV_WIDTH
COL_GATES = COL_QKVG + 4 * RET_WIDTH
N_EXPERTS = 32
TOP_K = 4
EXPERT_FF = D_MODEL
SWIGLU_LIMIT = 7.0
SWIGLU_ALPHA = 1.702
EPS = 1e-6

ROW_TILE = 256
HALF_TILE = 128
S5_STEPS = 64
S5_STRIP = 512
MOE_ROWS = 512
MOE_PART = 256
LOGIT_LANES = 128
VMEM_LIMIT = 56 * 1024 * 1024


def _cparams(sem):
    return pltpu.CompilerParams(dimension_semantics=sem, vmem_limit_bytes=VMEM_LIMIT)


def _rms(x):
    return x * lax.rsqrt(jnp.mean(x * x, axis=-1, keepdims=True) + EPS)


def _ada_kernel(s_ref, w_ref, b_ref, o_ref):
    s = s_ref[...]
    s = s * jax.nn.sigmoid(s)
    o_ref[0] = jnp.dot(s, w_ref[0], preferred_element_type=F32,
                       precision=lax.Precision.HIGHEST) + b_ref[0]


def _ada_mod(s_in, ada_w, ada_b):
    depth, d, n = ada_w.shape
    tn = 1536
    return pl.pallas_call(
        _ada_kernel,
        grid=(depth, n // tn),
        in_specs=[pl.BlockSpec((16, d), lambda i, j: (0, 0)),
                  pl.BlockSpec((1, d, tn), lambda i, j: (i, 0, j)),
                  pl.BlockSpec((1, 1, tn), lambda i, j: (i, 0, j))],
        out_specs=pl.BlockSpec((1, 16, tn), lambda i, j: (i, 0, j)),
        out_shape=jax.ShapeDtypeStruct((depth, 16, n), F32),
        compiler_params=_cparams(("arbitrary", "arbitrary")),
        name="ada_mod",
    )(s_in, ada_w, ada_b.reshape(depth, 1, n))


def _inproj_kernel(x_ref, mod_ref, g_ref, w_ref, u_ref, cv_ref, qkvg_ref, gates_ref):
    x = x_ref[0]
    mod = mod_ref[0, 0]
    h = (_rms(x) * g_ref[...]) * (1.0 + mod[1:2]) + mod[0:1]
    hb = h.astype(BF16)
    u_ref[0] = jnp.dot(hb, w_ref[:, 0:COL_CV], preferred_element_type=F32)
    cv_ref[0] = jnp.dot(hb, w_ref[:, COL_CV:COL_QKVG], preferred_element_type=F32)
    qkvg_ref[0] = jnp.dot(hb, w_ref[:, COL_QKVG:COL_GATES], preferred_element_type=F32).astype(BF16)
    gates = jnp.dot(hb, w_ref[:, COL_GATES:IN_WIDTH], preferred_element_type=F32)
    gates_ref[0] = jax.nn.sigmoid(gates).astype(BF16)


def _in_proj(x, modt, g1, w_in_bf, ctx_len):
    b, s_tot, d = x.shape
    nt = s_tot // ROW_TILE
    nctx = ctx_len // ROW_TILE
    seg = lambda j: jnp.where(j >= nctx, 1, 0)
    return pl.pallas_call(
        _inproj_kernel,
        grid=(b, nt),
        in_specs=[pl.BlockSpec((1, ROW_TILE, d), lambda i, j: (i, j, 0)),
                  pl.BlockSpec((1, 1, 8, d), lambda i, j: (i, seg(j), 0, 0)),
                  pl.BlockSpec((1, d), lambda i, j: (0, 0)),
                  pl.BlockSpec((d, IN_WIDTH), lambda i, j: (0, 0))],
        out_specs=[pl.BlockSpec((1, ROW_TILE, SSM_WIDTH), lambda i, j: (i, j, 0)),
                   pl.BlockSpec((1, ROW_TILE, 2 * CONV_WIDTH), lambda i, j: (i, j, 0)),
                   pl.BlockSpec((1, ROW_TILE, 4 * RET_WIDTH), lambda i, j: (i, j, 0)),
                   pl.BlockSpec((1, ROW_TILE, 3 * D_MODEL), lambda i, j: (i, j, 0))],
        out_shape=[jax.ShapeDtypeStruct((b, s_tot, SSM_WIDTH), F32),
                   jax.ShapeDtypeStruct((b, s_tot, 2 * CONV_WIDTH), F32),
                   jax.ShapeDtypeStruct((b, s_tot, 4 * RET_WIDTH), BF16),
                   jax.ShapeDtypeStruct((b, s_tot, 3 * D_MODEL), BF16)],
        compiler_params=_cparams(("arbitrary", "arbitrary")),
        name="in_proj",
    )(x, modt, g1.reshape(1, d), w_in_bf)


def _s5_params(lam_re, lam_im, log_dt, b_re, b_im, c_re, c_im):
    lam = lax.complex(jnp.minimum(lam_re.astype(F32), -1e-4), lam_im.astype(F32))
    lam_dt = lam * jnp.exp(log_dt.astype(F32))[..., None]
    lam_bar = jnp.exp(lam_dt)
    b = lax.complex(b_re.astype(F32), b_im.astype(F32))
    b_bar = ((lam_bar - 1.0) / lam)[..., None] * b
    eye = jnp.eye(SSM_GROUPS, dtype=F32)

    def in_mat(t):
        return jnp.einsum('dgpm,gh->dgmhp', t, eye).reshape(2, SSM_WIDTH, SSM_LANES)

    def out_mat(t):
        return jnp.einsum('dgmp,gh->dgphm', t, eye).reshape(2, SSM_LANES, SSM_WIDTH)

    bmat = jnp.concatenate([in_mat(jnp.real(b_bar)), in_mat(jnp.imag(b_bar))], axis=-1)
    cmat = jnp.concatenate([out_mat(c_re.astype(F32)), -out_mat(c_im.astype(F32))], axis=1)
    lvec = jnp.concatenate([jnp.real(lam_bar).reshape(2, 1, SSM_LANES),
                            jnp.imag(lam_bar).reshape(2, 1, SSM_LANES)], axis=-1)
    return bmat.astype(BF16), cmat.astype(BF16), lvec


def _s5_kernel(u_ref, perm_ref, permt_ref, bmat_ref, cmat_ref, lam_ref, dsk_ref, y_ref, st_ref, carry_ref):
    d = pl.program_id(0)
    c = pl.program_id(1)
    nb, steps, width = u_ref.shape

    @pl.when(c == 0)
    def _():
        carry_ref[...] = jnp.zeros_like(carry_ref)

    u = u_ref[...].reshape(nb * steps, width)
    u_tm = jnp.dot(perm_ref[...], u.astype(BF16), preferred_element_type=F32).astype(BF16)
    bu = jnp.dot(u_tm, bmat_ref[0], preferred_element_type=F32)
    st_ref[...] = bu.reshape(steps, nb, 2 * SSM_LANES)

    for k in range(SSM_LANES // S5_STRIP):
        re = pl.ds(k * S5_STRIP, S5_STRIP)
        im = pl.ds(SSM_LANES + k * S5_STRIP, S5_STRIP)
        lr = jnp.broadcast_to(lam_ref[0, :, re], (nb, S5_STRIP))
        li = jnp.broadcast_to(lam_ref[0, :, im], (nb, S5_STRIP))

        def step(t, s):
            sr, si = s
            tt = jnp.where(d == 1, steps - 1 - t, t)
            nr = lr * sr - li * si + st_ref[tt, :, re]
            ni = lr * si + li * sr + st_ref[tt, :, im]
            st_ref[tt, :, re] = nr
            st_ref[tt, :, im] = ni
            return nr, ni

        sr, si = lax.fori_loop(0, steps, step, (carry_ref[:, re], carry_ref[:, im]), unroll=2)
        carry_ref[:, re] = sr
        carry_ref[:, im] = si

    st = st_ref[...].reshape(steps * nb, 2 * SSM_LANES)
    y_tm = jnp.dot(st.astype(BF16), cmat_ref[0], preferred_element_type=F32)
    y = jnp.dot(permt_ref[...], y_tm.astype(BF16), preferred_element_type=F32)
    skip = jnp.where(d == 0, 1.0, 0.0) * dsk_ref[...]
    y_ref[0] = (y + u * skip).reshape(nb, steps, width)


def _s5_scan(u, bmat, cmat, lvec, d_skip, ctx_len):
    nb, s_tot, width = u.shape
    nch = s_tot // S5_STEPS
    nctx = ctx_len // S5_STEPS
    rows = nb * S5_STEPS
    r = np.arange(rows)
    perm = np.zeros((rows, rows), np.float32)
    perm[r, (r % nb) * S5_STEPS + r // nb] = 1.0
    perm_tm = jnp.asarray(perm, BF16)
    perm_bm = jnp.asarray(perm.T, BF16)

    def chunk(d, c):
        back = jnp.where(c < nctx, nctx - 1 - c, nctx + nch - 1 - c)
        return jnp.where(d == 1, back, c)

    return pl.pallas_call(
        _s5_kernel,
        grid=(2, nch),
        in_specs=[pl.BlockSpec((nb, S5_STEPS, width), lambda d, c: (0, chunk(d, c), 0)),
                  pl.BlockSpec((rows, rows), lambda d, c: (0, 0)),
                  pl.BlockSpec((rows, rows), lambda d, c: (0, 0)),
                  pl.BlockSpec((1, width, 2 * SSM_LANES), lambda d, c: (d, 0, 0)),
                  pl.BlockSpec((1, 2 * SSM_LANES, width), lambda d, c: (d, 0, 0)),
                  pl.BlockSpec((1, 1, 2 * SSM_LANES), lambda d, c: (d, 0, 0)),
                  pl.BlockSpec((1, width), lambda d, c: (0, 0))],
        out_specs=pl.BlockSpec((1, nb, S5_STEPS, width), lambda d, c: (d, 0, chunk(d, c), 0)),
        out_shape=jax.ShapeDtypeStruct((2, nb, s_tot, width), F32),
        scratch_shapes=[pltpu.VMEM((S5_STEPS, nb, 2 * SSM_LANES), F32),
                        pltpu.VMEM((nb, 2 * SSM_LANES), F32)],
        compiler_params=_cparams(("arbitrary", "arbitrary")),
        name="s5_scan",
    )(u, perm_tm, perm_bm, bmat, cmat, lvec, d_skip.reshape(1, width))


def _conv_kernel(cv_ref, w_ref, b_ref, lg_ref, lb_ref, o_ref, hp_ref, win_ref, shift_ref, *, ctx_len):
    s_tot = cv_ref.shape[1]
    lat = s_tot - ctx_len
    rc = RET_CHUNK
    zeros = jnp.zeros((CONV_PAD, CONV_WIDTH), F32)
    a = cv_ref[0, :, 0:CONV_WIDTH]
    g = cv_ref[0, :, CONV_WIDTH:2 * CONV_WIDTH]
    h = a * jax.nn.sigmoid(g)
    hp_ref[0:CONV_PAD] = zeros
    hp_ref[CONV_PAD:CONV_PAD + ctx_len] = h[0:ctx_len]
    hp_ref[CONV_PAD + ctx_len:2 * CONV_PAD + ctx_len] = zeros
    hp_ref[2 * CONV_PAD + ctx_len:2 * CONV_PAD + s_tot] = h[ctx_len:s_tot]
    hp_ref[2 * CONV_PAD + s_tot:3 * CONV_PAD + s_tot] = zeros
    del lat

    def chunk(c, carry):
        r0 = pl.multiple_of(c * rc, rc)
        wbase = pl.multiple_of(r0 + jnp.where(r0 >= ctx_len, CONV_PAD, 0), 8)
        win_ref[...] = hp_ref[pl.ds(wbase, rc + 2 * CONV_PAD), :]
        for r in range(1, 8):
            shift_ref[r] = win_ref[r:r + rc + 2 * CONV_PAD - 8, :]
        acc = jnp.zeros((rc, CONV_WIDTH), F32) + b_ref[...]
        for k in range(CONV_K):
            off = k + CONV_PAD - CONV_K // 2
            base = off - off % 8
            if off % 8 == 0:
                tap = win_ref[base:base + rc, :]
            else:
                tap = shift_ref[off % 8, base:base + rc, :]
            acc = acc + w_ref[k:k + 1, :] * tap
        mu = jnp.mean(acc, axis=-1, keepdims=True)
        var = jnp.mean(jnp.square(acc - mu), axis=-1, keepdims=True)
        y = (acc - mu) * lax.rsqrt(var + EPS) * lg_ref[...] + lb_ref[...]
        o_ref[0, pl.ds(r0, rc), :] = y * jax.nn.sigmoid(y)
        return carry

    lax.fori_loop(0, s_tot // rc, chunk, 0)


def _conv_branch(cv, w_dw, b_dw, ln_g, ln_b, ctx_len):
    b, s_tot, _ = cv.shape
    w = jnp.concatenate([w_dw.reshape(CONV_K, CONV_WIDTH), jnp.zeros((1, CONV_WIDTH), F32)], axis=0)
    vec = lambda t: t.reshape(1, CONV_WIDTH)
    cst = lambda shape: pl.BlockSpec(shape, lambda i: (0,) * len(shape))
    return pl.pallas_call(
        functools.partial(_conv_kernel, ctx_len=ctx_len),
        grid=(b,),
        in_specs=[pl.BlockSpec((1, s_tot, 2 * CONV_WIDTH), lambda i: (i, 0, 0)),
                  cst((CONV_K + 1, CONV_WIDTH)), cst((1, CONV_WIDTH)),
                  cst((1, CONV_WIDTH)), cst((1, CONV_WIDTH))],
        out_specs=pl.BlockSpec((1, s_tot, CONV_WIDTH), lambda i: (i, 0, 0)),
        out_shape=jax.ShapeDtypeStruct((b, s_tot, CONV_WIDTH), F32),
        scratch_shapes=[pltpu.VMEM((s_tot + 3 * CONV_PAD, CONV_WIDTH), F32),
                        pltpu.VMEM((RET_CHUNK + 2 * CONV_PAD, CONV_WIDTH), F32),
                        pltpu.VMEM((8, RET_CHUNK + 2 * CONV_PAD - 8, CONV_WIDTH), F32)],
        compiler_params=_cparams(("arbitrary",)),
        name="conv_branch",
    )(cv, w, vec(b_dw), vec(ln_g), vec(ln_b))


N_TAB = 5


def _ret_tables(ctx_len, seq):
    n = RET_HEAD_DIM // 4
    inv_freq = ROPE_BASE ** (-np.arange(n, dtype=np.float64) / n)
    pos = np.arange(seq)
    ang_r = (pos // GRID_W)[:, None] * inv_freq
    ang_c = (pos % GRID_W)[:, None] * inv_freq
    cos = np.concatenate([np.cos(ang_r), np.cos(ang_r), np.cos(ang_c), np.cos(ang_c)], axis=-1)
    sin = np.concatenate([-np.sin(ang_r), np.sin(ang_r), -np.sin(ang_c), np.sin(ang_c)], axis=-1)
    cos = np.concatenate([np.ones((ctx_len, RET_HEAD_DIM)), cos], axis=0)
    sin = np.concatenate([np.zeros((ctx_len, RET_HEAD_DIM)), sin], axis=0)
    log_g_fwd = np.log1p(-np.exp2(-5.0 - np.arange(RET_HEADS, dtype=np.float64)))
    idx = np.arange(RET_CHUNK, dtype=np.float64)
    diff = idx[:, None] - idx[None, :]
    tabs = np.zeros((2, RET_HEADS, N_TAB, RET_CHUNK, RET_CHUNK))
    ones = np.ones((RET_CHUNK, RET_CHUNK))
    for d, log_g in enumerate((log_g_fwd, log_g_fwd[::-1])):
        for hd in range(RET_HEADS):
            lg = log_g[hd]
            if d == 0:
                mask = np.where(diff >= 0, np.exp(lg * np.maximum(diff, 0.0)), 0.0)
                dec = np.exp(lg * (idx + 1.0))
                zeta = np.exp(lg * (RET_CHUNK - 1 - idx))
            else:
                mask = np.where(diff <= 0, np.exp(lg * np.maximum(-diff, 0.0)), 0.0)
                dec = np.exp(lg * (RET_CHUNK - idx))
                zeta = np.exp(lg * idx)
            tabs[d, hd, 0] = mask
            tabs[d, hd, 1] = dec[:, None] * ones
            tabs[d, hd, 2] = zeta[:, None] * ones
            tabs[d, hd, 3] = np.exp(lg * RET_CHUNK) * ones
            tabs[d, hd, 4] = zeta[None, :] * ones
    return (jnp.asarray(cos, F32), jnp.asarray(sin, F32), jnp.asarray(tabs, F32))


def _ret_kernel(q_ref, k_ref, v_ref, g_ref, cos_ref, sin_ref, tab_ref, o_ref,
                qs_ref, qd_ref, kt_ref, kz_ref, out_ref, state_ref, *, ctx_len):
    s_tot = q_ref.shape[1]
    rc = RET_CHUNK
    nch = s_tot // rc
    nctx = ctx_len // rc
    k_scale = RET_HEAD_DIM ** -0.5
    lane = lax.broadcasted_iota(jnp.int32, (rc, RET_HEAD_DIM), 1)
    first_half = (lane % (RET_HEAD_DIM // 2)) < (RET_HEAD_DIM // 4)

    def rope(t, cs, sn):
        quarter = RET_HEAD_DIM // 4
        partner = jnp.where(first_half, pltpu.roll(t, RET_HEAD_DIM - quarter, 1), pltpu.roll(t, quarter, 1))
        return t * cs + partner * sn

    def prepare(c, carry):
        rows = pl.ds(pl.multiple_of(c * rc, rc), rc)
        cs = cos_ref[rows, :]
        sn = sin_ref[rows, :]
        q = rope(q_ref[0, rows, :].astype(F32), cs, sn)
        kt = (rope(k_ref[0, rows, :].astype(F32), cs, sn) * k_scale).T
        qs_ref[c] = q.astype(BF16)
        kt_ref[c] = kt.astype(BF16)
        for d in range(2):
            qd_ref[d, c] = (q * tab_ref[d, 0, 1]).astype(BF16)
            kz_ref[d, c] = (kt * tab_ref[d, 0, 4]).astype(BF16)
        return carry

    lax.fori_loop(0, nch, prepare, 0, unroll=2)
    state_ref[...] = jnp.zeros_like(state_ref)

    def chunk(c, d):
        rows = pl.ds(pl.multiple_of(c * rc, rc), rc)
        vb = v_ref[0, rows, :]
        state = state_ref[d]
        scores = jnp.dot(qs_ref[c], kt_ref[c], preferred_element_type=F32) * tab_ref[d, 0, 0]
        o = (jnp.dot(scores.astype(BF16), vb, preferred_element_type=F32)
             + jnp.dot(qd_ref[d, c], state.astype(BF16), preferred_element_type=F32))
        kv = jnp.dot(kz_ref[d, c], vb, preferred_element_type=F32)
        state_ref[d] = tab_ref[d, 0, 3] * state + kv
        out_ref[d, rows, :] = o

    def step(j, carry):
        chunk(j, 0)
        chunk(jnp.where(j < nctx, nctx - 1 - j, nctx + nch - 1 - j), 1)
        return carry

    lax.fori_loop(0, nch, step, 0, unroll=2)
    g = g_ref[0].astype(F32)
    o_ref[0] = (_rms(out_ref[0] + out_ref[1]) * (g * jax.nn.sigmoid(g))).astype(o_ref.dtype)


def _retention(qkvg, cos, sin, tabs, ctx_len):
    b, s_tot, _ = qkvg.shape
    hd = RET_HEAD_DIM
    nch = s_tot // RET_CHUNK
    part = lambda p: pl.BlockSpec((1, s_tot, hd), lambda i, h: (i, 0, p * RET_HEADS + h))
    return pl.pallas_call(
        functools.partial(_ret_kernel, ctx_len=ctx_len),
        grid=(b, RET_HEADS),
        in_specs=[part(0), part(1), part(2), part(3),
                  pl.BlockSpec((s_tot, hd), lambda i, h: (0, 0)),
                  pl.BlockSpec((s_tot, hd), lambda i, h: (0, 0)),
                  pl.BlockSpec((2, 1, N_TAB, RET_CHUNK, RET_CHUNK), lambda i, h: (0, h, 0, 0, 0))],
        out_specs=pl.BlockSpec((1, s_tot, hd), lambda i, h: (i, 0, h)),
        out_shape=jax.ShapeDtypeStruct((b, s_tot, RET_WIDTH), BF16),
        scratch_shapes=[pltpu.VMEM((nch, RET_CHUNK, hd), BF16),
                        pltpu.VMEM((2, nch, RET_CHUNK, hd), BF16),
                        pltpu.VMEM((nch, hd, RET_CHUNK), BF16),
                        pltpu.VMEM((2, nch, hd, RET_CHUNK), BF16),
                        pltpu.VMEM((2, s_tot, hd), F32),
                        pltpu.VMEM((2, hd, hd), F32)],
        compiler_params=_cparams(("arbitrary", "arbitrary")),
        name="retention",
    )(qkvg, qkvg, qkvg, qkvg, cos, sin, tabs)


def _merge_kernel(x_ref, yf_ref, yb_ref, hc_ref, ro_ref, gates_ref, mod_ref, g2_ref,
                  wglu_ref, bglu_ref, wpw_ref, bpw_ref, wo_ref, wout_ref, wr_ref, br_ref,
                  x1_ref, h2_ref, ti_ref, tg_ref):
    d = D_MODEL
    mod = mod_ref[0, 0]
    for half in range(ROW_TILE // HALF_TILE):
        rows = pl.ds(half * HALF_TILE, HALF_TILE)
        ys = jax.nn.gelu(yf_ref[0, 0, rows, :] + yb_ref[0, 0, rows, :]).astype(BF16)
        z = jnp.dot(ys, wglu_ref[...], preferred_element_type=F32) + bglu_ref[...]
        y_ssm = z[:, 0:d] * jax.nn.sigmoid(z[:, d:2 * d])
        y_conv = jnp.dot(hc_ref[0, rows, :].astype(BF16), wpw_ref[...],
                         preferred_element_type=F32) + bpw_ref[...]
        y_ret = jnp.dot(ro_ref[0, rows, :].astype(BF16), wo_ref[...], preferred_element_type=F32)
        m = (gates_ref[0, rows, 0:d].astype(F32) * y_ssm
             + gates_ref[0, rows, d:2 * d].astype(F32) * y_conv
             + gates_ref[0, rows, 2 * d:3 * d].astype(F32) * y_ret)
        y = jnp.dot(m.astype(BF16), wout_ref[...], preferred_element_type=F32)
        x1 = x_ref[0, rows, :] + mod[2:3] * y
        x1_ref[0, rows, :] = x1
        h2 = (_rms(x1) * g2_ref[...]) * (1.0 + mod[4:5]) + mod[3:4]
        h2_ref[0, rows, :] = h2
        h_hi = h2.astype(BF16)
        h_lo = (h2 - h_hi.astype(F32)).astype(BF16)
        logits = (jnp.dot(h_hi, wr_ref[0], preferred_element_type=F32)
                  + jnp.dot(h_lo, wr_ref[0], preferred_element_type=F32)
                  + jnp.dot(h_hi, wr_ref[1], preferred_element_type=F32)) + br_ref[...]
        lane = lax.broadcasted_iota(jnp.int32, logits.shape, 1)
        lane_f = lane.astype(F32)
        neg = jnp.float32(-jnp.inf)
        l = jnp.where(lane < N_EXPERTS, logits, neg)
        ti = jnp.zeros(logits.shape, F32)
        vals = []
        for k in range(TOP_K):
            top = jnp.max(l, axis=-1, keepdims=True)
            idx = jnp.min(jnp.where(l == top, lane_f, float(LOGIT_LANES)), axis=-1, keepdims=True)
            ti = jnp.where(lane == k, idx, ti)
            vals.append(top)
            l = jnp.where(lane_f == idx, neg, l)
        es = [jnp.exp(v - vals[0]) for v in vals]
        tot = es[0]
        for e in es[1:]:
            tot = tot + e
        tg = jnp.zeros(logits.shape, F32)
        for k in range(TOP_K):
            tg = jnp.where(lane == k, es[k] / tot, tg)
        ti_ref[0, rows, :] = ti.astype(jnp.int32)
        tg_ref[0, rows, :] = tg


def _merge(x, y_s5, hc, ro, gates, modt, g2, wglu, bglu, wpw, bpw, wo, wout, wr, br, ctx_len):
    b, s_tot, d = x.shape
    nt = s_tot // ROW_TILE
    nctx = ctx_len // ROW_TILE
    seg = lambda j: jnp.where(j >= nctx, 1, 0)
    row = lambda w: pl.BlockSpec((1, ROW_TILE, w), lambda i, j: (i, j, 0))
    cst = lambda shape: pl.BlockSpec(shape, lambda i, j: (0,) * len(shape))
    y2 = y_s5
    return pl.pallas_call(
        _merge_kernel,
        grid=(b, nt),
        in_specs=[row(d),
                  pl.BlockSpec((1, 1, ROW_TILE, SSM_WIDTH), lambda i, j: (0, i, j, 0)),
                  pl.BlockSpec((1, 1, ROW_TILE, SSM_WIDTH), lambda i, j: (1, i, j, 0)),
                  row(CONV_WIDTH), row(RET_WIDTH), row(3 * d),
                  pl.BlockSpec((1, 1, 8, d), lambda i, j: (i, seg(j), 0, 0)),
                  cst((1, d)),
                  cst((SSM_WIDTH, 2 * d)), cst((1, 2 * d)),
                  cst((CONV_WIDTH, d)), cst((1, d)),
                  cst((RET_WIDTH, d)), cst((d, d)),
                  cst((2, d, LOGIT_LANES)), cst((1, LOGIT_LANES))],
        out_specs=[row(d), row(d), row(LOGIT_LANES), row(LOGIT_LANES)],
        out_shape=[jax.ShapeDtypeStruct((b, s_tot, d), F32),
                   jax.ShapeDtypeStruct((b, s_tot, d), F32),
                   jax.ShapeDtypeStruct((b, s_tot, LOGIT_LANES), jnp.int32),
                   jax.ShapeDtypeStruct((b, s_tot, LOGIT_LANES), F32)],
        compiler_params=_cparams(("arbitrary", "arbitrary")),
        name="merge_router",
    )(x, y2, y2, hc, ro, gates, modt, g2.reshape(1, d), wglu, bglu.reshape(1, 2 * d),
      wpw, bpw.reshape(1, d), wo, wout, wr, br)


def _route(top_idx):
    n_tok = top_idx.shape[0]
    n_assign = n_tok * TOP_K
    n_tiles = n_assign // MOE_ROWS
    n_visits = n_tiles + N_EXPERTS - 1
    i32 = jnp.int32
    flat_e = top_idx.reshape(-1)
    iota = jnp.arange(n_assign, dtype=i32)
    _, order = lax.sort((flat_e, iota), num_keys=1, is_stable=True)
    _, inv = lax.sort((order, iota), num_keys=1)
    experts = jnp.arange(N_EXPERTS, dtype=i32)
    counts = jnp.sum((flat_e[:, None] == experts[None, :]).astype(i32), axis=0)
    ends = jnp.cumsum(counts)
    starts = ends - counts
    first_tile = starts // MOE_ROWS
    tiles_per = jnp.where(counts > 0, (ends - 1) // MOE_ROWS - first_tile + 1, 0)
    vend = jnp.cumsum(tiles_per)
    vstart = vend - tiles_per
    total = vend[-1]
    v = jnp.arange(n_visits, dtype=i32)
    valid = v < total
    ve = jnp.minimum(jnp.sum((v[:, None] >= vend[None, :]).astype(i32), axis=1), N_EXPERTS - 1)
    ve = jnp.where(valid, ve, ve[jnp.maximum(total - 1, 0)])
    vt = jnp.where(valid, first_tile[ve] + v - vstart[ve], n_tiles - 1)
    lo = jnp.where(valid, jnp.clip(starts[ve] - vt * MOE_ROWS, 0, MOE_ROWS), 0)
    hi = jnp.where(valid, jnp.clip(ends[ve] - vt * MOE_ROWS, 0, MOE_ROWS), 0)
    return order // TOP_K, inv, (vt.astype(i32), ve.astype(i32), lo.astype(i32), hi.astype(i32))


def _expert_kernel(vt_ref, ve_ref, lo_ref, hi_ref, x_ref, wgu_ref, bgu_ref, wd_ref, bd_ref, y_ref,
                   wgu_bf, wd_bf):
    v = pl.program_id(0)
    prev = jnp.maximum(v - 1, 0)
    lo = lo_ref[v]
    hi = hi_ref[v]
    active = hi > lo
    fresh_weights = jnp.logical_or(v == 0, ve_ref[v] != ve_ref[prev])
    first_visit = jnp.logical_or(v == 0, vt_ref[v] != vt_ref[prev])

    @pl.when(jnp.logical_and(fresh_weights, active))
    def _():
        wgu_bf[...] = wgu_ref[0, 0].astype(BF16)
        wd_bf[...] = wd_ref[0, 0].astype(BF16)

    whole = jnp.logical_and(lo == 0, hi == MOE_ROWS)

    def ffn(rows):
        f = EXPERT_FF
        gu = jnp.dot(x_ref[rows, :].astype(BF16), wgu_bf[...],
                     preferred_element_type=F32) + bgu_ref[0, 0]
        gate = jnp.minimum(gu[:, 0:f], SWIGLU_LIMIT)
        up = jnp.clip(gu[:, f:2 * f], -SWIGLU_LIMIT, SWIGLU_LIMIT)
        act = (up + 1.0) * gate * jax.nn.sigmoid(gate * SWIGLU_ALPHA)
        y = jnp.dot(act.astype(BF16), wd_bf[...], preferred_element_type=F32) + bd_ref[0, 0]
        return y.astype(y_ref.dtype)

    parts = [pl.ds(p * MOE_PART, MOE_PART) for p in range(MOE_ROWS // MOE_PART)]

    @pl.when(whole)
    def _():
        for rows in parts:
            y_ref[rows, :] = ffn(rows)

    @pl.when(jnp.logical_and(first_visit, jnp.logical_not(whole)))
    def _():
        y_ref[...] = jnp.zeros_like(y_ref)

    for p, rows in enumerate(parts):
        touched = jnp.logical_and(lo < (p + 1) * MOE_PART, hi > p * MOE_PART)

        @pl.when(jnp.logical_and(touched, jnp.logical_not(whole)))
        def _():
            row = p * MOE_PART + lax.broadcasted_iota(jnp.int32, (MOE_PART, 1), 0)
            mine = jnp.logical_and(row >= lo, row < hi)
            y_ref[rows, :] = jnp.where(mine, ffn(rows), y_ref[rows, :])


def _experts(xs, visits, w_gu, b_gu, w_down, b_down, layer):
    n_rows, d = xs.shape
    n_visits = visits[0].shape[0]
    f = EXPERT_FF
    wmap = lambda v, vt, ve, lo, hi: (layer, ve[v], 0, 0)
    tmap = lambda v, vt, ve, lo, hi: (vt[v], 0)
    grid_spec = pltpu.PrefetchScalarGridSpec(
        num_scalar_prefetch=4,
        grid=(n_visits,),
        in_specs=[pl.BlockSpec((MOE_ROWS, d), tmap),
                  pl.BlockSpec((1, 1, d, 2 * f), wmap),
                  pl.BlockSpec((1, 1, 1, 2 * f), wmap),
                  pl.BlockSpec((1, 1, f, d), wmap),
                  pl.BlockSpec((1, 1, 1, d), wmap)],
        out_specs=pl.BlockSpec((MOE_ROWS, d), tmap),
        scratch_shapes=[pltpu.VMEM((d, 2 * f), BF16), pltpu.VMEM((f, d), BF16)],
    )
    return pl.pallas_call(
        _expert_kernel,
        grid_spec=grid_spec,
        out_shape=jax.ShapeDtypeStruct((n_rows, d), BF16),
        compiler_params=_cparams(("arbitrary",)),
        name="experts",
    )(*visits, xs, w_gu, b_gu.reshape(DEPTH, N_EXPERTS, 1, 2 * f),
      w_down, b_down.reshape(DEPTH, N_EXPERTS, 1, d))


def _combine_kernel(x_ref, y0_ref, y1_ref, y2_ref, y3_ref, gt_ref, mod_ref, o_ref):
    mod = mod_ref[0, 0]
    gt = gt_ref[0]
    y = gt[:, 0:1] * y0_ref[0].astype(F32)
    for k, y_ref in ((1, y1_ref), (2, y2_ref), (3, y3_ref)):
        y = y + gt[:, k:k + 1] * y_ref[0].astype(F32)
    o_ref[0] = x_ref[0] + mod[5:6] * y


def _combine(x1, yg, gates, modt, ctx_len):
    b, s_tot, d = x1.shape
    nt = s_tot // ROW_TILE
    nctx = ctx_len // ROW_TILE
    seg = lambda j: jnp.where(j >= nctx, 1, 0)
    choice = lambda k: pl.BlockSpec((1, ROW_TILE, d), lambda i, j: (k, i * nt + j, 0))
    return pl.pallas_call(
        _combine_kernel,
        grid=(b, nt),
        in_specs=[pl.BlockSpec((1, ROW_TILE, d), lambda i, j: (i, j, 0)),
                  choice(0), choice(1), choice(2), choice(3),
                  pl.BlockSpec((1, ROW_TILE, LOGIT_LANES), lambda i, j: (i, j, 0)),
                  pl.BlockSpec((1, 1, 8, d), lambda i, j: (i, seg(j), 0, 0))],
        out_specs=pl.BlockSpec((1, ROW_TILE, d), lambda i, j: (i, j, 0)),
        out_shape=jax.ShapeDtypeStruct((b, s_tot, d), F32),
        compiler_params=_cparams(("arbitrary", "arbitrary")),
        name="moe_combine",
    )(x1, yg, yg, yg, yg, gates, modt)


def _moe(x1, h2, top_idx, top_gate, modt, exp_gu_w, exp_gu_b, exp_down_w, exp_down_b, layer, ctx_len):
    b, s_tot, d = x1.shape
    n_tok = b * s_tot
    rows, inv, visits = _route(top_idx.reshape(n_tok, LOGIT_LANES)[:, :TOP_K])
    xs = h2.reshape(n_tok, d).at[rows].get(mode="promise_in_bounds")
    ys = _experts(xs, visits, exp_gu_w, exp_gu_b, exp_down_w, exp_down_b, layer)
    by_choice = inv.reshape(n_tok, TOP_K).T.reshape(-1)
    yg = ys.at[by_choice].get(mode="promise_in_bounds").reshape(TOP_K, n_tok, d)
    return _combine(x1, yg, top_gate, modt, ctx_len)


def _final_kernel(x_ref, g_ref, o_ref):
    o_ref[0] = _rms(x_ref[0]) * g_ref[...]


def _final_norm(x, g, ctx_len):
    b, s_tot, d = x.shape
    seq = s_tot - ctx_len
    off = ctx_len // ROW_TILE
    return pl.pallas_call(
        _final_kernel,
        grid=(b, seq // ROW_TILE),
        in_specs=[pl.BlockSpec((1, ROW_TILE, d), lambda i, j: (i, j + off, 0)),
                  pl.BlockSpec((1, d), lambda i, j: (0, 0))],
        out_specs=pl.BlockSpec((1, ROW_TILE, d), lambda i, j: (i, j, 0)),
        out_shape=jax.ShapeDtypeStruct((b, seq, d), F32),
        compiler_params=_cparams(("arbitrary", "arbitrary")),
        name="final_norm",
    )(x, g.reshape(1, d))


def _mixer_layer(x, modt, i, ctx_len, tables, norm1_g, w_in_bf, s5p, ssm_d, conv_p, merge_p):
    b, s_tot, d = x.shape
    u2, cv, qkvg, gates = _in_proj(x, modt, norm1_g, w_in_bf, ctx_len)
    bmat, cmat, lvec = s5p
    y_s5 = _s5_scan(u2, bmat, cmat, lvec, ssm_d, ctx_len)
    hc = _conv_branch(cv, *conv_p, ctx_len)
    cos, sin, tabs = tables
    ro = _retention(qkvg, cos, sin, tabs, ctx_len)
    return _merge(x, y_s5, hc, ro, gates, modt, *merge_p, ctx_len)


def kernel(x, c, ctx, c_ctx, ada_w, ada_b, norm1_g, w_in, ssm_lam_re, ssm_lam_im, ssm_log_dt, ssm_b_re, ssm_b_im, ssm_c_re, ssm_c_im, ssm_d, ssm_glu_w, ssm_glu_b, conv_dw_w, conv_dw_b, conv_ln_g, conv_ln_b, conv_pw_w, conv_pw_b, ret_w_o, w_out, norm2_g, router_w, router_b, exp_gu_w, exp_gu_b, exp_down_w, exp_down_b, final_g):
    b, seq, d = x.shape
    ctx_len = ctx.shape[1]
    depth = ada_w.shape[0]
    xs = jnp.concatenate([ctx, x], axis=1)
    s_in = jnp.concatenate([c, c_ctx[None], jnp.zeros((16 - b - 1, d), F32)], axis=0)
    mod = _ada_mod(s_in, ada_w, ada_b).reshape(depth, 16, N_MOD, d)
    mod = jnp.pad(mod, ((0, 0), (0, 0), (0, 8 - N_MOD), (0, 0)))
    modt = jnp.stack([jnp.broadcast_to(mod[:, b][:, None], (depth, b, 8, d)), mod[:, :b]], axis=2)
    tables = _ret_tables(ctx_len, seq)
    wr = jnp.pad(router_w, ((0, 0), (0, 0), (0, LOGIT_LANES - N_EXPERTS)))
    wr_hi = wr.astype(BF16)
    wr = jnp.stack([wr_hi, (wr - wr_hi.astype(F32)).astype(BF16)], axis=1)
    br = jnp.pad(router_b, ((0, 0), (0, LOGIT_LANES - N_EXPERTS))).reshape(depth, 1, LOGIT_LANES)
    for i in range(depth):
        s5p = _s5_params(ssm_lam_re[i], ssm_lam_im[i], ssm_log_dt[i], ssm_b_re[i], ssm_b_im[i],
                         ssm_c_re[i], ssm_c_im[i])
        conv_p = (conv_dw_w[i], conv_dw_b[i], conv_ln_g[i], conv_ln_b[i])
        merge_p = (norm2_g[i], ssm_glu_w[i].astype(BF16), ssm_glu_b[i], conv_pw_w[i].astype(BF16),
                   conv_pw_b[i], ret_w_o[i].astype(BF16), w_out[i].astype(BF16), wr[i], br[i])
        x1, h2, top_idx, top_gate = _mixer_layer(xs, modt[i], i, ctx_len, tables, norm1_g[i],
                                                 w_in[i].astype(BF16), s5p, ssm_d[i], conv_p, merge_p)
        xs = _moe(x1, h2, top_idx, top_gate, modt[i], exp_gu_w, exp_gu_b, exp_down_w, exp_down_b,
                  i, ctx_len)
    return _final_norm(xs, final_g, ctx_len)
```

```python
import functools
import math

import numpy as np
import jax
import jax.numpy as jnp
from jax import lax
from jax.experimental import pallas as pl
from jax.experimental.pallas import tpu as pltpu

F32 = jnp.float32
BF16 = jnp.bfloat16

D_MODEL = 1024
DEPTH = 4
GRID_W = 64
N_MOD = 6
SSM_WIDTH = 256
SSM_GROUP = 16
SSM_GROUPS = SSM_WIDTH // SSM_GROUP
SSM_STATE = 64
SSM_LANES = SSM_GROUPS * SSM_STATE
CONV_WIDTH = 256
CONV_K = 31
CONV_PAD = 16
RET_HEADS = 4
RET_HEAD_DIM = 128
RET_WIDTH = RET_HEADS * RET_HEAD_DIM
RET_CHUNK = 128
ROPE_BASE = 10000.0
IN_WIDTH = SSM_WIDTH + 2 * CONV_WIDTH + 4 * RET_WIDTH + 3 * D_MODEL
COL_CV = SSM_WIDTH
COL_QKVG = COL_CV + 2 * CONV_WIDTH
COL_GATES = COL_QKVG + 4 * RET_WIDTH
N_EXPERTS = 32
TOP_K = 4
EXPERT_FF = D_MODEL
SWIGLU_LIMIT = 7.0
SWIGLU_ALPHA = 1.702
EPS = 1e-6

ROW_TILE = 256
HALF_TILE = 128
S5_STEPS = 64
S5_STRIP = 512
MOE_ROWS = 512
MOE_PART = 256
LOGIT_LANES = 128
VMEM_LIMIT = 56 * 1024 * 1024


def _cparams(sem):
    return pltpu.CompilerParams(dimension_semantics=sem, vmem_limit_bytes=VMEM_LIMIT)


def _rms(x):
    return x * lax.rsqrt(jnp.mean(x * x, axis=-1, keepdims=True) + EPS)


def _ada_kernel(s_ref, w_ref, b_ref, o_ref):
    s = s_ref[...]
    s = s * jax.nn.sigmoid(s)
    o_ref[0] = jnp.dot(s, w_ref[0], preferred_element_type=F32,
                       precision=lax.Precision.HIGHEST) + b_ref[0]


def _ada_mod(s_in, ada_w, ada_b):
    depth, d, n = ada_w.shape
    tn = 1536
    return pl.pallas_call(
        _ada_kernel,
        grid=(depth, n // tn),
        in_specs=[pl.BlockSpec((16, d), lambda i, j: (0, 0)),
                  pl.BlockSpec((1, d, tn), lambda i, j: (i, 0, j)),
                  pl.BlockSpec((1, 1, tn), lambda i, j: (i, 0, j))],
        out_specs=pl.BlockSpec((1, 16, tn), lambda i, j: (i, 0, j)),
        out_shape=jax.ShapeDtypeStruct((depth, 16, n), F32),
        compiler_params=_cparams(("arbitrary", "arbitrary")),
        name="ada_mod",
    )(s_in, ada_w, ada_b.reshape(depth, 1, n))


def _inproj_kernel(x_ref, mod_ref, g_ref, w_ref, u_ref, cv_ref, qkvg_ref, gates_ref):
    x = x_ref[0]
    mod = mod_ref[0, 0]
    h = (_rms(x) * g_ref[...]) * (1.0 + mod[1:2]) + mod[0:1]
    hb = h.astype(BF16)
    u_ref[0] = jnp.dot(hb, w_ref[:, 0:COL_CV], preferred_element_type=F32)
    cv_ref[0] = jnp.dot(hb, w_ref[:, COL_CV:COL_QKVG], preferred_element_type=F32)
    qkvg_ref[0] = jnp.dot(hb, w_ref[:, COL_QKVG:COL_GATES], preferred_element_type=F32).astype(BF16)
    gates = jnp.dot(hb, w_ref[:, COL_GATES:IN_WIDTH], preferred_element_type=F32)
    gates_ref[0] = jax.nn.sigmoid(gates).astype(BF16)


def _in_proj(x, modt, g1, w_in_bf, ctx_len):
    b, s_tot, d = x.shape
    nt = s_tot // ROW_TILE
    nctx = ctx_len // ROW_TILE
    seg = lambda j: jnp.where(j >= nctx, 1, 0)
    return pl.pallas_call(
        _inproj_kernel,
        grid=(b, nt),
        in_specs=[pl.BlockSpec((1, ROW_TILE, d), lambda i, j: (i, j, 0)),
                  pl.BlockSpec((1, 1, 8, d), lambda i, j: (i, seg(j), 0, 0)),
                  pl.BlockSpec((1, d), lambda i, j: (0, 0)),
                  pl.BlockSpec((d, IN_WIDTH), lambda i, j: (0, 0))],
        out_specs=[pl.BlockSpec((1, ROW_TILE, SSM_WIDTH), lambda i, j: (i, j, 0)),
                   pl.BlockSpec((1, ROW_TILE, 2 * CONV_WIDTH), lambda i, j: (i, j, 0)),
                   pl.BlockSpec((1, ROW_TILE, 4 * RET_WIDTH), lambda i, j: (i, j, 0)),
                   pl.BlockSpec((1, ROW_TILE, 3 * D_MODEL), lambda i, j: (i, j, 0))],
        out_shape=[jax.ShapeDtypeStruct((b, s_tot, SSM_WIDTH), F32),
                   jax.ShapeDtypeStruct((b, s_tot, 2 * CONV_WIDTH), F32),
                   jax.ShapeDtypeStruct((b, s_tot, 4 * RET_WIDTH), BF16),
                   jax.ShapeDtypeStruct((b, s_tot, 3 * D_MODEL), BF16)],
        compiler_params=_cparams(("arbitrary", "arbitrary")),
        name="in_proj",
    )(x, modt, g1.reshape(1, d), w_in_bf)


def _s5_params(lam_re, lam_im, log_dt, b_re, b_im, c_re, c_im):
    lam = lax.complex(jnp.minimum(lam_re.astype(F32), -1e-4), lam_im.astype(F32))
    lam_dt = lam * jnp.exp(log_dt.astype(F32))[..., None]
    lam_bar = jnp.exp(lam_dt)
    b = lax.complex(b_re.astype(F32), b_im.astype(F32))
    b_bar = ((lam_bar - 1.0) / lam)[..., None] * b
    eye = jnp.eye(SSM_GROUPS, dtype=F32)

    def in_mat(t):
        return jnp.einsum('dgpm,gh->dgmhp', t, eye).reshape(2, SSM_WIDTH, SSM_LANES)

    def out_mat(t):
        return jnp.einsum('dgmp,gh->dgphm', t, eye).reshape(2, SSM_LANES, SSM_WIDTH)

    bmat = jnp.concatenate([in_mat(jnp.real(b_bar)), in_mat(jnp.imag(b_bar))], axis=-1)
    cmat = jnp.concatenate([out_mat(c_re.astype(F32)), -out_mat(c_im.astype(F32))], axis=1)
    lvec = jnp.concatenate([jnp.real(lam_bar).reshape(2, 1, SSM_LANES),
                            jnp.imag(lam_bar).reshape(2, 1, SSM_LANES)], axis=-1)
    return bmat.astype(BF16), cmat.astype(BF16), lvec


def _s5_kernel(u_ref, perm_ref, permt_ref, bmat_ref, cmat_ref, lam_ref, dsk_ref, y_ref, st_ref, carry_ref):
    d = pl.program_id(0)
    c = pl.program_id(1)
    nb, steps, width = u_ref.shape

    @pl.when(c == 0)
    def _():
        carry_ref[...] = jnp.zeros_like(carry_ref)

    u = u_ref[...].reshape(nb * steps, width)
    u_tm = jnp.dot(perm_ref[...], u.astype(BF16), preferred_element_type=F32).astype(BF16)
    bu = jnp.dot(u_tm, bmat_ref[0], preferred_element_type=F32)
    st_ref[...] = bu.reshape(steps, nb, 2 * SSM_LANES)

    for k in range(SSM_LANES // S5_STRIP):
        re = pl.ds(k * S5_STRIP, S5_STRIP)
        im = pl.ds(SSM_LANES + k * S5_STRIP, S5_STRIP)
        lr = jnp.broadcast_to(lam_ref[0, :, re], (nb, S5_STRIP))
        li = jnp.broadcast_to(lam_ref[0, :, im], (nb, S5_STRIP))

        def step(t, s):
            sr, si = s
            tt = jnp.where(d == 1, steps - 1 - t, t)
            nr = lr * sr - li * si + st_ref[tt, :, re]
            ni = lr * si + li * sr + st_ref[tt, :, im]
            st_ref[tt, :, re] = nr
            st_ref[tt, :, im] = ni
            return nr, ni

        sr, si = lax.fori_loop(0, steps, step, (carry_ref[:, re], carry_ref[:, im]), unroll=2)
        carry_ref[:, re] = sr
        carry_ref[:, im] = si

    st = st_ref[...].reshape(steps * nb, 2 * SSM_LANES)
    y_tm = jnp.dot(st.astype(BF16), cmat_ref[0], preferred_element_type=F32)
    y = jnp.dot(permt_ref[...], y_tm.astype(BF16), preferred_element_type=F32)
    skip = jnp.where(d == 0, 1.0, 0.0) * dsk_ref[...]
    y_ref[0] = (y + u * skip).reshape(nb, steps, width)


def _s5_scan(u, bmat, cmat, lvec, d_skip, ctx_len):
    nb, s_tot, width = u.shape
    nch = s_tot // S5_STEPS
    nctx = ctx_len // S5_STEPS
    rows = nb * S5_STEPS
    r = np.arange(rows)
    perm = np.zeros((rows, rows), np.float32)
    perm[r, (r % nb) * S5_STEPS + r // nb] = 1.0
    perm_tm = jnp.asarray(perm, BF16)
    perm_bm = jnp.asarray(perm.T, BF16)

    def chunk(d, c):
        back = jnp.where(c < nctx, nctx - 1 - c, nctx + nch - 1 - c)
        return jnp.where(d == 1, back, c)

    return pl.pallas_call(
        _s5_kernel,
        grid=(2, nch),
        in_specs=[pl.BlockSpec((nb, S5_STEPS, width), lambda d, c: (0, chunk(d, c), 0)),
                  pl.BlockSpec((rows, rows), lambda d, c: (0, 0)),
                  pl.BlockSpec((rows, rows), lambda d, c: (0, 0)),
                  pl.BlockSpec((1, width, 2 * SSM_LANES), lambda d, c: (d, 0, 0)),
                  pl.BlockSpec((1, 2 * SSM_LANES, width), lambda d, c: (d, 0, 0)),
                  pl.BlockSpec((1, 1, 2 * SSM_LANES), lambda d, c: (d, 0, 0)),
                  pl.BlockSpec((1, width), lambda d, c: (0, 0))],
        out_specs=pl.BlockSpec((1, nb, S5_STEPS, width), lambda d, c: (d, 0, chunk(d, c), 0)),
        out_shape=jax.ShapeDtypeStruct((2, nb, s_tot, width), F32),
        scratch_shapes=[pltpu.VMEM((S5_STEPS, nb, 2 * SSM_LANES), F32),
                        pltpu.VMEM((nb, 2 * SSM_LANES), F32)],
        compiler_params=_cparams(("arbitrary", "arbitrary")),
        name="s5_scan",
    )(u, perm_tm, perm_bm, bmat, cmat, lvec, d_skip.reshape(1, width))


def _conv_kernel(cv_ref, w_ref, b_ref, lg_ref, lb_ref, o_ref, hp_ref, win_ref, shift_ref, *, ctx_len):
    s_tot = cv_ref.shape[1]
    lat = s_tot - ctx_len
    rc = RET_CHUNK
    zeros = jnp.zeros((CONV_PAD, CONV_WIDTH), F32)
    a = cv_ref[0, :, 0:CONV_WIDTH]
    g = cv_ref[0, :, CONV_WIDTH:2 * CONV_WIDTH]
    h = a * jax.nn.sigmoid(g)
    hp_ref[0:CONV_PAD] = zeros
    hp_ref[CONV_PAD:CONV_PAD + ctx_len] = h[0:ctx_len]
    hp_ref[CONV_PAD + ctx_len:2 * CONV_PAD + ctx_len] = zeros
    hp_ref[2 * CONV_PAD + ctx_len:2 * CONV_PAD + s_tot] = h[ctx_len:s_tot]
    hp_ref[2 * CONV_PAD + s_tot:3 * CONV_PAD + s_tot] = zeros
    del lat

    def chunk(c, carry):
        r0 = pl.multiple_of(c * rc, rc)
        wbase = pl.multiple_of(r0 + jnp.where(r0 >= ctx_len, CONV_PAD, 0), 8)
        win_ref[...] = hp_ref[pl.ds(wbase, rc + 2 * CONV_PAD), :]
        for r in range(1, 8):
            shift_ref[r] = win_ref[r:r + rc + 2 * CONV_PAD - 8, :]
        acc = jnp.zeros((rc, CONV_WIDTH), F32) + b_ref[...]
        for k in range(CONV_K):
            off = k + CONV_PAD - CONV_K // 2
            base = off - off % 8
            if off % 8 == 0:
                tap = win_ref[base:base + rc, :]
            else:
                tap = shift_ref[off % 8, base:base + rc, :]
            acc = acc + w_ref[k:k + 1, :] * tap
        mu = jnp.mean(acc, axis=-1, keepdims=True)
        var = jnp.mean(jnp.square(acc - mu), axis=-1, keepdims=True)
        y = (acc - mu) * lax.rsqrt(var + EPS) * lg_ref[...] + lb_ref[...]
        o_ref[0, pl.ds(r0, rc), :] = y * jax.nn.sigmoid(y)
        return carry

    lax.fori_loop(0, s_tot // rc, chunk, 0)


def _conv_branch(cv, w_dw, b_dw, ln_g, ln_b, ctx_len):
    b, s_tot, _ = cv.shape
    w = jnp.concatenate([w_dw.reshape(CONV_K, CONV_WIDTH), jnp.zeros((1, CONV_WIDTH), F32)], axis=0)
    vec = lambda t: t.reshape(1, CONV_WIDTH)
    cst = lambda shape: pl.BlockSpec(shape, lambda i: (0,) * len(shape))
    return pl.pallas_call(
        functools.partial(_conv_kernel, ctx_len=ctx_len),
        grid=(b,),
        in_specs=[pl.BlockSpec((1, s_tot, 2 * CONV_WIDTH), lambda i: (i, 0, 0)),
                  cst((CONV_K + 1, CONV_WIDTH)), cst((1, CONV_WIDTH)),
                  cst((1, CONV_WIDTH)), cst((1, CONV_WIDTH))],
        out_specs=pl.BlockSpec((1, s_tot, CONV_WIDTH), lambda i: (i, 0, 0)),
        out_shape=jax.ShapeDtypeStruct((b, s_tot, CONV_WIDTH), F32),
        scratch_shapes=[pltpu.VMEM((s_tot + 3 * CONV_PAD, CONV_WIDTH), F32),
                        pltpu.VMEM((RET_CHUNK + 2 * CONV_PAD, CONV_WIDTH), F32),
                        pltpu.VMEM((8, RET_CHUNK + 2 * CONV_PAD - 8, CONV_WIDTH), F32)],
        compiler_params=_cparams(("arbitrary",)),
        name="conv_branch",
    )(cv, w, vec(b_dw), vec(ln_g), vec(ln_b))


N_TAB = 5


def _ret_tables(ctx_len, seq):
    n = RET_HEAD_DIM // 4
    inv_freq = ROPE_BASE ** (-np.arange(n, dtype=np.float64) / n)
    pos = np.arange(seq)
    ang_r = (pos // GRID_W)[:, None] * inv_freq
    ang_c = (pos % GRID_W)[:, None] * inv_freq
    cos = np.concatenate([np.cos(ang_r), np.cos(ang_r), np.cos(ang_c), np.cos(ang_c)], axis=-1)
    sin = np.concatenate([-np.sin(ang_r), np.sin(ang_r), -np.sin(ang_c), np.sin(ang_c)], axis=-1)
    cos = np.concatenate([np.ones((ctx_len, RET_HEAD_DIM)), cos], axis=0)
    sin = np.concatenate([np.zeros((ctx_len, RET_HEAD_DIM)), sin], axis=0)
    log_g_fwd = np.log1p(-np.exp2(-5.0 - np.arange(RET_HEADS, dtype=np.float64)))
    idx = np.arange(RET_CHUNK, dtype=np.float64)
    diff = idx[:, None] - idx[None, :]
    tabs = np.zeros((2, RET_HEADS, N_TAB, RET_CHUNK, RET_CHUNK))
    ones = np.ones((RET_CHUNK, RET_CHUNK))
    for d, log_g in enumerate((log_g_fwd, log_g_fwd[::-1])):
        for hd in range(RET_HEADS):
            lg = log_g[hd]
            if d == 0:
                mask = np.where(diff >= 0, np.exp(lg * np.maximum(diff, 0.0)), 0.0)
                dec = np.exp(lg * (idx + 1.0))
                zeta = np.exp(lg * (RET_CHUNK - 1 - idx))
            else:
                mask = np.where(diff <= 0, np.exp(lg * np.maximum(-diff, 0.0)), 0.0)
                dec = np.exp(lg * (RET_CHUNK - idx))
                zeta = np.exp(lg * idx)
            tabs[d, hd, 0] = mask
            tabs[d, hd, 1] = dec[:, None] * ones
            tabs[d, hd, 2] = zeta[:, None] * ones
            tabs[d, hd, 3] = np.exp(lg * RET_CHUNK) * ones
            tabs[d, hd, 4] = zeta[None, :] * ones
    return (jnp.asarray(cos, F32), jnp.asarray(sin, F32), jnp.asarray(tabs, F32))


def _ret_kernel(q_ref, k_ref, v_ref, g_ref, cos_ref, sin_ref, tab_ref, o_ref,
                qs_ref, qd_ref, kt_ref, kz_ref, out_ref, state_ref, *, ctx_len):
    s_tot = q_ref.shape[1]
    rc = RET_CHUNK
    nch = s_tot // rc
    nctx = ctx_len // rc
    k_scale = RET_HEAD_DIM ** -0.5
    lane = lax.broadcasted_iota(jnp.int32, (rc, RET_HEAD_DIM), 1)
    first_half = (lane % (RET_HEAD_DIM // 2)) < (RET_HEAD_DIM // 4)

    def rope(t, cs, sn):
        quarter = RET_HEAD_DIM // 4
        partner = jnp.where(first_half, pltpu.roll(t, RET_HEAD_DIM - quarter, 1), pltpu.roll(t, quarter, 1))
        return t * cs + partner * sn

    def prepare(c, carry):
        rows = pl.ds(pl.multiple_of(c * rc, rc), rc)
        cs = cos_ref[rows, :]
        sn = sin_ref[rows, :]
        q = rope(q_ref[0, rows, :].astype(F32), cs, sn)
        kt = (rope(k_ref[0, rows, :].astype(F32), cs, sn) * k_scale).T
        qs_ref[c] = q.astype(BF16)
        kt_ref[c] = kt.astype(BF16)
        for d in range(2):
            qd_ref[d, c] = (q * tab_ref[d, 0, 1]).astype(BF16)
            kz_ref[d, c] = (kt * tab_ref[d, 0, 4]).astype(BF16)
        return carry

    lax.fori_loop(0, nch, prepare, 0, unroll=2)
    state_ref[...] = jnp.zeros_like(state_ref)

    def chunk(c, d):
        rows = pl.ds(pl.multiple_of(c * rc, rc), rc)
        vb = v_ref[0, rows, :]
        state = state_ref[d]
        scores = jnp.dot(qs_ref[c], kt_ref[c], preferred_element_type=F32) * tab_ref[d, 0, 0]
        o = (jnp.dot(scores.astype(BF16), vb, preferred_element_type=F32)
             + jnp.dot(qd_ref[d, c], state.astype(BF16), preferred_element_type=F32))
        kv = jnp.dot(kz_ref[d, c], vb, preferred_element_type=F32)
        state_ref[d] = tab_ref[d, 0, 3] * state + kv
        out_ref[d, rows, :] = o

    def step(j, carry):
        chunk(j, 0)
        chunk(jnp.where(j < nctx, nctx - 1 - j, nctx + nch - 1 - j), 1)
        return carry

    lax.fori_loop(0, nch, step, 0, unroll=2)
    g = g_ref[0].astype(F32)
    o_ref[0] = (_rms(out_ref[0] + out_ref[1]) * (g * jax.nn.sigmoid(g))).astype(o_ref.dtype)


def _retention(qkvg, cos, sin, tabs, ctx_len):
    b, s_tot, _ = qkvg.shape
    hd = RET_HEAD_DIM
    nch = s_tot // RET_CHUNK
    part = lambda p: pl.BlockSpec((1, s_tot, hd), lambda i, h: (i, 0, p * RET_HEADS + h))
    return pl.pallas_call(
        functools.partial(_ret_kernel, ctx_len=ctx_len),
        grid=(b, RET_HEADS),
        in_specs=[part(0), part(1), part(2), part(3),
                  pl.BlockSpec((s_tot, hd), lambda i, h: (0, 0)),
                  pl.BlockSpec((s_tot, hd), lambda i, h: (0, 0)),
                  pl.BlockSpec((2, 1, N_TAB, RET_CHUNK, RET_CHUNK), lambda i, h: (0, h, 0, 0, 0))],
        out_specs=pl.BlockSpec((1, s_tot, hd), lambda i, h: (i, 0, h)),
        out_shape=jax.ShapeDtypeStruct((b, s_tot, RET_WIDTH), BF16),
        scratch_shapes=[pltpu.VMEM((nch, RET_CHUNK, hd), BF16),
                        pltpu.VMEM((2, nch, RET_CHUNK, hd), BF16),
                        pltpu.VMEM((nch, hd, RET_CHUNK), BF16),
                        pltpu.VMEM((2, nch, hd, RET_CHUNK), BF16),
                        pltpu.VMEM((2, s_tot, hd), F32),
                        pltpu.VMEM((2, hd, hd), F32)],
        compiler_params=_cparams(("arbitrary", "arbitrary")),
        name="retention",
    )(qkvg, qkvg, qkvg, qkvg, cos, sin, tabs)


def _merge_kernel(x_ref, yf_ref, yb_ref, hc_ref, ro_ref, gates_ref, mod_ref, g2_ref,
                  wglu_ref, bglu_ref, wpw_ref, bpw_ref, wo_ref, wout_ref, wr_ref, br_ref, tril_ref,
                  x1_ref, h2_ref, ti_ref, tg_ref, cnt_out_ref, cnt_ref):
    d = D_MODEL
    mod = mod_ref[0, 0]

    @pl.when(jnp.logical_and(pl.program_id(0) == 0, pl.program_id(1) == 0))
    def _():
        cnt_ref[...] = jnp.zeros_like(cnt_ref)

    for half in range(ROW_TILE // HALF_TILE):
        rows = pl.ds(half * HALF_TILE, HALF_TILE)
        ys = jax.nn.gelu(yf_ref[0, 0, rows, :] + yb_ref[0, 0, rows, :]).astype(BF16)
        z = jnp.dot(ys, wglu_ref[...], preferred_element_type=F32) + bglu_ref[...]
        y_ssm = z[:, 0:d] * jax.nn.sigmoid(z[:, d:2 * d])
        y_conv = jnp.dot(hc_ref[0, rows, :].astype(BF16), wpw_ref[...],
                         preferred_element_type=F32) + bpw_ref[...]
        y_ret = jnp.dot(ro_ref[0, rows, :].astype(BF16), wo_ref[...], preferred_element_type=F32)
        m = (gates_ref[0, rows, 0:d].astype(F32) * y_ssm
             + gates_ref[0, rows, d:2 * d].astype(F32) * y_conv
             + gates_ref[0, rows, 2 * d:3 * d].astype(F32) * y_ret)
        y = jnp.dot(m.astype(BF16), wout_ref[...], preferred_element_type=F32)
        x1 = x_ref[0, rows, :] + mod[2:3] * y
        x1_ref[0, rows, :] = x1
        h2 = (_rms(x1) * g2_ref[...]) * (1.0 + mod[4:5]) + mod[3:4]
        h2_ref[0, rows, :] = h2
        h_hi = h2.astype(BF16)
        h_lo = (h2 - h_hi.astype(F32)).astype(BF16)
        logits = (jnp.dot(h_hi, wr_ref[0], preferred_element_type=F32)
                  + jnp.dot(h_lo, wr_ref[0], preferred_element_type=F32)
                  + jnp.dot(h_hi, wr_ref[1], preferred_element_type=F32)) + br_ref[...]
        lane = lax.broadcasted_iota(jnp.int32, logits.shape, 1)
        lane_f = lane.astype(F32)
        neg = jnp.float32(-jnp.inf)
        l = jnp.where(lane < N_EXPERTS, logits, neg)
        ti = jnp.zeros(logits.shape, F32)
        vals = []
        picks = []
        for k in range(TOP_K):
            top = jnp.max(l, axis=-1, keepdims=True)
            idx = jnp.min(jnp.where(l == top, lane_f, float(LOGIT_LANES)), axis=-1, keepdims=True)
            ti = jnp.where(lane == k, idx, ti)
            vals.append(top)
            picked = lane_f == idx
            picks.append(picked.astype(F32))
            l = jnp.where(picked, neg, l)
        es = [jnp.exp(v - vals[0]) for v in vals]
        tot = es[0]
        for e in es[1:]:
            tot = tot + e
        tg = jnp.zeros(logits.shape, F32)
        for k in range(TOP_K):
            tg = jnp.where(lane == k, es[k] / tot, tg)
        chosen = picks[0] + picks[1] + picks[2] + picks[3]
        before = cnt_ref[0:1, :] + jnp.dot(tril_ref[...], chosen.astype(BF16),
                                           preferred_element_type=F32)
        for k in range(TOP_K):
            rank = jnp.sum(picks[k] * before, axis=-1, keepdims=True)
            ti = jnp.where(lane == TOP_K + k, rank, ti)
        cnt_ref[...] = cnt_ref[...] + jnp.sum(chosen, axis=0, keepdims=True)
        ti_ref[0, rows, :] = ti.astype(jnp.int32)
        tg_ref[0, rows, :] = tg
    cnt_out_ref[...] = cnt_ref[...]


def _merge(x, y_s5, hc, ro, gates, modt, g2, wglu, bglu, wpw, bpw, wo, wout, wr, br, ctx_len):
    b, s_tot, d = x.shape
    nt = s_tot // ROW_TILE
    nctx = ctx_len // ROW_TILE
    seg = lambda j: jnp.where(j >= nctx, 1, 0)
    row = lambda w: pl.BlockSpec((1, ROW_TILE, w), lambda i, j: (i, j, 0))
    cst = lambda shape: pl.BlockSpec(shape, lambda i, j: (0,) * len(shape))
    y2 = y_s5
    tril = jnp.asarray(np.tril(np.ones((HALF_TILE, HALF_TILE), np.float32), -1), BF16)
    return pl.pallas_call(
        _merge_kernel,
        grid=(b, nt),
        in_specs=[row(d),
                  pl.BlockSpec((1, 1, ROW_TILE, SSM_WIDTH), lambda i, j: (0, i, j, 0)),
                  pl.BlockSpec((1, 1, ROW_TILE, SSM_WIDTH), lambda i, j: (1, i, j, 0)),
                  row(CONV_WIDTH), row(RET_WIDTH), row(3 * d),
                  pl.BlockSpec((1, 1, 8, d), lambda i, j: (i, seg(j), 0, 0)),
                  cst((1, d)),
                  cst((SSM_WIDTH, 2 * d)), cst((1, 2 * d)),
                  cst((CONV_WIDTH, d)), cst((1, d)),
                  cst((RET_WIDTH, d)), cst((d, d)),
                  cst((2, d, LOGIT_LANES)), cst((1, LOGIT_LANES)),
                  cst((HALF_TILE, HALF_TILE))],
        out_specs=[row(d), row(d), row(LOGIT_LANES), row(LOGIT_LANES), cst((8, LOGIT_LANES))],
        out_shape=[jax.ShapeDtypeStruct((b, s_tot, d), F32),
                   jax.ShapeDtypeStruct((b, s_tot, d), F32),
                   jax.ShapeDtypeStruct((b, s_tot, LOGIT_LANES), jnp.int32),
                   jax.ShapeDtypeStruct((b, s_tot, LOGIT_LANES), F32),
                   jax.ShapeDtypeStruct((8, LOGIT_LANES), F32)],
        scratch_shapes=[pltpu.VMEM((8, LOGIT_LANES), F32)],
        compiler_params=_cparams(("arbitrary", "arbitrary")),
        name="merge_router",
    )(x, y2, y2, hc, ro, gates, modt, g2.reshape(1, d), wglu, bglu.reshape(1, 2 * d),
      wpw, bpw.reshape(1, d), wo, wout, wr, br, tril)


def _route(top_idx, top_rank, counts):
    n_tok = top_idx.shape[0]
    n_assign = n_tok * TOP_K
    n_tiles = n_assign // MOE_ROWS
    n_visits = n_tiles + N_EXPERTS - 1
    i32 = jnp.int32
    ends = jnp.cumsum(counts)
    starts = ends - counts
    inv = (jnp.take(starts, top_idx.reshape(-1)) + top_rank.reshape(-1)).astype(i32)
    iota = jnp.arange(n_assign, dtype=i32)
    _, order = lax.sort((inv, iota), num_keys=1)
    first_tile = starts // MOE_ROWS
    has_rows = counts > 0
    tiles_per = jnp.where(has_rows, (ends - 1) // MOE_ROWS - first_tile + 1, 0)
    vend = jnp.cumsum(tiles_per)
    vstart = vend - tiles_per
    total = vend[-1]
    v = jnp.arange(n_visits, dtype=i32)
    valid = v < total
    ve = jnp.minimum(jnp.sum((v[:, None] >= vend[None, :]).astype(i32), axis=1), N_EXPERTS - 1)
    ve = jnp.where(valid, ve, ve[jnp.maximum(total - 1, 0)])
    vt = jnp.where(valid, first_tile[ve] + v - vstart[ve], n_tiles - 1)
    lo = jnp.where(valid, jnp.clip(starts[ve] - vt * MOE_ROWS, 0, MOE_ROWS), 0)
    hi = jnp.where(valid, jnp.clip(ends[ve] - vt * MOE_ROWS, 0, MOE_ROWS), 0)
    experts = jnp.arange(N_EXPERTS, dtype=i32)
    later = jnp.where(has_rows, experts, N_EXPERTS)
    next_with_rows = jnp.flip(lax.cummin(jnp.flip(jnp.concatenate(
        [later[1:], jnp.full((1,), N_EXPERTS, i32)]))))
    next_with_rows = jnp.where(next_with_rows >= N_EXPERTS, -1, next_with_rows)
    slot = (jnp.cumsum(has_rows.astype(i32)) - 1) % 2
    visits = (vt, ve, lo, hi, next_with_rows[ve], slot[ve])
    return order // TOP_K, inv, tuple(t.astype(i32) for t in visits)


def _expert_kernel(vt_ref, ve_ref, lo_ref, hi_ref, nx_ref, sl_ref, x_ref, wgu_hbm, bgu_ref, wd_hbm,
                   bd_ref, y_ref, wgu_f32, wd_f32, wgu_bf, wd_bf, sem, *, layer):
    v = pl.program_id(0)
    prev = jnp.maximum(v - 1, 0)
    lo = lo_ref[v]
    hi = hi_ref[v]
    active = hi > lo
    fresh_weights = jnp.logical_or(v == 0, ve_ref[v] != ve_ref[prev])
    first_visit = jnp.logical_or(v == 0, vt_ref[v] != vt_ref[prev])

    def weight_copies(expert, slot):
        return (pltpu.make_async_copy(wgu_hbm.at[layer, expert], wgu_f32.at[slot], sem.at[0, slot]),
                pltpu.make_async_copy(wd_hbm.at[layer, expert], wd_f32.at[slot], sem.at[1, slot]))

    @pl.when(jnp.logical_and(fresh_weights, active))
    def _():
        slot = sl_ref[v]

        @pl.when(v == 0)
        def _():
            for c in weight_copies(ve_ref[v], slot):
                c.start()

        for c in weight_copies(ve_ref[v], slot):
            c.wait()

        @pl.when(nx_ref[v] >= 0)
        def _():
            for c in weight_copies(nx_ref[v], 1 - slot):
                c.start()

        wgu_bf[...] = wgu_f32[slot].astype(BF16)
        wd_bf[...] = wd_f32[slot].astype(BF16)

    whole = jnp.logical_and(lo == 0, hi == MOE_ROWS)

    def ffn(rows):
        f = EXPERT_FF
        gu = jnp.dot(x_ref[rows, :].astype(BF16), wgu_bf[...],
                     preferred_element_type=F32) + bgu_ref[0, 0]
        gate = jnp.minimum(gu[:, 0:f], SWIGLU_LIMIT)
        up = jnp.clip(gu[:, f:2 * f], -SWIGLU_LIMIT, SWIGLU_LIMIT)
        act = (up + 1.0) * gate * jax.nn.sigmoid(gate * SWIGLU_ALPHA)
        y = jnp.dot(act.astype(BF16), wd_bf[...], preferred_element_type=F32) + bd_ref[0, 0]
        return y.astype(y_ref.dtype)

    parts = [pl.ds(p * MOE_PART, MOE_PART) for p in range(MOE_ROWS // MOE_PART)]

    @pl.when(whole)
    def _():
        for rows in parts:
            y_ref[rows, :] = ffn(rows)

    @pl.when(jnp.logical_and(first_visit, jnp.logical_not(whole)))
    def _():
        y_ref[...] = jnp.zeros_like(y_ref)

    for p, rows in enumerate(parts):
        touched = jnp.logical_and(lo < (p + 1) * MOE_PART, hi > p * MOE_PART)

        @pl.when(jnp.logical_and(touched, jnp.logical_not(whole)))
        def _():
            row = p * MOE_PART + lax.broadcasted_iota(jnp.int32, (MOE_PART, 1), 0)
            mine = jnp.logical_and(row >= lo, row < hi)
            y_ref[rows, :] = jnp.where(mine, ffn(rows), y_ref[rows, :])


def _experts(xs, visits, w_gu, b_gu, w_down, b_down, layer):
    n_rows, d = xs.shape
    n_visits = visits[0].shape[0]
    f = EXPERT_FF
    wmap = lambda v, vt, ve, lo, hi, nx, sl: (layer, ve[v], 0, 0)
    tmap = lambda v, vt, ve, lo, hi, nx, sl: (vt[v], 0)
    grid_spec = pltpu.PrefetchScalarGridSpec(
        num_scalar_prefetch=6,
        grid=(n_visits,),
        in_specs=[pl.BlockSpec((MOE_ROWS, d), tmap),
                  pl.BlockSpec(memory_space=pl.ANY),
                  pl.BlockSpec((1, 1, 1, 2 * f), wmap),
                  pl.BlockSpec(memory_space=pl.ANY),
                  pl.BlockSpec((1, 1, 1, d), wmap)],
        out_specs=pl.BlockSpec((MOE_ROWS, d), tmap),
        scratch_shapes=[pltpu.VMEM((2, d, 2 * f), F32), pltpu.VMEM((2, f, d), F32),
                        pltpu.VMEM((d, 2 * f), BF16), pltpu.VMEM((f, d), BF16),
                        pltpu.SemaphoreType.DMA((2, 2))],
    )
    return pl.pallas_call(
        functools.partial(_expert_kernel, layer=layer),
        grid_spec=grid_spec,
        out_shape=jax.ShapeDtypeStruct((n_rows, d), BF16),
        compiler_params=_cparams(("arbitrary",)),
        name="experts",
    )(*visits, xs, w_gu, b_gu.reshape(DEPTH, N_EXPERTS, 1, 2 * f),
      w_down, b_down.reshape(DEPTH, N_EXPERTS, 1, d))


def _combine_kernel(x_ref, y0_ref, y1_ref, y2_ref, y3_ref, gt_ref, mod_ref, fg_ref, o_ref, *, final):
    mod = mod_ref[0, 0]
    gt = gt_ref[0]
    y = gt[:, 0:1] * y0_ref[0].astype(F32)
    for k, y_ref in ((1, y1_ref), (2, y2_ref), (3, y3_ref)):
        y = y + gt[:, k:k + 1] * y_ref[0].astype(F32)
    x2 = x_ref[0] + mod[5:6] * y
    o_ref[0] = _rms(x2) * fg_ref[...] if final else x2


def _combine(x1, yg, gates, modt, final_g, ctx_len, final):
    b, s_tot, d = x1.shape
    nt = s_tot // ROW_TILE
    nctx = ctx_len // ROW_TILE
    off = nctx if final else 0
    seg = lambda j: jnp.where(j + off >= nctx, 1, 0)
    choice = lambda k: pl.BlockSpec((1, ROW_TILE, d), lambda i, j: (k, i * nt + j + off, 0))
    return pl.pallas_call(
        functools.partial(_combine_kernel, final=final),
        grid=(b, nt - off),
        in_specs=[pl.BlockSpec((1, ROW_TILE, d), lambda i, j: (i, j + off, 0)),
                  choice(0), choice(1), choice(2), choice(3),
                  pl.BlockSpec((1, ROW_TILE, LOGIT_LANES), lambda i, j: (i, j + off, 0)),
                  pl.BlockSpec((1, 1, 8, d), lambda i, j: (i, seg(j), 0, 0)),
                  pl.BlockSpec((1, d), lambda i, j: (0, 0))],
        out_specs=pl.BlockSpec((1, ROW_TILE, d), lambda i, j: (i, j, 0)),
        out_shape=jax.ShapeDtypeStruct((b, s_tot - off * ROW_TILE, d), F32),
        compiler_params=_cparams(("arbitrary", "arbitrary")),
        name="moe_combine",
    )(x1, yg, yg, yg, yg, gates, modt, final_g.reshape(1, d))


def _moe(x1, h2, top, top_gate, counts, modt, final_g, exp_gu_w, exp_gu_b, exp_down_w, exp_down_b,
         layer, ctx_len, final):
    b, s_tot, d = x1.shape
    n_tok = b * s_tot
    top = top.reshape(n_tok, LOGIT_LANES)
    rows, inv, visits = _route(top[:, :TOP_K], top[:, TOP_K:2 * TOP_K],
                               counts[0, :N_EXPERTS].astype(jnp.int32))
    xs = h2.reshape(n_tok, d).at[rows].get(mode="promise_in_bounds")
    ys = _experts(xs, visits, exp_gu_w, exp_gu_b, exp_down_w, exp_down_b, layer)
    by_choice = inv.reshape(n_tok, TOP_K).T.reshape(-1)
    yg = ys.at[by_choice].get(mode="promise_in_bounds").reshape(TOP_K, n_tok, d)
    return _combine(x1, yg, top_gate, modt, final_g, ctx_len, final)


def _mixer_layer(x, modt, i, ctx_len, tables, norm1_g, w_in_bf, s5p, ssm_d, conv_p, merge_p):
    b, s_tot, d = x.shape
    u2, cv, qkvg, gates = _in_proj(x, modt, norm1_g, w_in_bf, ctx_len)
    bmat, cmat, lvec = s5p
    y_s5 = _s5_scan(u2, bmat, cmat, lvec, ssm_d, ctx_len)
    hc = _conv_branch(cv, *conv_p, ctx_len)
    cos, sin, tabs = tables
    ro = _retention(qkvg, cos, sin, tabs, ctx_len)
    return _merge(x, y_s5, hc, ro, gates, modt, *merge_p, ctx_len)


def kernel(x, c, ctx, c_ctx, ada_w, ada_b, norm1_g, w_in, ssm_lam_re, ssm_lam_im, ssm_log_dt, ssm_b_re, ssm_b_im, ssm_c_re, ssm_c_im, ssm_d, ssm_glu_w, ssm_glu_b, conv_dw_w, conv_dw_b, conv_ln_g, conv_ln_b, conv_pw_w, conv_pw_b, ret_w_o, w_out, norm2_g, router_w, router_b, exp_gu_w, exp_gu_b, exp_down_w, exp_down_b, final_g):
    b, seq, d = x.shape
    ctx_len = ctx.shape[1]
    depth = ada_w.shape[0]
    xs = jnp.concatenate([ctx, x], axis=1)
    s_in = jnp.concatenate([c, c_ctx[None], jnp.zeros((16 - b - 1, d), F32)], axis=0)
    mod = _ada_mod(s_in, ada_w, ada_b).reshape(depth, 16, N_MOD, d)
    mod = jnp.pad(mod, ((0, 0), (0, 0), (0, 8 - N_MOD), (0, 0)))
    modt = jnp.stack([jnp.broadcast_to(mod[:, b][:, None], (depth, b, 8, d)), mod[:, :b]], axis=2)
    tables = _ret_tables(ctx_len, seq)
    wr = jnp.pad(router_w, ((0, 0), (0, 0), (0, LOGIT_LANES - N_EXPERTS)))
    wr_hi = wr.astype(BF16)
    wr = jnp.stack([wr_hi, (wr - wr_hi.astype(F32)).astype(BF16)], axis=1)
    br = jnp.pad(router_b, ((0, 0), (0, LOGIT_LANES - N_EXPERTS))).reshape(depth, 1, LOGIT_LANES)
    for i in range(depth):
        s5p = _s5_params(ssm_lam_re[i], ssm_lam_im[i], ssm_log_dt[i], ssm_b_re[i], ssm_b_im[i],
                         ssm_c_re[i], ssm_c_im[i])
        conv_p = (conv_dw_w[i], conv_dw_b[i], conv_ln_g[i], conv_ln_b[i])
        merge_p = (norm2_g[i], ssm_glu_w[i].astype(BF16), ssm_glu_b[i], conv_pw_w[i].astype(BF16),
                   conv_pw_b[i], ret_w_o[i].astype(BF16), w_out[i].astype(BF16), wr[i], br[i])
        x1, h2, top, top_gate, counts = _mixer_layer(xs, modt[i], i, ctx_len, tables, norm1_g[i],
                                                     w_in[i].astype(BF16), s5p, ssm_d[i], conv_p, merge_p)
        xs = _moe(x1, h2, top, top_gate, counts, modt[i], final_g, exp_gu_w, exp_gu_b, exp_down_w,
                  exp_down_b, i, ctx_len, final=(i == depth - 1))
    return xs
```

```python
import functools
import math

import numpy as np
import jax
import jax.numpy as jnp
from jax import lax
from jax.experimental import pallas as pl
from jax.experimental.pallas import tpu as pltpu

F32 = jnp.float32
BF16 = jnp.bfloat16

D_MODEL = 1024
DEPTH = 4
GRID_W = 64
N_MOD = 6
SSM_WIDTH = 256
SSM_GROUP = 16
SSM_GROUPS = SSM_WIDTH // SSM_GROUP
SSM_STATE = 64
SSM_LANES = SSM_GROUPS * SSM_STATE
CONV_WIDTH = 256
CONV_K = 31
CONV_PAD = 16
RET_HEADS = 4
RET_HEAD_DIM = 128
RET_WIDTH = RET_HEADS * RET_HEAD_DIM
RET_CHUNK = 128
ROPE_BASE = 10000.0
IN_WIDTH = SSM_WIDTH + 2 * CONV_WIDTH + 4 * RET_WIDTH + 3 * D_MODEL
COL_CV = SSM_WIDTH
COL_QKVG = COL_CV + 2 * CONV_WIDTH
COL_GATES = COL_QKVG + 4 * RET_WIDTH
N_EXPERTS = 32
TOP_K = 4
EXPERT_FF = D_MODEL
SWIGLU_LIMIT = 7.0
SWIGLU_ALPHA = 1.702
EPS = 1e-6

ROW_TILE = 256
HALF_TILE = 128
S5_STEPS = 64
S5_STRIP = 512
MOE_ROWS = 512
MOE_PART = 256
LOGIT_LANES = 128
VMEM_LIMIT = 56 * 1024 * 1024


def _cparams(sem):
    return pltpu.CompilerParams(dimension_semantics=sem, vmem_limit_bytes=VMEM_LIMIT)


def _rms(x):
    return x * lax.rsqrt(jnp.mean(x * x, axis=-1, keepdims=True) + EPS)


def _ada_kernel(s_ref, w_ref, b_ref, o_ref):
    s = s_ref[...]
    s = s * jax.nn.sigmoid(s)
    o_ref[0] = jnp.dot(s, w_ref[0], preferred_element_type=F32,
                       precision=lax.Precision.HIGHEST) + b_ref[0]


def _ada_mod(s_in, ada_w, ada_b):
    depth, d, n = ada_w.shape
    tn = 1536
    return pl.pallas_call(
        _ada_kernel,
        grid=(depth, n // tn),
        in_specs=[pl.BlockSpec((16, d), lambda i, j: (0, 0)),
                  pl.BlockSpec((1, d, tn), lambda i, j: (i, 0, j)),
                  pl.BlockSpec((1, 1, tn), lambda i, j: (i, 0, j))],
        out_specs=pl.BlockSpec((1, 16, tn), lambda i, j: (i, 0, j)),
        out_shape=jax.ShapeDtypeStruct((depth, 16, n), F32),
        compiler_params=_cparams(("arbitrary", "arbitrary")),
        name="ada_mod",
    )(s_in, ada_w, ada_b.reshape(depth, 1, n))


def _inproj_kernel(x_ref, mod_ref, g_ref, w_ref, u_ref, cv_ref, qkvg_ref, gates_ref):
    x = x_ref[0]
    mod = mod_ref[0, 0]
    h = (_rms(x) * g_ref[...]) * (1.0 + mod[1:2]) + mod[0:1]
    hb = h.astype(BF16)
    u_ref[0] = jnp.dot(hb, w_ref[:, 0:COL_CV], preferred_element_type=F32)
    cv_ref[0] = jnp.dot(hb, w_ref[:, COL_CV:COL_QKVG], preferred_element_type=F32)
    qkvg_ref[0] = jnp.dot(hb, w_ref[:, COL_QKVG:COL_GATES], preferred_element_type=F32).astype(BF16)
    gates = jnp.dot(hb, w_ref[:, COL_GATES:IN_WIDTH], preferred_element_type=F32)
    gates_ref[0] = jax.nn.sigmoid(gates).astype(BF16)


def _in_proj(x, modt, g1, w_in_bf, ctx_len):
    b, s_tot, d = x.shape
    nt = s_tot // ROW_TILE
    nctx = ctx_len // ROW_TILE
    seg = lambda j: jnp.where(j >= nctx, 1, 0)
    return pl.pallas_call(
        _inproj_kernel,
        grid=(b, nt),
        in_specs=[pl.BlockSpec((1, ROW_TILE, d), lambda i, j: (i, j, 0)),
                  pl.BlockSpec((1, 1, 8, d), lambda i, j: (i, seg(j), 0, 0)),
                  pl.BlockSpec((1, d), lambda i, j: (0, 0)),
                  pl.BlockSpec((d, IN_WIDTH), lambda i, j: (0, 0))],
        out_specs=[pl.BlockSpec((1, ROW_TILE, SSM_WIDTH), lambda i, j: (i, j, 0)),
                   pl.BlockSpec((1, ROW_TILE, 2 * CONV_WIDTH), lambda i, j: (i, j, 0)),
                   pl.BlockSpec((1, ROW_TILE, 4 * RET_WIDTH), lambda i, j: (i, j, 0)),
                   pl.BlockSpec((1, ROW_TILE, 3 * D_MODEL), lambda i, j: (i, j, 0))],
        out_shape=[jax.ShapeDtypeStruct((b, s_tot, SSM_WIDTH), F32),
                   jax.ShapeDtypeStruct((b, s_tot, 2 * CONV_WIDTH), F32),
                   jax.ShapeDtypeStruct((b, s_tot, 4 * RET_WIDTH), BF16),
                   jax.ShapeDtypeStruct((b, s_tot, 3 * D_MODEL), BF16)],
        compiler_params=_cparams(("arbitrary", "arbitrary")),
        name="in_proj",
    )(x, modt, g1.reshape(1, d), w_in_bf)


def _s5_params(lam_re, lam_im, log_dt, b_re, b_im, c_re, c_im):
    lam = lax.complex(jnp.minimum(lam_re.astype(F32), -1e-4), lam_im.astype(F32))
    lam_dt = lam * jnp.exp(log_dt.astype(F32))[..., None]
    lam_bar = jnp.exp(lam_dt)
    b = lax.complex(b_re.astype(F32), b_im.astype(F32))
    b_bar = ((lam_bar - 1.0) / lam)[..., None] * b
    eye = jnp.eye(SSM_GROUPS, dtype=F32)

    def in_mat(t):
        return jnp.einsum('dgpm,gh->dgmhp', t, eye).reshape(2, SSM_WIDTH, SSM_LANES)

    def out_mat(t):
        return jnp.einsum('dgmp,gh->dgphm', t, eye).reshape(2, SSM_LANES, SSM_WIDTH)

    bmat = jnp.concatenate([in_mat(jnp.real(b_bar)), in_mat(jnp.imag(b_bar))], axis=-1)
    cmat = jnp.concatenate([out_mat(c_re.astype(F32)), -out_mat(c_im.astype(F32))], axis=1)
    lvec = jnp.concatenate([jnp.real(lam_bar).reshape(2, 1, SSM_LANES),
                            jnp.imag(lam_bar).reshape(2, 1, SSM_LANES)], axis=-1)
    return bmat.astype(BF16), cmat.astype(BF16), lvec


def _s5_kernel(u_ref, perm_ref, permt_ref, bmat_ref, cmat_ref, lam_ref, dsk_ref, y_ref, st_ref, carry_ref):
    d = pl.program_id(0)
    c = pl.program_id(1)
    nb, steps, width = u_ref.shape

    @pl.when(c == 0)
    def _():
        carry_ref[...] = jnp.zeros_like(carry_ref)

    u = u_ref[...].reshape(nb * steps, width)
    u_tm = jnp.dot(perm_ref[...], u.astype(BF16), preferred_element_type=F32).astype(BF16)
    bu = jnp.dot(u_tm, bmat_ref[0], preferred_element_type=F32)
    st_ref[...] = bu.reshape(steps, nb, 2 * SSM_LANES)

    for k in range(SSM_LANES // S5_STRIP):
        re = pl.ds(k * S5_STRIP, S5_STRIP)
        im = pl.ds(SSM_LANES + k * S5_STRIP, S5_STRIP)
        lr = jnp.broadcast_to(lam_ref[0, :, re], (nb, S5_STRIP))
        li = jnp.broadcast_to(lam_ref[0, :, im], (nb, S5_STRIP))

        def step(t, s):
            sr, si = s
            tt = jnp.where(d == 1, steps - 1 - t, t)
            nr = lr * sr - li * si + st_ref[tt, :, re]
            ni = lr * si + li * sr + st_ref[tt, :, im]
            st_ref[tt, :, re] = nr
            st_ref[tt, :, im] = ni
            return nr, ni

        sr, si = lax.fori_loop(0, steps, step, (carry_ref[:, re], carry_ref[:, im]), unroll=2)
        carry_ref[:, re] = sr
        carry_ref[:, im] = si

    st = st_ref[...].reshape(steps * nb, 2 * SSM_LANES)
    y_tm = jnp.dot(st.astype(BF16), cmat_ref[0], preferred_element_type=F32)
    y = jnp.dot(permt_ref[...], y_tm.astype(BF16), preferred_element_type=F32)
    skip = jnp.where(d == 0, 1.0, 0.0) * dsk_ref[...]
    y_ref[0] = (y + u * skip).reshape(nb, steps, width)


def _s5_scan(u, bmat, cmat, lvec, d_skip, ctx_len):
    nb, s_tot, width = u.shape
    nch = s_tot // S5_STEPS
    nctx = ctx_len // S5_STEPS
    rows = nb * S5_STEPS
    r = np.arange(rows)
    perm = np.zeros((rows, rows), np.float32)
    perm[r, (r % nb) * S5_STEPS + r // nb] = 1.0
    perm_tm = jnp.asarray(perm, BF16)
    perm_bm = jnp.asarray(perm.T, BF16)

    def chunk(d, c):
        back = jnp.where(c < nctx, nctx - 1 - c, nctx + nch - 1 - c)
        return jnp.where(d == 1, back, c)

    return pl.pallas_call(
        _s5_kernel,
        grid=(2, nch),
        in_specs=[pl.BlockSpec((nb, S5_STEPS, width), lambda d, c: (0, chunk(d, c), 0)),
                  pl.BlockSpec((rows, rows), lambda d, c: (0, 0)),
                  pl.BlockSpec((rows, rows), lambda d, c: (0, 0)),
                  pl.BlockSpec((1, width, 2 * SSM_LANES), lambda d, c: (d, 0, 0)),
                  pl.BlockSpec((1, 2 * SSM_LANES, width), lambda d, c: (d, 0, 0)),
                  pl.BlockSpec((1, 1, 2 * SSM_LANES), lambda d, c: (d, 0, 0)),
                  pl.BlockSpec((1, width), lambda d, c: (0, 0))],
        out_specs=pl.BlockSpec((1, nb, S5_STEPS, width), lambda d, c: (d, 0, chunk(d, c), 0)),
        out_shape=jax.ShapeDtypeStruct((2, nb, s_tot, width), F32),
        scratch_shapes=[pltpu.VMEM((S5_STEPS, nb, 2 * SSM_LANES), F32),
                        pltpu.VMEM((nb, 2 * SSM_LANES), F32)],
        compiler_params=_cparams(("arbitrary", "arbitrary")),
        name="s5_scan",
    )(u, perm_tm, perm_bm, bmat, cmat, lvec, d_skip.reshape(1, width))


def _conv_kernel(cv_ref, w_ref, b_ref, lg_ref, lb_ref, o_ref, hp_ref, win_ref, shift_ref, *, ctx_len):
    s_tot = cv_ref.shape[1]
    lat = s_tot - ctx_len
    rc = RET_CHUNK
    zeros = jnp.zeros((CONV_PAD, CONV_WIDTH), F32)
    a = cv_ref[0, :, 0:CONV_WIDTH]
    g = cv_ref[0, :, CONV_WIDTH:2 * CONV_WIDTH]
    h = a * jax.nn.sigmoid(g)
    hp_ref[0:CONV_PAD] = zeros
    hp_ref[CONV_PAD:CONV_PAD + ctx_len] = h[0:ctx_len]
    hp_ref[CONV_PAD + ctx_len:2 * CONV_PAD + ctx_len] = zeros
    hp_ref[2 * CONV_PAD + ctx_len:2 * CONV_PAD + s_tot] = h[ctx_len:s_tot]
    hp_ref[2 * CONV_PAD + s_tot:3 * CONV_PAD + s_tot] = zeros
    del lat

    def chunk(c, carry):
        r0 = pl.multiple_of(c * rc, rc)
        wbase = pl.multiple_of(r0 + jnp.where(r0 >= ctx_len, CONV_PAD, 0), 8)
        win_ref[...] = hp_ref[pl.ds(wbase, rc + 2 * CONV_PAD), :]
        for r in range(1, 8):
            shift_ref[r] = win_ref[r:r + rc + 2 * CONV_PAD - 8, :]
        acc = jnp.zeros((rc, CONV_WIDTH), F32) + b_ref[...]
        for k in range(CONV_K):
            off = k + CONV_PAD - CONV_K // 2
            base = off - off % 8
            if off % 8 == 0:
                tap = win_ref[base:base + rc, :]
            else:
                tap = shift_ref[off % 8, base:base + rc, :]
            acc = acc + w_ref[k:k + 1, :] * tap
        mu = jnp.mean(acc, axis=-1, keepdims=True)
        var = jnp.mean(jnp.square(acc - mu), axis=-1, keepdims=True)
        y = (acc - mu) * lax.rsqrt(var + EPS) * lg_ref[...] + lb_ref[...]
        o_ref[0, pl.ds(r0, rc), :] = y * jax.nn.sigmoid(y)
        return carry

    lax.fori_loop(0, s_tot // rc, chunk, 0)


def _conv_branch(cv, w_dw, b_dw, ln_g, ln_b, ctx_len):
    b, s_tot, _ = cv.shape
    w = jnp.concatenate([w_dw.reshape(CONV_K, CONV_WIDTH), jnp.zeros((1, CONV_WIDTH), F32)], axis=0)
    vec = lambda t: t.reshape(1, CONV_WIDTH)
    cst = lambda shape: pl.BlockSpec(shape, lambda i: (0,) * len(shape))
    return pl.pallas_call(
        functools.partial(_conv_kernel, ctx_len=ctx_len),
        grid=(b,),
        in_specs=[pl.BlockSpec((1, s_tot, 2 * CONV_WIDTH), lambda i: (i, 0, 0)),
                  cst((CONV_K + 1, CONV_WIDTH)), cst((1, CONV_WIDTH)),
                  cst((1, CONV_WIDTH)), cst((1, CONV_WIDTH))],
        out_specs=pl.BlockSpec((1, s_tot, CONV_WIDTH), lambda i: (i, 0, 0)),
        out_shape=jax.ShapeDtypeStruct((b, s_tot, CONV_WIDTH), F32),
        scratch_shapes=[pltpu.VMEM((s_tot + 3 * CONV_PAD, CONV_WIDTH), F32),
                        pltpu.VMEM((RET_CHUNK + 2 * CONV_PAD, CONV_WIDTH), F32),
                        pltpu.VMEM((8, RET_CHUNK + 2 * CONV_PAD - 8, CONV_WIDTH), F32)],
        compiler_params=_cparams(("arbitrary",)),
        name="conv_branch",
    )(cv, w, vec(b_dw), vec(ln_g), vec(ln_b))


N_TAB = 5


def _ret_tables(ctx_len, seq):
    n = RET_HEAD_DIM // 4
    inv_freq = ROPE_BASE ** (-np.arange(n, dtype=np.float64) / n)
    pos = np.arange(seq)
    ang_r = (pos // GRID_W)[:, None] * inv_freq
    ang_c = (pos % GRID_W)[:, None] * inv_freq
    cos = np.concatenate([np.cos(ang_r), np.cos(ang_r), np.cos(ang_c), np.cos(ang_c)], axis=-1)
    sin = np.concatenate([-np.sin(ang_r), np.sin(ang_r), -np.sin(ang_c), np.sin(ang_c)], axis=-1)
    cos = np.concatenate([np.ones((ctx_len, RET_HEAD_DIM)), cos], axis=0)
    sin = np.concatenate([np.zeros((ctx_len, RET_HEAD_DIM)), sin], axis=0)
    log_g_fwd = np.log1p(-np.exp2(-5.0 - np.arange(RET_HEADS, dtype=np.float64)))
    idx = np.arange(RET_CHUNK, dtype=np.float64)
    diff = idx[:, None] - idx[None, :]
    tabs = np.zeros((2, RET_HEADS, N_TAB, RET_CHUNK, RET_CHUNK))
    ones = np.ones((RET_CHUNK, RET_CHUNK))
    for d, log_g in enumerate((log_g_fwd, log_g_fwd[::-1])):
        for hd in range(RET_HEADS):
            lg = log_g[hd]
            if d == 0:
                mask = np.where(diff >= 0, np.exp(lg * np.maximum(diff, 0.0)), 0.0)
                dec = np.exp(lg * (idx + 1.0))
                zeta = np.exp(lg * (RET_CHUNK - 1 - idx))
            else:
                mask = np.where(diff <= 0, np.exp(lg * np.maximum(-diff, 0.0)), 0.0)
                dec = np.exp(lg * (RET_CHUNK - idx))
                zeta = np.exp(lg * idx)
            tabs[d, hd, 0] = mask
            tabs[d, hd, 1] = dec[:, None] * ones
            tabs[d, hd, 2] = zeta[:, None] * ones
            tabs[d, hd, 3] = np.exp(lg * RET_CHUNK) * ones
            tabs[d, hd, 4] = zeta[None, :] * ones
    return (jnp.asarray(cos, F32), jnp.asarray(sin, F32), jnp.asarray(tabs, F32))


def _ret_kernel(q_ref, k_ref, v_ref, g_ref, cos_ref, sin_ref, tab_ref, o_ref,
                qs_ref, qd_ref, kt_ref, kz_ref, out_ref, state_ref, *, ctx_len):
    s_tot = q_ref.shape[1]
    rc = RET_CHUNK
    nch = s_tot // rc
    nctx = ctx_len // rc
    k_scale = RET_HEAD_DIM ** -0.5
    lane = lax.broadcasted_iota(jnp.int32, (rc, RET_HEAD_DIM), 1)
    first_half = (lane % (RET_HEAD_DIM // 2)) < (RET_HEAD_DIM // 4)

    def rope(t, cs, sn):
        quarter = RET_HEAD_DIM // 4
        partner = jnp.where(first_half, pltpu.roll(t, RET_HEAD_DIM - quarter, 1), pltpu.roll(t, quarter, 1))
        return t * cs + partner * sn

    def prepare(c, carry):
        rows = pl.ds(pl.multiple_of(c * rc, rc), rc)
        cs = cos_ref[rows, :]
        sn = sin_ref[rows, :]
        q = rope(q_ref[0, rows, :].astype(F32), cs, sn)
        kt = (rope(k_ref[0, rows, :].astype(F32), cs, sn) * k_scale).T
        qs_ref[c] = q.astype(BF16)
        kt_ref[c] = kt.astype(BF16)
        for d in range(2):
            qd_ref[d, c] = (q * tab_ref[d, 0, 1]).astype(BF16)
            kz_ref[d, c] = (kt * tab_ref[d, 0, 4]).astype(BF16)
        return carry

    lax.fori_loop(0, nch, prepare, 0, unroll=2)
    state_ref[...] = jnp.zeros_like(state_ref)

    def chunk(c, d):
        rows = pl.ds(pl.multiple_of(c * rc, rc), rc)
        vb = v_ref[0, rows, :]
        state = state_ref[d]
        scores = jnp.dot(qs_ref[c], kt_ref[c], preferred_element_type=F32) * tab_ref[d, 0, 0]
        o = (jnp.dot(scores.astype(BF16), vb, preferred_element_type=F32)
             + jnp.dot(qd_ref[d, c], state.astype(BF16), preferred_element_type=F32))
        kv = jnp.dot(kz_ref[d, c], vb, preferred_element_type=F32)
        state_ref[d] = tab_ref[d, 0, 3] * state + kv
        out_ref[d, rows, :] = o

    def step(j, carry):
        chunk(j, 0)
        chunk(jnp.where(j < nctx, nctx - 1 - j, nctx + nch - 1 - j), 1)
        return carry

    lax.fori_loop(0, nch, step, 0, unroll=2)
    g = g_ref[0].astype(F32)
    o_ref[0] = (_rms(out_ref[0] + out_ref[1]) * (g * jax.nn.sigmoid(g))).astype(o_ref.dtype)


def _retention(qkvg, cos, sin, tabs, ctx_len):
    b, s_tot, _ = qkvg.shape
    hd = RET_HEAD_DIM
    nch = s_tot // RET_CHUNK
    part = lambda p: pl.BlockSpec((1, s_tot, hd), lambda i, h: (i, 0, p * RET_HEADS + h))
    return pl.pallas_call(
        functools.partial(_ret_kernel, ctx_len=ctx_len),
        grid=(b, RET_HEADS),
        in_specs=[part(0), part(1), part(2), part(3),
                  pl.BlockSpec((s_tot, hd), lambda i, h: (0, 0)),
                  pl.BlockSpec((s_tot, hd), lambda i, h: (0, 0)),
                  pl.BlockSpec((2, 1, N_TAB, RET_CHUNK, RET_CHUNK), lambda i, h: (0, h, 0, 0, 0))],
        out_specs=pl.BlockSpec((1, s_tot, hd), lambda i, h: (i, 0, h)),
        out_shape=jax.ShapeDtypeStruct((b, s_tot, RET_WIDTH), BF16),
        scratch_shapes=[pltpu.VMEM((nch, RET_CHUNK, hd), BF16),
                        pltpu.VMEM((2, nch, RET_CHUNK, hd), BF16),
                        pltpu.VMEM((nch, hd, RET_CHUNK), BF16),
                        pltpu.VMEM((2, nch, hd, RET_CHUNK), BF16),
                        pltpu.VMEM((2, s_tot, hd), F32),
                        pltpu.VMEM((2, hd, hd), F32)],
        compiler_params=_cparams(("arbitrary", "arbitrary")),
        name="retention",
    )(qkvg, qkvg, qkvg, qkvg, cos, sin, tabs)


def _merge_kernel(x_ref, yf_ref, yb_ref, hc_ref, ro_ref, gates_ref, mod_ref, g2_ref,
                  wglu_ref, bglu_ref, wpw_ref, bpw_ref, wo_ref, wout_ref, wr_ref, br_ref, tril_ref,
                  x1_ref, h2_ref, ti_ref, tg_ref, cnt_out_ref, cnt_ref):
    d = D_MODEL
    mod = mod_ref[0, 0]

    @pl.when(jnp.logical_and(pl.program_id(0) == 0, pl.program_id(1) == 0))
    def _():
        cnt_ref[...] = jnp.zeros_like(cnt_ref)

    seen = cnt_ref[0:1, :]
    for half in range(ROW_TILE // HALF_TILE):
        rows = pl.ds(half * HALF_TILE, HALF_TILE)
        ys = jax.nn.gelu(yf_ref[0, 0, rows, :] + yb_ref[0, 0, rows, :]).astype(BF16)
        z = jnp.dot(ys, wglu_ref[...], preferred_element_type=F32) + bglu_ref[...]
        y_ssm = z[:, 0:d] * jax.nn.sigmoid(z[:, d:2 * d])
        y_conv = jnp.dot(hc_ref[0, rows, :].astype(BF16), wpw_ref[...],
                         preferred_element_type=F32) + bpw_ref[...]
        y_ret = jnp.dot(ro_ref[0, rows, :].astype(BF16), wo_ref[...], preferred_element_type=F32)
        m = (gates_ref[0, rows, 0:d].astype(F32) * y_ssm
             + gates_ref[0, rows, d:2 * d].astype(F32) * y_conv
             + gates_ref[0, rows, 2 * d:3 * d].astype(F32) * y_ret)
        y = jnp.dot(m.astype(BF16), wout_ref[...], preferred_element_type=F32)
        x1 = x_ref[0, rows, :] + mod[2:3] * y
        x1_ref[0, rows, :] = x1
        h2 = (_rms(x1) * g2_ref[...]) * (1.0 + mod[4:5]) + mod[3:4]
        h2_ref[0, rows, :] = h2
        h_hi = h2.astype(BF16)
        h_lo = (h2 - h_hi.astype(F32)).astype(BF16)
        logits = (jnp.dot(h_hi, wr_ref[0], preferred_element_type=F32)
                  + jnp.dot(h_lo, wr_ref[0], preferred_element_type=F32)
                  + jnp.dot(h_hi, wr_ref[1], preferred_element_type=F32)) + br_ref[...]
        lane = lax.broadcasted_iota(jnp.int32, logits.shape, 1)
        lane_f = lane.astype(F32)
        neg = jnp.float32(-jnp.inf)
        l = jnp.where(lane < N_EXPERTS, logits, neg)
        ti = jnp.zeros(logits.shape, F32)
        vals = []
        picks = []
        for k in range(TOP_K):
            top = jnp.max(l, axis=-1, keepdims=True)
            idx = jnp.min(jnp.where(l == top, lane_f, float(LOGIT_LANES)), axis=-1, keepdims=True)
            ti = jnp.where(lane == k, idx, ti)
            vals.append(top)
            picked = lane_f == idx
            picks.append(picked.astype(F32))
            l = jnp.where(picked, neg, l)
        es = [jnp.exp(v - vals[0]) for v in vals]
        tot = es[0]
        for e in es[1:]:
            tot = tot + e
        tg = jnp.zeros(logits.shape, F32)
        for k in range(TOP_K):
            tg = jnp.where(lane == k, es[k] / tot, tg)
        chosen = picks[0] + picks[1] + picks[2] + picks[3]
        before = seen + jnp.dot(tril_ref[...], chosen.astype(BF16), preferred_element_type=F32)
        for k in range(TOP_K):
            rank = jnp.sum(picks[k] * before, axis=-1, keepdims=True)
            ti = jnp.where(lane == TOP_K + k, rank, ti)
        seen = seen + jnp.sum(chosen, axis=0, keepdims=True)
        ti_ref[0, rows, :] = ti.astype(jnp.int32)
        tg_ref[0, rows, :] = tg
    cnt_ref[...] = jnp.broadcast_to(seen, cnt_ref.shape)
    cnt_out_ref[...] = jnp.broadcast_to(seen, cnt_out_ref.shape)


def _merge(x, y_s5, hc, ro, gates, modt, g2, wglu, bglu, wpw, bpw, wo, wout, wr, br, ctx_len):
    b, s_tot, d = x.shape
    nt = s_tot // ROW_TILE
    nctx = ctx_len // ROW_TILE
    seg = lambda j: jnp.where(j >= nctx, 1, 0)
    row = lambda w: pl.BlockSpec((1, ROW_TILE, w), lambda i, j: (i, j, 0))
    cst = lambda shape: pl.BlockSpec(shape, lambda i, j: (0,) * len(shape))
    y2 = y_s5
    tril = jnp.asarray(np.tril(np.ones((HALF_TILE, HALF_TILE), np.float32), -1), BF16)
    return pl.pallas_call(
        _merge_kernel,
        grid=(b, nt),
        in_specs=[row(d),
                  pl.BlockSpec((1, 1, ROW_TILE, SSM_WIDTH), lambda i, j: (0, i, j, 0)),
                  pl.BlockSpec((1, 1, ROW_TILE, SSM_WIDTH), lambda i, j: (1, i, j, 0)),
                  row(CONV_WIDTH), row(RET_WIDTH), row(3 * d),
                  pl.BlockSpec((1, 1, 8, d), lambda i, j: (i, seg(j), 0, 0)),
                  cst((1, d)),
                  cst((SSM_WIDTH, 2 * d)), cst((1, 2 * d)),
                  cst((CONV_WIDTH, d)), cst((1, d)),
                  cst((RET_WIDTH, d)), cst((d, d)),
                  cst((2, d, LOGIT_LANES)), cst((1, LOGIT_LANES)),
                  cst((HALF_TILE, HALF_TILE))],
        out_specs=[row(d), row(d), row(LOGIT_LANES), row(LOGIT_LANES), cst((8, LOGIT_LANES))],
        out_shape=[jax.ShapeDtypeStruct((b, s_tot, d), F32),
                   jax.ShapeDtypeStruct((b, s_tot, d), F32),
                   jax.ShapeDtypeStruct((b, s_tot, LOGIT_LANES), jnp.int32),
                   jax.ShapeDtypeStruct((b, s_tot, LOGIT_LANES), F32),
                   jax.ShapeDtypeStruct((8, LOGIT_LANES), F32)],
        scratch_shapes=[pltpu.VMEM((8, LOGIT_LANES), F32)],
        compiler_params=_cparams(("arbitrary", "arbitrary")),
        name="merge_router",
    )(x, y2, y2, hc, ro, gates, modt, g2.reshape(1, d), wglu, bglu.reshape(1, 2 * d),
      wpw, bpw.reshape(1, d), wo, wout, wr, br, tril)


PLAN_TILE, PLAN_EXPERT, PLAN_LO, PLAN_HI, PLAN_NEXT, PLAN_SLOT = range(6)
PLAN_LANES = 256


def _plan_kernel(cnt_ref, plan_ref, *, n_tiles):
    ne = N_EXPERTS
    rows_per = float(MOE_ROWS)
    e = lax.broadcasted_iota(jnp.int32, (ne, PLAN_LANES), 0)
    e_f = e.astype(F32)
    v = lax.broadcasted_iota(jnp.int32, (ne, PLAN_LANES), 1).astype(F32)

    def cumsum(t):
        for s in (1, 2, 4, 8, 16):
            t = t + jnp.where(e >= s, pltpu.roll(t, s, 0), 0.0)
        return t

    def lookup(table, onehot):
        return jnp.sum(onehot * table, axis=0, keepdims=True)

    c = jnp.broadcast_to(cnt_ref[...], (ne, PLAN_LANES))
    has = c > 0.0
    ends = cumsum(c)
    starts = ends - c
    first_tile = jnp.floor(starts / rows_per)
    tiles_per = jnp.where(has, jnp.floor((ends - 1.0) / rows_per) - first_tile + 1.0, 0.0)
    vend = cumsum(tiles_per)
    vstart = vend - tiles_per
    total = jnp.max(vend, axis=0, keepdims=True)
    slot = cumsum(has.astype(F32)) - 1.0
    slot = slot - 2.0 * jnp.floor(slot / 2.0)
    later = jnp.where(has, e_f, float(ne))
    nxt = jnp.where(e < ne - 1, pltpu.roll(later, ne - 1, 0), float(ne))
    for s in (1, 2, 4, 8, 16):
        nxt = jnp.minimum(nxt, jnp.where(e < ne - s, pltpu.roll(nxt, ne - s, 0), float(ne)))
    nxt = jnp.where(nxt >= float(ne), -1.0, nxt)
    last_with_rows = jnp.max(jnp.where(has, e_f, 0.0), axis=0, keepdims=True)

    valid = v[0:1] < total
    ve = jnp.sum((v >= vend).astype(F32), axis=0, keepdims=True)
    ve = jnp.where(valid, jnp.minimum(ve, float(ne - 1)), last_with_rows)
    onehot = (e_f == ve).astype(F32)
    vt = jnp.where(valid, lookup(first_tile, onehot) + v[0:1] - lookup(vstart, onehot),
                   float(n_tiles - 1))
    lo = jnp.where(valid, jnp.clip(lookup(starts, onehot) - vt * rows_per, 0.0, rows_per), 0.0)
    hi = jnp.where(valid, jnp.clip(lookup(ends, onehot) - vt * rows_per, 0.0, rows_per), 0.0)
    fields = {PLAN_TILE: vt, PLAN_EXPERT: ve, PLAN_LO: lo, PLAN_HI: hi,
              PLAN_NEXT: lookup(nxt, onehot), PLAN_SLOT: lookup(slot, onehot)}
    row = lax.broadcasted_iota(jnp.int32, (8, PLAN_LANES), 0)
    plan = jnp.zeros((8, PLAN_LANES), F32)
    for r, val in fields.items():
        plan = jnp.where(row == r, val, plan)
    plan_ref[...] = plan.astype(jnp.int32)


def _plan(counts_col, n_tiles):
    return pl.pallas_call(
        functools.partial(_plan_kernel, n_tiles=n_tiles),
        out_shape=jax.ShapeDtypeStruct((8, PLAN_LANES), jnp.int32),
        name="moe_plan",
    )(counts_col)


def _expert_kernel(plan_ref, x_ref, wgu_hbm, bgu_ref, wd_hbm, bd_ref, y_ref,
                   wgu_f32, wd_f32, wgu_bf, wd_bf, sem, *, layer):
    v = pl.program_id(0)
    prev = jnp.maximum(v - 1, 0)
    lo = plan_ref[PLAN_LO, v]
    hi = plan_ref[PLAN_HI, v]
    expert = plan_ref[PLAN_EXPERT, v]
    next_expert = plan_ref[PLAN_NEXT, v]
    active = hi > lo
    fresh_weights = jnp.logical_or(v == 0, expert != plan_ref[PLAN_EXPERT, prev])
    first_visit = jnp.logical_or(v == 0, plan_ref[PLAN_TILE, v] != plan_ref[PLAN_TILE, prev])

    def weight_copies(expert, slot):
        return (pltpu.make_async_copy(wgu_hbm.at[layer, expert], wgu_f32.at[slot], sem.at[0, slot]),
                pltpu.make_async_copy(wd_hbm.at[layer, expert], wd_f32.at[slot], sem.at[1, slot]))

    @pl.when(jnp.logical_and(fresh_weights, active))
    def _():
        slot = plan_ref[PLAN_SLOT, v]

        @pl.when(v == 0)
        def _():
            for c in weight_copies(expert, slot):
                c.start()

        for c in weight_copies(expert, slot):
            c.wait()

        @pl.when(next_expert >= 0)
        def _():
            for c in weight_copies(next_expert, 1 - slot):
                c.start()

        wgu_bf[...] = wgu_f32[slot].astype(BF16)
        wd_bf[...] = wd_f32[slot].astype(BF16)

    whole = jnp.logical_and(lo == 0, hi == MOE_ROWS)

    def ffn(rows):
        f = EXPERT_FF
        gu = jnp.dot(x_ref[rows, :].astype(BF16), wgu_bf[...],
                     preferred_element_type=F32) + bgu_ref[0, 0]
        gate = jnp.minimum(gu[:, 0:f], SWIGLU_LIMIT)
        up = jnp.clip(gu[:, f:2 * f], -SWIGLU_LIMIT, SWIGLU_LIMIT)
        act = (up + 1.0) * gate * jax.nn.sigmoid(gate * SWIGLU_ALPHA)
        y = jnp.dot(act.astype(BF16), wd_bf[...], preferred_element_type=F32) + bd_ref[0, 0]
        return y.astype(y_ref.dtype)

    parts = [pl.ds(p * MOE_PART, MOE_PART) for p in range(MOE_ROWS // MOE_PART)]

    @pl.when(whole)
    def _():
        for rows in parts:
            y_ref[rows, :] = ffn(rows)

    @pl.when(jnp.logical_and(first_visit, jnp.logical_not(whole)))
    def _():
        y_ref[...] = jnp.zeros_like(y_ref)

    for p, rows in enumerate(parts):
        touched = jnp.logical_and(lo < (p + 1) * MOE_PART, hi > p * MOE_PART)

        @pl.when(jnp.logical_and(touched, jnp.logical_not(whole)))
        def _():
            row = p * MOE_PART + lax.broadcasted_iota(jnp.int32, (MOE_PART, 1), 0)
            mine = jnp.logical_and(row >= lo, row < hi)
            y_ref[rows, :] = jnp.where(mine, ffn(rows), y_ref[rows, :])


def _experts(xs, plan, w_gu, b_gu, w_down, b_down, layer):
    n_rows, d = xs.shape
    n_visits = n_rows // MOE_ROWS + N_EXPERTS - 1
    assert n_visits <= PLAN_LANES
    f = EXPERT_FF
    wmap = lambda v, plan: (layer, plan[PLAN_EXPERT, v], 0, 0)
    tmap = lambda v, plan: (plan[PLAN_TILE, v], 0)
    grid_spec = pltpu.PrefetchScalarGridSpec(
        num_scalar_prefetch=1,
        grid=(n_visits,),
        in_specs=[pl.BlockSpec((MOE_ROWS, d), tmap),
                  pl.BlockSpec(memory_space=pl.ANY),
                  pl.BlockSpec((1, 1, 1, 2 * f), wmap),
                  pl.BlockSpec(memory_space=pl.ANY),
                  pl.BlockSpec((1, 1, 1, d), wmap)],
        out_specs=pl.BlockSpec((MOE_ROWS, d), tmap),
        scratch_shapes=[pltpu.VMEM((2, d, 2 * f), F32), pltpu.VMEM((2, f, d), F32),
                        pltpu.VMEM((d, 2 * f), BF16), pltpu.VMEM((f, d), BF16),
                        pltpu.SemaphoreType.DMA((2, 2))],
    )
    return pl.pallas_call(
        functools.partial(_expert_kernel, layer=layer),
        grid_spec=grid_spec,
        out_shape=jax.ShapeDtypeStruct((n_rows, d), BF16),
        compiler_params=_cparams(("arbitrary",)),
        name="experts",
    )(plan, xs, w_gu, b_gu.reshape(DEPTH, N_EXPERTS, 1, 2 * f),
      w_down, b_down.reshape(DEPTH, N_EXPERTS, 1, d))


def _combine_kernel(x_ref, y0_ref, y1_ref, y2_ref, y3_ref, gt_ref, mod_ref, fg_ref, o_ref, *, final):
    mod = mod_ref[0, 0]
    gt = gt_ref[0]
    y = gt[:, 0:1] * y0_ref[0].astype(F32)
    for k, y_ref in ((1, y1_ref), (2, y2_ref), (3, y3_ref)):
        y = y + gt[:, k:k + 1] * y_ref[0].astype(F32)
    x2 = x_ref[0] + mod[5:6] * y
    o_ref[0] = _rms(x2) * fg_ref[...] if final else x2


def _combine(x1, yg, gates, modt, final_g, ctx_len, final):
    b, s_tot, d = x1.shape
    nt = s_tot // ROW_TILE
    nctx = ctx_len // ROW_TILE
    off = nctx if final else 0
    seg = lambda j: jnp.where(j + off >= nctx, 1, 0)
    choice = lambda k: pl.BlockSpec((1, ROW_TILE, d), lambda i, j: (k, i * nt + j + off, 0))
    return pl.pallas_call(
        functools.partial(_combine_kernel, final=final),
        grid=(b, nt - off),
        in_specs=[pl.BlockSpec((1, ROW_TILE, d), lambda i, j: (i, j + off, 0)),
                  choice(0), choice(1), choice(2), choice(3),
                  pl.BlockSpec((1, ROW_TILE, LOGIT_LANES), lambda i, j: (i, j + off, 0)),
                  pl.BlockSpec((1, 1, 8, d), lambda i, j: (i, seg(j), 0, 0)),
                  pl.BlockSpec((1, d), lambda i, j: (0, 0))],
        out_specs=pl.BlockSpec((1, ROW_TILE, d), lambda i, j: (i, j, 0)),
        out_shape=jax.ShapeDtypeStruct((b, s_tot - off * ROW_TILE, d), F32),
        compiler_params=_cparams(("arbitrary", "arbitrary")),
        name="moe_combine",
    )(x1, yg, yg, yg, yg, gates, modt, final_g.reshape(1, d))


def _moe(x1, h2, top, top_gate, counts, modt, final_g, exp_gu_w, exp_gu_b, exp_down_w, exp_down_b,
         layer, ctx_len, final):
    b, s_tot, d = x1.shape
    n_tok = b * s_tot
    top = top.reshape(n_tok, LOGIT_LANES)
    n_assign = n_tok * TOP_K
    cnt = counts[0, :N_EXPERTS]
    plan = _plan(cnt.reshape(N_EXPERTS, 1), n_assign // MOE_ROWS)
    starts = (jnp.cumsum(cnt) - cnt).astype(jnp.int32)
    inv = jnp.take(starts, top[:, :TOP_K].reshape(-1)) + top[:, TOP_K:2 * TOP_K].reshape(-1)
    _, order = lax.sort((inv, jnp.arange(n_assign, dtype=jnp.int32)), num_keys=1)
    rows = order // TOP_K
    xs = h2.reshape(n_tok, d).at[rows].get(mode="promise_in_bounds")
    ys = _experts(xs, plan, exp_gu_w, exp_gu_b, exp_down_w, exp_down_b, layer)
    by_choice = inv.reshape(n_tok, TOP_K).T.reshape(-1)
    yg = ys.at[by_choice].get(mode="promise_in_bounds").reshape(TOP_K, n_tok, d)
    return _combine(x1, yg, top_gate, modt, final_g, ctx_len, final)


def _mixer_layer(x, modt, i, ctx_len, tables, norm1_g, w_in_bf, s5p, ssm_d, conv_p, merge_p):
    b, s_tot, d = x.shape
    u2, cv, qkvg, gates = _in_proj(x, modt, norm1_g, w_in_bf, ctx_len)
    bmat, cmat, lvec = s5p
    y_s5 = _s5_scan(u2, bmat, cmat, lvec, ssm_d, ctx_len)
    hc = _conv_branch(cv, *conv_p, ctx_len)
    cos, sin, tabs = tables
    ro = _retention(qkvg, cos, sin, tabs, ctx_len)
    return _merge(x, y_s5, hc, ro, gates, modt, *merge_p, ctx_len)


def kernel(x, c, ctx, c_ctx, ada_w, ada_b, norm1_g, w_in, ssm_lam_re, ssm_lam_im, ssm_log_dt, ssm_b_re, ssm_b_im, ssm_c_re, ssm_c_im, ssm_d, ssm_glu_w, ssm_glu_b, conv_dw_w, conv_dw_b, conv_ln_g, conv_ln_b, conv_pw_w, conv_pw_b, ret_w_o, w_out, norm2_g, router_w, router_b, exp_gu_w, exp_gu_b, exp_down_w, exp_down_b, final_g):
    b, seq, d = x.shape
    ctx_len = ctx.shape[1]
    depth = ada_w.shape[0]
    xs = jnp.concatenate([ctx, x], axis=1)
    s_in = jnp.concatenate([c, c_ctx[None], jnp.zeros((16 - b - 1, d), F32)], axis=0)
    mod = _ada_mod(s_in, ada_w, ada_b).reshape(depth, 16, N_MOD, d)
    mod = jnp.pad(mod, ((0, 0), (0, 0), (0, 8 - N_MOD), (0, 0)))
    modt = jnp.stack([jnp.broadcast_to(mod[:, b][:, None], (depth, b, 8, d)), mod[:, :b]], axis=2)
    tables = _ret_tables(ctx_len, seq)
    wr = jnp.pad(router_w, ((0, 0), (0, 0), (0, LOGIT_LANES - N_EXPERTS)))
    wr_hi = wr.astype(BF16)
    wr = jnp.stack([wr_hi, (wr - wr_hi.astype(F32)).astype(BF16)], axis=1)
    br = jnp.pad(router_b, ((0, 0), (0, LOGIT_LANES - N_EXPERTS))).reshape(depth, 1, LOGIT_LANES)
    for i in range(depth):
        s5p = _s5_params(ssm_lam_re[i], ssm_lam_im[i], ssm_log_dt[i], ssm_b_re[i], ssm_b_im[i],
                         ssm_c_re[i], ssm_c_im[i])
        conv_p = (conv_dw_w[i], conv_dw_b[i], conv_ln_g[i], conv_ln_b[i])
        merge_p = (norm2_g[i], ssm_glu_w[i].astype(BF16), ssm_glu_b[i], conv_pw_w[i].astype(BF16),
                   conv_pw_b[i], ret_w_o[i].astype(BF16), w_out[i].astype(BF16), wr[i], br[i])
        x1, h2, top, top_gate, counts = _mixer_layer(xs, modt[i], i, ctx_len, tables, norm1_g[i],
                                                     w_in[i].astype(BF16), s5p, ssm_d[i], conv_p, merge_p)
        xs = _moe(x1, h2, top, top_gate, counts, modt[i], final_g, exp_gu_w, exp_gu_b, exp_down_w,
                  exp_down_b, i, ctx_len, final=(i == depth - 1))
    return xs
```

```python
import functools
import math

import numpy as np
import jax
import jax.numpy as jnp
from jax import lax
from jax.experimental import pallas as pl
from jax.experimental.pallas import tpu as pltpu

F32 = jnp.float32
BF16 = jnp.bfloat16

D_MODEL = 1024
DEPTH = 4
GRID_W = 64
N_MOD = 6
SSM_WIDTH = 256
SSM_GROUP = 16
SSM_GROUPS = SSM_WIDTH // SSM_GROUP
SSM_STATE = 64
SSM_LANES = SSM_GROUPS * SSM_STATE
CONV_WIDTH = 256
CONV_K = 31
CONV_PAD = 16
RET_HEADS = 4
RET_HEAD_DIM = 128
RET_WIDTH = RET_HEADS * RET_HEAD_DIM
RET_CHUNK = 128
ROPE_BASE = 10000.0
IN_WIDTH = SSM_WIDTH + 2 * CONV_WIDTH + 4 * RET_WIDTH + 3 * D_MODEL
COL_CV = SSM_WIDTH
COL_QKVG = COL_CV + 2 * CONV_WIDTH
COL_GATES = COL_QKVG + 4 * RET_WIDTH
N_EXPERTS = 32
TOP_K = 4
EXPERT_FF = D_MODEL
SWIGLU_LIMIT = 7.0
SWIGLU_ALPHA = 1.702
EPS = 1e-6

ROW_TILE = 256
HALF_TILE = 128
S5_STEPS = 64
S5_STRIP = 512
MOE_ROWS = 512
MOE_PART = 256
LOGIT_LANES = 128
VMEM_LIMIT = 56 * 1024 * 1024


def _cparams(sem):
    return pltpu.CompilerParams(dimension_semantics=sem, vmem_limit_bytes=VMEM_LIMIT)


def _rms(x):
    return x * lax.rsqrt(jnp.mean(x * x, axis=-1, keepdims=True) + EPS)


def _ada_kernel(s_ref, w_ref, b_ref, o_ref):
    s = s_ref[...]
    s = s * jax.nn.sigmoid(s)
    o_ref[0] = jnp.dot(s, w_ref[0], preferred_element_type=F32,
                       precision=lax.Precision.HIGHEST) + b_ref[0]


def _ada_mod(s_in, ada_w, ada_b):
    depth, d, n = ada_w.shape
    tn = 1536
    return pl.pallas_call(
        _ada_kernel,
        grid=(depth, n // tn),
        in_specs=[pl.BlockSpec((16, d), lambda i, j: (0, 0)),
                  pl.BlockSpec((1, d, tn), lambda i, j: (i, 0, j)),
                  pl.BlockSpec((1, 1, tn), lambda i, j: (i, 0, j))],
        out_specs=pl.BlockSpec((1, 16, tn), lambda i, j: (i, 0, j)),
        out_shape=jax.ShapeDtypeStruct((depth, 16, n), F32),
        compiler_params=_cparams(("arbitrary", "arbitrary")),
        name="ada_mod",
    )(s_in, ada_w, ada_b.reshape(depth, 1, n))


def _inproj_kernel(x_ref, mod_ref, g_ref, w_ref, u_ref, cv_ref, qkvg_ref, gates_ref):
    x = x_ref[0]
    mod = mod_ref[0, 0]
    h = (_rms(x) * g_ref[...]) * (1.0 + mod[1:2]) + mod[0:1]
    hb = h.astype(BF16)
    u_ref[0] = jnp.dot(hb, w_ref[:, 0:COL_CV], preferred_element_type=F32)
    cv_ref[0] = jnp.dot(hb, w_ref[:, COL_CV:COL_QKVG], preferred_element_type=F32)
    qkvg_ref[0] = jnp.dot(hb, w_ref[:, COL_QKVG:COL_GATES], preferred_element_type=F32).astype(BF16)
    gates = jnp.dot(hb, w_ref[:, COL_GATES:IN_WIDTH], preferred_element_type=F32)
    gates_ref[0] = jax.nn.sigmoid(gates).astype(BF16)


def _in_proj(x, modt, g1, w_in_bf, ctx_len):
    b, s_tot, d = x.shape
    nt = s_tot // ROW_TILE
    nctx = ctx_len // ROW_TILE
    seg = lambda j: jnp.where(j >= nctx, 1, 0)
    return pl.pallas_call(
        _inproj_kernel,
        grid=(b, nt),
        in_specs=[pl.BlockSpec((1, ROW_TILE, d), lambda i, j: (i, j, 0)),
                  pl.BlockSpec((1, 1, 8, d), lambda i, j: (i, seg(j), 0, 0)),
                  pl.BlockSpec((1, d), lambda i, j: (0, 0)),
                  pl.BlockSpec((d, IN_WIDTH), lambda i, j: (0, 0))],
        out_specs=[pl.BlockSpec((1, ROW_TILE, SSM_WIDTH), lambda i, j: (i, j, 0)),
                   pl.BlockSpec((1, ROW_TILE, 2 * CONV_WIDTH), lambda i, j: (i, j, 0)),
                   pl.BlockSpec((1, ROW_TILE, 4 * RET_WIDTH), lambda i, j: (i, j, 0)),
                   pl.BlockSpec((1, ROW_TILE, 3 * D_MODEL), lambda i, j: (i, j, 0))],
        out_shape=[jax.ShapeDtypeStruct((b, s_tot, SSM_WIDTH), F32),
                   jax.ShapeDtypeStruct((b, s_tot, 2 * CONV_WIDTH), F32),
                   jax.ShapeDtypeStruct((b, s_tot, 4 * RET_WIDTH), BF16),
                   jax.ShapeDtypeStruct((b, s_tot, 3 * D_MODEL), BF16)],
        compiler_params=_cparams(("arbitrary", "arbitrary")),
        name="in_proj",
    )(x, modt, g1.reshape(1, d), w_in_bf)


def _s5_params(lam_re, lam_im, log_dt, b_re, b_im, c_re, c_im):
    lam = lax.complex(jnp.minimum(lam_re.astype(F32), -1e-4), lam_im.astype(F32))
    lam_dt = lam * jnp.exp(log_dt.astype(F32))[..., None]
    lam_bar = jnp.exp(lam_dt)
    b = lax.complex(b_re.astype(F32), b_im.astype(F32))
    b_bar = ((lam_bar - 1.0) / lam)[..., None] * b
    eye = jnp.eye(SSM_GROUPS, dtype=F32)

    def in_mat(t):
        return jnp.einsum('dgpm,gh->dgmhp', t, eye).reshape(2, SSM_WIDTH, SSM_LANES)

    def out_mat(t):
        return jnp.einsum('dgmp,gh->dgphm', t, eye).reshape(2, SSM_LANES, SSM_WIDTH)

    bmat = jnp.concatenate([in_mat(jnp.real(b_bar)), in_mat(jnp.imag(b_bar))], axis=-1)
    cmat = jnp.concatenate([out_mat(c_re.astype(F32)), -out_mat(c_im.astype(F32))], axis=1)
    lvec = jnp.concatenate([jnp.real(lam_bar).reshape(2, 1, SSM_LANES),
                            jnp.imag(lam_bar).reshape(2, 1, SSM_LANES)], axis=-1)
    return bmat.astype(BF16), cmat.astype(BF16), lvec


def _s5_kernel(u_ref, perm_ref, permt_ref, bmat_ref, cmat_ref, lam_ref, dsk_ref, y_ref, st_ref, carry_ref):
    d = pl.program_id(0)
    c = pl.program_id(1)
    nb, steps, width = u_ref.shape

    @pl.when(c == 0)
    def _():
        carry_ref[...] = jnp.zeros_like(carry_ref)

    u = u_ref[...].reshape(nb * steps, width)
    u_tm = jnp.dot(perm_ref[...], u.astype(BF16), preferred_element_type=F32).astype(BF16)
    bu = jnp.dot(u_tm, bmat_ref[0], preferred_element_type=F32)
    st_ref[...] = bu.reshape(steps, nb, 2 * SSM_LANES)

    for k in range(SSM_LANES // S5_STRIP):
        re = pl.ds(k * S5_STRIP, S5_STRIP)
        im = pl.ds(SSM_LANES + k * S5_STRIP, S5_STRIP)
        lr = jnp.broadcast_to(lam_ref[0, :, re], (nb, S5_STRIP))
        li = jnp.broadcast_to(lam_ref[0, :, im], (nb, S5_STRIP))

        def step(t, s):
            sr, si = s
            tt = jnp.where(d == 1, steps - 1 - t, t)
            nr = lr * sr - li * si + st_ref[tt, :, re]
            ni = lr * si + li * sr + st_ref[tt, :, im]
            st_ref[tt, :, re] = nr
            st_ref[tt, :, im] = ni
            return nr, ni

        sr, si = lax.fori_loop(0, steps, step, (carry_ref[:, re], carry_ref[:, im]), unroll=2)
        carry_ref[:, re] = sr
        carry_ref[:, im] = si

    st = st_ref[...].reshape(steps * nb, 2 * SSM_LANES)
    y_tm = jnp.dot(st.astype(BF16), cmat_ref[0], preferred_element_type=F32)
    y = jnp.dot(permt_ref[...], y_tm.astype(BF16), preferred_element_type=F32)
    skip = jnp.where(d == 0, 1.0, 0.0) * dsk_ref[...]
    y_ref[0] = (y + u * skip).reshape(nb, steps, width)


def _s5_scan(u, bmat, cmat, lvec, d_skip, ctx_len):
    nb, s_tot, width = u.shape
    nch = s_tot // S5_STEPS
    nctx = ctx_len // S5_STEPS
    rows = nb * S5_STEPS
    r = np.arange(rows)
    perm = np.zeros((rows, rows), np.float32)
    perm[r, (r % nb) * S5_STEPS + r // nb] = 1.0
    perm_tm = jnp.asarray(perm, BF16)
    perm_bm = jnp.asarray(perm.T, BF16)

    def chunk(d, c):
        back = jnp.where(c < nctx, nctx - 1 - c, nctx + nch - 1 - c)
        return jnp.where(d == 1, back, c)

    return pl.pallas_call(
        _s5_kernel,
        grid=(2, nch),
        in_specs=[pl.BlockSpec((nb, S5_STEPS, width), lambda d, c: (0, chunk(d, c), 0)),
                  pl.BlockSpec((rows, rows), lambda d, c: (0, 0)),
                  pl.BlockSpec((rows, rows), lambda d, c: (0, 0)),
                  pl.BlockSpec((1, width, 2 * SSM_LANES), lambda d, c: (d, 0, 0)),
                  pl.BlockSpec((1, 2 * SSM_LANES, width), lambda d, c: (d, 0, 0)),
                  pl.BlockSpec((1, 1, 2 * SSM_LANES), lambda d, c: (d, 0, 0)),
                  pl.BlockSpec((1, width), lambda d, c: (0, 0))],
        out_specs=pl.BlockSpec((1, nb, S5_STEPS, width), lambda d, c: (d, 0, chunk(d, c), 0)),
        out_shape=jax.ShapeDtypeStruct((2, nb, s_tot, width), F32),
        scratch_shapes=[pltpu.VMEM((S5_STEPS, nb, 2 * SSM_LANES), F32),
                        pltpu.VMEM((nb, 2 * SSM_LANES), F32)],
        compiler_params=_cparams(("arbitrary", "arbitrary")),
        name="s5_scan",
    )(u, perm_tm, perm_bm, bmat, cmat, lvec, d_skip.reshape(1, width))


def _conv_kernel(cv_ref, w_ref, b_ref, lg_ref, lb_ref, o_ref, hp_ref, win_ref, shift_ref, *, ctx_len):
    s_tot = cv_ref.shape[1]
    lat = s_tot - ctx_len
    rc = RET_CHUNK
    zeros = jnp.zeros((CONV_PAD, CONV_WIDTH), F32)
    a = cv_ref[0, :, 0:CONV_WIDTH]
    g = cv_ref[0, :, CONV_WIDTH:2 * CONV_WIDTH]
    h = a * jax.nn.sigmoid(g)
    hp_ref[0:CONV_PAD] = zeros
    hp_ref[CONV_PAD:CONV_PAD + ctx_len] = h[0:ctx_len]
    hp_ref[CONV_PAD + ctx_len:2 * CONV_PAD + ctx_len] = zeros
    hp_ref[2 * CONV_PAD + ctx_len:2 * CONV_PAD + s_tot] = h[ctx_len:s_tot]
    hp_ref[2 * CONV_PAD + s_tot:3 * CONV_PAD + s_tot] = zeros
    del lat

    def chunk(c, carry):
        r0 = pl.multiple_of(c * rc, rc)
        wbase = pl.multiple_of(r0 + jnp.where(r0 >= ctx_len, CONV_PAD, 0), 8)
        win_ref[...] = hp_ref[pl.ds(wbase, rc + 2 * CONV_PAD), :]
        for r in range(1, 8):
            shift_ref[r] = win_ref[r:r + rc + 2 * CONV_PAD - 8, :]
        acc = jnp.zeros((rc, CONV_WIDTH), F32) + b_ref[...]
        for k in range(CONV_K):
            off = k + CONV_PAD - CONV_K // 2
            base = off - off % 8
            if off % 8 == 0:
                tap = win_ref[base:base + rc, :]
            else:
                tap = shift_ref[off % 8, base:base + rc, :]
            acc = acc + w_ref[k:k + 1, :] * tap
        mu = jnp.mean(acc, axis=-1, keepdims=True)
        var = jnp.mean(jnp.square(acc - mu), axis=-1, keepdims=True)
        y = (acc - mu) * lax.rsqrt(var + EPS) * lg_ref[...] + lb_ref[...]
        o_ref[0, pl.ds(r0, rc), :] = y * jax.nn.sigmoid(y)
        return carry

    lax.fori_loop(0, s_tot // rc, chunk, 0)


def _conv_branch(cv, w_dw, b_dw, ln_g, ln_b, ctx_len):
    b, s_tot, _ = cv.shape
    w = jnp.concatenate([w_dw.reshape(CONV_K, CONV_WIDTH), jnp.zeros((1, CONV_WIDTH), F32)], axis=0)
    vec = lambda t: t.reshape(1, CONV_WIDTH)
    cst = lambda shape: pl.BlockSpec(shape, lambda i: (0,) * len(shape))
    return pl.pallas_call(
        functools.partial(_conv_kernel, ctx_len=ctx_len),
        grid=(b,),
        in_specs=[pl.BlockSpec((1, s_tot, 2 * CONV_WIDTH), lambda i: (i, 0, 0)),
                  cst((CONV_K + 1, CONV_WIDTH)), cst((1, CONV_WIDTH)),
                  cst((1, CONV_WIDTH)), cst((1, CONV_WIDTH))],
        out_specs=pl.BlockSpec((1, s_tot, CONV_WIDTH), lambda i: (i, 0, 0)),
        out_shape=jax.ShapeDtypeStruct((b, s_tot, CONV_WIDTH), F32),
        scratch_shapes=[pltpu.VMEM((s_tot + 3 * CONV_PAD, CONV_WIDTH), F32),
                        pltpu.VMEM((RET_CHUNK + 2 * CONV_PAD, CONV_WIDTH), F32),
                        pltpu.VMEM((8, RET_CHUNK + 2 * CONV_PAD - 8, CONV_WIDTH), F32)],
        compiler_params=_cparams(("arbitrary",)),
        name="conv_branch",
    )(cv, w, vec(b_dw), vec(ln_g), vec(ln_b))


N_TAB = 5


def _ret_tables(ctx_len, seq):
    n = RET_HEAD_DIM // 4
    inv_freq = ROPE_BASE ** (-np.arange(n, dtype=np.float64) / n)
    pos = np.arange(seq)
    ang_r = (pos // GRID_W)[:, None] * inv_freq
    ang_c = (pos % GRID_W)[:, None] * inv_freq
    cos = np.concatenate([np.cos(ang_r), np.cos(ang_r), np.cos(ang_c), np.cos(ang_c)], axis=-1)
    sin = np.concatenate([-np.sin(ang_r), np.sin(ang_r), -np.sin(ang_c), np.sin(ang_c)], axis=-1)
    cos = np.concatenate([np.ones((ctx_len, RET_HEAD_DIM)), cos], axis=0)
    sin = np.concatenate([np.zeros((ctx_len, RET_HEAD_DIM)), sin], axis=0)
    log_g_fwd = np.log1p(-np.exp2(-5.0 - np.arange(RET_HEADS, dtype=np.float64)))
    idx = np.arange(RET_CHUNK, dtype=np.float64)
    diff = idx[:, None] - idx[None, :]
    tabs = np.zeros((2, RET_HEADS, N_TAB, RET_CHUNK, RET_CHUNK))
    ones = np.ones((RET_CHUNK, RET_CHUNK))
    for d, log_g in enumerate((log_g_fwd, log_g_fwd[::-1])):
        for hd in range(RET_HEADS):
            lg = log_g[hd]
            if d == 0:
                mask = np.where(diff >= 0, np.exp(lg * np.maximum(diff, 0.0)), 0.0)
                dec = np.exp(lg * (idx + 1.0))
                zeta = np.exp(lg * (RET_CHUNK - 1 - idx))
            else:
                mask = np.where(diff <= 0, np.exp(lg * np.maximum(-diff, 0.0)), 0.0)
                dec = np.exp(lg * (RET_CHUNK - idx))
                zeta = np.exp(lg * idx)
            tabs[d, hd, 0] = mask
            tabs[d, hd, 1] = dec[:, None] * ones
            tabs[d, hd, 2] = zeta[:, None] * ones
            tabs[d, hd, 3] = np.exp(lg * RET_CHUNK) * ones
            tabs[d, hd, 4] = zeta[None, :] * ones
    return (jnp.asarray(cos, F32), jnp.asarray(sin, F32), jnp.asarray(tabs, F32))


def _ret_kernel(q_ref, k_ref, v_ref, g_ref, cos_ref, sin_ref, tab_ref, o_ref,
                qs_ref, qd_ref, kt_ref, kz_ref, out_ref, state_ref, *, ctx_len):
    s_tot = q_ref.shape[1]
    rc = RET_CHUNK
    nch = s_tot // rc
    nctx = ctx_len // rc
    k_scale = RET_HEAD_DIM ** -0.5
    lane = lax.broadcasted_iota(jnp.int32, (rc, RET_HEAD_DIM), 1)
    first_half = (lane % (RET_HEAD_DIM // 2)) < (RET_HEAD_DIM // 4)

    def rope(t, cs, sn):
        quarter = RET_HEAD_DIM // 4
        partner = jnp.where(first_half, pltpu.roll(t, RET_HEAD_DIM - quarter, 1), pltpu.roll(t, quarter, 1))
        return t * cs + partner * sn

    def prepare(c, carry):
        rows = pl.ds(pl.multiple_of(c * rc, rc), rc)
        cs = cos_ref[rows, :]
        sn = sin_ref[rows, :]
        q = rope(q_ref[0, rows, :].astype(F32), cs, sn)
        kt = (rope(k_ref[0, rows, :].astype(F32), cs, sn) * k_scale).T
        qs_ref[c] = q.astype(BF16)
        kt_ref[c] = kt.astype(BF16)
        for d in range(2):
            qd_ref[d, c] = (q * tab_ref[d, 0, 1]).astype(BF16)
            kz_ref[d, c] = (kt * tab_ref[d, 0, 4]).astype(BF16)
        return carry

    lax.fori_loop(0, nch, prepare, 0, unroll=2)
    state_ref[...] = jnp.zeros_like(state_ref)

    def chunk(c, d):
        rows = pl.ds(pl.multiple_of(c * rc, rc), rc)
        vb = v_ref[0, rows, :]
        state = state_ref[d]
        scores = jnp.dot(qs_ref[c], kt_ref[c], preferred_element_type=F32) * tab_ref[d, 0, 0]
        o = (jnp.dot(scores.astype(BF16), vb, preferred_element_type=F32)
             + jnp.dot(qd_ref[d, c], state.astype(BF16), preferred_element_type=F32))
        kv = jnp.dot(kz_ref[d, c], vb, preferred_element_type=F32)
        state_ref[d] = tab_ref[d, 0, 3] * state + kv
        out_ref[d, rows, :] = o

    def step(j, carry):
        chunk(j, 0)
        chunk(jnp.where(j < nctx, nctx - 1 - j, nctx + nch - 1 - j), 1)
        return carry

    lax.fori_loop(0, nch, step, 0, unroll=2)
    g = g_ref[0].astype(F32)
    o_ref[0] = (_rms(out_ref[0] + out_ref[1]) * (g * jax.nn.sigmoid(g))).astype(o_ref.dtype)


def _retention(qkvg, cos, sin, tabs, ctx_len):
    b, s_tot, _ = qkvg.shape
    hd = RET_HEAD_DIM
    nch = s_tot // RET_CHUNK
    part = lambda p: pl.BlockSpec((1, s_tot, hd), lambda i, h: (i, 0, p * RET_HEADS + h))
    return pl.pallas_call(
        functools.partial(_ret_kernel, ctx_len=ctx_len),
        grid=(b, RET_HEADS),
        in_specs=[part(0), part(1), part(2), part(3),
                  pl.BlockSpec((s_tot, hd), lambda i, h: (0, 0)),
                  pl.BlockSpec((s_tot, hd), lambda i, h: (0, 0)),
                  pl.BlockSpec((2, 1, N_TAB, RET_CHUNK, RET_CHUNK), lambda i, h: (0, h, 0, 0, 0))],
        out_specs=pl.BlockSpec((1, s_tot, hd), lambda i, h: (i, 0, h)),
        out_shape=jax.ShapeDtypeStruct((b, s_tot, RET_WIDTH), BF16),
        scratch_shapes=[pltpu.VMEM((nch, RET_CHUNK, hd), BF16),
                        pltpu.VMEM((2, nch, RET_CHUNK, hd), BF16),
                        pltpu.VMEM((nch, hd, RET_CHUNK), BF16),
                        pltpu.VMEM((2, nch, hd, RET_CHUNK), BF16),
                        pltpu.VMEM((2, s_tot, hd), F32),
                        pltpu.VMEM((2, hd, hd), F32)],
        compiler_params=_cparams(("arbitrary", "arbitrary")),
        name="retention",
    )(qkvg, qkvg, qkvg, qkvg, cos, sin, tabs)


def _merge_kernel(x_ref, yf_ref, yb_ref, hc_ref, ro_ref, gates_ref, mod_ref, g2_ref,
                  wglu_ref, bglu_ref, wpw_ref, bpw_ref, wo_ref, wout_ref, wr_ref, br_ref, tril_ref,
                  x1_ref, h2_ref, ti_ref, tg_ref, cnt_out_ref, cnt_ref, lg_ref):
    d = D_MODEL
    mod = mod_ref[0, 0]

    @pl.when(jnp.logical_and(pl.program_id(0) == 0, pl.program_id(1) == 0))
    def _():
        cnt_ref[...] = jnp.zeros_like(cnt_ref)

    for half in range(ROW_TILE // HALF_TILE):
        rows = pl.ds(half * HALF_TILE, HALF_TILE)
        ys = jax.nn.gelu(yf_ref[0, 0, rows, :] + yb_ref[0, 0, rows, :]).astype(BF16)
        z = jnp.dot(ys, wglu_ref[...], preferred_element_type=F32) + bglu_ref[...]
        y_ssm = z[:, 0:d] * jax.nn.sigmoid(z[:, d:2 * d])
        y_conv = jnp.dot(hc_ref[0, rows, :].astype(BF16), wpw_ref[...],
                         preferred_element_type=F32) + bpw_ref[...]
        y_ret = jnp.dot(ro_ref[0, rows, :].astype(BF16), wo_ref[...], preferred_element_type=F32)
        m = (gates_ref[0, rows, 0:d].astype(F32) * y_ssm
             + gates_ref[0, rows, d:2 * d].astype(F32) * y_conv
             + gates_ref[0, rows, 2 * d:3 * d].astype(F32) * y_ret)
        y = jnp.dot(m.astype(BF16), wout_ref[...], preferred_element_type=F32)
        x1 = x_ref[0, rows, :] + mod[2:3] * y
        x1_ref[0, rows, :] = x1
        h2 = (_rms(x1) * g2_ref[...]) * (1.0 + mod[4:5]) + mod[3:4]
        h2_ref[0, rows, :] = h2
        h_hi = h2.astype(BF16)
        h_lo = (h2 - h_hi.astype(F32)).astype(BF16)
        lg_ref[rows, :] = (jnp.dot(h_hi, wr_ref[0], preferred_element_type=F32)
                           + jnp.dot(h_lo, wr_ref[0], preferred_element_type=F32)
                           + jnp.dot(h_hi, wr_ref[1], preferred_element_type=F32)) + br_ref[...]
    logits = lg_ref[...]
    lane = lax.broadcasted_iota(jnp.int32, logits.shape, 1)
    lane_f = lane.astype(F32)
    neg = jnp.float32(-jnp.inf)
    l = jnp.where(lane < N_EXPERTS, logits, neg)
    ti = jnp.zeros(logits.shape, F32)
    vals = []
    picks = []
    for k in range(TOP_K):
        top = jnp.max(l, axis=-1, keepdims=True)
        idx = jnp.min(jnp.where(l == top, lane_f, float(LOGIT_LANES)), axis=-1, keepdims=True)
        ti = jnp.where(lane == k, idx, ti)
        vals.append(top)
        picked = lane_f == idx
        picks.append(picked.astype(F32))
        l = jnp.where(picked, neg, l)
    es = [jnp.exp(v - vals[0]) for v in vals]
    tot = es[0]
    for e in es[1:]:
        tot = tot + e
    tg = jnp.zeros(logits.shape, F32)
    for k in range(TOP_K):
        tg = jnp.where(lane == k, es[k] / tot, tg)
    seen = cnt_ref[0:1, :]
    chosen = picks[0] + picks[1] + picks[2] + picks[3]
    before = seen + jnp.dot(tril_ref[...], chosen.astype(BF16), preferred_element_type=F32)
    for k in range(TOP_K):
        rank = jnp.sum(picks[k] * before, axis=-1, keepdims=True)
        ti = jnp.where(lane == TOP_K + k, rank, ti)
    seen = seen + jnp.sum(chosen, axis=0, keepdims=True)
    ti_ref[0] = ti.astype(jnp.int32)
    tg_ref[0] = tg
    cnt_ref[...] = jnp.broadcast_to(seen, cnt_ref.shape)
    cnt_out_ref[...] = jnp.broadcast_to(seen, cnt_out_ref.shape)


def _merge(x, y_s5, hc, ro, gates, modt, g2, wglu, bglu, wpw, bpw, wo, wout, wr, br, ctx_len):
    b, s_tot, d = x.shape
    nt = s_tot // ROW_TILE
    nctx = ctx_len // ROW_TILE
    seg = lambda j: jnp.where(j >= nctx, 1, 0)
    row = lambda w: pl.BlockSpec((1, ROW_TILE, w), lambda i, j: (i, j, 0))
    cst = lambda shape: pl.BlockSpec(shape, lambda i, j: (0,) * len(shape))
    y2 = y_s5
    tril = jnp.asarray(np.tril(np.ones((ROW_TILE, ROW_TILE), np.float32), -1), BF16)
    return pl.pallas_call(
        _merge_kernel,
        grid=(b, nt),
        in_specs=[row(d),
                  pl.BlockSpec((1, 1, ROW_TILE, SSM_WIDTH), lambda i, j: (0, i, j, 0)),
                  pl.BlockSpec((1, 1, ROW_TILE, SSM_WIDTH), lambda i, j: (1, i, j, 0)),
                  row(CONV_WIDTH), row(RET_WIDTH), row(3 * d),
                  pl.BlockSpec((1, 1, 8, d), lambda i, j: (i, seg(j), 0, 0)),
                  cst((1, d)),
                  cst((SSM_WIDTH, 2 * d)), cst((1, 2 * d)),
                  cst((CONV_WIDTH, d)), cst((1, d)),
                  cst((RET_WIDTH, d)), cst((d, d)),
                  cst((2, d, LOGIT_LANES)), cst((1, LOGIT_LANES)),
                  cst((ROW_TILE, ROW_TILE))],
        out_specs=[row(d), row(d), row(LOGIT_LANES), row(LOGIT_LANES), cst((8, LOGIT_LANES))],
        out_shape=[jax.ShapeDtypeStruct((b, s_tot, d), F32),
                   jax.ShapeDtypeStruct((b, s_tot, d), F32),
                   jax.ShapeDtypeStruct((b, s_tot, LOGIT_LANES), jnp.int32),
                   jax.ShapeDtypeStruct((b, s_tot, LOGIT_LANES), F32),
                   jax.ShapeDtypeStruct((8, LOGIT_LANES), F32)],
        scratch_shapes=[pltpu.VMEM((8, LOGIT_LANES), F32), pltpu.VMEM((ROW_TILE, LOGIT_LANES), F32)],
        compiler_params=_cparams(("arbitrary", "arbitrary")),
        name="merge_router",
    )(x, y2, y2, hc, ro, gates, modt, g2.reshape(1, d), wglu, bglu.reshape(1, 2 * d),
      wpw, bpw.reshape(1, d), wo, wout, wr, br, tril)


PLAN_TILE, PLAN_EXPERT, PLAN_LO, PLAN_HI, PLAN_NEXT, PLAN_SLOT = range(6)
PLAN_LANES = 256


def _plan_kernel(cnt_ref, plan_ref, *, n_tiles):
    ne = N_EXPERTS
    rows_per = float(MOE_ROWS)
    e = lax.broadcasted_iota(jnp.int32, (ne, PLAN_LANES), 0)
    e_f = e.astype(F32)
    v = lax.broadcasted_iota(jnp.int32, (ne, PLAN_LANES), 1).astype(F32)

    def cumsum(t):
        for s in (1, 2, 4, 8, 16):
            t = t + jnp.where(e >= s, pltpu.roll(t, s, 0), 0.0)
        return t

    def lookup(table, onehot):
        return jnp.sum(onehot * table, axis=0, keepdims=True)

    c = jnp.broadcast_to(cnt_ref[...], (ne, PLAN_LANES))
    has = c > 0.0
    ends = cumsum(c)
    starts = ends - c
    first_tile = jnp.floor(starts / rows_per)
    tiles_per = jnp.where(has, jnp.floor((ends - 1.0) / rows_per) - first_tile + 1.0, 0.0)
    vend = cumsum(tiles_per)
    vstart = vend - tiles_per
    total = jnp.max(vend, axis=0, keepdims=True)
    slot = cumsum(has.astype(F32)) - 1.0
    slot = slot - 2.0 * jnp.floor(slot / 2.0)
    later = jnp.where(has, e_f, float(ne))
    nxt = jnp.where(e < ne - 1, pltpu.roll(later, ne - 1, 0), float(ne))
    for s in (1, 2, 4, 8, 16):
        nxt = jnp.minimum(nxt, jnp.where(e < ne - s, pltpu.roll(nxt, ne - s, 0), float(ne)))
    nxt = jnp.where(nxt >= float(ne), -1.0, nxt)
    last_with_rows = jnp.max(jnp.where(has, e_f, 0.0), axis=0, keepdims=True)

    valid = v[0:1] < total
    ve = jnp.sum((v >= vend).astype(F32), axis=0, keepdims=True)
    ve = jnp.where(valid, jnp.minimum(ve, float(ne - 1)), last_with_rows)
    onehot = (e_f == ve).astype(F32)
    vt = jnp.where(valid, lookup(first_tile, onehot) + v[0:1] - lookup(vstart, onehot),
                   float(n_tiles - 1))
    lo = jnp.where(valid, jnp.clip(lookup(starts, onehot) - vt * rows_per, 0.0, rows_per), 0.0)
    hi = jnp.where(valid, jnp.clip(lookup(ends, onehot) - vt * rows_per, 0.0, rows_per), 0.0)
    fields = {PLAN_TILE: vt, PLAN_EXPERT: ve, PLAN_LO: lo, PLAN_HI: hi,
              PLAN_NEXT: lookup(nxt, onehot), PLAN_SLOT: lookup(slot, onehot)}
    row = lax.broadcasted_iota(jnp.int32, (8, PLAN_LANES), 0)
    plan = jnp.zeros((8, PLAN_LANES), F32)
    for r, val in fields.items():
        plan = jnp.where(row == r, val, plan)
    plan_ref[...] = plan.astype(jnp.int32)


def _plan(counts_col, n_tiles):
    return pl.pallas_call(
        functools.partial(_plan_kernel, n_tiles=n_tiles),
        out_shape=jax.ShapeDtypeStruct((8, PLAN_LANES), jnp.int32),
        name="moe_plan",
    )(counts_col)


def _expert_kernel(plan_ref, x_ref, wgu_hbm, bgu_ref, wd_hbm, bd_ref, y_ref,
                   wgu_f32, wd_f32, wgu_bf, wd_bf, sem, *, layer):
    v = pl.program_id(0)
    prev = jnp.maximum(v - 1, 0)
    lo = plan_ref[PLAN_LO, v]
    hi = plan_ref[PLAN_HI, v]
    expert = plan_ref[PLAN_EXPERT, v]
    next_expert = plan_ref[PLAN_NEXT, v]
    active = hi > lo
    fresh_weights = jnp.logical_or(v == 0, expert != plan_ref[PLAN_EXPERT, prev])
    first_visit = jnp.logical_or(v == 0, plan_ref[PLAN_TILE, v] != plan_ref[PLAN_TILE, prev])

    def weight_copies(expert, slot):
        return (pltpu.make_async_copy(wgu_hbm.at[layer, expert], wgu_f32.at[slot], sem.at[0, slot]),
                pltpu.make_async_copy(wd_hbm.at[layer, expert], wd_f32.at[slot], sem.at[1, slot]))

    @pl.when(jnp.logical_and(fresh_weights, active))
    def _():
        slot = plan_ref[PLAN_SLOT, v]

        @pl.when(v == 0)
        def _():
            for c in weight_copies(expert, slot):
                c.start()

        for c in weight_copies(expert, slot):
            c.wait()

        @pl.when(next_expert >= 0)
        def _():
            for c in weight_copies(next_expert, 1 - slot):
                c.start()

        wgu_bf[...] = wgu_f32[slot].astype(BF16)
        wd_bf[...] = wd_f32[slot].astype(BF16)

    whole = jnp.logical_and(lo == 0, hi == MOE_ROWS)

    def ffn(rows):
        f = EXPERT_FF
        gu = jnp.dot(x_ref[rows, :].astype(BF16), wgu_bf[...],
                     preferred_element_type=F32) + bgu_ref[0, 0]
        gate = jnp.minimum(gu[:, 0:f], SWIGLU_LIMIT)
        up = jnp.clip(gu[:, f:2 * f], -SWIGLU_LIMIT, SWIGLU_LIMIT)
        act = (up + 1.0) * gate * jax.nn.sigmoid(gate * SWIGLU_ALPHA)
        y = jnp.dot(act.astype(BF16), wd_bf[...], preferred_element_type=F32) + bd_ref[0, 0]
        return y.astype(y_ref.dtype)

    parts = [pl.ds(p * MOE_PART, MOE_PART) for p in range(MOE_ROWS // MOE_PART)]

    @pl.when(whole)
    def _():
        for rows in parts:
            y_ref[rows, :] = ffn(rows)

    @pl.when(jnp.logical_and(first_visit, jnp.logical_not(whole)))
    def _():
        y_ref[...] = jnp.zeros_like(y_ref)

    for p, rows in enumerate(parts):
        touched = jnp.logical_and(lo < (p + 1) * MOE_PART, hi > p * MOE_PART)

        @pl.when(jnp.logical_and(touched, jnp.logical_not(whole)))
        def _():
            row = p * MOE_PART + lax.broadcasted_iota(jnp.int32, (MOE_PART, 1), 0)
            mine = jnp.logical_and(row >= lo, row < hi)
            y_ref[rows, :] = jnp.where(mine, ffn(rows), y_ref[rows, :])


def _experts(xs, plan, w_gu, b_gu, w_down, b_down, layer):
    n_rows, d = xs.shape
    n_visits = n_rows // MOE_ROWS + N_EXPERTS - 1
    assert n_visits <= PLAN_LANES
    f = EXPERT_FF
    wmap = lambda v, plan: (layer, plan[PLAN_EXPERT, v], 0, 0)
    tmap = lambda v, plan: (plan[PLAN_TILE, v], 0)
    grid_spec = pltpu.PrefetchScalarGridSpec(
        num_scalar_prefetch=1,
        grid=(n_visits,),
        in_specs=[pl.BlockSpec((MOE_ROWS, d), tmap),
                  pl.BlockSpec(memory_space=pl.ANY),
                  pl.BlockSpec((1, 1, 1, 2 * f), wmap),
                  pl.BlockSpec(memory_space=pl.ANY),
                  pl.BlockSpec((1, 1, 1, d), wmap)],
        out_specs=pl.BlockSpec((MOE_ROWS, d), tmap),
        scratch_shapes=[pltpu.VMEM((2, d, 2 * f), F32), pltpu.VMEM((2, f, d), F32),
                        pltpu.VMEM((d, 2 * f), BF16), pltpu.VMEM((f, d), BF16),
                        pltpu.SemaphoreType.DMA((2, 2))],
    )
    return pl.pallas_call(
        functools.partial(_expert_kernel, layer=layer),
        grid_spec=grid_spec,
        out_shape=jax.ShapeDtypeStruct((n_rows, d), BF16),
        compiler_params=_cparams(("arbitrary",)),
        name="experts",
    )(plan, xs, w_gu, b_gu.reshape(DEPTH, N_EXPERTS, 1, 2 * f),
      w_down, b_down.reshape(DEPTH, N_EXPERTS, 1, d))


def _combine_kernel(x_ref, y0_ref, y1_ref, y2_ref, y3_ref, gt_ref, mod_ref, fg_ref, o_ref, *, final):
    mod = mod_ref[0, 0]
    gt = gt_ref[0]
    y = gt[:, 0:1] * y0_ref[0].astype(F32)
    for k, y_ref in ((1, y1_ref), (2, y2_ref), (3, y3_ref)):
        y = y + gt[:, k:k + 1] * y_ref[0].astype(F32)
    x2 = x_ref[0] + mod[5:6] * y
    o_ref[0] = _rms(x2) * fg_ref[...] if final else x2


def _combine(x1, yg, gates, modt, final_g, ctx_len, final):
    b, s_tot, d = x1.shape
    nt = s_tot // ROW_TILE
    nctx = ctx_len // ROW_TILE
    off = nctx if final else 0
    seg = lambda j: jnp.where(j + off >= nctx, 1, 0)
    choice = lambda k: pl.BlockSpec((1, ROW_TILE, d), lambda i, j: (k, i * nt + j + off, 0))
    return pl.pallas_call(
        functools.partial(_combine_kernel, final=final),
        grid=(b, nt - off),
        in_specs=[pl.BlockSpec((1, ROW_TILE, d), lambda i, j: (i, j + off, 0)),
                  choice(0), choice(1), choice(2), choice(3),
                  pl.BlockSpec((1, ROW_TILE, LOGIT_LANES), lambda i, j: (i, j + off, 0)),
                  pl.BlockSpec((1, 1, 8, d), lambda i, j: (i, seg(j), 0, 0)),
                  pl.BlockSpec((1, d), lambda i, j: (0, 0))],
        out_specs=pl.BlockSpec((1, ROW_TILE, d), lambda i, j: (i, j, 0)),
        out_shape=jax.ShapeDtypeStruct((b, s_tot - off * ROW_TILE, d), F32),
        compiler_params=_cparams(("arbitrary", "arbitrary")),
        name="moe_combine",
    )(x1, yg, yg, yg, yg, gates, modt, final_g.reshape(1, d))


def _moe(x1, h2, top, top_gate, counts, modt, final_g, exp_gu_w, exp_gu_b, exp_down_w, exp_down_b,
         layer, ctx_len, final):
    b, s_tot, d = x1.shape
    n_tok = b * s_tot
    top = top.reshape(n_tok, LOGIT_LANES)
    n_assign = n_tok * TOP_K
    cnt = counts[0, :N_EXPERTS]
    plan = _plan(cnt.reshape(N_EXPERTS, 1), n_assign // MOE_ROWS)
    starts = (jnp.cumsum(cnt) - cnt).astype(jnp.int32)
    inv = jnp.take(starts, top[:, :TOP_K].reshape(-1)) + top[:, TOP_K:2 * TOP_K].reshape(-1)
    _, order = lax.sort((inv, jnp.arange(n_assign, dtype=jnp.int32)), num_keys=1)
    rows = order // TOP_K
    xs = h2.reshape(n_tok, d).at[rows].get(mode="promise_in_bounds")
    ys = _experts(xs, plan, exp_gu_w, exp_gu_b, exp_down_w, exp_down_b, layer)
    by_choice = inv.reshape(n_tok, TOP_K).T.reshape(-1)
    yg = ys.at[by_choice].get(mode="promise_in_bounds").reshape(TOP_K, n_tok, d)
    return _combine(x1, yg, top_gate, modt, final_g, ctx_len, final)


def _mixer_layer(x, modt, i, ctx_len, tables, norm1_g, w_in_bf, s5p, ssm_d, conv_p, merge_p):
    b, s_tot, d = x.shape
    u2, cv, qkvg, gates = _in_proj(x, modt, norm1_g, w_in_bf, ctx_len)
    bmat, cmat, lvec = s5p
    y_s5 = _s5_scan(u2, bmat, cmat, lvec, ssm_d, ctx_len)
    hc = _conv_branch(cv, *conv_p, ctx_len)
    cos, sin, tabs = tables
    ro = _retention(qkvg, cos, sin, tabs, ctx_len)
    return _merge(x, y_s5, hc, ro, gates, modt, *merge_p, ctx_len)


def kernel(x, c, ctx, c_ctx, ada_w, ada_b, norm1_g, w_in, ssm_lam_re, ssm_lam_im, ssm_log_dt, ssm_b_re, ssm_b_im, ssm_c_re, ssm_c_im, ssm_d, ssm_glu_w, ssm_glu_b, conv_dw_w, conv_dw_b, conv_ln_g, conv_ln_b, conv_pw_w, conv_pw_b, ret_w_o, w_out, norm2_g, router_w, router_b, exp_gu_w, exp_gu_b, exp_down_w, exp_down_b, final_g):
    b, seq, d = x.shape
    ctx_len = ctx.shape[1]
    depth = ada_w.shape[0]
    xs = jnp.concatenate([ctx, x], axis=1)
    s_in = jnp.concatenate([c, c_ctx[None], jnp.zeros((16 - b - 1, d), F32)], axis=0)
    mod = _ada_mod(s_in, ada_w, ada_b).reshape(depth, 16, N_MOD, d)
    mod = jnp.pad(mod, ((0, 0), (0, 0), (0, 8 - N_MOD), (0, 0)))
    modt = jnp.stack([jnp.broadcast_to(mod[:, b][:, None], (depth, b, 8, d)), mod[:, :b]], axis=2)
    tables = _ret_tables(ctx_len, seq)
    wr = jnp.pad(router_w, ((0, 0), (0, 0), (0, LOGIT_LANES - N_EXPERTS)))
    wr_hi = wr.astype(BF16)
    wr = jnp.stack([wr_hi, (wr - wr_hi.astype(F32)).astype(BF16)], axis=1)
    br = jnp.pad(router_b, ((0, 0), (0, LOGIT_LANES - N_EXPERTS))).reshape(depth, 1, LOGIT_LANES)
    for i in range(depth):
        s5p = _s5_params(ssm_lam_re[i], ssm_lam_im[i], ssm_log_dt[i], ssm_b_re[i], ssm_b_im[i],
                         ssm_c_re[i], ssm_c_im[i])
        conv_p = (conv_dw_w[i], conv_dw_b[i], conv_ln_g[i], conv_ln_b[i])
        merge_p = (norm2_g[i], ssm_glu_w[i].astype(BF16), ssm_glu_b[i], conv_pw_w[i].astype(BF16),
                   conv_pw_b[i], ret_w_o[i].astype(BF16), w_out[i].astype(BF16), wr[i], br[i])
        x1, h2, top, top_gate, counts = _mixer_layer(xs, modt[i], i, ctx_len, tables, norm1_g[i],
                                                     w_in[i].astype(BF16), s5p, ssm_d[i], conv_p, merge_p)
        xs = _moe(x1, h2, top, top_gate, counts, modt[i], final_g, exp_gu_w, exp_gu_b, exp_down_w,
                  exp_down_b, i, ctx_len, final=(i == depth - 1))
    return xs
```

```python
import functools
import math

import numpy as np
import jax
import jax.numpy as jnp
from jax import lax
from jax.experimental import pallas as pl
from jax.experimental.pallas import tpu as pltpu

F32 = jnp.float32
BF16 = jnp.bfloat16

D_MODEL = 1024
DEPTH = 4
GRID_W = 64
N_MOD = 6
SSM_WIDTH = 256
SSM_GROUP = 16
SSM_GROUPS = SSM_WIDTH // SSM_GROUP
SSM_STATE = 64
SSM_LANES = SSM_GROUPS * SSM_STATE
CONV_WIDTH = 256
CONV_K = 31
CONV_PAD = 16
RET_HEADS = 4
RET_HEAD_DIM = 128
RET_WIDTH = RET_HEADS * RET_HEAD_DIM
RET_CHUNK = 128
ROPE_BASE = 10000.0
IN_WIDTH = SSM_WIDTH + 2 * CONV_WIDTH + 4 * RET_WIDTH + 3 * D_MODEL
COL_CV = SSM_WIDTH
COL_QKVG = COL_CV + 2 * CONV_WIDTH
COL_GATES = COL_QKVG + 4 * RET_WIDTH
N_EXPERTS = 32
TOP_K = 4
EXPERT_FF = D_MODEL
SWIGLU_LIMIT = 7.0
SWIGLU_ALPHA = 1.702
EPS = 1e-6

ROW_TILE = 256
HALF_TILE = 128
S5_STEPS = 64
S5_STRIP = 512
MOE_ROWS = 512
MOE_PART = 256
MOE_GROUP_BATCH = 4
LOGIT_LANES = 128
VMEM_LIMIT = 56 * 1024 * 1024


def _cparams(sem):
    return pltpu.CompilerParams(dimension_semantics=sem, vmem_limit_bytes=VMEM_LIMIT)


def _rms(x):
    return x * lax.rsqrt(jnp.mean(x * x, axis=-1, keepdims=True) + EPS)


def _ada_kernel(s_ref, w_ref, b_ref, o_ref):
    s = s_ref[...]
    s = s * jax.nn.sigmoid(s)
    o_ref[0] = jnp.dot(s, w_ref[0], preferred_element_type=F32,
                       precision=lax.Precision.HIGHEST) + b_ref[0]


def _ada_mod(s_in, ada_w, ada_b):
    depth, d, n = ada_w.shape
    tn = 1536
    return pl.pallas_call(
        _ada_kernel,
        grid=(depth, n // tn),
        in_specs=[pl.BlockSpec((16, d), lambda i, j: (0, 0)),
                  pl.BlockSpec((1, d, tn), lambda i, j: (i, 0, j)),
                  pl.BlockSpec((1, 1, tn), lambda i, j: (i, 0, j))],
        out_specs=pl.BlockSpec((1, 16, tn), lambda i, j: (i, 0, j)),
        out_shape=jax.ShapeDtypeStruct((depth, 16, n), F32),
        compiler_params=_cparams(("arbitrary", "arbitrary")),
        name="ada_mod",
    )(s_in, ada_w, ada_b.reshape(depth, 1, n))


def _inproj_kernel(x_ref, mod_ref, g_ref, w_ref, u_ref, cv_ref, qkvg_ref, gates_ref):
    x = x_ref[0]
    mod = mod_ref[0, 0]
    h = (_rms(x) * g_ref[...]) * (1.0 + mod[1:2]) + mod[0:1]
    hb = h.astype(BF16)
    u_ref[0] = jnp.dot(hb, w_ref[:, 0:COL_CV], preferred_element_type=F32)
    cv_ref[0] = jnp.dot(hb, w_ref[:, COL_CV:COL_QKVG], preferred_element_type=F32)
    qkvg_ref[0] = jnp.dot(hb, w_ref[:, COL_QKVG:COL_GATES], preferred_element_type=F32).astype(BF16)
    gates = jnp.dot(hb, w_ref[:, COL_GATES:IN_WIDTH], preferred_element_type=F32)
    gates_ref[0] = jax.nn.sigmoid(gates).astype(BF16)


def _in_proj(x, modt, g1, w_in_bf, ctx_len):
    b, s_tot, d = x.shape
    nt = s_tot // ROW_TILE
    nctx = ctx_len // ROW_TILE
    seg = lambda j: jnp.where(j >= nctx, 1, 0)
    return pl.pallas_call(
        _inproj_kernel,
        grid=(b, nt),
        in_specs=[pl.BlockSpec((1, ROW_TILE, d), lambda i, j: (i, j, 0)),
                  pl.BlockSpec((1, 1, 8, d), lambda i, j: (i, seg(j), 0, 0)),
                  pl.BlockSpec((1, d), lambda i, j: (0, 0)),
                  pl.BlockSpec((d, IN_WIDTH), lambda i, j: (0, 0))],
        out_specs=[pl.BlockSpec((1, ROW_TILE, SSM_WIDTH), lambda i, j: (i, j, 0)),
                   pl.BlockSpec((1, ROW_TILE, 2 * CONV_WIDTH), lambda i, j: (i, j, 0)),
                   pl.BlockSpec((1, ROW_TILE, 4 * RET_WIDTH), lambda i, j: (i, j, 0)),
                   pl.BlockSpec((1, ROW_TILE, 3 * D_MODEL), lambda i, j: (i, j, 0))],
        out_shape=[jax.ShapeDtypeStruct((b, s_tot, SSM_WIDTH), F32),
                   jax.ShapeDtypeStruct((b, s_tot, 2 * CONV_WIDTH), F32),
                   jax.ShapeDtypeStruct((b, s_tot, 4 * RET_WIDTH), BF16),
                   jax.ShapeDtypeStruct((b, s_tot, 3 * D_MODEL), BF16)],
        compiler_params=_cparams(("arbitrary", "arbitrary")),
        name="in_proj",
    )(x, modt, g1.reshape(1, d), w_in_bf)


def _s5_params(lam_re, lam_im, log_dt, b_re, b_im, c_re, c_im):
    lam = lax.complex(jnp.minimum(lam_re.astype(F32), -1e-4), lam_im.astype(F32))
    lam_dt = lam * jnp.exp(log_dt.astype(F32))[..., None]
    lam_bar = jnp.exp(lam_dt)
    b = lax.complex(b_re.astype(F32), b_im.astype(F32))
    b_bar = ((lam_bar - 1.0) / lam)[..., None] * b
    eye = jnp.eye(SSM_GROUPS, dtype=F32)

    def in_mat(t):
        return jnp.einsum('dgpm,gh->dgmhp', t, eye).reshape(2, SSM_WIDTH, SSM_LANES)

    def out_mat(t):
        return jnp.einsum('dgmp,gh->dgphm', t, eye).reshape(2, SSM_LANES, SSM_WIDTH)

    bmat = jnp.concatenate([in_mat(jnp.real(b_bar)), in_mat(jnp.imag(b_bar))], axis=-1)
    cmat = jnp.concatenate([out_mat(c_re.astype(F32)), -out_mat(c_im.astype(F32))], axis=1)
    lvec = jnp.concatenate([jnp.real(lam_bar).reshape(2, 1, SSM_LANES),
                            jnp.imag(lam_bar).reshape(2, 1, SSM_LANES)], axis=-1)
    return bmat.astype(BF16), cmat.astype(BF16), lvec


def _s5_kernel(u_ref, perm_ref, permt_ref, bmat_ref, cmat_ref, lam_ref, dsk_ref, y_ref, st_ref, carry_ref):
    d = pl.program_id(0)
    c = pl.program_id(1)
    nb, steps, width = u_ref.shape

    @pl.when(c == 0)
    def _():
        carry_ref[...] = jnp.zeros_like(carry_ref)

    u = u_ref[...].reshape(nb * steps, width)
    u_tm = jnp.dot(perm_ref[...], u.astype(BF16), preferred_element_type=F32).astype(BF16)
    bu = jnp.dot(u_tm, bmat_ref[0], preferred_element_type=F32)
    st_ref[...] = bu.reshape(steps, nb, 2 * SSM_LANES)

    for k in range(SSM_LANES // S5_STRIP):
        re = pl.ds(k * S5_STRIP, S5_STRIP)
        im = pl.ds(SSM_LANES + k * S5_STRIP, S5_STRIP)
        lr = jnp.broadcast_to(lam_ref[0, :, re], (nb, S5_STRIP))
        li = jnp.broadcast_to(lam_ref[0, :, im], (nb, S5_STRIP))

        def step(t, s):
            sr, si = s
            tt = jnp.where(d == 1, steps - 1 - t, t)
            nr = lr * sr - li * si + st_ref[tt, :, re]
            ni = lr * si + li * sr + st_ref[tt, :, im]
            st_ref[tt, :, re] = nr
            st_ref[tt, :, im] = ni
            return nr, ni

        sr, si = lax.fori_loop(0, steps, step, (carry_ref[:, re], carry_ref[:, im]), unroll=2)
        carry_ref[:, re] = sr
        carry_ref[:, im] = si

    st = st_ref[...].reshape(steps * nb, 2 * SSM_LANES)
    y_tm = jnp.dot(st.astype(BF16), cmat_ref[0], preferred_element_type=F32)
    y = jnp.dot(permt_ref[...], y_tm.astype(BF16), preferred_element_type=F32)
    skip = jnp.where(d == 0, 1.0, 0.0) * dsk_ref[...]
    y_ref[0] = (y + u * skip).reshape(nb, steps, width)


def _s5_scan(u, bmat, cmat, lvec, d_skip, ctx_len):
    nb, s_tot, width = u.shape
    nch = s_tot // S5_STEPS
    nctx = ctx_len // S5_STEPS
    rows = nb * S5_STEPS
    r = np.arange(rows)
    perm = np.zeros((rows, rows), np.float32)
    perm[r, (r % nb) * S5_STEPS + r // nb] = 1.0
    perm_tm = jnp.asarray(perm, BF16)
    perm_bm = jnp.asarray(perm.T, BF16)

    def chunk(d, c):
        back = jnp.where(c < nctx, nctx - 1 - c, nctx + nch - 1 - c)
        return jnp.where(d == 1, back, c)

    return pl.pallas_call(
        _s5_kernel,
        grid=(2, nch),
        in_specs=[pl.BlockSpec((nb, S5_STEPS, width), lambda d, c: (0, chunk(d, c), 0)),
                  pl.BlockSpec((rows, rows), lambda d, c: (0, 0)),
                  pl.BlockSpec((rows, rows), lambda d, c: (0, 0)),
                  pl.BlockSpec((1, width, 2 * SSM_LANES), lambda d, c: (d, 0, 0)),
                  pl.BlockSpec((1, 2 * SSM_LANES, width), lambda d, c: (d, 0, 0)),
                  pl.BlockSpec((1, 1, 2 * SSM_LANES), lambda d, c: (d, 0, 0)),
                  pl.BlockSpec((1, width), lambda d, c: (0, 0))],
        out_specs=pl.BlockSpec((1, nb, S5_STEPS, width), lambda d, c: (d, 0, chunk(d, c), 0)),
        out_shape=jax.ShapeDtypeStruct((2, nb, s_tot, width), F32),
        scratch_shapes=[pltpu.VMEM((S5_STEPS, nb, 2 * SSM_LANES), F32),
                        pltpu.VMEM((nb, 2 * SSM_LANES), F32)],
        compiler_params=_cparams(("arbitrary", "arbitrary")),
        name="s5_scan",
    )(u, perm_tm, perm_bm, bmat, cmat, lvec, d_skip.reshape(1, width))


def _conv_kernel(cv_ref, w_ref, b_ref, lg_ref, lb_ref, o_ref, hp_ref, win_ref, shift_ref, *, ctx_len):
    s_tot = cv_ref.shape[1]
    lat = s_tot - ctx_len
    rc = RET_CHUNK
    zeros = jnp.zeros((CONV_PAD, CONV_WIDTH), F32)
    a = cv_ref[0, :, 0:CONV_WIDTH]
    g = cv_ref[0, :, CONV_WIDTH:2 * CONV_WIDTH]
    h = a * jax.nn.sigmoid(g)
    hp_ref[0:CONV_PAD] = zeros
    hp_ref[CONV_PAD:CONV_PAD + ctx_len] = h[0:ctx_len]
    hp_ref[CONV_PAD + ctx_len:2 * CONV_PAD + ctx_len] = zeros
    hp_ref[2 * CONV_PAD + ctx_len:2 * CONV_PAD + s_tot] = h[ctx_len:s_tot]
    hp_ref[2 * CONV_PAD + s_tot:3 * CONV_PAD + s_tot] = zeros
    del lat

    def chunk(c, carry):
        r0 = pl.multiple_of(c * rc, rc)
        wbase = pl.multiple_of(r0 + jnp.where(r0 >= ctx_len, CONV_PAD, 0), 8)
        win_ref[...] = hp_ref[pl.ds(wbase, rc + 2 * CONV_PAD), :]
        for r in range(1, 8):
            shift_ref[r] = win_ref[r:r + rc + 2 * CONV_PAD - 8, :]
        acc = jnp.zeros((rc, CONV_WIDTH), F32) + b_ref[...]
        for k in range(CONV_K):
            off = k + CONV_PAD - CONV_K // 2
            base = off - off % 8
            if off % 8 == 0:
                tap = win_ref[base:base + rc, :]
            else:
                tap = shift_ref[off % 8, base:base + rc, :]
            acc = acc + w_ref[k:k + 1, :] * tap
        mu = jnp.mean(acc, axis=-1, keepdims=True)
        var = jnp.mean(jnp.square(acc - mu), axis=-1, keepdims=True)
        y = (acc - mu) * lax.rsqrt(var + EPS) * lg_ref[...] + lb_ref[...]
        o_ref[0, pl.ds(r0, rc), :] = y * jax.nn.sigmoid(y)
        return carry

    lax.fori_loop(0, s_tot // rc, chunk, 0)


def _conv_branch(cv, w_dw, b_dw, ln_g, ln_b, ctx_len):
    b, s_tot, _ = cv.shape
    w = jnp.concatenate([w_dw.reshape(CONV_K, CONV_WIDTH), jnp.zeros((1, CONV_WIDTH), F32)], axis=0)
    vec = lambda t: t.reshape(1, CONV_WIDTH)
    cst = lambda shape: pl.BlockSpec(shape, lambda i: (0,) * len(shape))
    return pl.pallas_call(
        functools.partial(_conv_kernel, ctx_len=ctx_len),
        grid=(b,),
        in_specs=[pl.BlockSpec((1, s_tot, 2 * CONV_WIDTH), lambda i: (i, 0, 0)),
                  cst((CONV_K + 1, CONV_WIDTH)), cst((1, CONV_WIDTH)),
                  cst((1, CONV_WIDTH)), cst((1, CONV_WIDTH))],
        out_specs=pl.BlockSpec((1, s_tot, CONV_WIDTH), lambda i: (i, 0, 0)),
        out_shape=jax.ShapeDtypeStruct((b, s_tot, CONV_WIDTH), F32),
        scratch_shapes=[pltpu.VMEM((s_tot + 3 * CONV_PAD, CONV_WIDTH), F32),
                        pltpu.VMEM((RET_CHUNK + 2 * CONV_PAD, CONV_WIDTH), F32),
                        pltpu.VMEM((8, RET_CHUNK + 2 * CONV_PAD - 8, CONV_WIDTH), F32)],
        compiler_params=_cparams(("arbitrary",)),
        name="conv_branch",
    )(cv, w, vec(b_dw), vec(ln_g), vec(ln_b))


N_TAB = 5


def _ret_tables(ctx_len, seq):
    n = RET_HEAD_DIM // 4
    inv_freq = ROPE_BASE ** (-np.arange(n, dtype=np.float64) / n)
    pos = np.arange(seq)
    ang_r = (pos // GRID_W)[:, None] * inv_freq
    ang_c = (pos % GRID_W)[:, None] * inv_freq
    cos = np.concatenate([np.cos(ang_r), np.cos(ang_r), np.cos(ang_c), np.cos(ang_c)], axis=-1)
    sin = np.concatenate([-np.sin(ang_r), np.sin(ang_r), -np.sin(ang_c), np.sin(ang_c)], axis=-1)
    cos = np.concatenate([np.ones((ctx_len, RET_HEAD_DIM)), cos], axis=0)
    sin = np.concatenate([np.zeros((ctx_len, RET_HEAD_DIM)), sin], axis=0)
    log_g_fwd = np.log1p(-np.exp2(-5.0 - np.arange(RET_HEADS, dtype=np.float64)))
    idx = np.arange(RET_CHUNK, dtype=np.float64)
    diff = idx[:, None] - idx[None, :]
    tabs = np.zeros((2, RET_HEADS, N_TAB, RET_CHUNK, RET_CHUNK))
    ones = np.ones((RET_CHUNK, RET_CHUNK))
    for d, log_g in enumerate((log_g_fwd, log_g_fwd[::-1])):
        for hd in range(RET_HEADS):
            lg = log_g[hd]
            if d == 0:
                mask = np.where(diff >= 0, np.exp(lg * np.maximum(diff, 0.0)), 0.0)
                dec = np.exp(lg * (idx + 1.0))
                zeta = np.exp(lg * (RET_CHUNK - 1 - idx))
            else:
                mask = np.where(diff <= 0, np.exp(lg * np.maximum(-diff, 0.0)), 0.0)
                dec = np.exp(lg * (RET_CHUNK - idx))
                zeta = np.exp(lg * idx)
            tabs[d, hd, 0] = mask
            tabs[d, hd, 1] = dec[:, None] * ones
            tabs[d, hd, 2] = zeta[:, None] * ones
            tabs[d, hd, 3] = np.exp(lg * RET_CHUNK) * ones
            tabs[d, hd, 4] = zeta[None, :] * ones
    return (jnp.asarray(cos, F32), jnp.asarray(sin, F32), jnp.asarray(tabs, F32))


def _ret_kernel(q_ref, k_ref, v_ref, g_ref, cos_ref, sin_ref, tab_ref, o_ref,
                qs_ref, qd_ref, kt_ref, kz_ref, out_ref, state_ref, *, ctx_len):
    s_tot = q_ref.shape[1]
    rc = RET_CHUNK
    nch = s_tot // rc
    nctx = ctx_len // rc
    k_scale = RET_HEAD_DIM ** -0.5
    lane = lax.broadcasted_iota(jnp.int32, (rc, RET_HEAD_DIM), 1)
    first_half = (lane % (RET_HEAD_DIM // 2)) < (RET_HEAD_DIM // 4)

    def rope(t, cs, sn):
        quarter = RET_HEAD_DIM // 4
        partner = jnp.where(first_half, pltpu.roll(t, RET_HEAD_DIM - quarter, 1), pltpu.roll(t, quarter, 1))
        return t * cs + partner * sn

    def prepare(c, carry):
        rows = pl.ds(pl.multiple_of(c * rc, rc), rc)
        cs = cos_ref[rows, :]
        sn = sin_ref[rows, :]
        q = rope(q_ref[0, rows, :].astype(F32), cs, sn)
        kt = (rope(k_ref[0, rows, :].astype(F32), cs, sn) * k_scale).T
        qs_ref[c] = q.astype(BF16)
        kt_ref[c] = kt.astype(BF16)
        for d in range(2):
            qd_ref[d, c] = (q * tab_ref[d, 0, 1]).astype(BF16)
            kz_ref[d, c] = (kt * tab_ref[d, 0, 4]).astype(BF16)
        return carry

    lax.fori_loop(0, nch, prepare, 0, unroll=2)
    state_ref[...] = jnp.zeros_like(state_ref)

    def chunk(c, d):
        rows = pl.ds(pl.multiple_of(c * rc, rc), rc)
        vb = v_ref[0, rows, :]
        state = state_ref[d]
        scores = jnp.dot(qs_ref[c], kt_ref[c], preferred_element_type=F32) * tab_ref[d, 0, 0]
        o = (jnp.dot(scores.astype(BF16), vb, preferred_element_type=F32)
             + jnp.dot(qd_ref[d, c], state.astype(BF16), preferred_element_type=F32))
        kv = jnp.dot(kz_ref[d, c], vb, preferred_element_type=F32)
        state_ref[d] = tab_ref[d, 0, 3] * state + kv
        out_ref[d, rows, :] = o

    def step(j, carry):
        chunk(j, 0)
        chunk(jnp.where(j < nctx, nctx - 1 - j, nctx + nch - 1 - j), 1)
        return carry

    lax.fori_loop(0, nch, step, 0, unroll=2)
    g = g_ref[0].astype(F32)
    o_ref[0] = (_rms(out_ref[0] + out_ref[1]) * (g * jax.nn.sigmoid(g))).astype(o_ref.dtype)


def _retention(qkvg, cos, sin, tabs, ctx_len):
    b, s_tot, _ = qkvg.shape
    hd = RET_HEAD_DIM
    nch = s_tot // RET_CHUNK
    part = lambda p: pl.BlockSpec((1, s_tot, hd), lambda i, h: (i, 0, p * RET_HEADS + h))
    return pl.pallas_call(
        functools.partial(_ret_kernel, ctx_len=ctx_len),
        grid=(b, RET_HEADS),
        in_specs=[part(0), part(1), part(2), part(3),
                  pl.BlockSpec((s_tot, hd), lambda i, h: (0, 0)),
                  pl.BlockSpec((s_tot, hd), lambda i, h: (0, 0)),
                  pl.BlockSpec((2, 1, N_TAB, RET_CHUNK, RET_CHUNK), lambda i, h: (0, h, 0, 0, 0))],
        out_specs=pl.BlockSpec((1, s_tot, hd), lambda i, h: (i, 0, h)),
        out_shape=jax.ShapeDtypeStruct((b, s_tot, RET_WIDTH), BF16),
        scratch_shapes=[pltpu.VMEM((nch, RET_CHUNK, hd), BF16),
                        pltpu.VMEM((2, nch, RET_CHUNK, hd), BF16),
                        pltpu.VMEM((nch, hd, RET_CHUNK), BF16),
                        pltpu.VMEM((2, nch, hd, RET_CHUNK), BF16),
                        pltpu.VMEM((2, s_tot, hd), F32),
                        pltpu.VMEM((2, hd, hd), F32)],
        compiler_params=_cparams(("arbitrary", "arbitrary")),
        name="retention",
    )(qkvg, qkvg, qkvg, qkvg, cos, sin, tabs)


def _merge_kernel(x_ref, yf_ref, yb_ref, hc_ref, ro_ref, gates_ref, mod_ref, g2_ref,
                  wglu_ref, bglu_ref, wpw_ref, bpw_ref, wo_ref, wout_ref, wr_ref, br_ref, tril_ref,
                  x1_ref, h2_ref, ti_ref, tg_ref, cnt_out_ref, cnt_ref, lg_ref):
    d = D_MODEL
    mod = mod_ref[0, 0]

    @pl.when(jnp.logical_and(pl.program_id(0) % MOE_GROUP_BATCH == 0, pl.program_id(1) == 0))
    def _():
        cnt_ref[...] = jnp.zeros_like(cnt_ref)

    for half in range(ROW_TILE // HALF_TILE):
        rows = pl.ds(half * HALF_TILE, HALF_TILE)
        ys = jax.nn.gelu(yf_ref[0, 0, rows, :] + yb_ref[0, 0, rows, :]).astype(BF16)
        z = jnp.dot(ys, wglu_ref[...], preferred_element_type=F32) + bglu_ref[...]
        y_ssm = z[:, 0:d] * jax.nn.sigmoid(z[:, d:2 * d])
        y_conv = jnp.dot(hc_ref[0, rows, :].astype(BF16), wpw_ref[...],
                         preferred_element_type=F32) + bpw_ref[...]
        y_ret = jnp.dot(ro_ref[0, rows, :].astype(BF16), wo_ref[...], preferred_element_type=F32)
        m = (gates_ref[0, rows, 0:d].astype(F32) * y_ssm
             + gates_ref[0, rows, d:2 * d].astype(F32) * y_conv
             + gates_ref[0, rows, 2 * d:3 * d].astype(F32) * y_ret)
        y = jnp.dot(m.astype(BF16), wout_ref[...], preferred_element_type=F32)
        x1 = x_ref[0, rows, :] + mod[2:3] * y
        x1_ref[0, rows, :] = x1
        h2 = (_rms(x1) * g2_ref[...]) * (1.0 + mod[4:5]) + mod[3:4]
        h2_ref[0, rows, :] = h2
        h_hi = h2.astype(BF16)
        h_lo = (h2 - h_hi.astype(F32)).astype(BF16)
        lg_ref[rows, :] = (jnp.dot(h_hi, wr_ref[0], preferred_element_type=F32)
                           + jnp.dot(h_lo, wr_ref[0], preferred_element_type=F32)
                           + jnp.dot(h_hi, wr_ref[1], preferred_element_type=F32)) + br_ref[...]
    logits = lg_ref[...]
    lane = lax.broadcasted_iota(jnp.int32, logits.shape, 1)
    lane_f = lane.astype(F32)
    neg = jnp.float32(-jnp.inf)
    l = jnp.where(lane < N_EXPERTS, logits, neg)
    ti = jnp.zeros(logits.shape, F32)
    vals = []
    picks = []
    for k in range(TOP_K):
        top = jnp.max(l, axis=-1, keepdims=True)
        idx = jnp.min(jnp.where(l == top, lane_f, float(LOGIT_LANES)), axis=-1, keepdims=True)
        ti = jnp.where(lane == k, idx, ti)
        vals.append(top)
        picked = lane_f == idx
        picks.append(picked.astype(F32))
        l = jnp.where(picked, neg, l)
    es = [jnp.exp(v - vals[0]) for v in vals]
    tot = es[0]
    for e in es[1:]:
        tot = tot + e
    tg = jnp.zeros(logits.shape, F32)
    for k in range(TOP_K):
        tg = jnp.where(lane == k, es[k] / tot, tg)
    seen = cnt_ref[0:1, :]
    chosen = picks[0] + picks[1] + picks[2] + picks[3]
    before = seen + jnp.dot(tril_ref[...], chosen.astype(BF16), preferred_element_type=F32)
    for k in range(TOP_K):
        rank = jnp.sum(picks[k] * before, axis=-1, keepdims=True)
        ti = jnp.where(lane == TOP_K + k, rank, ti)
    seen = seen + jnp.sum(chosen, axis=0, keepdims=True)
    ti_ref[0] = ti.astype(jnp.int32)
    tg_ref[0] = tg
    cnt_ref[...] = jnp.broadcast_to(seen, cnt_ref.shape)
    cnt_out_ref[0] = jnp.broadcast_to(seen, cnt_out_ref.shape[1:])


def _merge(x, y_s5, hc, ro, gates, modt, g2, wglu, bglu, wpw, bpw, wo, wout, wr, br, ctx_len):
    b, s_tot, d = x.shape
    nt = s_tot // ROW_TILE
    nctx = ctx_len // ROW_TILE
    seg = lambda j: jnp.where(j >= nctx, 1, 0)
    row = lambda w: pl.BlockSpec((1, ROW_TILE, w), lambda i, j: (i, j, 0))
    cst = lambda shape: pl.BlockSpec(shape, lambda i, j: (0,) * len(shape))
    y2 = y_s5
    tril = jnp.asarray(np.tril(np.ones((ROW_TILE, ROW_TILE), np.float32), -1), BF16)
    return pl.pallas_call(
        _merge_kernel,
        grid=(b, nt),
        in_specs=[row(d),
                  pl.BlockSpec((1, 1, ROW_TILE, SSM_WIDTH), lambda i, j: (0, i, j, 0)),
                  pl.BlockSpec((1, 1, ROW_TILE, SSM_WIDTH), lambda i, j: (1, i, j, 0)),
                  row(CONV_WIDTH), row(RET_WIDTH), row(3 * d),
                  pl.BlockSpec((1, 1, 8, d), lambda i, j: (i, seg(j), 0, 0)),
                  cst((1, d)),
                  cst((SSM_WIDTH, 2 * d)), cst((1, 2 * d)),
                  cst((CONV_WIDTH, d)), cst((1, d)),
                  cst((RET_WIDTH, d)), cst((d, d)),
                  cst((2, d, LOGIT_LANES)), cst((1, LOGIT_LANES)),
                  cst((ROW_TILE, ROW_TILE))],
        out_specs=[row(d), row(d), row(LOGIT_LANES), row(LOGIT_LANES),
                   pl.BlockSpec((1, 8, LOGIT_LANES), lambda i, j: (i // MOE_GROUP_BATCH, 0, 0))],
        out_shape=[jax.ShapeDtypeStruct((b, s_tot, d), F32),
                   jax.ShapeDtypeStruct((b, s_tot, d), F32),
                   jax.ShapeDtypeStruct((b, s_tot, LOGIT_LANES), jnp.int32),
                   jax.ShapeDtypeStruct((b, s_tot, LOGIT_LANES), F32),
                   jax.ShapeDtypeStruct((b // MOE_GROUP_BATCH, 8, LOGIT_LANES), F32)],
        scratch_shapes=[pltpu.VMEM((8, LOGIT_LANES), F32), pltpu.VMEM((ROW_TILE, LOGIT_LANES), F32)],
        compiler_params=_cparams(("arbitrary", "arbitrary")),
        name="merge_router",
    )(x, y2, y2, hc, ro, gates, modt, g2.reshape(1, d), wglu, bglu.reshape(1, 2 * d),
      wpw, bpw.reshape(1, d), wo, wout, wr, br, tril)


PLAN_TILE, PLAN_EXPERT, PLAN_LO, PLAN_HI, PLAN_NEXT, PLAN_SLOT = range(6)
PLAN_LANES = 256


def _plan_kernel(cnt_ref, plan_ref, *, n_tiles):
    ne = N_EXPERTS
    rows_per = float(MOE_ROWS)
    e = lax.broadcasted_iota(jnp.int32, (ne, PLAN_LANES), 0)
    e_f = e.astype(F32)
    v = lax.broadcasted_iota(jnp.int32, (ne, PLAN_LANES), 1).astype(F32)

    def cumsum(t):
        for s in (1, 2, 4, 8, 16):
            t = t + jnp.where(e >= s, pltpu.roll(t, s, 0), 0.0)
        return t

    def lookup(table, onehot):
        return jnp.sum(onehot * table, axis=0, keepdims=True)

    c = jnp.broadcast_to(cnt_ref[...], (ne, PLAN_LANES))
    has = c > 0.0
    ends = cumsum(c)
    starts = ends - c
    first_tile = jnp.floor(starts / rows_per)
    tiles_per = jnp.where(has, jnp.floor((ends - 1.0) / rows_per) - first_tile + 1.0, 0.0)
    vend = cumsum(tiles_per)
    vstart = vend - tiles_per
    total = jnp.max(vend, axis=0, keepdims=True)
    slot = cumsum(has.astype(F32)) - 1.0
    slot = slot - 2.0 * jnp.floor(slot / 2.0)
    later = jnp.where(has, e_f, float(ne))
    nxt = jnp.where(e < ne - 1, pltpu.roll(later, ne - 1, 0), float(ne))
    for s in (1, 2, 4, 8, 16):
        nxt = jnp.minimum(nxt, jnp.where(e < ne - s, pltpu.roll(nxt, ne - s, 0), float(ne)))
    nxt = jnp.where(nxt >= float(ne), -1.0, nxt)
    last_with_rows = jnp.max(jnp.where(has, e_f, 0.0), axis=0, keepdims=True)

    valid = v[0:1] < total
    ve = jnp.sum((v >= vend).astype(F32), axis=0, keepdims=True)
    ve = jnp.where(valid, jnp.minimum(ve, float(ne - 1)), last_with_rows)
    onehot = (e_f == ve).astype(F32)
    vt = jnp.where(valid, lookup(first_tile, onehot) + v[0:1] - lookup(vstart, onehot),
                   float(n_tiles - 1))
    lo = jnp.where(valid, jnp.clip(lookup(starts, onehot) - vt * rows_per, 0.0, rows_per), 0.0)
    hi = jnp.where(valid, jnp.clip(lookup(ends, onehot) - vt * rows_per, 0.0, rows_per), 0.0)
    fields = {PLAN_TILE: vt, PLAN_EXPERT: ve, PLAN_LO: lo, PLAN_HI: hi,
              PLAN_NEXT: lookup(nxt, onehot), PLAN_SLOT: lookup(slot, onehot)}
    row = lax.broadcasted_iota(jnp.int32, (8, PLAN_LANES), 0)
    plan = jnp.zeros((8, PLAN_LANES), F32)
    for r, val in fields.items():
        plan = jnp.where(row == r, val, plan)
    plan_ref[...] = plan.astype(jnp.int32)


def _plan(counts_col, n_tiles):
    return pl.pallas_call(
        functools.partial(_plan_kernel, n_tiles=n_tiles),
        out_shape=jax.ShapeDtypeStruct((8, PLAN_LANES), jnp.int32),
        name="moe_plan",
    )(counts_col)


def _expert_kernel(plan_ref, x_ref, wgu_hbm, bgu_ref, wd_hbm, bd_ref, y_ref,
                   wgu_f32, wd_f32, wgu_bf, wd_bf, sem, *, layer):
    v = pl.program_id(0)
    prev = jnp.maximum(v - 1, 0)
    lo = plan_ref[PLAN_LO, v]
    hi = plan_ref[PLAN_HI, v]
    expert = plan_ref[PLAN_EXPERT, v]
    next_expert = plan_ref[PLAN_NEXT, v]
    active = hi > lo
    fresh_weights = jnp.logical_or(v == 0, expert != plan_ref[PLAN_EXPERT, prev])
    first_visit = jnp.logical_or(v == 0, plan_ref[PLAN_TILE, v] != plan_ref[PLAN_TILE, prev])

    def weight_copies(expert, slot):
        return (pltpu.make_async_copy(wgu_hbm.at[layer, expert], wgu_f32.at[slot], sem.at[0, slot]),
                pltpu.make_async_copy(wd_hbm.at[layer, expert], wd_f32.at[slot], sem.at[1, slot]))

    @pl.when(jnp.logical_and(fresh_weights, active))
    def _():
        slot = plan_ref[PLAN_SLOT, v]

        @pl.when(v == 0)
        def _():
            for c in weight_copies(expert, slot):
                c.start()

        for c in weight_copies(expert, slot):
            c.wait()

        @pl.when(next_expert >= 0)
        def _():
            for c in weight_copies(next_expert, 1 - slot):
                c.start()

        wgu_bf[...] = wgu_f32[slot].astype(BF16)
        wd_bf[...] = wd_f32[slot].astype(BF16)

    whole = jnp.logical_and(lo == 0, hi == MOE_ROWS)

    def ffn(rows):
        f = EXPERT_FF
        gu = jnp.dot(x_ref[rows, :].astype(BF16), wgu_bf[...],
                     preferred_element_type=F32) + bgu_ref[0, 0]
        gate = jnp.minimum(gu[:, 0:f], SWIGLU_LIMIT)
        up = jnp.clip(gu[:, f:2 * f], -SWIGLU_LIMIT, SWIGLU_LIMIT)
        act = (up + 1.0) * gate * jax.nn.sigmoid(gate * SWIGLU_ALPHA)
        y = jnp.dot(act.astype(BF16), wd_bf[...], preferred_element_type=F32) + bd_ref[0, 0]
        return y.astype(y_ref.dtype)

    parts = [pl.ds(p * MOE_PART, MOE_PART) for p in range(MOE_ROWS // MOE_PART)]

    @pl.when(whole)
    def _():
        for rows in parts:
            y_ref[rows, :] = ffn(rows)

    @pl.when(jnp.logical_and(first_visit, jnp.logical_not(whole)))
    def _():
        y_ref[...] = jnp.zeros_like(y_ref)

    for p, rows in enumerate(parts):
        touched = jnp.logical_and(lo < (p + 1) * MOE_PART, hi > p * MOE_PART)

        @pl.when(jnp.logical_and(touched, jnp.logical_not(whole)))
        def _():
            row = p * MOE_PART + lax.broadcasted_iota(jnp.int32, (MOE_PART, 1), 0)
            mine = jnp.logical_and(row >= lo, row < hi)
            y_ref[rows, :] = jnp.where(mine, ffn(rows), y_ref[rows, :])


def _experts(xs, plan, w_gu, b_gu, w_down, b_down, layer):
    n_rows, d = xs.shape
    n_visits = n_rows // MOE_ROWS + N_EXPERTS - 1
    assert n_visits <= PLAN_LANES
    f = EXPERT_FF
    wmap = lambda v, plan: (layer, plan[PLAN_EXPERT, v], 0, 0)
    tmap = lambda v, plan: (plan[PLAN_TILE, v], 0)
    grid_spec = pltpu.PrefetchScalarGridSpec(
        num_scalar_prefetch=1,
        grid=(n_visits,),
        in_specs=[pl.BlockSpec((MOE_ROWS, d), tmap),
                  pl.BlockSpec(memory_space=pl.ANY),
                  pl.BlockSpec((1, 1, 1, 2 * f), wmap),
                  pl.BlockSpec(memory_space=pl.ANY),
                  pl.BlockSpec((1, 1, 1, d), wmap)],
        out_specs=pl.BlockSpec((MOE_ROWS, d), tmap),
        scratch_shapes=[pltpu.VMEM((2, d, 2 * f), F32), pltpu.VMEM((2, f, d), F32),
                        pltpu.VMEM((d, 2 * f), BF16), pltpu.VMEM((f, d), BF16),
                        pltpu.SemaphoreType.DMA((2, 2))],
    )
    return pl.pallas_call(
        functools.partial(_expert_kernel, layer=layer),
        grid_spec=grid_spec,
        out_shape=jax.ShapeDtypeStruct((n_rows, d), BF16),
        compiler_params=_cparams(("arbitrary",)),
        name="experts",
    )(plan, xs, w_gu, b_gu.reshape(DEPTH, N_EXPERTS, 1, 2 * f),
      w_down, b_down.reshape(DEPTH, N_EXPERTS, 1, d))


def _combine_kernel(x_ref, y0_ref, y1_ref, y2_ref, y3_ref, gt_ref, mod_ref, fg_ref, o_ref, *, final):
    mod = mod_ref[0, 0]
    gt = gt_ref[0]
    y = gt[:, 0:1] * y0_ref[0].astype(F32)
    for k, y_ref in ((1, y1_ref), (2, y2_ref), (3, y3_ref)):
        y = y + gt[:, k:k + 1] * y_ref[0].astype(F32)
    x2 = x_ref[0] + mod[5:6] * y
    o_ref[0] = _rms(x2) * fg_ref[...] if final else x2


def _combine(x1, yg, gates, modt, final_g, ctx_len, final, group):
    b, s_tot, d = x1.shape
    gb = MOE_GROUP_BATCH
    b0 = group * gb
    nt = s_tot // ROW_TILE
    nctx = ctx_len // ROW_TILE
    off = nctx if final else 0
    seg = lambda j: jnp.where(j + off >= nctx, 1, 0)
    choice = lambda k: pl.BlockSpec((1, ROW_TILE, d), lambda i, j: (k, i * nt + j + off, 0))
    if final:
        out_spec = pl.BlockSpec((1, ROW_TILE, d), lambda i, j: (i, j, 0))
        out_shape = jax.ShapeDtypeStruct((gb, s_tot - off * ROW_TILE, d), F32)
        aliases = {}
    else:
        out_spec = pl.BlockSpec((1, ROW_TILE, d), lambda i, j: (i + b0, j, 0))
        out_shape = jax.ShapeDtypeStruct((b, s_tot, d), F32)
        aliases = {0: 0}
    return pl.pallas_call(
        functools.partial(_combine_kernel, final=final),
        grid=(gb, nt - off),
        in_specs=[pl.BlockSpec((1, ROW_TILE, d), lambda i, j: (i + b0, j + off, 0)),
                  choice(0), choice(1), choice(2), choice(3),
                  pl.BlockSpec((1, ROW_TILE, LOGIT_LANES), lambda i, j: (i + b0, j + off, 0)),
                  pl.BlockSpec((1, 1, 8, d), lambda i, j: (i + b0, seg(j), 0, 0)),
                  pl.BlockSpec((1, d), lambda i, j: (0, 0))],
        out_specs=out_spec,
        out_shape=out_shape,
        input_output_aliases=aliases,
        compiler_params=_cparams(("arbitrary", "arbitrary")),
        name="moe_combine",
    )(x1, yg, yg, yg, yg, gates, modt, final_g.reshape(1, d))


def _moe(x1, h2, top, top_gate, counts, modt, final_g, exp_gu_w, exp_gu_b, exp_down_w, exp_down_b,
         layer, ctx_len, final):
    b, s_tot, d = x1.shape
    n_grp = MOE_GROUP_BATCH * s_tot
    n_assign = n_grp * TOP_K
    top = top.reshape(b * s_tot, LOGIT_LANES)
    h2 = h2.reshape(b * s_tot, d)
    x = x1
    finals = []
    for g in range(b // MOE_GROUP_BATCH):
        top_g = top[g * n_grp:(g + 1) * n_grp]
        cnt = counts[g, 0, :N_EXPERTS]
        plan = _plan(cnt.reshape(N_EXPERTS, 1), n_assign // MOE_ROWS)
        starts = (jnp.cumsum(cnt) - cnt).astype(jnp.int32)
        inv = jnp.take(starts, top_g[:, :TOP_K].reshape(-1)) + top_g[:, TOP_K:2 * TOP_K].reshape(-1)
        _, order = lax.sort((inv, jnp.arange(n_assign, dtype=jnp.int32)), num_keys=1)
        rows = order // TOP_K + g * n_grp
        xs = h2.at[rows].get(mode="promise_in_bounds")
        ys = _experts(xs, plan, exp_gu_w, exp_gu_b, exp_down_w, exp_down_b, layer)
        by_choice = inv.reshape(n_grp, TOP_K).T.reshape(-1)
        yg = ys.at[by_choice].get(mode="promise_in_bounds").reshape(TOP_K, n_grp, d)
        out = _combine(x, yg, top_gate, modt, final_g, ctx_len, final, g)
        if final:
            finals.append(out)
        else:
            x = out
    return jnp.concatenate(finals, axis=0) if final else x


def _mixer_layer(x, modt, i, ctx_len, tables, norm1_g, w_in_bf, s5p, ssm_d, conv_p, merge_p):
    b, s_tot, d = x.shape
    u2, cv, qkvg, gates = _in_proj(x, modt, norm1_g, w_in_bf, ctx_len)
    bmat, cmat, lvec = s5p
    y_s5 = _s5_scan(u2, bmat, cmat, lvec, ssm_d, ctx_len)
    hc = _conv_branch(cv, *conv_p, ctx_len)
    cos, sin, tabs = tables
    ro = _retention(qkvg, cos, sin, tabs, ctx_len)
    return _merge(x, y_s5, hc, ro, gates, modt, *merge_p, ctx_len)


def kernel(x, c, ctx, c_ctx, ada_w, ada_b, norm1_g, w_in, ssm_lam_re, ssm_lam_im, ssm_log_dt, ssm_b_re, ssm_b_im, ssm_c_re, ssm_c_im, ssm_d, ssm_glu_w, ssm_glu_b, conv_dw_w, conv_dw_b, conv_ln_g, conv_ln_b, conv_pw_w, conv_pw_b, ret_w_o, w_out, norm2_g, router_w, router_b, exp_gu_w, exp_gu_b, exp_down_w, exp_down_b, final_g):
    b, seq, d = x.shape
    ctx_len = ctx.shape[1]
    depth = ada_w.shape[0]
    xs = jnp.concatenate([ctx, x], axis=1)
    s_in = jnp.concatenate([c, c_ctx[None], jnp.zeros((16 - b - 1, d), F32)], axis=0)
    mod = _ada_mod(s_in, ada_w, ada_b).reshape(depth, 16, N_MOD, d)
    mod = jnp.pad(mod, ((0, 0), (0, 0), (0, 8 - N_MOD), (0, 0)))
    modt = jnp.stack([jnp.broadcast_to(mod[:, b][:, None], (depth, b, 8, d)), mod[:, :b]], axis=2)
    tables = _ret_tables(ctx_len, seq)
    wr = jnp.pad(router_w, ((0, 0), (0, 0), (0, LOGIT_LANES - N_EXPERTS)))
    wr_hi = wr.astype(BF16)
    wr = jnp.stack([wr_hi, (wr - wr_hi.astype(F32)).astype(BF16)], axis=1)
    br = jnp.pad(router_b, ((0, 0), (0, LOGIT_LANES - N_EXPERTS))).reshape(depth, 1, LOGIT_LANES)
    for i in range(depth):
        s5p = _s5_params(ssm_lam_re[i], ssm_lam_im[i], ssm_log_dt[i], ssm_b_re[i], ssm_b_im[i],
                         ssm_c_re[i], ssm_c_im[i])
        conv_p = (conv_dw_w[i], conv_dw_b[i], conv_ln_g[i], conv_ln_b[i])
        merge_p = (norm2_g[i], ssm_glu_w[i].astype(BF16), ssm_glu_b[i], conv_pw_w[i].astype(BF16),
                   conv_pw_b[i], ret_w_o[i].astype(BF16), w_out[i].astype(BF16), wr[i], br[i])
        x1, h2, top, top_gate, counts = _mixer_layer(xs, modt[i], i, ctx_len, tables, norm1_g[i],
                                                     w_in[i].astype(BF16), s5p, ssm_d[i], conv_p, merge_p)
        xs = _moe(x1, h2, top, top_gate, counts, modt[i], final_g, exp_gu_w, exp_gu_b, exp_down_w,
                  exp_down_b, i, ctx_len, final=(i == depth - 1))
    return xs
```

```python
import functools
import math

import numpy as np
import jax
import jax.numpy as jnp
from jax import lax
from jax.experimental import pallas as pl
from jax.experimental.pallas import tpu as pltpu

F32 = jnp.float32
BF16 = jnp.bfloat16

D_MODEL = 1024
DEPTH = 4
GRID_W = 64
N_MOD = 6
SSM_WIDTH = 256
SSM_GROUP = 16
SSM_GROUPS = SSM_WIDTH // SSM_GROUP
SSM_STATE = 64
SSM_LANES = SSM_GROUPS * SSM_STATE
CONV_WIDTH = 256
CONV_K = 31
CONV_PAD = 16
RET_HEADS = 4
RET_HEAD_DIM = 128
RET_WIDTH = RET_HEADS * RET_HEAD_DIM
RET_CHUNK = 128
ROPE_BASE = 10000.0
IN_WIDTH = SSM_WIDTH + 2 * CONV_WIDTH + 4 * RET_WIDTH + 3 * D_MODEL
COL_CV = SSM_WIDTH
COL_QKVG = COL_CV + 2 * CONV_WIDTH
COL_GATES = COL_QKVG + 4 * RET_WIDTH
N_EXPERTS = 32
TOP_K = 4
EXPERT_FF = D_MODEL
SWIGLU_LIMIT = 7.0
SWIGLU_ALPHA = 1.702
EPS = 1e-6

ROW_TILE = 256
HALF_TILE = 128
S5_STEPS = 64
S5_STRIP = 512
MOE_ROWS = 512
MOE_PART = 256
MOE_EDGE = 128
MOE_GROUP_BATCH = 4
LOGIT_LANES = 128
VMEM_LIMIT = 56 * 1024 * 1024


def _cparams(sem):
    return pltpu.CompilerParams(dimension_semantics=sem, vmem_limit_bytes=VMEM_LIMIT)


def _rms(x):
    return x * lax.rsqrt(jnp.mean(x * x, axis=-1, keepdims=True) + EPS)


def _ada_kernel(s_ref, w_ref, b_ref, o_ref):
    s = s_ref[...]
    s = s * jax.nn.sigmoid(s)
    o_ref[0] = jnp.dot(s, w_ref[0], preferred_element_type=F32,
                       precision=lax.Precision.HIGHEST) + b_ref[0]


def _ada_mod(s_in, ada_w, ada_b):
    depth, d, n = ada_w.shape
    tn = 1536
    return pl.pallas_call(
        _ada_kernel,
        grid=(depth, n // tn),
        in_specs=[pl.BlockSpec((16, d), lambda i, j: (0, 0)),
                  pl.BlockSpec((1, d, tn), lambda i, j: (i, 0, j)),
                  pl.BlockSpec((1, 1, tn), lambda i, j: (i, 0, j))],
        out_specs=pl.BlockSpec((1, 16, tn), lambda i, j: (i, 0, j)),
        out_shape=jax.ShapeDtypeStruct((depth, 16, n), F32),
        compiler_params=_cparams(("arbitrary", "arbitrary")),
        name="ada_mod",
    )(s_in, ada_w, ada_b.reshape(depth, 1, n))


def _inproj_kernel(x_ref, mod_ref, g_ref, w_ref, u_ref, cv_ref, qkvg_ref, gates_ref):
    x = x_ref[0]
    mod = mod_ref[0, 0]
    h = (_rms(x) * g_ref[...]) * (1.0 + mod[1:2]) + mod[0:1]
    hb = h.astype(BF16)
    u_ref[0] = jnp.dot(hb, w_ref[:, 0:COL_CV], preferred_element_type=F32)
    cv_ref[0] = jnp.dot(hb, w_ref[:, COL_CV:COL_QKVG], preferred_element_type=F32)
    qkvg_ref[0] = jnp.dot(hb, w_ref[:, COL_QKVG:COL_GATES], preferred_element_type=F32).astype(BF16)
    gates = jnp.dot(hb, w_ref[:, COL_GATES:IN_WIDTH], preferred_element_type=F32)
    gates_ref[0] = jax.nn.sigmoid(gates).astype(BF16)


def _in_proj(x, modt, g1, w_in_bf, ctx_len):
    b, s_tot, d = x.shape
    nt = s_tot // ROW_TILE
    nctx = ctx_len // ROW_TILE
    seg = lambda j: jnp.where(j >= nctx, 1, 0)
    return pl.pallas_call(
        _inproj_kernel,
        grid=(b, nt),
        in_specs=[pl.BlockSpec((1, ROW_TILE, d), lambda i, j: (i, j, 0)),
                  pl.BlockSpec((1, 1, 8, d), lambda i, j: (i, seg(j), 0, 0)),
                  pl.BlockSpec((1, d), lambda i, j: (0, 0)),
                  pl.BlockSpec((d, IN_WIDTH), lambda i, j: (0, 0))],
        out_specs=[pl.BlockSpec((1, ROW_TILE, SSM_WIDTH), lambda i, j: (i, j, 0)),
                   pl.BlockSpec((1, ROW_TILE, 2 * CONV_WIDTH), lambda i, j: (i, j, 0)),
                   pl.BlockSpec((1, ROW_TILE, 4 * RET_WIDTH), lambda i, j: (i, j, 0)),
                   pl.BlockSpec((1, ROW_TILE, 3 * D_MODEL), lambda i, j: (i, j, 0))],
        out_shape=[jax.ShapeDtypeStruct((b, s_tot, SSM_WIDTH), F32),
                   jax.ShapeDtypeStruct((b, s_tot, 2 * CONV_WIDTH), F32),
                   jax.ShapeDtypeStruct((b, s_tot, 4 * RET_WIDTH), BF16),
                   jax.ShapeDtypeStruct((b, s_tot, 3 * D_MODEL), BF16)],
        compiler_params=_cparams(("arbitrary", "arbitrary")),
        name="in_proj",
    )(x, modt, g1.reshape(1, d), w_in_bf)


def _s5_params(lam_re, lam_im, log_dt, b_re, b_im, c_re, c_im):
    lam = lax.complex(jnp.minimum(lam_re.astype(F32), -1e-4), lam_im.astype(F32))
    lam_dt = lam * jnp.exp(log_dt.astype(F32))[..., None]
    lam_bar = jnp.exp(lam_dt)
    b = lax.complex(b_re.astype(F32), b_im.astype(F32))
    b_bar = ((lam_bar - 1.0) / lam)[..., None] * b
    eye = jnp.eye(SSM_GROUPS, dtype=F32)

    def in_mat(t):
        return jnp.einsum('dgpm,gh->dgmhp', t, eye).reshape(2, SSM_WIDTH, SSM_LANES)

    def out_mat(t):
        return jnp.einsum('dgmp,gh->dgphm', t, eye).reshape(2, SSM_LANES, SSM_WIDTH)

    bmat = jnp.concatenate([in_mat(jnp.real(b_bar)), in_mat(jnp.imag(b_bar))], axis=-1)
    cmat = jnp.concatenate([out_mat(c_re.astype(F32)), -out_mat(c_im.astype(F32))], axis=1)
    lvec = jnp.concatenate([jnp.real(lam_bar).reshape(2, 1, SSM_LANES),
                            jnp.imag(lam_bar).reshape(2, 1, SSM_LANES)], axis=-1)
    return bmat.astype(BF16), cmat.astype(BF16), lvec


def _s5_kernel(u_ref, perm_ref, permt_ref, bmat_ref, cmat_ref, lam_ref, dsk_ref, y_ref, st_ref, carry_ref):
    d = pl.program_id(0)
    c = pl.program_id(1)
    nb, steps, width = u_ref.shape

    @pl.when(c == 0)
    def _():
        carry_ref[...] = jnp.zeros_like(carry_ref)

    u = u_ref[...].reshape(nb * steps, width)
    u_tm = jnp.dot(perm_ref[...], u.astype(BF16), preferred_element_type=F32).astype(BF16)
    bu = jnp.dot(u_tm, bmat_ref[0], preferred_element_type=F32)
    st_ref[...] = bu.reshape(steps, nb, 2 * SSM_LANES)

    for k in range(SSM_LANES // S5_STRIP):
        re = pl.ds(k * S5_STRIP, S5_STRIP)
        im = pl.ds(SSM_LANES + k * S5_STRIP, S5_STRIP)
        lr = jnp.broadcast_to(lam_ref[0, :, re], (nb, S5_STRIP))
        li = jnp.broadcast_to(lam_ref[0, :, im], (nb, S5_STRIP))

        def step(t, s):
            sr, si = s
            tt = jnp.where(d == 1, steps - 1 - t, t)
            nr = lr * sr - li * si + st_ref[tt, :, re]
            ni = lr * si + li * sr + st_ref[tt, :, im]
            st_ref[tt, :, re] = nr
            st_ref[tt, :, im] = ni
            return nr, ni

        sr, si = lax.fori_loop(0, steps, step, (carry_ref[:, re], carry_ref[:, im]), unroll=4)
        carry_ref[:, re] = sr
        carry_ref[:, im] = si

    st = st_ref[...].reshape(steps * nb, 2 * SSM_LANES)
    y_tm = jnp.dot(st.astype(BF16), cmat_ref[0], preferred_element_type=F32)
    y = jnp.dot(permt_ref[...], y_tm.astype(BF16), preferred_element_type=F32)
    skip = jnp.where(d == 0, 1.0, 0.0) * dsk_ref[...]
    y_ref[0] = (y + u * skip).reshape(nb, steps, width)


def _s5_scan(u, bmat, cmat, lvec, d_skip, ctx_len):
    nb, s_tot, width = u.shape
    nch = s_tot // S5_STEPS
    nctx = ctx_len // S5_STEPS
    rows = nb * S5_STEPS
    r = np.arange(rows)
    perm = np.zeros((rows, rows), np.float32)
    perm[r, (r % nb) * S5_STEPS + r // nb] = 1.0
    perm_tm = jnp.asarray(perm, BF16)
    perm_bm = jnp.asarray(perm.T, BF16)

    def chunk(d, c):
        back = jnp.where(c < nctx, nctx - 1 - c, nctx + nch - 1 - c)
        return jnp.where(d == 1, back, c)

    return pl.pallas_call(
        _s5_kernel,
        grid=(2, nch),
        in_specs=[pl.BlockSpec((nb, S5_STEPS, width), lambda d, c: (0, chunk(d, c), 0)),
                  pl.BlockSpec((rows, rows), lambda d, c: (0, 0)),
                  pl.BlockSpec((rows, rows), lambda d, c: (0, 0)),
                  pl.BlockSpec((1, width, 2 * SSM_LANES), lambda d, c: (d, 0, 0)),
                  pl.BlockSpec((1, 2 * SSM_LANES, width), lambda d, c: (d, 0, 0)),
                  pl.BlockSpec((1, 1, 2 * SSM_LANES), lambda d, c: (d, 0, 0)),
                  pl.BlockSpec((1, width), lambda d, c: (0, 0))],
        out_specs=pl.BlockSpec((1, nb, S5_STEPS, width), lambda d, c: (d, 0, chunk(d, c), 0)),
        out_shape=jax.ShapeDtypeStruct((2, nb, s_tot, width), F32),
        scratch_shapes=[pltpu.VMEM((S5_STEPS, nb, 2 * SSM_LANES), F32),
                        pltpu.VMEM((nb, 2 * SSM_LANES), F32)],
        compiler_params=_cparams(("arbitrary", "arbitrary")),
        name="s5_scan",
    )(u, perm_tm, perm_bm, bmat, cmat, lvec, d_skip.reshape(1, width))


def _conv_kernel(cv_ref, w_ref, b_ref, lg_ref, lb_ref, o_ref, hp_ref, win_ref, shift_ref, *, ctx_len):
    s_tot = cv_ref.shape[1]
    lat = s_tot - ctx_len
    rc = RET_CHUNK
    zeros = jnp.zeros((CONV_PAD, CONV_WIDTH), F32)
    a = cv_ref[0, :, 0:CONV_WIDTH]
    g = cv_ref[0, :, CONV_WIDTH:2 * CONV_WIDTH]
    h = a * jax.nn.sigmoid(g)
    hp_ref[0:CONV_PAD] = zeros
    hp_ref[CONV_PAD:CONV_PAD + ctx_len] = h[0:ctx_len]
    hp_ref[CONV_PAD + ctx_len:2 * CONV_PAD + ctx_len] = zeros
    hp_ref[2 * CONV_PAD + ctx_len:2 * CONV_PAD + s_tot] = h[ctx_len:s_tot]
    hp_ref[2 * CONV_PAD + s_tot:3 * CONV_PAD + s_tot] = zeros
    del lat

    def chunk(c, carry):
        r0 = pl.multiple_of(c * rc, rc)
        wbase = pl.multiple_of(r0 + jnp.where(r0 >= ctx_len, CONV_PAD, 0), 8)
        win_ref[...] = hp_ref[pl.ds(wbase, rc + 2 * CONV_PAD), :]
        for r in range(1, 8):
            shift_ref[r] = win_ref[r:r + rc + 2 * CONV_PAD - 8, :]
        acc = jnp.zeros((rc, CONV_WIDTH), F32) + b_ref[...]
        for k in range(CONV_K):
            off = k + CONV_PAD - CONV_K // 2
            base = off - off % 8
            if off % 8 == 0:
                tap = win_ref[base:base + rc, :]
            else:
                tap = shift_ref[off % 8, base:base + rc, :]
            acc = acc + w_ref[k:k + 1, :] * tap
        mu = jnp.mean(acc, axis=-1, keepdims=True)
        var = jnp.mean(jnp.square(acc - mu), axis=-1, keepdims=True)
        y = (acc - mu) * lax.rsqrt(var + EPS) * lg_ref[...] + lb_ref[...]
        o_ref[0, pl.ds(r0, rc), :] = y * jax.nn.sigmoid(y)
        return carry

    lax.fori_loop(0, s_tot // rc, chunk, 0)


def _conv_branch(cv, w_dw, b_dw, ln_g, ln_b, ctx_len):
    b, s_tot, _ = cv.shape
    w = jnp.concatenate([w_dw.reshape(CONV_K, CONV_WIDTH), jnp.zeros((1, CONV_WIDTH), F32)], axis=0)
    vec = lambda t: t.reshape(1, CONV_WIDTH)
    cst = lambda shape: pl.BlockSpec(shape, lambda i: (0,) * len(shape))
    return pl.pallas_call(
        functools.partial(_conv_kernel, ctx_len=ctx_len),
        grid=(b,),
        in_specs=[pl.BlockSpec((1, s_tot, 2 * CONV_WIDTH), lambda i: (i, 0, 0)),
                  cst((CONV_K + 1, CONV_WIDTH)), cst((1, CONV_WIDTH)),
                  cst((1, CONV_WIDTH)), cst((1, CONV_WIDTH))],
        out_specs=pl.BlockSpec((1, s_tot, CONV_WIDTH), lambda i: (i, 0, 0)),
        out_shape=jax.ShapeDtypeStruct((b, s_tot, CONV_WIDTH), F32),
        scratch_shapes=[pltpu.VMEM((s_tot + 3 * CONV_PAD, CONV_WIDTH), F32),
                        pltpu.VMEM((RET_CHUNK + 2 * CONV_PAD, CONV_WIDTH), F32),
                        pltpu.VMEM((8, RET_CHUNK + 2 * CONV_PAD - 8, CONV_WIDTH), F32)],
        compiler_params=_cparams(("arbitrary",)),
        name="conv_branch",
    )(cv, w, vec(b_dw), vec(ln_g), vec(ln_b))


N_TAB = 5


def _ret_tables(ctx_len, seq):
    n = RET_HEAD_DIM // 4
    inv_freq = ROPE_BASE ** (-np.arange(n, dtype=np.float64) / n)
    pos = np.arange(seq)
    ang_r = (pos // GRID_W)[:, None] * inv_freq
    ang_c = (pos % GRID_W)[:, None] * inv_freq
    cos = np.concatenate([np.cos(ang_r), np.cos(ang_r), np.cos(ang_c), np.cos(ang_c)], axis=-1)
    sin = np.concatenate([-np.sin(ang_r), np.sin(ang_r), -np.sin(ang_c), np.sin(ang_c)], axis=-1)
    cos = np.concatenate([np.ones((ctx_len, RET_HEAD_DIM)), cos], axis=0)
    sin = np.concatenate([np.zeros((ctx_len, RET_HEAD_DIM)), sin], axis=0)
    log_g_fwd = np.log1p(-np.exp2(-5.0 - np.arange(RET_HEADS, dtype=np.float64)))
    idx = np.arange(RET_CHUNK, dtype=np.float64)
    diff = idx[:, None] - idx[None, :]
    tabs = np.zeros((2, RET_HEADS, N_TAB, RET_CHUNK, RET_CHUNK))
    ones = np.ones((RET_CHUNK, RET_CHUNK))
    for d, log_g in enumerate((log_g_fwd, log_g_fwd[::-1])):
        for hd in range(RET_HEADS):
            lg = log_g[hd]
            if d == 0:
                mask = np.where(diff >= 0, np.exp(lg * np.maximum(diff, 0.0)), 0.0)
                dec = np.exp(lg * (idx + 1.0))
                zeta = np.exp(lg * (RET_CHUNK - 1 - idx))
            else:
                mask = np.where(diff <= 0, np.exp(lg * np.maximum(-diff, 0.0)), 0.0)
                dec = np.exp(lg * (RET_CHUNK - idx))
                zeta = np.exp(lg * idx)
            tabs[d, hd, 0] = mask
            tabs[d, hd, 1] = dec[:, None] * ones
            tabs[d, hd, 2] = zeta[:, None] * ones
            tabs[d, hd, 3] = np.exp(lg * RET_CHUNK) * ones
            tabs[d, hd, 4] = zeta[None, :] * ones
    return (jnp.asarray(cos, F32), jnp.asarray(sin, F32), jnp.asarray(tabs, F32))


def _ret_kernel(q_ref, k_ref, v_ref, g_ref, cos_ref, sin_ref, tab_ref, o_ref,
                qs_ref, qd_ref, kt_ref, kz_ref, out_ref, state_ref, *, ctx_len):
    s_tot = q_ref.shape[1]
    rc = RET_CHUNK
    nch = s_tot // rc
    nctx = ctx_len // rc
    k_scale = RET_HEAD_DIM ** -0.5
    lane = lax.broadcasted_iota(jnp.int32, (rc, RET_HEAD_DIM), 1)
    first_half = (lane % (RET_HEAD_DIM // 2)) < (RET_HEAD_DIM // 4)

    def rope(t, cs, sn):
        quarter = RET_HEAD_DIM // 4
        partner = jnp.where(first_half, pltpu.roll(t, RET_HEAD_DIM - quarter, 1), pltpu.roll(t, quarter, 1))
        return t * cs + partner * sn

    def prepare(c, carry):
        rows = pl.ds(pl.multiple_of(c * rc, rc), rc)
        cs = cos_ref[rows, :]
        sn = sin_ref[rows, :]
        q = rope(q_ref[0, rows, :].astype(F32), cs, sn)
        kt = (rope(k_ref[0, rows, :].astype(F32), cs, sn) * k_scale).T
        qs_ref[c] = q.astype(BF16)
        kt_ref[c] = kt.astype(BF16)
        for d in range(2):
            qd_ref[d, c] = (q * tab_ref[d, 0, 1]).astype(BF16)
            kz_ref[d, c] = (kt * tab_ref[d, 0, 4]).astype(BF16)
        return carry

    lax.fori_loop(0, nch, prepare, 0, unroll=2)
    state_ref[...] = jnp.zeros_like(state_ref)

    def chunk(c, d):
        rows = pl.ds(pl.multiple_of(c * rc, rc), rc)
        vb = v_ref[0, rows, :]
        state = state_ref[d]
        scores = jnp.dot(qs_ref[c], kt_ref[c], preferred_element_type=F32) * tab_ref[d, 0, 0]
        o = (jnp.dot(scores.astype(BF16), vb, preferred_element_type=F32)
             + jnp.dot(qd_ref[d, c], state.astype(BF16), preferred_element_type=F32))
        kv = jnp.dot(kz_ref[d, c], vb, preferred_element_type=F32)
        state_ref[d] = tab_ref[d, 0, 3] * state + kv
        out_ref[d, rows, :] = o

    def step(j, carry):
        chunk(j, 0)
        chunk(jnp.where(j < nctx, nctx - 1 - j, nctx + nch - 1 - j), 1)
        return carry

    lax.fori_loop(0, nch, step, 0, unroll=3)
    g = g_ref[0].astype(F32)
    o_ref[0] = (_rms(out_ref[0] + out_ref[1]) * (g * jax.nn.sigmoid(g))).astype(o_ref.dtype)


def _retention(qkvg, cos, sin, tabs, ctx_len):
    b, s_tot, _ = qkvg.shape
    hd = RET_HEAD_DIM
    nch = s_tot // RET_CHUNK
    part = lambda p: pl.BlockSpec((1, s_tot, hd), lambda i, h: (i, 0, p * RET_HEADS + h))
    return pl.pallas_call(
        functools.partial(_ret_kernel, ctx_len=ctx_len),
        grid=(b, RET_HEADS),
        in_specs=[part(0), part(1), part(2), part(3),
                  pl.BlockSpec((s_tot, hd), lambda i, h: (0, 0)),
                  pl.BlockSpec((s_tot, hd), lambda i, h: (0, 0)),
                  pl.BlockSpec((2, 1, N_TAB, RET_CHUNK, RET_CHUNK), lambda i, h: (0, h, 0, 0, 0))],
        out_specs=pl.BlockSpec((1, s_tot, hd), lambda i, h: (i, 0, h)),
        out_shape=jax.ShapeDtypeStruct((b, s_tot, RET_WIDTH), BF16),
        scratch_shapes=[pltpu.VMEM((nch, RET_CHUNK, hd), BF16),
                        pltpu.VMEM((2, nch, RET_CHUNK, hd), BF16),
                        pltpu.VMEM((nch, hd, RET_CHUNK), BF16),
                        pltpu.VMEM((2, nch, hd, RET_CHUNK), BF16),
                        pltpu.VMEM((2, s_tot, hd), F32),
                        pltpu.VMEM((2, hd, hd), F32)],
        compiler_params=_cparams(("arbitrary", "arbitrary")),
        name="retention",
    )(qkvg, qkvg, qkvg, qkvg, cos, sin, tabs)


def _merge_kernel(x_ref, yf_ref, yb_ref, hc_ref, ro_ref, gates_ref, mod_ref, g2_ref,
                  wglu_ref, bglu_ref, wpw_ref, bpw_ref, wo_ref, wout_ref, wr_ref, br_ref, tril_ref,
                  x1_ref, h2_ref, ti_ref, tg_ref, cnt_out_ref, cnt_ref, lg_ref):
    d = D_MODEL
    mod = mod_ref[0, 0]

    @pl.when(jnp.logical_and(pl.program_id(0) % MOE_GROUP_BATCH == 0, pl.program_id(1) == 0))
    def _():
        cnt_ref[...] = jnp.zeros_like(cnt_ref)

    for half in range(ROW_TILE // HALF_TILE):
        rows = pl.ds(half * HALF_TILE, HALF_TILE)
        ys = jax.nn.gelu(yf_ref[0, 0, rows, :] + yb_ref[0, 0, rows, :]).astype(BF16)
        z = jnp.dot(ys, wglu_ref[...], preferred_element_type=F32) + bglu_ref[...]
        y_ssm = z[:, 0:d] * jax.nn.sigmoid(z[:, d:2 * d])
        y_conv = jnp.dot(hc_ref[0, rows, :].astype(BF16), wpw_ref[...],
                         preferred_element_type=F32) + bpw_ref[...]
        y_ret = jnp.dot(ro_ref[0, rows, :].astype(BF16), wo_ref[...], preferred_element_type=F32)
        m = (gates_ref[0, rows, 0:d].astype(F32) * y_ssm
             + gates_ref[0, rows, d:2 * d].astype(F32) * y_conv
             + gates_ref[0, rows, 2 * d:3 * d].astype(F32) * y_ret)
        y = jnp.dot(m.astype(BF16), wout_ref[...], preferred_element_type=F32)
        x1 = x_ref[0, rows, :] + mod[2:3] * y
        x1_ref[0, rows, :] = x1
        h2 = (_rms(x1) * g2_ref[...]) * (1.0 + mod[4:5]) + mod[3:4]
        h2_ref[0, rows, :] = h2
        h_hi = h2.astype(BF16)
        h_lo = (h2 - h_hi.astype(F32)).astype(BF16)
        lg_ref[rows, :] = (jnp.dot(h_hi, wr_ref[0], preferred_element_type=F32)
                           + jnp.dot(h_lo, wr_ref[0], preferred_element_type=F32)
                           + jnp.dot(h_hi, wr_ref[1], preferred_element_type=F32)) + br_ref[...]
    logits = lg_ref[...]
    lane = lax.broadcasted_iota(jnp.int32, logits.shape, 1)
    lane_f = lane.astype(F32)
    neg = jnp.float32(-jnp.inf)
    l = jnp.where(lane < N_EXPERTS, logits, neg)
    ti = jnp.zeros(logits.shape, F32)
    vals = []
    picks = []
    for k in range(TOP_K):
        top = jnp.max(l, axis=-1, keepdims=True)
        idx = jnp.min(jnp.where(l == top, lane_f, float(LOGIT_LANES)), axis=-1, keepdims=True)
        ti = jnp.where(lane == k, idx, ti)
        vals.append(top)
        picked = lane_f == idx
        picks.append(picked.astype(F32))
        l = jnp.where(picked, neg, l)
    es = [jnp.exp(v - vals[0]) for v in vals]
    tot = es[0]
    for e in es[1:]:
        tot = tot + e
    tg = jnp.zeros(logits.shape, F32)
    for k in range(TOP_K):
        tg = jnp.where(lane == k, es[k] / tot, tg)
    seen = cnt_ref[0:1, :]
    chosen = picks[0] + picks[1] + picks[2] + picks[3]
    before = seen + jnp.dot(tril_ref[...], chosen.astype(BF16), preferred_element_type=F32)
    for k in range(TOP_K):
        rank = jnp.sum(picks[k] * before, axis=-1, keepdims=True)
        ti = jnp.where(lane == TOP_K + k, rank, ti)
    seen = seen + jnp.sum(chosen, axis=0, keepdims=True)
    ti_ref[0] = ti.astype(jnp.int32)
    tg_ref[0] = tg
    cnt_ref[...] = jnp.broadcast_to(seen, cnt_ref.shape)
    cnt_out_ref[0] = jnp.broadcast_to(seen, cnt_out_ref.shape[1:])


def _merge(x, y_s5, hc, ro, gates, modt, g2, wglu, bglu, wpw, bpw, wo, wout, wr, br, ctx_len):
    b, s_tot, d = x.shape
    nt = s_tot // ROW_TILE
    nctx = ctx_len // ROW_TILE
    seg = lambda j: jnp.where(j >= nctx, 1, 0)
    row = lambda w: pl.BlockSpec((1, ROW_TILE, w), lambda i, j: (i, j, 0))
    cst = lambda shape: pl.BlockSpec(shape, lambda i, j: (0,) * len(shape))
    y2 = y_s5
    tril = jnp.asarray(np.tril(np.ones((ROW_TILE, ROW_TILE), np.float32), -1), BF16)
    return pl.pallas_call(
        _merge_kernel,
        grid=(b, nt),
        in_specs=[row(d),
                  pl.BlockSpec((1, 1, ROW_TILE, SSM_WIDTH), lambda i, j: (0, i, j, 0)),
                  pl.BlockSpec((1, 1, ROW_TILE, SSM_WIDTH), lambda i, j: (1, i, j, 0)),
                  row(CONV_WIDTH), row(RET_WIDTH), row(3 * d),
                  pl.BlockSpec((1, 1, 8, d), lambda i, j: (i, seg(j), 0, 0)),
                  cst((1, d)),
                  cst((SSM_WIDTH, 2 * d)), cst((1, 2 * d)),
                  cst((CONV_WIDTH, d)), cst((1, d)),
                  cst((RET_WIDTH, d)), cst((d, d)),
                  cst((2, d, LOGIT_LANES)), cst((1, LOGIT_LANES)),
                  cst((ROW_TILE, ROW_TILE))],
        out_specs=[row(d), row(d), row(LOGIT_LANES), row(LOGIT_LANES),
                   pl.BlockSpec((1, 8, LOGIT_LANES), lambda i, j: (i // MOE_GROUP_BATCH, 0, 0))],
        out_shape=[jax.ShapeDtypeStruct((b, s_tot, d), F32),
                   jax.ShapeDtypeStruct((b, s_tot, d), F32),
                   jax.ShapeDtypeStruct((b, s_tot, LOGIT_LANES), jnp.int32),
                   jax.ShapeDtypeStruct((b, s_tot, LOGIT_LANES), F32),
                   jax.ShapeDtypeStruct((b // MOE_GROUP_BATCH, 8, LOGIT_LANES), F32)],
        scratch_shapes=[pltpu.VMEM((8, LOGIT_LANES), F32), pltpu.VMEM((ROW_TILE, LOGIT_LANES), F32)],
        compiler_params=_cparams(("arbitrary", "arbitrary")),
        name="merge_router",
    )(x, y2, y2, hc, ro, gates, modt, g2.reshape(1, d), wglu, bglu.reshape(1, 2 * d),
      wpw, bpw.reshape(1, d), wo, wout, wr, br, tril)


PLAN_TILE, PLAN_EXPERT, PLAN_LO, PLAN_HI, PLAN_NEXT, PLAN_SLOT = range(6)
PLAN_LANES = 256


def _plan_kernel(cnt_ref, plan_ref, *, n_tiles):
    ne = N_EXPERTS
    rows_per = float(MOE_ROWS)
    e = lax.broadcasted_iota(jnp.int32, (ne, PLAN_LANES), 0)
    e_f = e.astype(F32)
    v = lax.broadcasted_iota(jnp.int32, (ne, PLAN_LANES), 1).astype(F32)

    def cumsum(t):
        for s in (1, 2, 4, 8, 16):
            t = t + jnp.where(e >= s, pltpu.roll(t, s, 0), 0.0)
        return t

    def lookup(table, onehot):
        return jnp.sum(onehot * table, axis=0, keepdims=True)

    c = jnp.broadcast_to(cnt_ref[...], (ne, PLAN_LANES))
    has = c > 0.0
    ends = cumsum(c)
    starts = ends - c
    first_tile = jnp.floor(starts / rows_per)
    tiles_per = jnp.where(has, jnp.floor((ends - 1.0) / rows_per) - first_tile + 1.0, 0.0)
    vend = cumsum(tiles_per)
    vstart = vend - tiles_per
    total = jnp.max(vend, axis=0, keepdims=True)
    slot = cumsum(has.astype(F32)) - 1.0
    slot = slot - 2.0 * jnp.floor(slot / 2.0)
    later = jnp.where(has, e_f, float(ne))
    nxt = jnp.where(e < ne - 1, pltpu.roll(later, ne - 1, 0), float(ne))
    for s in (1, 2, 4, 8, 16):
        nxt = jnp.minimum(nxt, jnp.where(e < ne - s, pltpu.roll(nxt, ne - s, 0), float(ne)))
    nxt = jnp.where(nxt >= float(ne), -1.0, nxt)
    last_with_rows = jnp.max(jnp.where(has, e_f, 0.0), axis=0, keepdims=True)

    valid = v[0:1] < total
    ve = jnp.sum((v >= vend).astype(F32), axis=0, keepdims=True)
    ve = jnp.where(valid, jnp.minimum(ve, float(ne - 1)), last_with_rows)
    onehot = (e_f == ve).astype(F32)
    vt = jnp.where(valid, lookup(first_tile, onehot) + v[0:1] - lookup(vstart, onehot),
                   float(n_tiles - 1))
    lo = jnp.where(valid, jnp.clip(lookup(starts, onehot) - vt * rows_per, 0.0, rows_per), 0.0)
    hi = jnp.where(valid, jnp.clip(lookup(ends, onehot) - vt * rows_per, 0.0, rows_per), 0.0)
    fields = {PLAN_TILE: vt, PLAN_EXPERT: ve, PLAN_LO: lo, PLAN_HI: hi,
              PLAN_NEXT: lookup(nxt, onehot), PLAN_SLOT: lookup(slot, onehot)}
    row = lax.broadcasted_iota(jnp.int32, (8, PLAN_LANES), 0)
    plan = jnp.zeros((8, PLAN_LANES), F32)
    for r, val in fields.items():
        plan = jnp.where(row == r, val, plan)
    plan_ref[...] = plan.astype(jnp.int32)


def _plan(counts_col, n_tiles):
    return pl.pallas_call(
        functools.partial(_plan_kernel, n_tiles=n_tiles),
        out_shape=jax.ShapeDtypeStruct((8, PLAN_LANES), jnp.int32),
        name="moe_plan",
    )(counts_col)


def _expert_kernel(plan_ref, x_ref, wgu_hbm, bgu_ref, wd_hbm, bd_ref, y_ref,
                   wgu_f32, wd_f32, wgu_bf, wd_bf, sem, *, layer):
    v = pl.program_id(0)
    prev = jnp.maximum(v - 1, 0)
    lo = plan_ref[PLAN_LO, v]
    hi = plan_ref[PLAN_HI, v]
    expert = plan_ref[PLAN_EXPERT, v]
    next_expert = plan_ref[PLAN_NEXT, v]
    active = hi > lo
    fresh_weights = jnp.logical_or(v == 0, expert != plan_ref[PLAN_EXPERT, prev])
    first_visit = jnp.logical_or(v == 0, plan_ref[PLAN_TILE, v] != plan_ref[PLAN_TILE, prev])

    def weight_copies(expert, slot):
        return (pltpu.make_async_copy(wgu_hbm.at[layer, expert], wgu_f32.at[slot], sem.at[0, slot]),
                pltpu.make_async_copy(wd_hbm.at[layer, expert], wd_f32.at[slot], sem.at[1, slot]))

    @pl.when(jnp.logical_and(fresh_weights, active))
    def _():
        slot = plan_ref[PLAN_SLOT, v]

        @pl.when(v == 0)
        def _():
            for c in weight_copies(expert, slot):
                c.start()

        for c in weight_copies(expert, slot):
            c.wait()

        @pl.when(next_expert >= 0)
        def _():
            for c in weight_copies(next_expert, 1 - slot):
                c.start()

        wgu_bf[...] = wgu_f32[slot].astype(BF16)
        wd_bf[...] = wd_f32[slot].astype(BF16)

    whole = jnp.logical_and(lo == 0, hi == MOE_ROWS)

    def ffn(rows):
        f = EXPERT_FF
        gu = jnp.dot(x_ref[rows, :].astype(BF16), wgu_bf[...],
                     preferred_element_type=F32) + bgu_ref[0, 0]
        gate = jnp.minimum(gu[:, 0:f], SWIGLU_LIMIT)
        up = jnp.clip(gu[:, f:2 * f], -SWIGLU_LIMIT, SWIGLU_LIMIT)
        act = (up + 1.0) * gate * jax.nn.sigmoid(gate * SWIGLU_ALPHA)
        y = jnp.dot(act.astype(BF16), wd_bf[...], preferred_element_type=F32) + bd_ref[0, 0]
        return y.astype(y_ref.dtype)

    parts = [pl.ds(p * MOE_PART, MOE_PART) for p in range(MOE_ROWS // MOE_PART)]

    @pl.when(whole)
    def _():
        for rows in parts:
            y_ref[rows, :] = ffn(rows)

    @pl.when(jnp.logical_and(first_visit, jnp.logical_not(whole)))
    def _():
        y_ref[...] = jnp.zeros_like(y_ref)

    for p in range(MOE_ROWS // MOE_EDGE):
        rows = pl.ds(p * MOE_EDGE, MOE_EDGE)
        touched = jnp.logical_and(lo < (p + 1) * MOE_EDGE, hi > p * MOE_EDGE)

        @pl.when(jnp.logical_and(touched, jnp.logical_not(whole)))
        def _():
            row = p * MOE_EDGE + lax.broadcasted_iota(jnp.int32, (MOE_EDGE, 1), 0)
            mine = jnp.logical_and(row >= lo, row < hi)
            y_ref[rows, :] = jnp.where(mine, ffn(rows), y_ref[rows, :])


def _experts(xs, plan, w_gu, b_gu, w_down, b_down, layer):
    n_rows, d = xs.shape
    n_visits = n_rows // MOE_ROWS + N_EXPERTS - 1
    assert n_visits <= PLAN_LANES
    f = EXPERT_FF
    wmap = lambda v, plan: (layer, plan[PLAN_EXPERT, v], 0, 0)
    tmap = lambda v, plan: (plan[PLAN_TILE, v], 0)
    grid_spec = pltpu.PrefetchScalarGridSpec(
        num_scalar_prefetch=1,
        grid=(n_visits,),
        in_specs=[pl.BlockSpec((MOE_ROWS, d), tmap),
                  pl.BlockSpec(memory_space=pl.ANY),
                  pl.BlockSpec((1, 1, 1, 2 * f), wmap),
                  pl.BlockSpec(memory_space=pl.ANY),
                  pl.BlockSpec((1, 1, 1, d), wmap)],
        out_specs=pl.BlockSpec((MOE_ROWS, d), tmap),
        scratch_shapes=[pltpu.VMEM((2, d, 2 * f), F32), pltpu.VMEM((2, f, d), F32),
                        pltpu.VMEM((d, 2 * f), BF16), pltpu.VMEM((f, d), BF16),
                        pltpu.SemaphoreType.DMA((2, 2))],
    )
    return pl.pallas_call(
        functools.partial(_expert_kernel, layer=layer),
        grid_spec=grid_spec,
        out_shape=jax.ShapeDtypeStruct((n_rows, d), BF16),
        compiler_params=_cparams(("arbitrary",)),
        name="experts",
    )(plan, xs, w_gu, b_gu.reshape(DEPTH, N_EXPERTS, 1, 2 * f),
      w_down, b_down.reshape(DEPTH, N_EXPERTS, 1, d))


def _combine_kernel(x_ref, y0_ref, y1_ref, y2_ref, y3_ref, gt_ref, mod_ref, fg_ref, o_ref, *, final):
    mod = mod_ref[0, 0]
    gt = gt_ref[0]
    y = gt[:, 0:1] * y0_ref[0].astype(F32)
    for k, y_ref in ((1, y1_ref), (2, y2_ref), (3, y3_ref)):
        y = y + gt[:, k:k + 1] * y_ref[0].astype(F32)
    x2 = x_ref[0] + mod[5:6] * y
    o_ref[0] = _rms(x2) * fg_ref[...] if final else x2


def _combine(x1, yg, gates, modt, final_g, ctx_len, final, group):
    b, s_tot, d = x1.shape
    gb = MOE_GROUP_BATCH
    b0 = group * gb
    nt = s_tot // ROW_TILE
    nctx = ctx_len // ROW_TILE
    off = nctx if final else 0
    seg = lambda j: jnp.where(j + off >= nctx, 1, 0)
    choice = lambda k: pl.BlockSpec((1, ROW_TILE, d), lambda i, j: (k, i * nt + j + off, 0))
    if final:
        out_spec = pl.BlockSpec((1, ROW_TILE, d), lambda i, j: (i, j, 0))
        out_shape = jax.ShapeDtypeStruct((gb, s_tot - off * ROW_TILE, d), F32)
        aliases = {}
    else:
        out_spec = pl.BlockSpec((1, ROW_TILE, d), lambda i, j: (i + b0, j, 0))
        out_shape = jax.ShapeDtypeStruct((b, s_tot, d), F32)
        aliases = {0: 0}
    return pl.pallas_call(
        functools.partial(_combine_kernel, final=final),
        grid=(gb, nt - off),
        in_specs=[pl.BlockSpec((1, ROW_TILE, d), lambda i, j: (i + b0, j + off, 0)),
                  choice(0), choice(1), choice(2), choice(3),
                  pl.BlockSpec((1, ROW_TILE, LOGIT_LANES), lambda i, j: (i + b0, j + off, 0)),
                  pl.BlockSpec((1, 1, 8, d), lambda i, j: (i + b0, seg(j), 0, 0)),
                  pl.BlockSpec((1, d), lambda i, j: (0, 0))],
        out_specs=out_spec,
        out_shape=out_shape,
        input_output_aliases=aliases,
        compiler_params=_cparams(("arbitrary", "arbitrary")),
        name="moe_combine",
    )(x1, yg, yg, yg, yg, gates, modt, final_g.reshape(1, d))


def _moe(x1, h2, top, top_gate, counts, modt, final_g, exp_gu_w, exp_gu_b, exp_down_w, exp_down_b,
         layer, ctx_len, final):
    b, s_tot, d = x1.shape
    n_grp = MOE_GROUP_BATCH * s_tot
    n_assign = n_grp * TOP_K
    top = top.reshape(b * s_tot, LOGIT_LANES)
    h2 = h2.reshape(b * s_tot, d)
    x = x1
    finals = []
    for g in range(b // MOE_GROUP_BATCH):
        top_g = top[g * n_grp:(g + 1) * n_grp]
        cnt = counts[g, 0, :N_EXPERTS]
        plan = _plan(cnt.reshape(N_EXPERTS, 1), n_assign // MOE_ROWS)
        starts = (jnp.cumsum(cnt) - cnt).astype(jnp.int32)
        inv = jnp.take(starts, top_g[:, :TOP_K].reshape(-1)) + top_g[:, TOP_K:2 * TOP_K].reshape(-1)
        _, order = lax.sort((inv, jnp.arange(n_assign, dtype=jnp.int32)), num_keys=1)
        rows = order // TOP_K + g * n_grp
        xs = h2.at[rows].get(mode="promise_in_bounds")
        ys = _experts(xs, plan, exp_gu_w, exp_gu_b, exp_down_w, exp_down_b, layer)
        by_choice = inv.reshape(n_grp, TOP_K).T.reshape(-1)
        yg = ys.at[by_choice].get(mode="promise_in_bounds").reshape(TOP_K, n_grp, d)
        out = _combine(x, yg, top_gate, modt, final_g, ctx_len, final, g)
        if final:
            finals.append(out)
        else:
            x = out
    return jnp.concatenate(finals, axis=0) if final else x


def _mixer_layer(x, modt, i, ctx_len, tables, norm1_g, w_in_bf, s5p, ssm_d, conv_p, merge_p):
    b, s_tot, d = x.shape
    u2, cv, qkvg, gates = _in_proj(x, modt, norm1_g, w_in_bf, ctx_len)
    bmat, cmat, lvec = s5p
    y_s5 = _s5_scan(u2, bmat, cmat, lvec, ssm_d, ctx_len)
    hc = _conv_branch(cv, *conv_p, ctx_len)
    cos, sin, tabs = tables
    ro = _retention(qkvg, cos, sin, tabs, ctx_len)
    return _merge(x, y_s5, hc, ro, gates, modt, *merge_p, ctx_len)


def kernel(x, c, ctx, c_ctx, ada_w, ada_b, norm1_g, w_in, ssm_lam_re, ssm_lam_im, ssm_log_dt, ssm_b_re, ssm_b_im, ssm_c_re, ssm_c_im, ssm_d, ssm_glu_w, ssm_glu_b, conv_dw_w, conv_dw_b, conv_ln_g, conv_ln_b, conv_pw_w, conv_pw_b, ret_w_o, w_out, norm2_g, router_w, router_b, exp_gu_w, exp_gu_b, exp_down_w, exp_down_b, final_g):
    b, seq, d = x.shape
    ctx_len = ctx.shape[1]
    depth = ada_w.shape[0]
    xs = jnp.concatenate([ctx, x], axis=1)
    s_in = jnp.concatenate([c, c_ctx[None], jnp.zeros((16 - b - 1, d), F32)], axis=0)
    mod = _ada_mod(s_in, ada_w, ada_b).reshape(depth, 16, N_MOD, d)
    mod = jnp.pad(mod, ((0, 0), (0, 0), (0, 8 - N_MOD), (0, 0)))
    modt = jnp.stack([jnp.broadcast_to(mod[:, b][:, None], (depth, b, 8, d)), mod[:, :b]], axis=2)
    tables = _ret_tables(ctx_len, seq)
    wr = jnp.pad(router_w, ((0, 0), (0, 0), (0, LOGIT_LANES - N_EXPERTS)))
    wr_hi = wr.astype(BF16)
    wr = jnp.stack([wr_hi, (wr - wr_hi.astype(F32)).astype(BF16)], axis=1)
    br = jnp.pad(router_b, ((0, 0), (0, LOGIT_LANES - N_EXPERTS))).reshape(depth, 1, LOGIT_LANES)
    for i in range(depth):
        s5p = _s5_params(ssm_lam_re[i], ssm_lam_im[i], ssm_log_dt[i], ssm_b_re[i], ssm_b_im[i],
                         ssm_c_re[i], ssm_c_im[i])
        conv_p = (conv_dw_w[i], conv_dw_b[i], conv_ln_g[i], conv_ln_b[i])
        merge_p = (norm2_g[i], ssm_glu_w[i].astype(BF16), ssm_glu_b[i], conv_pw_w[i].astype(BF16),
                   conv_pw_b[i], ret_w_o[i].astype(BF16), w_out[i].astype(BF16), wr[i], br[i])
        x1, h2, top, top_gate, counts = _mixer_layer(xs, modt[i], i, ctx_len, tables, norm1_g[i],
                                                     w_in[i].astype(BF16), s5p, ssm_d[i], conv_p, merge_p)
        xs = _moe(x1, h2, top, top_gate, counts, modt[i], final_g, exp_gu_w, exp_gu_b, exp_down_w,
                  exp_down_b, i, ctx_len, final=(i == depth - 1))
    return xs
```

```python
import functools
import math

import numpy as np
import jax
import jax.numpy as jnp
from jax import lax
from jax.experimental import pallas as pl
from jax.experimental.pallas import tpu as pltpu

F32 = jnp.float32
BF16 = jnp.bfloat16

D_MODEL = 1024
DEPTH = 4
GRID_W = 64
N_MOD = 6
SSM_WIDTH = 256
SSM_GROUP = 16
SSM_GROUPS = SSM_WIDTH // SSM_GROUP
SSM_STATE = 64
SSM_LANES = SSM_GROUPS * SSM_STATE
CONV_WIDTH = 256
CONV_K = 31
CONV_PAD = 16
RET_HEADS = 4
RET_HEAD_DIM = 128
RET_WIDTH = RET_HEADS * RET_HEAD_DIM
RET_CHUNK = 128
ROPE_BASE = 10000.0
IN_WIDTH = SSM_WIDTH + 2 * CONV_WIDTH + 4 * RET_WIDTH + 3 * D_MODEL
COL_CV = SSM_WIDTH
COL_QKVG = COL_CV + 2 * CONV_WIDTH
COL_GATES = COL_QKVG + 4 * RET_WIDTH
N_EXPERTS = 32
TOP_K = 4
EXPERT_FF = D_MODEL
SWIGLU_LIMIT = 7.0
SWIGLU_ALPHA = 1.702
EPS = 1e-6

ROW_TILE = 256
HALF_TILE = 128
S5_STEPS = 64
S5_STRIP = 512
MOE_ROWS = 512
MOE_PART = 256
MOE_EDGE = 256
ROUTE_ROWS = 768
MOE_GROUP_BATCH = 4
LOGIT_LANES = 128
VMEM_LIMIT = 56 * 1024 * 1024


def _cparams(sem):
    return pltpu.CompilerParams(dimension_semantics=sem, vmem_limit_bytes=VMEM_LIMIT)


def _rms(x):
    return x * lax.rsqrt(jnp.mean(x * x, axis=-1, keepdims=True) + EPS)


def _ada_kernel(s_ref, w_ref, b_ref, o_ref):
    s = s_ref[...]
    s = s * jax.nn.sigmoid(s)
    o_ref[0] = jnp.dot(s, w_ref[0], preferred_element_type=F32,
                       precision=lax.Precision.HIGHEST) + b_ref[0]


def _ada_mod(s_in, ada_w, ada_b):
    depth, d, n = ada_w.shape
    tn = 1536
    return pl.pallas_call(
        _ada_kernel,
        grid=(depth, n // tn),
        in_specs=[pl.BlockSpec((16, d), lambda i, j: (0, 0)),
                  pl.BlockSpec((1, d, tn), lambda i, j: (i, 0, j)),
                  pl.BlockSpec((1, 1, tn), lambda i, j: (i, 0, j))],
        out_specs=pl.BlockSpec((1, 16, tn), lambda i, j: (i, 0, j)),
        out_shape=jax.ShapeDtypeStruct((depth, 16, n), F32),
        compiler_params=_cparams(("arbitrary", "arbitrary")),
        name="ada_mod",
    )(s_in, ada_w, ada_b.reshape(depth, 1, n))


def _inproj_kernel(x_ref, mod_ref, g_ref, w_ref, u_ref, cv_ref, qkvg_ref, gates_ref):
    x = x_ref[0]
    mod = mod_ref[0, 0]
    h = (_rms(x) * g_ref[...]) * (1.0 + mod[1:2]) + mod[0:1]
    hb = h.astype(BF16)
    u_ref[0] = jnp.dot(hb, w_ref[:, 0:COL_CV], preferred_element_type=F32)
    cv_ref[0] = jnp.dot(hb, w_ref[:, COL_CV:COL_QKVG], preferred_element_type=F32)
    qkvg_ref[0] = jnp.dot(hb, w_ref[:, COL_QKVG:COL_GATES], preferred_element_type=F32).astype(BF16)
    gates = jnp.dot(hb, w_ref[:, COL_GATES:IN_WIDTH], preferred_element_type=F32)
    gates_ref[0] = jax.nn.sigmoid(gates).astype(BF16)


def _in_proj(x, modt, g1, w_in_bf, ctx_len):
    b, s_tot, d = x.shape
    nt = s_tot // ROW_TILE
    nctx = ctx_len // ROW_TILE
    seg = lambda j: jnp.where(j >= nctx, 1, 0)
    return pl.pallas_call(
        _inproj_kernel,
        grid=(b, nt),
        in_specs=[pl.BlockSpec((1, ROW_TILE, d), lambda i, j: (i, j, 0)),
                  pl.BlockSpec((1, 1, 8, d), lambda i, j: (i, seg(j), 0, 0)),
                  pl.BlockSpec((1, d), lambda i, j: (0, 0)),
                  pl.BlockSpec((d, IN_WIDTH), lambda i, j: (0, 0))],
        out_specs=[pl.BlockSpec((1, ROW_TILE, SSM_WIDTH), lambda i, j: (i, j, 0)),
                   pl.BlockSpec((1, ROW_TILE, 2 * CONV_WIDTH), lambda i, j: (i, j, 0)),
                   pl.BlockSpec((1, ROW_TILE, 4 * RET_WIDTH), lambda i, j: (i, j, 0)),
                   pl.BlockSpec((1, ROW_TILE, 3 * D_MODEL), lambda i, j: (i, j, 0))],
        out_shape=[jax.ShapeDtypeStruct((b, s_tot, SSM_WIDTH), F32),
                   jax.ShapeDtypeStruct((b, s_tot, 2 * CONV_WIDTH), F32),
                   jax.ShapeDtypeStruct((b, s_tot, 4 * RET_WIDTH), BF16),
                   jax.ShapeDtypeStruct((b, s_tot, 3 * D_MODEL), BF16)],
        compiler_params=_cparams(("arbitrary", "arbitrary")),
        name="in_proj",
    )(x, modt, g1.reshape(1, d), w_in_bf)


def _s5_params(lam_re, lam_im, log_dt, b_re, b_im, c_re, c_im):
    lam = lax.complex(jnp.minimum(lam_re.astype(F32), -1e-4), lam_im.astype(F32))
    lam_dt = lam * jnp.exp(log_dt.astype(F32))[..., None]
    lam_bar = jnp.exp(lam_dt)
    b = lax.complex(b_re.astype(F32), b_im.astype(F32))
    b_bar = ((lam_bar - 1.0) / lam)[..., None] * b
    eye = jnp.eye(SSM_GROUPS, dtype=F32)

    def in_mat(t):
        return jnp.einsum('dgpm,gh->dgmhp', t, eye).reshape(2, SSM_WIDTH, SSM_LANES)

    def out_mat(t):
        return jnp.einsum('dgmp,gh->dgphm', t, eye).reshape(2, SSM_LANES, SSM_WIDTH)

    bmat = jnp.concatenate([in_mat(jnp.real(b_bar)), in_mat(jnp.imag(b_bar))], axis=-1)
    cmat = jnp.concatenate([out_mat(c_re.astype(F32)), -out_mat(c_im.astype(F32))], axis=1)
    lvec = jnp.concatenate([jnp.real(lam_bar).reshape(2, 1, SSM_LANES),
                            jnp.imag(lam_bar).reshape(2, 1, SSM_LANES)], axis=-1)
    return bmat.astype(BF16), cmat.astype(BF16), lvec


def _s5_kernel(u_ref, perm_ref, permt_ref, bmat_ref, cmat_ref, lam_ref, dsk_ref, y_ref, st_ref, carry_ref):
    d = pl.program_id(0)
    c = pl.program_id(1)
    nb, steps, width = u_ref.shape

    @pl.when(c == 0)
    def _():
        carry_ref[...] = jnp.zeros_like(carry_ref)

    u = u_ref[...].reshape(nb * steps, width)
    u_tm = jnp.dot(perm_ref[...], u.astype(BF16), preferred_element_type=F32).astype(BF16)
    bu = jnp.dot(u_tm, bmat_ref[0], preferred_element_type=F32)
    st_ref[...] = bu.reshape(steps, nb, 2 * SSM_LANES)

    for k in range(SSM_LANES // S5_STRIP):
        re = pl.ds(k * S5_STRIP, S5_STRIP)
        im = pl.ds(SSM_LANES + k * S5_STRIP, S5_STRIP)
        lr = jnp.broadcast_to(lam_ref[0, :, re], (nb, S5_STRIP))
        li = jnp.broadcast_to(lam_ref[0, :, im], (nb, S5_STRIP))

        def step(t, s):
            sr, si = s
            tt = jnp.where(d == 1, steps - 1 - t, t)
            nr = lr * sr - li * si + st_ref[tt, :, re]
            ni = lr * si + li * sr + st_ref[tt, :, im]
            st_ref[tt, :, re] = nr
            st_ref[tt, :, im] = ni
            return nr, ni

        sr, si = lax.fori_loop(0, steps, step, (carry_ref[:, re], carry_ref[:, im]), unroll=4)
        carry_ref[:, re] = sr
        carry_ref[:, im] = si

    st = st_ref[...].reshape(steps * nb, 2 * SSM_LANES)
    y_tm = jnp.dot(st.astype(BF16), cmat_ref[0], preferred_element_type=F32)
    y = jnp.dot(permt_ref[...], y_tm.astype(BF16), preferred_element_type=F32)
    skip = jnp.where(d == 0, 1.0, 0.0) * dsk_ref[...]
    y_ref[0] = (y + u * skip).reshape(nb, steps, width)


def _s5_scan(u, bmat, cmat, lvec, d_skip, ctx_len):
    nb, s_tot, width = u.shape
    nch = s_tot // S5_STEPS
    nctx = ctx_len // S5_STEPS
    rows = nb * S5_STEPS
    r = np.arange(rows)
    perm = np.zeros((rows, rows), np.float32)
    perm[r, (r % nb) * S5_STEPS + r // nb] = 1.0
    perm_tm = jnp.asarray(perm, BF16)
    perm_bm = jnp.asarray(perm.T, BF16)

    def chunk(d, c):
        back = jnp.where(c < nctx, nctx - 1 - c, nctx + nch - 1 - c)
        return jnp.where(d == 1, back, c)

    return pl.pallas_call(
        _s5_kernel,
        grid=(2, nch),
        in_specs=[pl.BlockSpec((nb, S5_STEPS, width), lambda d, c: (0, chunk(d, c), 0)),
                  pl.BlockSpec((rows, rows), lambda d, c: (0, 0)),
                  pl.BlockSpec((rows, rows), lambda d, c: (0, 0)),
                  pl.BlockSpec((1, width, 2 * SSM_LANES), lambda d, c: (d, 0, 0)),
                  pl.BlockSpec((1, 2 * SSM_LANES, width), lambda d, c: (d, 0, 0)),
                  pl.BlockSpec((1, 1, 2 * SSM_LANES), lambda d, c: (d, 0, 0)),
                  pl.BlockSpec((1, width), lambda d, c: (0, 0))],
        out_specs=pl.BlockSpec((1, nb, S5_STEPS, width), lambda d, c: (d, 0, chunk(d, c), 0)),
        out_shape=jax.ShapeDtypeStruct((2, nb, s_tot, width), F32),
        scratch_shapes=[pltpu.VMEM((S5_STEPS, nb, 2 * SSM_LANES), F32),
                        pltpu.VMEM((nb, 2 * SSM_LANES), F32)],
        compiler_params=_cparams(("arbitrary", "arbitrary")),
        name="s5_scan",
    )(u, perm_tm, perm_bm, bmat, cmat, lvec, d_skip.reshape(1, width))


def _conv_kernel(cv_ref, w_ref, b_ref, lg_ref, lb_ref, o_ref, hp_ref, win_ref, shift_ref, *, ctx_len):
    s_tot = cv_ref.shape[1]
    lat = s_tot - ctx_len
    rc = RET_CHUNK
    zeros = jnp.zeros((CONV_PAD, CONV_WIDTH), F32)
    a = cv_ref[0, :, 0:CONV_WIDTH]
    g = cv_ref[0, :, CONV_WIDTH:2 * CONV_WIDTH]
    h = a * jax.nn.sigmoid(g)
    hp_ref[0:CONV_PAD] = zeros
    hp_ref[CONV_PAD:CONV_PAD + ctx_len] = h[0:ctx_len]
    hp_ref[CONV_PAD + ctx_len:2 * CONV_PAD + ctx_len] = zeros
    hp_ref[2 * CONV_PAD + ctx_len:2 * CONV_PAD + s_tot] = h[ctx_len:s_tot]
    hp_ref[2 * CONV_PAD + s_tot:3 * CONV_PAD + s_tot] = zeros
    del lat

    def chunk(c, carry):
        r0 = pl.multiple_of(c * rc, rc)
        wbase = pl.multiple_of(r0 + jnp.where(r0 >= ctx_len, CONV_PAD, 0), 8)
        win_ref[...] = hp_ref[pl.ds(wbase, rc + 2 * CONV_PAD), :]
        for r in range(1, 8):
            shift_ref[r] = win_ref[r:r + rc + 2 * CONV_PAD - 8, :]
        acc = jnp.zeros((rc, CONV_WIDTH), F32) + b_ref[...]
        for k in range(CONV_K):
            off = k + CONV_PAD - CONV_K // 2
            base = off - off % 8
            if off % 8 == 0:
                tap = win_ref[base:base + rc, :]
            else:
                tap = shift_ref[off % 8, base:base + rc, :]
            acc = acc + w_ref[k:k + 1, :] * tap
        mu = jnp.mean(acc, axis=-1, keepdims=True)
        var = jnp.mean(jnp.square(acc - mu), axis=-1, keepdims=True)
        y = (acc - mu) * lax.rsqrt(var + EPS) * lg_ref[...] + lb_ref[...]
        o_ref[0, pl.ds(r0, rc), :] = y * jax.nn.sigmoid(y)
        return carry

    lax.fori_loop(0, s_tot // rc, chunk, 0)


def _conv_branch(cv, w_dw, b_dw, ln_g, ln_b, ctx_len):
    b, s_tot, _ = cv.shape
    w = jnp.concatenate([w_dw.reshape(CONV_K, CONV_WIDTH), jnp.zeros((1, CONV_WIDTH), F32)], axis=0)
    vec = lambda t: t.reshape(1, CONV_WIDTH)
    cst = lambda shape: pl.BlockSpec(shape, lambda i: (0,) * len(shape))
    return pl.pallas_call(
        functools.partial(_conv_kernel, ctx_len=ctx_len),
        grid=(b,),
        in_specs=[pl.BlockSpec((1, s_tot, 2 * CONV_WIDTH), lambda i: (i, 0, 0)),
                  cst((CONV_K + 1, CONV_WIDTH)), cst((1, CONV_WIDTH)),
                  cst((1, CONV_WIDTH)), cst((1, CONV_WIDTH))],
        out_specs=pl.BlockSpec((1, s_tot, CONV_WIDTH), lambda i: (i, 0, 0)),
        out_shape=jax.ShapeDtypeStruct((b, s_tot, CONV_WIDTH), F32),
        scratch_shapes=[pltpu.VMEM((s_tot + 3 * CONV_PAD, CONV_WIDTH), F32),
                        pltpu.VMEM((RET_CHUNK + 2 * CONV_PAD, CONV_WIDTH), F32),
                        pltpu.VMEM((8, RET_CHUNK + 2 * CONV_PAD - 8, CONV_WIDTH), F32)],
        compiler_params=_cparams(("arbitrary",)),
        name="conv_branch",
    )(cv, w, vec(b_dw), vec(ln_g), vec(ln_b))


N_TAB = 5


def _ret_tables(ctx_len, seq):
    n = RET_HEAD_DIM // 4
    inv_freq = ROPE_BASE ** (-np.arange(n, dtype=np.float64) / n)
    pos = np.arange(seq)
    ang_r = (pos // GRID_W)[:, None] * inv_freq
    ang_c = (pos % GRID_W)[:, None] * inv_freq
    cos = np.concatenate([np.cos(ang_r), np.cos(ang_r), np.cos(ang_c), np.cos(ang_c)], axis=-1)
    sin = np.concatenate([-np.sin(ang_r), np.sin(ang_r), -np.sin(ang_c), np.sin(ang_c)], axis=-1)
    cos = np.concatenate([np.ones((ctx_len, RET_HEAD_DIM)), cos], axis=0)
    sin = np.concatenate([np.zeros((ctx_len, RET_HEAD_DIM)), sin], axis=0)
    log_g_fwd = np.log1p(-np.exp2(-5.0 - np.arange(RET_HEADS, dtype=np.float64)))
    idx = np.arange(RET_CHUNK, dtype=np.float64)
    diff = idx[:, None] - idx[None, :]
    tabs = np.zeros((2, RET_HEADS, N_TAB, RET_CHUNK, RET_CHUNK))
    ones = np.ones((RET_CHUNK, RET_CHUNK))
    for d, log_g in enumerate((log_g_fwd, log_g_fwd[::-1])):
        for hd in range(RET_HEADS):
            lg = log_g[hd]
            if d == 0:
                mask = np.where(diff >= 0, np.exp(lg * np.maximum(diff, 0.0)), 0.0)
                dec = np.exp(lg * (idx + 1.0))
                zeta = np.exp(lg * (RET_CHUNK - 1 - idx))
            else:
                mask = np.where(diff <= 0, np.exp(lg * np.maximum(-diff, 0.0)), 0.0)
                dec = np.exp(lg * (RET_CHUNK - idx))
                zeta = np.exp(lg * idx)
            tabs[d, hd, 0] = mask
            tabs[d, hd, 1] = dec[:, None] * ones
            tabs[d, hd, 2] = zeta[:, None] * ones
            tabs[d, hd, 3] = np.exp(lg * RET_CHUNK) * ones
            tabs[d, hd, 4] = zeta[None, :] * ones
    return (jnp.asarray(cos, F32), jnp.asarray(sin, F32), jnp.asarray(tabs, F32))


def _ret_kernel(q_ref, k_ref, v_ref, g_ref, cos_ref, sin_ref, tab_ref, o_ref,
                qs_ref, qd_ref, kt_ref, kz_ref, out_ref, state_ref, *, ctx_len):
    s_tot = q_ref.shape[1]
    rc = RET_CHUNK
    nch = s_tot // rc
    nctx = ctx_len // rc
    k_scale = RET_HEAD_DIM ** -0.5
    lane = lax.broadcasted_iota(jnp.int32, (rc, RET_HEAD_DIM), 1)
    first_half = (lane % (RET_HEAD_DIM // 2)) < (RET_HEAD_DIM // 4)

    def rope(t, cs, sn):
        quarter = RET_HEAD_DIM // 4
        partner = jnp.where(first_half, pltpu.roll(t, RET_HEAD_DIM - quarter, 1), pltpu.roll(t, quarter, 1))
        return t * cs + partner * sn

    def prepare(c, carry):
        rows = pl.ds(pl.multiple_of(c * rc, rc), rc)
        cs = cos_ref[rows, :]
        sn = sin_ref[rows, :]
        q = rope(q_ref[0, rows, :].astype(F32), cs, sn)
        kt = (rope(k_ref[0, rows, :].astype(F32), cs, sn) * k_scale).T
        qs_ref[c] = q.astype(BF16)
        kt_ref[c] = kt.astype(BF16)
        for d in range(2):
            qd_ref[d, c] = (q * tab_ref[d, 0, 1]).astype(BF16)
            kz_ref[d, c] = (kt * tab_ref[d, 0, 4]).astype(BF16)
        return carry

    lax.fori_loop(0, nch, prepare, 0, unroll=2)
    state_ref[...] = jnp.zeros_like(state_ref)

    def chunk(c, d):
        rows = pl.ds(pl.multiple_of(c * rc, rc), rc)
        vb = v_ref[0, rows, :]
        state = state_ref[d]
        scores = jnp.dot(qs_ref[c], kt_ref[c], preferred_element_type=F32) * tab_ref[d, 0, 0]
        o = (jnp.dot(scores.astype(BF16), vb, preferred_element_type=F32)
             + jnp.dot(qd_ref[d, c], state.astype(BF16), preferred_element_type=F32))
        kv = jnp.dot(kz_ref[d, c], vb, preferred_element_type=F32)
        state_ref[d] = tab_ref[d, 0, 3] * state + kv
        out_ref[d, rows, :] = o

    def step(j, carry):
        chunk(j, 0)
        chunk(jnp.where(j < nctx, nctx - 1 - j, nctx + nch - 1 - j), 1)
        return carry

    lax.fori_loop(0, nch, step, 0, unroll=3)
    g = g_ref[0].astype(F32)
    o_ref[0] = (_rms(out_ref[0] + out_ref[1]) * (g * jax.nn.sigmoid(g))).astype(o_ref.dtype)


def _retention(qkvg, cos, sin, tabs, ctx_len):
    b, s_tot, _ = qkvg.shape
    hd = RET_HEAD_DIM
    nch = s_tot // RET_CHUNK
    part = lambda p: pl.BlockSpec((1, s_tot, hd), lambda i, h: (i, 0, p * RET_HEADS + h))
    return pl.pallas_call(
        functools.partial(_ret_kernel, ctx_len=ctx_len),
        grid=(b, RET_HEADS),
        in_specs=[part(0), part(1), part(2), part(3),
                  pl.BlockSpec((s_tot, hd), lambda i, h: (0, 0)),
                  pl.BlockSpec((s_tot, hd), lambda i, h: (0, 0)),
                  pl.BlockSpec((2, 1, N_TAB, RET_CHUNK, RET_CHUNK), lambda i, h: (0, h, 0, 0, 0))],
        out_specs=pl.BlockSpec((1, s_tot, hd), lambda i, h: (i, 0, h)),
        out_shape=jax.ShapeDtypeStruct((b, s_tot, RET_WIDTH), BF16),
        scratch_shapes=[pltpu.VMEM((nch, RET_CHUNK, hd), BF16),
                        pltpu.VMEM((2, nch, RET_CHUNK, hd), BF16),
                        pltpu.VMEM((nch, hd, RET_CHUNK), BF16),
                        pltpu.VMEM((2, nch, hd, RET_CHUNK), BF16),
                        pltpu.VMEM((2, s_tot, hd), F32),
                        pltpu.VMEM((2, hd, hd), F32)],
        compiler_params=_cparams(("arbitrary", "arbitrary")),
        name="retention",
    )(qkvg, qkvg, qkvg, qkvg, cos, sin, tabs)


def _merge_kernel(x_ref, yf_ref, yb_ref, hc_ref, ro_ref, gates_ref, mod_ref, g2_ref,
                  wglu_ref, bglu_ref, wpw_ref, bpw_ref, wo_ref, wout_ref, wr_ref, br_ref,
                  x1_ref, h2_ref, lg_ref):
    d = D_MODEL
    mod = mod_ref[0, 0]

    for half in range(ROW_TILE // HALF_TILE):
        rows = pl.ds(half * HALF_TILE, HALF_TILE)
        ys = jax.nn.gelu(yf_ref[0, 0, rows, :] + yb_ref[0, 0, rows, :]).astype(BF16)
        z = jnp.dot(ys, wglu_ref[...], preferred_element_type=F32) + bglu_ref[...]
        y_ssm = z[:, 0:d] * jax.nn.sigmoid(z[:, d:2 * d])
        y_conv = jnp.dot(hc_ref[0, rows, :].astype(BF16), wpw_ref[...],
                         preferred_element_type=F32) + bpw_ref[...]
        y_ret = jnp.dot(ro_ref[0, rows, :].astype(BF16), wo_ref[...], preferred_element_type=F32)
        m = (gates_ref[0, rows, 0:d].astype(F32) * y_ssm
             + gates_ref[0, rows, d:2 * d].astype(F32) * y_conv
             + gates_ref[0, rows, 2 * d:3 * d].astype(F32) * y_ret)
        y = jnp.dot(m.astype(BF16), wout_ref[...], preferred_element_type=F32)
        x1 = x_ref[0, rows, :] + mod[2:3] * y
        x1_ref[0, rows, :] = x1
        h2 = (_rms(x1) * g2_ref[...]) * (1.0 + mod[4:5]) + mod[3:4]
        h2_ref[0, rows, :] = h2
        h_hi = h2.astype(BF16)
        h_lo = (h2 - h_hi.astype(F32)).astype(BF16)
        lg_ref[0, rows, :] = (jnp.dot(h_hi, wr_ref[0], preferred_element_type=F32)
                              + jnp.dot(h_lo, wr_ref[0], preferred_element_type=F32)
                              + jnp.dot(h_hi, wr_ref[1], preferred_element_type=F32)) + br_ref[...]


def _router_kernel(lg_ref, tril_ref, ti_ref, tg_ref, cnt_out_ref, cnt_ref):
    @pl.when(jnp.logical_and(pl.program_id(0) % MOE_GROUP_BATCH == 0, pl.program_id(1) == 0))
    def _():
        cnt_ref[...] = jnp.zeros_like(cnt_ref)

    seen = cnt_ref[0:1, :]
    for part in range(ROUTE_ROWS // ROW_TILE):
        rows = pl.ds(part * ROW_TILE, ROW_TILE)
        logits = lg_ref[0, rows, :]
        lane = lax.broadcasted_iota(jnp.int32, logits.shape, 1)
        lane_f = lane.astype(F32)
        neg = jnp.float32(-jnp.inf)
        l = jnp.where(lane < N_EXPERTS, logits, neg)
        ti = jnp.zeros(logits.shape, F32)
        vals = []
        picks = []
        for k in range(TOP_K):
            top = jnp.max(l, axis=-1, keepdims=True)
            idx = jnp.min(jnp.where(l == top, lane_f, float(LOGIT_LANES)), axis=-1, keepdims=True)
            ti = jnp.where(lane == k, idx, ti)
            vals.append(top)
            picked = lane_f == idx
            picks.append(picked.astype(F32))
            l = jnp.where(picked, neg, l)
        es = [jnp.exp(v - vals[0]) for v in vals]
        tot = es[0]
        for e in es[1:]:
            tot = tot + e
        tg = jnp.zeros(logits.shape, F32)
        for k in range(TOP_K):
            tg = jnp.where(lane == k, es[k] / tot, tg)
        chosen = picks[0] + picks[1] + picks[2] + picks[3]
        before = seen + jnp.dot(tril_ref[...], chosen.astype(BF16), preferred_element_type=F32)
        for k in range(TOP_K):
            rank = jnp.sum(picks[k] * before, axis=-1, keepdims=True)
            ti = jnp.where(lane == TOP_K + k, rank, ti)
        seen = seen + jnp.sum(chosen, axis=0, keepdims=True)
        ti_ref[0, rows, :] = ti.astype(jnp.int32)
        tg_ref[0, rows, :] = tg
    cnt_ref[...] = jnp.broadcast_to(seen, cnt_ref.shape)
    cnt_out_ref[0] = jnp.broadcast_to(seen, cnt_out_ref.shape[1:])


def _router(logits):
    b, s_tot, _ = logits.shape
    tril = jnp.asarray(np.tril(np.ones((ROW_TILE, ROW_TILE), np.float32), -1), BF16)
    row = pl.BlockSpec((1, ROUTE_ROWS, LOGIT_LANES), lambda i, j: (i, j, 0))
    return pl.pallas_call(
        _router_kernel,
        grid=(b, s_tot // ROUTE_ROWS),
        in_specs=[row, pl.BlockSpec((ROW_TILE, ROW_TILE), lambda i, j: (0, 0))],
        out_specs=[row, row,
                   pl.BlockSpec((1, 8, LOGIT_LANES), lambda i, j: (i // MOE_GROUP_BATCH, 0, 0))],
        out_shape=[jax.ShapeDtypeStruct((b, s_tot, LOGIT_LANES), jnp.int32),
                   jax.ShapeDtypeStruct((b, s_tot, LOGIT_LANES), F32),
                   jax.ShapeDtypeStruct((b // MOE_GROUP_BATCH, 8, LOGIT_LANES), F32)],
        scratch_shapes=[pltpu.VMEM((8, LOGIT_LANES), F32)],
        compiler_params=_cparams(("arbitrary", "arbitrary")),
        name="router",
    )(logits, tril)


def _merge(x, y_s5, hc, ro, gates, modt, g2, wglu, bglu, wpw, bpw, wo, wout, wr, br, ctx_len):
    b, s_tot, d = x.shape
    nt = s_tot // ROW_TILE
    nctx = ctx_len // ROW_TILE
    seg = lambda j: jnp.where(j >= nctx, 1, 0)
    row = lambda w: pl.BlockSpec((1, ROW_TILE, w), lambda i, j: (i, j, 0))
    cst = lambda shape: pl.BlockSpec(shape, lambda i, j: (0,) * len(shape))
    y2 = y_s5
    return pl.pallas_call(
        _merge_kernel,
        grid=(b, nt),
        in_specs=[row(d),
                  pl.BlockSpec((1, 1, ROW_TILE, SSM_WIDTH), lambda i, j: (0, i, j, 0)),
                  pl.BlockSpec((1, 1, ROW_TILE, SSM_WIDTH), lambda i, j: (1, i, j, 0)),
                  row(CONV_WIDTH), row(RET_WIDTH), row(3 * d),
                  pl.BlockSpec((1, 1, 8, d), lambda i, j: (i, seg(j), 0, 0)),
                  cst((1, d)),
                  cst((SSM_WIDTH, 2 * d)), cst((1, 2 * d)),
                  cst((CONV_WIDTH, d)), cst((1, d)),
                  cst((RET_WIDTH, d)), cst((d, d)),
                  cst((2, d, LOGIT_LANES)), cst((1, LOGIT_LANES))],
        out_specs=[row(d), row(d), row(LOGIT_LANES)],
        out_shape=[jax.ShapeDtypeStruct((b, s_tot, d), F32),
                   jax.ShapeDtypeStruct((b, s_tot, d), F32),
                   jax.ShapeDtypeStruct((b, s_tot, LOGIT_LANES), F32)],
        compiler_params=_cparams(("arbitrary", "arbitrary")),
        name="merge_logits",
    )(x, y2, y2, hc, ro, gates, modt, g2.reshape(1, d), wglu, bglu.reshape(1, 2 * d),
      wpw, bpw.reshape(1, d), wo, wout, wr, br)


PLAN_TILE, PLAN_EXPERT, PLAN_LO, PLAN_HI, PLAN_NEXT, PLAN_SLOT = range(6)
PLAN_LANES = 256


def _plan_kernel(cnt_ref, plan_ref, *, n_tiles):
    ne = N_EXPERTS
    rows_per = float(MOE_ROWS)
    e = lax.broadcasted_iota(jnp.int32, (ne, PLAN_LANES), 0)
    e_f = e.astype(F32)
    v = lax.broadcasted_iota(jnp.int32, (ne, PLAN_LANES), 1).astype(F32)

    def cumsum(t):
        for s in (1, 2, 4, 8, 16):
            t = t + jnp.where(e >= s, pltpu.roll(t, s, 0), 0.0)
        return t

    def lookup(table, onehot):
        return jnp.sum(onehot * table, axis=0, keepdims=True)

    c = jnp.broadcast_to(cnt_ref[...], (ne, PLAN_LANES))
    has = c > 0.0
    ends = cumsum(c)
    starts = ends - c
    first_tile = jnp.floor(starts / rows_per)
    tiles_per = jnp.where(has, jnp.floor((ends - 1.0) / rows_per) - first_tile + 1.0, 0.0)
    vend = cumsum(tiles_per)
    vstart = vend - tiles_per
    total = jnp.max(vend, axis=0, keepdims=True)
    slot = cumsum(has.astype(F32)) - 1.0
    slot = slot - 2.0 * jnp.floor(slot / 2.0)
    later = jnp.where(has, e_f, float(ne))
    nxt = jnp.where(e < ne - 1, pltpu.roll(later, ne - 1, 0), float(ne))
    for s in (1, 2, 4, 8, 16):
        nxt = jnp.minimum(nxt, jnp.where(e < ne - s, pltpu.roll(nxt, ne - s, 0), float(ne)))
    nxt = jnp.where(nxt >= float(ne), -1.0, nxt)
    last_with_rows = jnp.max(jnp.where(has, e_f, 0.0), axis=0, keepdims=True)

    valid = v[0:1] < total
    ve = jnp.sum((v >= vend).astype(F32), axis=0, keepdims=True)
    ve = jnp.where(valid, jnp.minimum(ve, float(ne - 1)), last_with_rows)
    onehot = (e_f == ve).astype(F32)
    vt = jnp.where(valid, lookup(first_tile, onehot) + v[0:1] - lookup(vstart, onehot),
                   float(n_tiles - 1))
    lo = jnp.where(valid, jnp.clip(lookup(starts, onehot) - vt * rows_per, 0.0, rows_per), 0.0)
    hi = jnp.where(valid, jnp.clip(lookup(ends, onehot) - vt * rows_per, 0.0, rows_per), 0.0)
    fields = {PLAN_TILE: vt, PLAN_EXPERT: ve, PLAN_LO: lo, PLAN_HI: hi,
              PLAN_NEXT: lookup(nxt, onehot), PLAN_SLOT: lookup(slot, onehot)}
    row = lax.broadcasted_iota(jnp.int32, (8, PLAN_LANES), 0)
    plan = jnp.zeros((8, PLAN_LANES), F32)
    for r, val in fields.items():
        plan = jnp.where(row == r, val, plan)
    plan_ref[...] = plan.astype(jnp.int32)


def _plan(counts_col, n_tiles):
    return pl.pallas_call(
        functools.partial(_plan_kernel, n_tiles=n_tiles),
        out_shape=jax.ShapeDtypeStruct((8, PLAN_LANES), jnp.int32),
        name="moe_plan",
    )(counts_col)


def _expert_kernel(plan_ref, x_ref, wgu_hbm, bgu_ref, wd_hbm, bd_ref, y_ref,
                   wgu_f32, wd_f32, wgu_bf, wd_bf, sem, *, layer):
    v = pl.program_id(0)
    prev = jnp.maximum(v - 1, 0)
    lo = plan_ref[PLAN_LO, v]
    hi = plan_ref[PLAN_HI, v]
    expert = plan_ref[PLAN_EXPERT, v]
    next_expert = plan_ref[PLAN_NEXT, v]
    active = hi > lo
    fresh_weights = jnp.logical_or(v == 0, expert != plan_ref[PLAN_EXPERT, prev])
    first_visit = jnp.logical_or(v == 0, plan_ref[PLAN_TILE, v] != plan_ref[PLAN_TILE, prev])

    def weight_copies(expert, slot):
        return (pltpu.make_async_copy(wgu_hbm.at[layer, expert], wgu_f32.at[slot], sem.at[0, slot]),
                pltpu.make_async_copy(wd_hbm.at[layer, expert], wd_f32.at[slot], sem.at[1, slot]))

    @pl.when(jnp.logical_and(fresh_weights, active))
    def _():
        slot = plan_ref[PLAN_SLOT, v]

        @pl.when(v == 0)
        def _():
            for c in weight_copies(expert, slot):
                c.start()

        for c in weight_copies(expert, slot):
            c.wait()

        @pl.when(next_expert >= 0)
        def _():
            for c in weight_copies(next_expert, 1 - slot):
                c.start()

        wgu_bf[...] = wgu_f32[slot].astype(BF16)
        wd_bf[...] = wd_f32[slot].astype(BF16)

    whole = jnp.logical_and(lo == 0, hi == MOE_ROWS)

    def ffn(rows):
        f = EXPERT_FF
        gu = jnp.dot(x_ref[rows, :].astype(BF16), wgu_bf[...],
                     preferred_element_type=F32) + bgu_ref[0, 0]
        gate = jnp.minimum(gu[:, 0:f], SWIGLU_LIMIT)
        up = jnp.clip(gu[:, f:2 * f], -SWIGLU_LIMIT, SWIGLU_LIMIT)
        act = (up + 1.0) * gate * jax.nn.sigmoid(gate * SWIGLU_ALPHA)
        y = jnp.dot(act.astype(BF16), wd_bf[...], preferred_element_type=F32) + bd_ref[0, 0]
        return y.astype(y_ref.dtype)

    parts = [pl.ds(p * MOE_PART, MOE_PART) for p in range(MOE_ROWS // MOE_PART)]

    @pl.when(whole)
    def _():
        for rows in parts:
            y_ref[rows, :] = ffn(rows)

    @pl.when(jnp.logical_and(first_visit, jnp.logical_not(whole)))
    def _():
        y_ref[...] = jnp.zeros_like(y_ref)

    for p in range(MOE_ROWS // MOE_EDGE):
        rows = pl.ds(p * MOE_EDGE, MOE_EDGE)
        touched = jnp.logical_and(lo < (p + 1) * MOE_EDGE, hi > p * MOE_EDGE)

        @pl.when(jnp.logical_and(touched, jnp.logical_not(whole)))
        def _():
            row = p * MOE_EDGE + lax.broadcasted_iota(jnp.int32, (MOE_EDGE, 1), 0)
            mine = jnp.logical_and(row >= lo, row < hi)
            y_ref[rows, :] = jnp.where(mine, ffn(rows), y_ref[rows, :])


def _experts(xs, plan, w_gu, b_gu, w_down, b_down, layer):
    n_rows, d = xs.shape
    n_visits = n_rows // MOE_ROWS + N_EXPERTS - 1
    assert n_visits <= PLAN_LANES
    f = EXPERT_FF
    wmap = lambda v, plan: (layer, plan[PLAN_EXPERT, v], 0, 0)
    tmap = lambda v, plan: (plan[PLAN_TILE, v], 0)
    grid_spec = pltpu.PrefetchScalarGridSpec(
        num_scalar_prefetch=1,
        grid=(n_visits,),
        in_specs=[pl.BlockSpec((MOE_ROWS, d), tmap),
                  pl.BlockSpec(memory_space=pl.ANY),
                  pl.BlockSpec((1, 1, 1, 2 * f), wmap),
                  pl.BlockSpec(memory_space=pl.ANY),
                  pl.BlockSpec((1, 1, 1, d), wmap)],
        out_specs=pl.BlockSpec((MOE_ROWS, d), tmap),
        scratch_shapes=[pltpu.VMEM((2, d, 2 * f), F32), pltpu.VMEM((2, f, d), F32),
                        pltpu.VMEM((d, 2 * f), BF16), pltpu.VMEM((f, d), BF16),
                        pltpu.SemaphoreType.DMA((2, 2))],
    )
    return pl.pallas_call(
        functools.partial(_expert_kernel, layer=layer),
        grid_spec=grid_spec,
        out_shape=jax.ShapeDtypeStruct((n_rows, d), BF16),
        compiler_params=_cparams(("arbitrary",)),
        name="experts",
    )(plan, xs, w_gu, b_gu.reshape(DEPTH, N_EXPERTS, 1, 2 * f),
      w_down, b_down.reshape(DEPTH, N_EXPERTS, 1, d))


def _combine_kernel(x_ref, y0_ref, y1_ref, y2_ref, y3_ref, gt_ref, mod_ref, fg_ref, o_ref, *, final):
    mod = mod_ref[0, 0]
    gt = gt_ref[0]
    y = gt[:, 0:1] * y0_ref[0].astype(F32)
    for k, y_ref in ((1, y1_ref), (2, y2_ref), (3, y3_ref)):
        y = y + gt[:, k:k + 1] * y_ref[0].astype(F32)
    x2 = x_ref[0] + mod[5:6] * y
    o_ref[0] = _rms(x2) * fg_ref[...] if final else x2


def _combine(x1, yg, gates, modt, final_g, ctx_len, final, group):
    b, s_tot, d = x1.shape
    gb = MOE_GROUP_BATCH
    b0 = group * gb
    nt = s_tot // ROW_TILE
    nctx = ctx_len // ROW_TILE
    off = nctx if final else 0
    seg = lambda j: jnp.where(j + off >= nctx, 1, 0)
    choice = lambda k: pl.BlockSpec((1, ROW_TILE, d), lambda i, j: (k, i * nt + j + off, 0))
    if final:
        out_spec = pl.BlockSpec((1, ROW_TILE, d), lambda i, j: (i, j, 0))
        out_shape = jax.ShapeDtypeStruct((gb, s_tot - off * ROW_TILE, d), F32)
        aliases = {}
    else:
        out_spec = pl.BlockSpec((1, ROW_TILE, d), lambda i, j: (i + b0, j, 0))
        out_shape = jax.ShapeDtypeStruct((b, s_tot, d), F32)
        aliases = {0: 0}
    return pl.pallas_call(
        functools.partial(_combine_kernel, final=final),
        grid=(gb, nt - off),
        in_specs=[pl.BlockSpec((1, ROW_TILE, d), lambda i, j: (i + b0, j + off, 0)),
                  choice(0), choice(1), choice(2), choice(3),
                  pl.BlockSpec((1, ROW_TILE, LOGIT_LANES), lambda i, j: (i + b0, j + off, 0)),
                  pl.BlockSpec((1, 1, 8, d), lambda i, j: (i + b0, seg(j), 0, 0)),
                  pl.BlockSpec((1, d), lambda i, j: (0, 0))],
        out_specs=out_spec,
        out_shape=out_shape,
        input_output_aliases=aliases,
        compiler_params=_cparams(("arbitrary", "arbitrary")),
        name="moe_combine",
    )(x1, yg, yg, yg, yg, gates, modt, final_g.reshape(1, d))


def _moe(x1, h2, top, top_gate, counts, modt, final_g, exp_gu_w, exp_gu_b, exp_down_w, exp_down_b,
         layer, ctx_len, final):
    b, s_tot, d = x1.shape
    n_grp = MOE_GROUP_BATCH * s_tot
    n_assign = n_grp * TOP_K
    top = top.reshape(b * s_tot, LOGIT_LANES)
    h2 = h2.reshape(b * s_tot, d)
    x = x1
    finals = []
    for g in range(b // MOE_GROUP_BATCH):
        top_g = top[g * n_grp:(g + 1) * n_grp]
        cnt = counts[g, 0, :N_EXPERTS]
        plan = _plan(cnt.reshape(N_EXPERTS, 1), n_assign // MOE_ROWS)
        starts = (jnp.cumsum(cnt) - cnt).astype(jnp.int32)
        inv = jnp.take(starts, top_g[:, :TOP_K].reshape(-1)) + top_g[:, TOP_K:2 * TOP_K].reshape(-1)
        _, order = lax.sort((inv, jnp.arange(n_assign, dtype=jnp.int32)), num_keys=1)
        rows = order // TOP_K + g * n_grp
        xs = h2.at[rows].get(mode="promise_in_bounds")
        ys = _experts(xs, plan, exp_gu_w, exp_gu_b, exp_down_w, exp_down_b, layer)
        by_choice = inv.reshape(n_grp, TOP_K).T.reshape(-1)
        yg = ys.at[by_choice].get(mode="promise_in_bounds").reshape(TOP_K, n_grp, d)
        out = _combine(x, yg, top_gate, modt, final_g, ctx_len, final, g)
        if final:
            finals.append(out)
        else:
            x = out
    return jnp.concatenate(finals, axis=0) if final else x


def _mixer_layer(x, modt, i, ctx_len, tables, norm1_g, w_in_bf, s5p, ssm_d, conv_p, merge_p):
    b, s_tot, d = x.shape
    u2, cv, qkvg, gates = _in_proj(x, modt, norm1_g, w_in_bf, ctx_len)
    bmat, cmat, lvec = s5p
    y_s5 = _s5_scan(u2, bmat, cmat, lvec, ssm_d, ctx_len)
    hc = _conv_branch(cv, *conv_p, ctx_len)
    cos, sin, tabs = tables
    ro = _retention(qkvg, cos, sin, tabs, ctx_len)
    x1, h2, logits = _merge(x, y_s5, hc, ro, gates, modt, *merge_p, ctx_len)
    return (x1, h2) + tuple(_router(logits))


def kernel(x, c, ctx, c_ctx, ada_w, ada_b, norm1_g, w_in, ssm_lam_re, ssm_lam_im, ssm_log_dt, ssm_b_re, ssm_b_im, ssm_c_re, ssm_c_im, ssm_d, ssm_glu_w, ssm_glu_b, conv_dw_w, conv_dw_b, conv_ln_g, conv_ln_b, conv_pw_w, conv_pw_b, ret_w_o, w_out, norm2_g, router_w, router_b, exp_gu_w, exp_gu_b, exp_down_w, exp_down_b, final_g):
    b, seq, d = x.shape
    ctx_len = ctx.shape[1]
    depth = ada_w.shape[0]
    xs = jnp.concatenate([ctx, x], axis=1)
    s_in = jnp.concatenate([c, c_ctx[None], jnp.zeros((16 - b - 1, d), F32)], axis=0)
    mod = _ada_mod(s_in, ada_w, ada_b).reshape(depth, 16, N_MOD, d)
    mod = jnp.pad(mod, ((0, 0), (0, 0), (0, 8 - N_MOD), (0, 0)))
    modt = jnp.stack([jnp.broadcast_to(mod[:, b][:, None], (depth, b, 8, d)), mod[:, :b]], axis=2)
    tables = _ret_tables(ctx_len, seq)
    wr = jnp.pad(router_w, ((0, 0), (0, 0), (0, LOGIT_LANES - N_EXPERTS)))
    wr_hi = wr.astype(BF16)
    wr = jnp.stack([wr_hi, (wr - wr_hi.astype(F32)).astype(BF16)], axis=1)
    br = jnp.pad(router_b, ((0, 0), (0, LOGIT_LANES - N_EXPERTS))).reshape(depth, 1, LOGIT_LANES)
    for i in range(depth):
        s5p = _s5_params(ssm_lam_re[i], ssm_lam_im[i], ssm_log_dt[i], ssm_b_re[i], ssm_b_im[i],
                         ssm_c_re[i], ssm_c_im[i])
        conv_p = (conv_dw_w[i], conv_dw_b[i], conv_ln_g[i], conv_ln_b[i])
        merge_p = (norm2_g[i], ssm_glu_w[i].astype(BF16), ssm_glu_b[i], conv_pw_w[i].astype(BF16),
                   conv_pw_b[i], ret_w_o[i].astype(BF16), w_out[i].astype(BF16), wr[i], br[i])
        x1, h2, top, top_gate, counts = _mixer_layer(xs, modt[i], i, ctx_len, tables, norm1_g[i],
                                                     w_in[i].astype(BF16), s5p, ssm_d[i], conv_p, merge_p)
        xs = _moe(x1, h2, top, top_gate, counts, modt[i], final_g, exp_gu_w, exp_gu_b, exp_down_w,
                  exp_down_b, i, ctx_len, final=(i == depth - 1))
    return xs
```

```python
import functools
import math

import numpy as np
import jax
import jax.numpy as jnp
from jax import lax
from jax.experimental import pallas as pl
from jax.experimental.pallas import tpu as pltpu

F32 = jnp.float32
BF16 = jnp.bfloat16

D_MODEL = 1024
DEPTH = 4
GRID_W = 64
N_MOD = 6
SSM_WIDTH = 256
SSM_GROUP = 16
SSM_GROUPS = SSM_WIDTH // SSM_GROUP
SSM_STATE = 64
SSM_LANES = SSM_GROUPS * SSM_STATE
CONV_WIDTH = 256
CONV_K = 31
CONV_PAD = 16
RET_HEADS = 4
RET_HEAD_DIM = 128
RET_WIDTH = RET_HEADS * RET_HEAD_DIM
RET_CHUNK = 128
ROPE_BASE = 10000.0
IN_WIDTH = SSM_WIDTH + 2 * CONV_WIDTH + 4 * RET_WIDTH + 3 * D_MODEL
COL_CV = SSM_WIDTH
COL_QKVG = COL_CV + 2 * CONV_WIDTH
COL_GATES = COL_QKVG + 4 * RET_WIDTH
N_EXPERTS = 32
TOP_K = 4
EXPERT_FF = D_MODEL
SWIGLU_LIMIT = 7.0
SWIGLU_ALPHA = 1.702
EPS = 1e-6

ROW_TILE = 256
HALF_TILE = 128
S5_STEPS = 64
S5_STRIP = 512
MOE_ROWS = 512
MOE_PART = 256
MOE_EDGE = 256
MOE_GROUP_BATCH = 4
LOGIT_LANES = 128
VMEM_LIMIT = 56 * 1024 * 1024


def _cparams(sem):
    return pltpu.CompilerParams(dimension_semantics=sem, vmem_limit_bytes=VMEM_LIMIT)


def _rms(x):
    return x * lax.rsqrt(jnp.mean(x * x, axis=-1, keepdims=True) + EPS)


def _ada_kernel(s_ref, w_ref, b_ref, o_ref):
    s = s_ref[...]
    s = s * jax.nn.sigmoid(s)
    o_ref[0] = jnp.dot(s, w_ref[0], preferred_element_type=F32,
                       precision=lax.Precision.HIGHEST) + b_ref[0]


def _ada_mod(s_in, ada_w, ada_b):
    depth, d, n = ada_w.shape
    tn = 1536
    return pl.pallas_call(
        _ada_kernel,
        grid=(depth, n // tn),
        in_specs=[pl.BlockSpec((16, d), lambda i, j: (0, 0)),
                  pl.BlockSpec((1, d, tn), lambda i, j: (i, 0, j)),
                  pl.BlockSpec((1, 1, tn), lambda i, j: (i, 0, j))],
        out_specs=pl.BlockSpec((1, 16, tn), lambda i, j: (i, 0, j)),
        out_shape=jax.ShapeDtypeStruct((depth, 16, n), F32),
        compiler_params=_cparams(("arbitrary", "arbitrary")),
        name="ada_mod",
    )(s_in, ada_w, ada_b.reshape(depth, 1, n))


def _inproj_kernel(x_ref, mod_ref, g_ref, w_ref, u_ref, cv_ref, qkvg_ref, gates_ref):
    x = x_ref[0]
    mod = mod_ref[0, 0]
    h = (_rms(x) * g_ref[...]) * (1.0 + mod[1:2]) + mod[0:1]
    hb = h.astype(BF16)
    u_ref[0] = jnp.dot(hb, w_ref[:, 0:COL_CV], preferred_element_type=F32)
    cv_ref[0] = jnp.dot(hb, w_ref[:, COL_CV:COL_QKVG], preferred_element_type=F32)
    qkvg_ref[0] = jnp.dot(hb, w_ref[:, COL_QKVG:COL_GATES], preferred_element_type=F32).astype(BF16)
    gates = jnp.dot(hb, w_ref[:, COL_GATES:IN_WIDTH], preferred_element_type=F32)
    gates_ref[0] = jax.nn.sigmoid(gates).astype(BF16)


def _in_proj(x, modt, g1, w_in_bf, ctx_len):
    b, s_tot, d = x.shape
    nt = s_tot // ROW_TILE
    nctx = ctx_len // ROW_TILE
    seg = lambda j: jnp.where(j >= nctx, 1, 0)
    return pl.pallas_call(
        _inproj_kernel,
        grid=(b, nt),
        in_specs=[pl.BlockSpec((1, ROW_TILE, d), lambda i, j: (i, j, 0)),
                  pl.BlockSpec((1, 1, 8, d), lambda i, j: (i, seg(j), 0, 0)),
                  pl.BlockSpec((1, d), lambda i, j: (0, 0)),
                  pl.BlockSpec((d, IN_WIDTH), lambda i, j: (0, 0))],
        out_specs=[pl.BlockSpec((1, ROW_TILE, SSM_WIDTH), lambda i, j: (i, j, 0)),
                   pl.BlockSpec((1, ROW_TILE, 2 * CONV_WIDTH), lambda i, j: (i, j, 0)),
                   pl.BlockSpec((1, ROW_TILE, 4 * RET_WIDTH), lambda i, j: (i, j, 0)),
                   pl.BlockSpec((1, ROW_TILE, 3 * D_MODEL), lambda i, j: (i, j, 0))],
        out_shape=[jax.ShapeDtypeStruct((b, s_tot, SSM_WIDTH), F32),
                   jax.ShapeDtypeStruct((b, s_tot, 2 * CONV_WIDTH), F32),
                   jax.ShapeDtypeStruct((b, s_tot, 4 * RET_WIDTH), BF16),
                   jax.ShapeDtypeStruct((b, s_tot, 3 * D_MODEL), BF16)],
        compiler_params=_cparams(("arbitrary", "arbitrary")),
        name="in_proj",
    )(x, modt, g1.reshape(1, d), w_in_bf)


def _s5_params(lam_re, lam_im, log_dt, b_re, b_im, c_re, c_im):
    lam = lax.complex(jnp.minimum(lam_re.astype(F32), -1e-4), lam_im.astype(F32))
    lam_dt = lam * jnp.exp(log_dt.astype(F32))[..., None]
    lam_bar = jnp.exp(lam_dt)
    b = lax.complex(b_re.astype(F32), b_im.astype(F32))
    b_bar = ((lam_bar - 1.0) / lam)[..., None] * b
    eye = jnp.eye(SSM_GROUPS, dtype=F32)

    def in_mat(t):
        return jnp.einsum('dgpm,gh->dgmhp', t, eye).reshape(2, SSM_WIDTH, SSM_LANES)

    def out_mat(t):
        return jnp.einsum('dgmp,gh->dgphm', t, eye).reshape(2, SSM_LANES, SSM_WIDTH)

    bmat = jnp.concatenate([in_mat(jnp.real(b_bar)), in_mat(jnp.imag(b_bar))], axis=-1)
    cmat = jnp.concatenate([out_mat(c_re.astype(F32)), -out_mat(c_im.astype(F32))], axis=1)
    lvec = jnp.concatenate([jnp.real(lam_bar).reshape(2, 1, SSM_LANES),
                            jnp.imag(lam_bar).reshape(2, 1, SSM_LANES)], axis=-1)
    return bmat.astype(BF16), cmat.astype(BF16), lvec


def _s5_kernel(u_ref, perm_ref, permt_ref, bmat_ref, cmat_ref, lam_ref, dsk_ref, y_ref, st_ref, carry_ref):
    d = pl.program_id(0)
    c = pl.program_id(1)
    nb, steps, width = u_ref.shape

    @pl.when(c == 0)
    def _():
        carry_ref[...] = jnp.zeros_like(carry_ref)

    u = u_ref[...].reshape(nb * steps, width)
    u_tm = jnp.dot(perm_ref[...], u.astype(BF16), preferred_element_type=F32).astype(BF16)
    bu = jnp.dot(u_tm, bmat_ref[0], preferred_element_type=F32)
    st_ref[...] = bu.reshape(steps, nb, 2 * SSM_LANES)

    for k in range(SSM_LANES // S5_STRIP):
        re = pl.ds(k * S5_STRIP, S5_STRIP)
        im = pl.ds(SSM_LANES + k * S5_STRIP, S5_STRIP)
        lr = jnp.broadcast_to(lam_ref[0, :, re], (nb, S5_STRIP))
        li = jnp.broadcast_to(lam_ref[0, :, im], (nb, S5_STRIP))

        def step(t, s):
            sr, si = s
            tt = jnp.where(d == 1, steps - 1 - t, t)
            nr = lr * sr - li * si + st_ref[tt, :, re]
            ni = lr * si + li * sr + st_ref[tt, :, im]
            st_ref[tt, :, re] = nr
            st_ref[tt, :, im] = ni
            return nr, ni

        sr, si = lax.fori_loop(0, steps, step, (carry_ref[:, re], carry_ref[:, im]), unroll=4)
        carry_ref[:, re] = sr
        carry_ref[:, im] = si

    st = st_ref[...].reshape(steps * nb, 2 * SSM_LANES)
    y_tm = jnp.dot(st.astype(BF16), cmat_ref[0], preferred_element_type=F32)
    y = jnp.dot(permt_ref[...], y_tm.astype(BF16), preferred_element_type=F32)
    skip = jnp.where(d == 0, 1.0, 0.0) * dsk_ref[...]
    y_ref[0] = (y + u * skip).reshape(nb, steps, width)


def _s5_scan(u, bmat, cmat, lvec, d_skip, ctx_len):
    nb, s_tot, width = u.shape
    nch = s_tot // S5_STEPS
    nctx = ctx_len // S5_STEPS
    rows = nb * S5_STEPS
    r = np.arange(rows)
    perm = np.zeros((rows, rows), np.float32)
    perm[r, (r % nb) * S5_STEPS + r // nb] = 1.0
    perm_tm = jnp.asarray(perm, BF16)
    perm_bm = jnp.asarray(perm.T, BF16)

    def chunk(d, c):
        back = jnp.where(c < nctx, nctx - 1 - c, nctx + nch - 1 - c)
        return jnp.where(d == 1, back, c)

    return pl.pallas_call(
        _s5_kernel,
        grid=(2, nch),
        in_specs=[pl.BlockSpec((nb, S5_STEPS, width), lambda d, c: (0, chunk(d, c), 0)),
                  pl.BlockSpec((rows, rows), lambda d, c: (0, 0)),
                  pl.BlockSpec((rows, rows), lambda d, c: (0, 0)),
                  pl.BlockSpec((1, width, 2 * SSM_LANES), lambda d, c: (d, 0, 0)),
                  pl.BlockSpec((1, 2 * SSM_LANES, width), lambda d, c: (d, 0, 0)),
                  pl.BlockSpec((1, 1, 2 * SSM_LANES), lambda d, c: (d, 0, 0)),
                  pl.BlockSpec((1, width), lambda d, c: (0, 0))],
        out_specs=pl.BlockSpec((1, nb, S5_STEPS, width), lambda d, c: (d, 0, chunk(d, c), 0)),
        out_shape=jax.ShapeDtypeStruct((2, nb, s_tot, width), F32),
        scratch_shapes=[pltpu.VMEM((S5_STEPS, nb, 2 * SSM_LANES), F32),
                        pltpu.VMEM((nb, 2 * SSM_LANES), F32)],
        compiler_params=_cparams(("arbitrary", "arbitrary")),
        name="s5_scan",
    )(u, perm_tm, perm_bm, bmat, cmat, lvec, d_skip.reshape(1, width))


def _conv_kernel(cv_ref, w_ref, b_ref, lg_ref, lb_ref, o_ref, hp_ref, win_ref, shift_ref, *, ctx_len):
    s_tot = cv_ref.shape[1]
    lat = s_tot - ctx_len
    rc = RET_CHUNK
    zeros = jnp.zeros((CONV_PAD, CONV_WIDTH), F32)
    a = cv_ref[0, :, 0:CONV_WIDTH]
    g = cv_ref[0, :, CONV_WIDTH:2 * CONV_WIDTH]
    h = a * jax.nn.sigmoid(g)
    hp_ref[0:CONV_PAD] = zeros
    hp_ref[CONV_PAD:CONV_PAD + ctx_len] = h[0:ctx_len]
    hp_ref[CONV_PAD + ctx_len:2 * CONV_PAD + ctx_len] = zeros
    hp_ref[2 * CONV_PAD + ctx_len:2 * CONV_PAD + s_tot] = h[ctx_len:s_tot]
    hp_ref[2 * CONV_PAD + s_tot:3 * CONV_PAD + s_tot] = zeros
    del lat

    def chunk(c, carry):
        r0 = pl.multiple_of(c * rc, rc)
        wbase = pl.multiple_of(r0 + jnp.where(r0 >= ctx_len, CONV_PAD, 0), 8)
        win_ref[...] = hp_ref[pl.ds(wbase, rc + 2 * CONV_PAD), :]
        for r in range(1, 8):
            shift_ref[r] = win_ref[r:r + rc + 2 * CONV_PAD - 8, :]
        acc = jnp.zeros((rc, CONV_WIDTH), F32) + b_ref[...]
        for k in range(CONV_K):
            off = k + CONV_PAD - CONV_K // 2
            base = off - off % 8
            if off % 8 == 0:
                tap = win_ref[base:base + rc, :]
            else:
                tap = shift_ref[off % 8, base:base + rc, :]
            acc = acc + w_ref[k:k + 1, :] * tap
        mu = jnp.mean(acc, axis=-1, keepdims=True)
        var = jnp.mean(jnp.square(acc - mu), axis=-1, keepdims=True)
        y = (acc - mu) * lax.rsqrt(var + EPS) * lg_ref[...] + lb_ref[...]
        o_ref[0, pl.ds(r0, rc), :] = y * jax.nn.sigmoid(y)
        return carry

    lax.fori_loop(0, s_tot // rc, chunk, 0)


def _conv_branch(cv, w_dw, b_dw, ln_g, ln_b, ctx_len):
    b, s_tot, _ = cv.shape
    w = jnp.concatenate([w_dw.reshape(CONV_K, CONV_WIDTH), jnp.zeros((1, CONV_WIDTH), F32)], axis=0)
    vec = lambda t: t.reshape(1, CONV_WIDTH)
    cst = lambda shape: pl.BlockSpec(shape, lambda i: (0,) * len(shape))
    return pl.pallas_call(
        functools.partial(_conv_kernel, ctx_len=ctx_len),
        grid=(b,),
        in_specs=[pl.BlockSpec((1, s_tot, 2 * CONV_WIDTH), lambda i: (i, 0, 0)),
                  cst((CONV_K + 1, CONV_WIDTH)), cst((1, CONV_WIDTH)),
                  cst((1, CONV_WIDTH)), cst((1, CONV_WIDTH))],
        out_specs=pl.BlockSpec((1, s_tot, CONV_WIDTH), lambda i: (i, 0, 0)),
        out_shape=jax.ShapeDtypeStruct((b, s_tot, CONV_WIDTH), F32),
        scratch_shapes=[pltpu.VMEM((s_tot + 3 * CONV_PAD, CONV_WIDTH), F32),
                        pltpu.VMEM((RET_CHUNK + 2 * CONV_PAD, CONV_WIDTH), F32),
                        pltpu.VMEM((8, RET_CHUNK + 2 * CONV_PAD - 8, CONV_WIDTH), F32)],
        compiler_params=_cparams(("arbitrary",)),
        name="conv_branch",
    )(cv, w, vec(b_dw), vec(ln_g), vec(ln_b))


N_TAB = 5


def _ret_tables(ctx_len, seq):
    n = RET_HEAD_DIM // 4
    inv_freq = ROPE_BASE ** (-np.arange(n, dtype=np.float64) / n)
    pos = np.arange(seq)
    ang_r = (pos // GRID_W)[:, None] * inv_freq
    ang_c = (pos % GRID_W)[:, None] * inv_freq
    cos = np.concatenate([np.cos(ang_r), np.cos(ang_r), np.cos(ang_c), np.cos(ang_c)], axis=-1)
    sin = np.concatenate([-np.sin(ang_r), np.sin(ang_r), -np.sin(ang_c), np.sin(ang_c)], axis=-1)
    cos = np.concatenate([np.ones((ctx_len, RET_HEAD_DIM)), cos], axis=0)
    sin = np.concatenate([np.zeros((ctx_len, RET_HEAD_DIM)), sin], axis=0)
    log_g_fwd = np.log1p(-np.exp2(-5.0 - np.arange(RET_HEADS, dtype=np.float64)))
    idx = np.arange(RET_CHUNK, dtype=np.float64)
    diff = idx[:, None] - idx[None, :]
    tabs = np.zeros((2, RET_HEADS, N_TAB, RET_CHUNK, RET_CHUNK))
    ones = np.ones((RET_CHUNK, RET_CHUNK))
    for d, log_g in enumerate((log_g_fwd, log_g_fwd[::-1])):
        for hd in range(RET_HEADS):
            lg = log_g[hd]
            if d == 0:
                mask = np.where(diff >= 0, np.exp(lg * np.maximum(diff, 0.0)), 0.0)
                dec = np.exp(lg * (idx + 1.0))
                zeta = np.exp(lg * (RET_CHUNK - 1 - idx))
            else:
                mask = np.where(diff <= 0, np.exp(lg * np.maximum(-diff, 0.0)), 0.0)
                dec = np.exp(lg * (RET_CHUNK - idx))
                zeta = np.exp(lg * idx)
            tabs[d, hd, 0] = mask
            tabs[d, hd, 1] = dec[:, None] * ones
            tabs[d, hd, 2] = zeta[:, None] * ones
            tabs[d, hd, 3] = np.exp(lg * RET_CHUNK) * ones
            tabs[d, hd, 4] = zeta[None, :] * ones
    return (jnp.asarray(cos, F32), jnp.asarray(sin, F32), jnp.asarray(tabs, F32))


def _ret_kernel(q_ref, k_ref, v_ref, g_ref, cos_ref, sin_ref, tab_ref, o_ref,
                qs_ref, qd_ref, kt_ref, kz_ref, out_ref, state_ref, *, ctx_len):
    s_tot = q_ref.shape[1]
    rc = RET_CHUNK
    nch = s_tot // rc
    nctx = ctx_len // rc
    k_scale = RET_HEAD_DIM ** -0.5
    lane = lax.broadcasted_iota(jnp.int32, (rc, RET_HEAD_DIM), 1)
    first_half = (lane % (RET_HEAD_DIM // 2)) < (RET_HEAD_DIM // 4)

    def rope(t, cs, sn):
        quarter = RET_HEAD_DIM // 4
        partner = jnp.where(first_half, pltpu.roll(t, RET_HEAD_DIM - quarter, 1), pltpu.roll(t, quarter, 1))
        return t * cs + partner * sn

    def prepare(c, carry):
        rows = pl.ds(pl.multiple_of(c * rc, rc), rc)
        cs = cos_ref[rows, :]
        sn = sin_ref[rows, :]
        q = rope(q_ref[0, rows, :].astype(F32), cs, sn)
        kt = (rope(k_ref[0, rows, :].astype(F32), cs, sn) * k_scale).T
        qs_ref[c] = q.astype(BF16)
        kt_ref[c] = kt.astype(BF16)
        for d in range(2):
            qd_ref[d, c] = (q * tab_ref[d, 0, 1]).astype(BF16)
            kz_ref[d, c] = (kt * tab_ref[d, 0, 4]).astype(BF16)
        return carry

    lax.fori_loop(0, nch, prepare, 0, unroll=2)
    state_ref[...] = jnp.zeros_like(state_ref)

    def chunk(c, d):
        rows = pl.ds(pl.multiple_of(c * rc, rc), rc)
        vb = v_ref[0, rows, :]
        state = state_ref[d]
        scores = jnp.dot(qs_ref[c], kt_ref[c], preferred_element_type=F32) * tab_ref[d, 0, 0]
        o = (jnp.dot(scores.astype(BF16), vb, preferred_element_type=F32)
             + jnp.dot(qd_ref[d, c], state.astype(BF16), preferred_element_type=F32))
        kv = jnp.dot(kz_ref[d, c], vb, preferred_element_type=F32)
        state_ref[d] = tab_ref[d, 0, 3] * state + kv
        out_ref[d, rows, :] = o

    def step(j, carry):
        chunk(j, 0)
        chunk(jnp.where(j < nctx, nctx - 1 - j, nctx + nch - 1 - j), 1)
        return carry

    lax.fori_loop(0, nch, step, 0, unroll=3)
    g = g_ref[0].astype(F32)
    o_ref[0] = (_rms(out_ref[0] + out_ref[1]) * (g * jax.nn.sigmoid(g))).astype(o_ref.dtype)


def _retention(qkvg, cos, sin, tabs, ctx_len):
    b, s_tot, _ = qkvg.shape
    hd = RET_HEAD_DIM
    nch = s_tot // RET_CHUNK
    part = lambda p: pl.BlockSpec((1, s_tot, hd), lambda i, h: (i, 0, p * RET_HEADS + h))
    return pl.pallas_call(
        functools.partial(_ret_kernel, ctx_len=ctx_len),
        grid=(b, RET_HEADS),
        in_specs=[part(0), part(1), part(2), part(3),
                  pl.BlockSpec((s_tot, hd), lambda i, h: (0, 0)),
                  pl.BlockSpec((s_tot, hd), lambda i, h: (0, 0)),
                  pl.BlockSpec((2, 1, N_TAB, RET_CHUNK, RET_CHUNK), lambda i, h: (0, h, 0, 0, 0))],
        out_specs=pl.BlockSpec((1, s_tot, hd), lambda i, h: (i, 0, h)),
        out_shape=jax.ShapeDtypeStruct((b, s_tot, RET_WIDTH), BF16),
        scratch_shapes=[pltpu.VMEM((nch, RET_CHUNK, hd), BF16),
                        pltpu.VMEM((2, nch, RET_CHUNK, hd), BF16),
                        pltpu.VMEM((nch, hd, RET_CHUNK), BF16),
                        pltpu.VMEM((2, nch, hd, RET_CHUNK), BF16),
                        pltpu.VMEM((2, s_tot, hd), F32),
                        pltpu.VMEM((2, hd, hd), F32)],
        compiler_params=_cparams(("arbitrary", "arbitrary")),
        name="retention",
    )(qkvg, qkvg, qkvg, qkvg, cos, sin, tabs)


def _merge_kernel(x_ref, yf_ref, yb_ref, hc_ref, ro_ref, gates_ref, mod_ref, g2_ref,
                  wglu_ref, bglu_ref, wpw_ref, bpw_ref, wo_ref, wout_ref, wr_ref, br_ref,
                  x1_ref, h2_ref, lg_ref):
    d = D_MODEL
    mod = mod_ref[0, 0]

    for half in range(ROW_TILE // HALF_TILE):
        rows = pl.ds(half * HALF_TILE, HALF_TILE)
        ys = jax.nn.gelu(yf_ref[0, 0, rows, :] + yb_ref[0, 0, rows, :]).astype(BF16)
        z = jnp.dot(ys, wglu_ref[...], preferred_element_type=F32) + bglu_ref[...]
        y_ssm = z[:, 0:d] * jax.nn.sigmoid(z[:, d:2 * d])
        y_conv = jnp.dot(hc_ref[0, rows, :].astype(BF16), wpw_ref[...],
                         preferred_element_type=F32) + bpw_ref[...]
        y_ret = jnp.dot(ro_ref[0, rows, :].astype(BF16), wo_ref[...], preferred_element_type=F32)
        m = (gates_ref[0, rows, 0:d].astype(F32) * y_ssm
             + gates_ref[0, rows, d:2 * d].astype(F32) * y_conv
             + gates_ref[0, rows, 2 * d:3 * d].astype(F32) * y_ret)
        y = jnp.dot(m.astype(BF16), wout_ref[...], preferred_element_type=F32)
        x1 = x_ref[0, rows, :] + mod[2:3] * y
        x1_ref[0, rows, :] = x1
        h2 = (_rms(x1) * g2_ref[...]) * (1.0 + mod[4:5]) + mod[3:4]
        h2_ref[0, rows, :] = h2
        h_hi = h2.astype(BF16)
        h_lo = (h2 - h_hi.astype(F32)).astype(BF16)
        lg_ref[0, rows, :] = (jnp.dot(h_hi, wr_ref[0], preferred_element_type=F32)
                              + jnp.dot(h_lo, wr_ref[0], preferred_element_type=F32)
                              + jnp.dot(h_hi, wr_ref[1], preferred_element_type=F32)) + br_ref[...]


def _router_kernel(lg_ref, tril_ref, ti_ref, tg_ref, cnt_out_ref, cnt_ref, *, group_batch):
    @pl.when(pl.program_id(0) % group_batch == 0)
    def _():
        cnt_ref[...] = jnp.zeros_like(cnt_ref)

    seen = cnt_ref[0:1, :]
    for part in range(lg_ref.shape[1] // ROW_TILE):
        rows = pl.ds(part * ROW_TILE, ROW_TILE)
        logits = lg_ref[0, rows, :]
        lane = lax.broadcasted_iota(jnp.int32, logits.shape, 1)
        lane_f = lane.astype(F32)
        neg = jnp.float32(-jnp.inf)
        l = jnp.where(lane < N_EXPERTS, logits, neg)
        ti = jnp.zeros(logits.shape, F32)
        vals = []
        picks = []
        for k in range(TOP_K):
            top = jnp.max(l, axis=-1, keepdims=True)
            idx = jnp.min(jnp.where(l == top, lane_f, float(LOGIT_LANES)), axis=-1, keepdims=True)
            ti = jnp.where(lane == k, idx, ti)
            vals.append(top)
            picked = lane_f == idx
            picks.append(picked.astype(F32))
            l = jnp.where(picked, neg, l)
        es = [jnp.exp(v - vals[0]) for v in vals]
        tot = es[0]
        for e in es[1:]:
            tot = tot + e
        tg = jnp.zeros(logits.shape, F32)
        for k in range(TOP_K):
            tg = jnp.where(lane == k, es[k] / tot, tg)
        chosen = picks[0] + picks[1] + picks[2] + picks[3]
        before = seen + jnp.dot(tril_ref[...], chosen.astype(BF16), preferred_element_type=F32)
        for k in range(TOP_K):
            rank = jnp.sum(picks[k] * before, axis=-1, keepdims=True)
            ti = jnp.where(lane == TOP_K + k, rank, ti)
        seen = seen + jnp.sum(chosen, axis=0, keepdims=True)
        ti_ref[0, rows, :] = ti.astype(jnp.int32)
        tg_ref[0, rows, :] = tg
    cnt_ref[...] = jnp.broadcast_to(seen, cnt_ref.shape)
    cnt_out_ref[0] = jnp.broadcast_to(seen, cnt_out_ref.shape[1:])


def _router(logits, group_batch):
    b, s_tot, _ = logits.shape
    tril = jnp.asarray(np.tril(np.ones((ROW_TILE, ROW_TILE), np.float32), -1), BF16)
    row = pl.BlockSpec((1, s_tot, LOGIT_LANES), lambda i: (i, 0, 0))
    return pl.pallas_call(
        functools.partial(_router_kernel, group_batch=group_batch),
        grid=(b,),
        in_specs=[row, pl.BlockSpec((ROW_TILE, ROW_TILE), lambda i: (0, 0))],
        out_specs=[row, row,
                   pl.BlockSpec((1, 8, LOGIT_LANES), lambda i: (i // group_batch, 0, 0))],
        out_shape=[jax.ShapeDtypeStruct((b, s_tot, LOGIT_LANES), jnp.int32),
                   jax.ShapeDtypeStruct((b, s_tot, LOGIT_LANES), F32),
                   jax.ShapeDtypeStruct((b // group_batch, 8, LOGIT_LANES), F32)],
        scratch_shapes=[pltpu.VMEM((8, LOGIT_LANES), F32)],
        compiler_params=_cparams(("arbitrary",)),
        name="router",
    )(logits, tril)


def _positions_kernel(top_ref, starts_ref, pos_ref):
    top = top_ref[...]
    lane = lax.broadcasted_iota(jnp.int32, top.shape, 1)
    starts = starts_ref[0:1, :]
    pos = jnp.zeros(top.shape, F32)
    for k in range(TOP_K):
        mine = lane == top[:, k:k + 1]
        first = jnp.sum(jnp.where(mine, starts, 0.0), axis=-1, keepdims=True)
        pos = jnp.where(lane == k, first + top[:, TOP_K + k:TOP_K + k + 1].astype(F32), pos)
    pos_ref[...] = pos.astype(jnp.int32)


def _positions(top, starts_row):
    n = top.shape[0]
    tile = 1024
    return pl.pallas_call(
        _positions_kernel,
        grid=(n // tile,),
        in_specs=[pl.BlockSpec((tile, LOGIT_LANES), lambda i: (i, 0)),
                  pl.BlockSpec((8, LOGIT_LANES), lambda i: (0, 0))],
        out_specs=pl.BlockSpec((tile, LOGIT_LANES), lambda i: (i, 0)),
        out_shape=jax.ShapeDtypeStruct((n, LOGIT_LANES), jnp.int32),
        compiler_params=_cparams(("arbitrary",)),
        name="moe_positions",
    )(top, starts_row)


def _merge(x, y_s5, hc, ro, gates, modt, g2, wglu, bglu, wpw, bpw, wo, wout, wr, br, ctx_len):
    b, s_tot, d = x.shape
    nt = s_tot // ROW_TILE
    nctx = ctx_len // ROW_TILE
    seg = lambda j: jnp.where(j >= nctx, 1, 0)
    row = lambda w: pl.BlockSpec((1, ROW_TILE, w), lambda i, j: (i, j, 0))
    cst = lambda shape: pl.BlockSpec(shape, lambda i, j: (0,) * len(shape))
    y2 = y_s5
    return pl.pallas_call(
        _merge_kernel,
        grid=(b, nt),
        in_specs=[row(d),
                  pl.BlockSpec((1, 1, ROW_TILE, SSM_WIDTH), lambda i, j: (0, i, j, 0)),
                  pl.BlockSpec((1, 1, ROW_TILE, SSM_WIDTH), lambda i, j: (1, i, j, 0)),
                  row(CONV_WIDTH), row(RET_WIDTH), row(3 * d),
                  pl.BlockSpec((1, 1, 8, d), lambda i, j: (i, seg(j), 0, 0)),
                  cst((1, d)),
                  cst((SSM_WIDTH, 2 * d)), cst((1, 2 * d)),
                  cst((CONV_WIDTH, d)), cst((1, d)),
                  cst((RET_WIDTH, d)), cst((d, d)),
                  cst((2, d, LOGIT_LANES)), cst((1, LOGIT_LANES))],
        out_specs=[row(d), row(d), row(LOGIT_LANES)],
        out_shape=[jax.ShapeDtypeStruct((b, s_tot, d), F32),
                   jax.ShapeDtypeStruct((b, s_tot, d), F32),
                   jax.ShapeDtypeStruct((b, s_tot, LOGIT_LANES), F32)],
        compiler_params=_cparams(("arbitrary", "arbitrary")),
        name="merge_logits",
    )(x, y2, y2, hc, ro, gates, modt, g2.reshape(1, d), wglu, bglu.reshape(1, 2 * d),
      wpw, bpw.reshape(1, d), wo, wout, wr, br)


PLAN_TILE, PLAN_EXPERT, PLAN_LO, PLAN_HI, PLAN_NEXT, PLAN_SLOT = range(6)
PLAN_LANES = 256


def _plan_kernel(cnt_ref, plan_ref, *, n_tiles):
    ne = N_EXPERTS
    rows_per = float(MOE_ROWS)
    e = lax.broadcasted_iota(jnp.int32, (ne, PLAN_LANES), 0)
    e_f = e.astype(F32)
    v = lax.broadcasted_iota(jnp.int32, (ne, PLAN_LANES), 1).astype(F32)

    def cumsum(t):
        for s in (1, 2, 4, 8, 16):
            t = t + jnp.where(e >= s, pltpu.roll(t, s, 0), 0.0)
        return t

    def lookup(table, onehot):
        return jnp.sum(onehot * table, axis=0, keepdims=True)

    c = jnp.broadcast_to(cnt_ref[...], (ne, PLAN_LANES))
    has = c > 0.0
    ends = cumsum(c)
    starts = ends - c
    first_tile = jnp.floor(starts / rows_per)
    tiles_per = jnp.where(has, jnp.floor((ends - 1.0) / rows_per) - first_tile + 1.0, 0.0)
    vend = cumsum(tiles_per)
    vstart = vend - tiles_per
    total = jnp.max(vend, axis=0, keepdims=True)
    slot = cumsum(has.astype(F32)) - 1.0
    slot = slot - 2.0 * jnp.floor(slot / 2.0)
    later = jnp.where(has, e_f, float(ne))
    nxt = jnp.where(e < ne - 1, pltpu.roll(later, ne - 1, 0), float(ne))
    for s in (1, 2, 4, 8, 16):
        nxt = jnp.minimum(nxt, jnp.where(e < ne - s, pltpu.roll(nxt, ne - s, 0), float(ne)))
    nxt = jnp.where(nxt >= float(ne), -1.0, nxt)
    last_with_rows = jnp.max(jnp.where(has, e_f, 0.0), axis=0, keepdims=True)

    valid = v[0:1] < total
    ve = jnp.sum((v >= vend).astype(F32), axis=0, keepdims=True)
    ve = jnp.where(valid, jnp.minimum(ve, float(ne - 1)), last_with_rows)
    onehot = (e_f == ve).astype(F32)
    vt = jnp.where(valid, lookup(first_tile, onehot) + v[0:1] - lookup(vstart, onehot),
                   float(n_tiles - 1))
    lo = jnp.where(valid, jnp.clip(lookup(starts, onehot) - vt * rows_per, 0.0, rows_per), 0.0)
    hi = jnp.where(valid, jnp.clip(lookup(ends, onehot) - vt * rows_per, 0.0, rows_per), 0.0)
    fields = {PLAN_TILE: vt, PLAN_EXPERT: ve, PLAN_LO: lo, PLAN_HI: hi,
              PLAN_NEXT: lookup(nxt, onehot), PLAN_SLOT: lookup(slot, onehot)}
    row = lax.broadcasted_iota(jnp.int32, (8, PLAN_LANES), 0)
    plan = jnp.zeros((8, PLAN_LANES), F32)
    for r, val in fields.items():
        plan = jnp.where(row == r, val, plan)
    plan_ref[...] = plan.astype(jnp.int32)


def _plan(counts_col, n_tiles):
    return pl.pallas_call(
        functools.partial(_plan_kernel, n_tiles=n_tiles),
        out_shape=jax.ShapeDtypeStruct((8, PLAN_LANES), jnp.int32),
        name="moe_plan",
    )(counts_col)


def _expert_kernel(plan_ref, x_ref, wgu_hbm, bgu_ref, wd_hbm, bd_ref, y_ref,
                   wgu_f32, wd_f32, wgu_bf, wd_bf, sem, *, layer):
    v = pl.program_id(0)
    prev = jnp.maximum(v - 1, 0)
    lo = plan_ref[PLAN_LO, v]
    hi = plan_ref[PLAN_HI, v]
    expert = plan_ref[PLAN_EXPERT, v]
    next_expert = plan_ref[PLAN_NEXT, v]
    active = hi > lo
    fresh_weights = jnp.logical_or(v == 0, expert != plan_ref[PLAN_EXPERT, prev])
    first_visit = jnp.logical_or(v == 0, plan_ref[PLAN_TILE, v] != plan_ref[PLAN_TILE, prev])

    def weight_copies(expert, slot):
        return (pltpu.make_async_copy(wgu_hbm.at[layer, expert], wgu_f32.at[slot], sem.at[0, slot]),
                pltpu.make_async_copy(wd_hbm.at[layer, expert], wd_f32.at[slot], sem.at[1, slot]))

    @pl.when(jnp.logical_and(fresh_weights, active))
    def _():
        slot = plan_ref[PLAN_SLOT, v]

        @pl.when(v == 0)
        def _():
            for c in weight_copies(expert, slot):
                c.start()

        for c in weight_copies(expert, slot):
            c.wait()

        @pl.when(next_expert >= 0)
        def _():
            for c in weight_copies(next_expert, 1 - slot):
                c.start()

        wgu_bf[...] = wgu_f32[slot].astype(BF16)
        wd_bf[...] = wd_f32[slot].astype(BF16)

    whole = jnp.logical_and(lo == 0, hi == MOE_ROWS)

    def ffn(rows):
        f = EXPERT_FF
        gu = jnp.dot(x_ref[rows, :].astype(BF16), wgu_bf[...],
                     preferred_element_type=F32) + bgu_ref[0, 0]
        gate = jnp.minimum(gu[:, 0:f], SWIGLU_LIMIT)
        up = jnp.clip(gu[:, f:2 * f], -SWIGLU_LIMIT, SWIGLU_LIMIT)
        act = (up + 1.0) * gate * jax.nn.sigmoid(gate * SWIGLU_ALPHA)
        y = jnp.dot(act.astype(BF16), wd_bf[...], preferred_element_type=F32) + bd_ref[0, 0]
        return y.astype(y_ref.dtype)

    parts = [pl.ds(p * MOE_PART, MOE_PART) for p in range(MOE_ROWS // MOE_PART)]

    @pl.when(whole)
    def _():
        for rows in parts:
            y_ref[rows, :] = ffn(rows)

    @pl.when(jnp.logical_and(first_visit, jnp.logical_not(whole)))
    def _():
        y_ref[...] = jnp.zeros_like(y_ref)

    for p in range(MOE_ROWS // MOE_EDGE):
        rows = pl.ds(p * MOE_EDGE, MOE_EDGE)
        touched = jnp.logical_and(lo < (p + 1) * MOE_EDGE, hi > p * MOE_EDGE)

        @pl.when(jnp.logical_and(touched, jnp.logical_not(whole)))
        def _():
            row = p * MOE_EDGE + lax.broadcasted_iota(jnp.int32, (MOE_EDGE, 1), 0)
            mine = jnp.logical_and(row >= lo, row < hi)
            y_ref[rows, :] = jnp.where(mine, ffn(rows), y_ref[rows, :])


def _experts(xs, plan, w_gu, b_gu, w_down, b_down, layer):
    n_rows, d = xs.shape
    n_visits = n_rows // MOE_ROWS + N_EXPERTS - 1
    assert n_visits <= PLAN_LANES
    f = EXPERT_FF
    wmap = lambda v, plan: (layer, plan[PLAN_EXPERT, v], 0, 0)
    tmap = lambda v, plan: (plan[PLAN_TILE, v], 0)
    grid_spec = pltpu.PrefetchScalarGridSpec(
        num_scalar_prefetch=1,
        grid=(n_visits,),
        in_specs=[pl.BlockSpec((MOE_ROWS, d), tmap),
                  pl.BlockSpec(memory_space=pl.ANY),
                  pl.BlockSpec((1, 1, 1, 2 * f), wmap),
                  pl.BlockSpec(memory_space=pl.ANY),
                  pl.BlockSpec((1, 1, 1, d), wmap)],
        out_specs=pl.BlockSpec((MOE_ROWS, d), tmap),
        scratch_shapes=[pltpu.VMEM((2, d, 2 * f), F32), pltpu.VMEM((2, f, d), F32),
                        pltpu.VMEM((d, 2 * f), BF16), pltpu.VMEM((f, d), BF16),
                        pltpu.SemaphoreType.DMA((2, 2))],
    )
    return pl.pallas_call(
        functools.partial(_expert_kernel, layer=layer),
        grid_spec=grid_spec,
        out_shape=jax.ShapeDtypeStruct((n_rows, d), BF16),
        compiler_params=_cparams(("arbitrary",)),
        name="experts",
    )(plan, xs, w_gu, b_gu.reshape(DEPTH, N_EXPERTS, 1, 2 * f),
      w_down, b_down.reshape(DEPTH, N_EXPERTS, 1, d))


def _combine_kernel(x_ref, y0_ref, y1_ref, y2_ref, y3_ref, gt_ref, mod_ref, fg_ref, o_ref, *, final):
    mod = mod_ref[0, 0]
    gt = gt_ref[0]
    y = gt[:, 0:1] * y0_ref[0].astype(F32)
    for k, y_ref in ((1, y1_ref), (2, y2_ref), (3, y3_ref)):
        y = y + gt[:, k:k + 1] * y_ref[0].astype(F32)
    x2 = x_ref[0] + mod[5:6] * y
    o_ref[0] = _rms(x2) * fg_ref[...] if final else x2


def _combine(x1, yg, gates, modt, final_g, ctx_len, final, gb, group):
    b, s_tot, d = x1.shape
    b0 = group * gb
    nt = s_tot // ROW_TILE
    nctx = ctx_len // ROW_TILE
    off = nctx if final else 0
    seg = lambda j: jnp.where(j + off >= nctx, 1, 0)
    choice = lambda k: pl.BlockSpec((1, ROW_TILE, d), lambda i, j: (k, i * nt + j + off, 0))
    if final:
        out_spec = pl.BlockSpec((1, ROW_TILE, d), lambda i, j: (i, j, 0))
        out_shape = jax.ShapeDtypeStruct((gb, s_tot - off * ROW_TILE, d), F32)
        aliases = {}
    else:
        out_spec = pl.BlockSpec((1, ROW_TILE, d), lambda i, j: (i + b0, j, 0))
        out_shape = jax.ShapeDtypeStruct((b, s_tot, d), F32)
        aliases = {0: 0}
    return pl.pallas_call(
        functools.partial(_combine_kernel, final=final),
        grid=(gb, nt - off),
        in_specs=[pl.BlockSpec((1, ROW_TILE, d), lambda i, j: (i + b0, j + off, 0)),
                  choice(0), choice(1), choice(2), choice(3),
                  pl.BlockSpec((1, ROW_TILE, LOGIT_LANES), lambda i, j: (i + b0, j + off, 0)),
                  pl.BlockSpec((1, 1, 8, d), lambda i, j: (i + b0, seg(j), 0, 0)),
                  pl.BlockSpec((1, d), lambda i, j: (0, 0))],
        out_specs=out_spec,
        out_shape=out_shape,
        input_output_aliases=aliases,
        compiler_params=_cparams(("arbitrary", "arbitrary")),
        name="moe_combine",
    )(x1, yg, yg, yg, yg, gates, modt, final_g.reshape(1, d))


def _moe(x1, h2, top, top_gate, counts, modt, final_g, exp_gu_w, exp_gu_b, exp_down_w, exp_down_b,
         layer, ctx_len, final):
    b, s_tot, d = x1.shape
    gb = b // counts.shape[0]
    n_grp = gb * s_tot
    n_assign = n_grp * TOP_K
    top = top.reshape(b * s_tot, LOGIT_LANES)
    h2 = h2.reshape(b * s_tot, d)
    x = x1
    for g in range(b // gb):
        top_g = top[g * n_grp:(g + 1) * n_grp]
        cnt = counts[g]
        plan = _plan(cnt[0, :N_EXPERTS].reshape(N_EXPERTS, 1), n_assign // MOE_ROWS)
        pos = _positions(top_g, jnp.cumsum(cnt, axis=1) - cnt)[:, :TOP_K]
        inv = pos.reshape(-1)
        _, order = lax.sort((inv, jnp.arange(n_assign, dtype=jnp.int32)), num_keys=1)
        rows = order // TOP_K + g * n_grp
        xs = h2.at[rows].get(mode="promise_in_bounds")
        ys = _experts(xs, plan, exp_gu_w, exp_gu_b, exp_down_w, exp_down_b, layer)
        by_choice = pos.T.reshape(-1)
        yg = ys.at[by_choice].get(mode="promise_in_bounds").reshape(TOP_K, n_grp, d)
        x = _combine(x, yg, top_gate, modt, final_g, ctx_len, final, gb, g)
    return x


def _mixer_layer(x, modt, i, ctx_len, tables, norm1_g, w_in_bf, s5p, ssm_d, conv_p, merge_p):
    b, s_tot, d = x.shape
    u2, cv, qkvg, gates = _in_proj(x, modt, norm1_g, w_in_bf, ctx_len)
    bmat, cmat, lvec = s5p
    y_s5 = _s5_scan(u2, bmat, cmat, lvec, ssm_d, ctx_len)
    hc = _conv_branch(cv, *conv_p, ctx_len)
    cos, sin, tabs = tables
    ro = _retention(qkvg, cos, sin, tabs, ctx_len)
    x1, h2, logits = _merge(x, y_s5, hc, ro, gates, modt, *merge_p, ctx_len)
    group_batch = b if i == DEPTH - 1 else MOE_GROUP_BATCH
    return (x1, h2) + tuple(_router(logits, group_batch))


def kernel(x, c, ctx, c_ctx, ada_w, ada_b, norm1_g, w_in, ssm_lam_re, ssm_lam_im, ssm_log_dt, ssm_b_re, ssm_b_im, ssm_c_re, ssm_c_im, ssm_d, ssm_glu_w, ssm_glu_b, conv_dw_w, conv_dw_b, conv_ln_g, conv_ln_b, conv_pw_w, conv_pw_b, ret_w_o, w_out, norm2_g, router_w, router_b, exp_gu_w, exp_gu_b, exp_down_w, exp_down_b, final_g):
    b, seq, d = x.shape
    ctx_len = ctx.shape[1]
    depth = ada_w.shape[0]
    xs = jnp.concatenate([ctx, x], axis=1)
    s_in = jnp.concatenate([c, c_ctx[None], jnp.zeros((16 - b - 1, d), F32)], axis=0)
    mod = _ada_mod(s_in, ada_w, ada_b).reshape(depth, 16, N_MOD, d)
    mod = jnp.pad(mod, ((0, 0), (0, 0), (0, 8 - N_MOD), (0, 0)))
    modt = jnp.stack([jnp.broadcast_to(mod[:, b][:, None], (depth, b, 8, d)), mod[:, :b]], axis=2)
    tables = _ret_tables(ctx_len, seq)
    wr = jnp.pad(router_w, ((0, 0), (0, 0), (0, LOGIT_LANES - N_EXPERTS)))
    wr_hi = wr.astype(BF16)
    wr = jnp.stack([wr_hi, (wr - wr_hi.astype(F32)).astype(BF16)], axis=1)
    br = jnp.pad(router_b, ((0, 0), (0, LOGIT_LANES - N_EXPERTS))).reshape(depth, 1, LOGIT_LANES)
    for i in range(depth):
        s5p = _s5_params(ssm_lam_re[i], ssm_lam_im[i], ssm_log_dt[i], ssm_b_re[i], ssm_b_im[i],
                         ssm_c_re[i], ssm_c_im[i])
        conv_p = (conv_dw_w[i], conv_dw_b[i], conv_ln_g[i], conv_ln_b[i])
        merge_p = (norm2_g[i], ssm_glu_w[i].astype(BF16), ssm_glu_b[i], conv_pw_w[i].astype(BF16),
                   conv_pw_b[i], ret_w_o[i].astype(BF16), w_out[i].astype(BF16), wr[i], br[i])
        x1, h2, top, top_gate, counts = _mixer_layer(xs, modt[i], i, ctx_len, tables, norm1_g[i],
                                                     w_in[i].astype(BF16), s5p, ssm_d[i], conv_p, merge_p)
        xs = _moe(x1, h2, top, top_gate, counts, modt[i], final_g, exp_gu_w, exp_gu_b, exp_down_w,
                  exp_down_b, i, ctx_len, final=(i == depth - 1))
    return xs
```

```python
import functools
import math

import numpy as np
import jax
import jax.numpy as jnp
from jax import lax
from jax.experimental import pallas as pl
from jax.experimental.pallas import tpu as pltpu

F32 = jnp.float32
BF16 = jnp.bfloat16

D_MODEL = 1024
DEPTH = 4
GRID_W = 64
N_MOD = 6
SSM_WIDTH = 256
SSM_GROUP = 16
SSM_GROUPS = SSM_WIDTH // SSM_GROUP
SSM_STATE = 64
SSM_LANES = SSM_GROUPS * SSM_STATE
CONV_WIDTH = 256
CONV_K = 31
CONV_PAD = 16
RET_HEADS = 4
RET_HEAD_DIM = 128
RET_WIDTH = RET_HEADS * RET_HEAD_DIM
RET_CHUNK = 128
ROPE_BASE = 10000.0
IN_WIDTH = SSM_WIDTH + 2 * CONV_WIDTH + 4 * RET_WIDTH + 3 * D_MODEL
COL_CV = SSM_WIDTH
COL_QKVG = COL_CV + 2 * CONV_WIDTH
COL_GATES = COL_QKVG + 4 * RET_WIDTH
N_EXPERTS = 32
TOP_K = 4
EXPERT_FF = D_MODEL
SWIGLU_LIMIT = 7.0
SWIGLU_ALPHA = 1.702
EPS = 1e-6

ROW_TILE = 256
HALF_TILE = 128
S5_STEPS = 64
S5_STRIP = 512
MOE_ROWS = 512
MOE_PART = 256
MOE_EDGE = 256
ROUTE_ROWS = 768
MOE_GROUP_BATCH = 4
LOGIT_LANES = 128
VMEM_LIMIT = 56 * 1024 * 1024


def _cparams(sem):
    return pltpu.CompilerParams(dimension_semantics=sem, vmem_limit_bytes=VMEM_LIMIT)


def _rms(x):
    return x * lax.rsqrt(jnp.mean(x * x, axis=-1, keepdims=True) + EPS)


def _ada_kernel(s_ref, w_ref, b_ref, o_ref):
    s = s_ref[...]
    s = s * jax.nn.sigmoid(s)
    o_ref[0] = jnp.dot(s, w_ref[0], preferred_element_type=F32,
                       precision=lax.Precision.HIGHEST) + b_ref[0]


def _ada_mod(s_in, ada_w, ada_b):
    depth, d, n = ada_w.shape
    tn = 1536
    return pl.pallas_call(
        _ada_kernel,
        grid=(depth, n // tn),
        in_specs=[pl.BlockSpec((16, d), lambda i, j: (0, 0)),
                  pl.BlockSpec((1, d, tn), lambda i, j: (i, 0, j)),
                  pl.BlockSpec((1, 1, tn), lambda i, j: (i, 0, j))],
        out_specs=pl.BlockSpec((1, 16, tn), lambda i, j: (i, 0, j)),
        out_shape=jax.ShapeDtypeStruct((depth, 16, n), F32),
        compiler_params=_cparams(("arbitrary", "arbitrary")),
        name="ada_mod",
    )(s_in, ada_w, ada_b.reshape(depth, 1, n))


def _inproj_kernel(x_ref, mod_ref, g_ref, w_ref, u_ref, cv_ref, qkvg_ref, gates_ref):
    x = x_ref[0]
    mod = mod_ref[0, 0]
    h = (_rms(x) * g_ref[...]) * (1.0 + mod[1:2]) + mod[0:1]
    hb = h.astype(BF16)
    u_ref[0] = jnp.dot(hb, w_ref[:, 0:COL_CV], preferred_element_type=F32)
    cv_ref[0] = jnp.dot(hb, w_ref[:, COL_CV:COL_QKVG], preferred_element_type=F32)
    qkvg_ref[0] = jnp.dot(hb, w_ref[:, COL_QKVG:COL_GATES], preferred_element_type=F32).astype(BF16)
    gates = jnp.dot(hb, w_ref[:, COL_GATES:IN_WIDTH], preferred_element_type=F32)
    gates_ref[0] = jax.nn.sigmoid(gates).astype(BF16)


def _in_proj(x, modt, g1, w_in_bf, ctx_len):
    b, s_tot, d = x.shape
    nt = s_tot // ROW_TILE
    nctx = ctx_len // ROW_TILE
    seg = lambda j: jnp.where(j >= nctx, 1, 0)
    return pl.pallas_call(
        _inproj_kernel,
        grid=(b, nt),
        in_specs=[pl.BlockSpec((1, ROW_TILE, d), lambda i, j: (i, j, 0)),
                  pl.BlockSpec((1, 1, 8, d), lambda i, j: (i, seg(j), 0, 0)),
                  pl.BlockSpec((1, d), lambda i, j: (0, 0)),
                  pl.BlockSpec((d, IN_WIDTH), lambda i, j: (0, 0))],
        out_specs=[pl.BlockSpec((1, ROW_TILE, SSM_WIDTH), lambda i, j: (i, j, 0)),
                   pl.BlockSpec((1, ROW_TILE, 2 * CONV_WIDTH), lambda i, j: (i, j, 0)),
                   pl.BlockSpec((1, ROW_TILE, 4 * RET_WIDTH), lambda i, j: (i, j, 0)),
                   pl.BlockSpec((1, ROW_TILE, 3 * D_MODEL), lambda i, j: (i, j, 0))],
        out_shape=[jax.ShapeDtypeStruct((b, s_tot, SSM_WIDTH), F32),
                   jax.ShapeDtypeStruct((b, s_tot, 2 * CONV_WIDTH), F32),
                   jax.ShapeDtypeStruct((b, s_tot, 4 * RET_WIDTH), BF16),
                   jax.ShapeDtypeStruct((b, s_tot, 3 * D_MODEL), BF16)],
        compiler_params=_cparams(("arbitrary", "arbitrary")),
        name="in_proj",
    )(x, modt, g1.reshape(1, d), w_in_bf)


def _s5_params(lam_re, lam_im, log_dt, b_re, b_im, c_re, c_im):
    lam = lax.complex(jnp.minimum(lam_re.astype(F32), -1e-4), lam_im.astype(F32))
    lam_dt = lam * jnp.exp(log_dt.astype(F32))[..., None]
    lam_bar = jnp.exp(lam_dt)
    b = lax.complex(b_re.astype(F32), b_im.astype(F32))
    b_bar = ((lam_bar - 1.0) / lam)[..., None] * b
    eye = jnp.eye(SSM_GROUPS, dtype=F32)

    def in_mat(t):
        return jnp.einsum('dgpm,gh->dgmhp', t, eye).reshape(2, SSM_WIDTH, SSM_LANES)

    def out_mat(t):
        return jnp.einsum('dgmp,gh->dgphm', t, eye).reshape(2, SSM_LANES, SSM_WIDTH)

    bmat = jnp.concatenate([in_mat(jnp.real(b_bar)), in_mat(jnp.imag(b_bar))], axis=-1)
    cmat = jnp.concatenate([out_mat(c_re.astype(F32)), -out_mat(c_im.astype(F32))], axis=1)
    lvec = jnp.concatenate([jnp.real(lam_bar).reshape(2, 1, SSM_LANES),
                            jnp.imag(lam_bar).reshape(2, 1, SSM_LANES)], axis=-1)
    return bmat.astype(BF16), cmat.astype(BF16), lvec


def _s5_kernel(u_ref, perm_ref, permt_ref, bmat_ref, cmat_ref, lam_ref, dsk_ref, y_ref, st_ref, carry_ref):
    d = pl.program_id(0)
    c = pl.program_id(1)
    nb, steps, width = u_ref.shape

    @pl.when(c == 0)
    def _():
        carry_ref[...] = jnp.zeros_like(carry_ref)

    u = u_ref[...].reshape(nb * steps, width)
    u_tm = jnp.dot(perm_ref[...], u.astype(BF16), preferred_element_type=F32).astype(BF16)
    bu = jnp.dot(u_tm, bmat_ref[0], preferred_element_type=F32)
    st_ref[...] = bu.reshape(steps, nb, 2 * SSM_LANES)

    for k in range(SSM_LANES // S5_STRIP):
        re = pl.ds(k * S5_STRIP, S5_STRIP)
        im = pl.ds(SSM_LANES + k * S5_STRIP, S5_STRIP)
        lr = jnp.broadcast_to(lam_ref[0, :, re], (nb, S5_STRIP))
        li = jnp.broadcast_to(lam_ref[0, :, im], (nb, S5_STRIP))

        def step(t, s):
            sr, si = s
            tt = jnp.where(d == 1, steps - 1 - t, t)
            nr = lr * sr - li * si + st_ref[tt, :, re]
            ni = lr * si + li * sr + st_ref[tt, :, im]
            st_ref[tt, :, re] = nr
            st_ref[tt, :, im] = ni
            return nr, ni

        sr, si = lax.fori_loop(0, steps, step, (carry_ref[:, re], carry_ref[:, im]), unroll=4)
        carry_ref[:, re] = sr
        carry_ref[:, im] = si

    st = st_ref[...].reshape(steps * nb, 2 * SSM_LANES)
    y_tm = jnp.dot(st.astype(BF16), cmat_ref[0], preferred_element_type=F32)
    y = jnp.dot(permt_ref[...], y_tm.astype(BF16), preferred_element_type=F32)
    skip = jnp.where(d == 0, 1.0, 0.0) * dsk_ref[...]
    y_ref[0] = (y + u * skip).reshape(nb, steps, width)


def _s5_scan(u, bmat, cmat, lvec, d_skip, ctx_len):
    nb, s_tot, width = u.shape
    nch = s_tot // S5_STEPS
    nctx = ctx_len // S5_STEPS
    rows = nb * S5_STEPS
    r = np.arange(rows)
    perm = np.zeros((rows, rows), np.float32)
    perm[r, (r % nb) * S5_STEPS + r // nb] = 1.0
    perm_tm = jnp.asarray(perm, BF16)
    perm_bm = jnp.asarray(perm.T, BF16)

    def chunk(d, c):
        back = jnp.where(c < nctx, nctx - 1 - c, nctx + nch - 1 - c)
        return jnp.where(d == 1, back, c)

    return pl.pallas_call(
        _s5_kernel,
        grid=(2, nch),
        in_specs=[pl.BlockSpec((nb, S5_STEPS, width), lambda d, c: (0, chunk(d, c), 0)),
                  pl.BlockSpec((rows, rows), lambda d, c: (0, 0)),
                  pl.BlockSpec((rows, rows), lambda d, c: (0, 0)),
                  pl.BlockSpec((1, width, 2 * SSM_LANES), lambda d, c: (d, 0, 0)),
                  pl.BlockSpec((1, 2 * SSM_LANES, width), lambda d, c: (d, 0, 0)),
                  pl.BlockSpec((1, 1, 2 * SSM_LANES), lambda d, c: (d, 0, 0)),
                  pl.BlockSpec((1, width), lambda d, c: (0, 0))],
        out_specs=pl.BlockSpec((1, nb, S5_STEPS, width), lambda d, c: (d, 0, chunk(d, c), 0)),
        out_shape=jax.ShapeDtypeStruct((2, nb, s_tot, width), F32),
        scratch_shapes=[pltpu.VMEM((S5_STEPS, nb, 2 * SSM_LANES), F32),
                        pltpu.VMEM((nb, 2 * SSM_LANES), F32)],
        compiler_params=_cparams(("arbitrary", "arbitrary")),
        name="s5_scan",
    )(u, perm_tm, perm_bm, bmat, cmat, lvec, d_skip.reshape(1, width))


def _conv_kernel(cv_ref, w_ref, b_ref, lg_ref, lb_ref, o_ref, hp_ref, win_ref, shift_ref, *, ctx_len):
    s_tot = cv_ref.shape[1]
    lat = s_tot - ctx_len
    rc = RET_CHUNK
    zeros = jnp.zeros((CONV_PAD, CONV_WIDTH), F32)
    a = cv_ref[0, :, 0:CONV_WIDTH]
    g = cv_ref[0, :, CONV_WIDTH:2 * CONV_WIDTH]
    h = a * jax.nn.sigmoid(g)
    hp_ref[0:CONV_PAD] = zeros
    hp_ref[CONV_PAD:CONV_PAD + ctx_len] = h[0:ctx_len]
    hp_ref[CONV_PAD + ctx_len:2 * CONV_PAD + ctx_len] = zeros
    hp_ref[2 * CONV_PAD + ctx_len:2 * CONV_PAD + s_tot] = h[ctx_len:s_tot]
    hp_ref[2 * CONV_PAD + s_tot:3 * CONV_PAD + s_tot] = zeros
    del lat

    def chunk(c, carry):
        r0 = pl.multiple_of(c * rc, rc)
        wbase = pl.multiple_of(r0 + jnp.where(r0 >= ctx_len, CONV_PAD, 0), 8)
        win_ref[...] = hp_ref[pl.ds(wbase, rc + 2 * CONV_PAD), :]
        for r in range(1, 8):
            shift_ref[r] = win_ref[r:r + rc + 2 * CONV_PAD - 8, :]
        acc = jnp.zeros((rc, CONV_WIDTH), F32) + b_ref[...]
        for k in range(CONV_K):
            off = k + CONV_PAD - CONV_K // 2
            base = off - off % 8
            if off % 8 == 0:
                tap = win_ref[base:base + rc, :]
            else:
                tap = shift_ref[off % 8, base:base + rc, :]
            acc = acc + w_ref[k:k + 1, :] * tap
        mu = jnp.mean(acc, axis=-1, keepdims=True)
        var = jnp.mean(jnp.square(acc - mu), axis=-1, keepdims=True)
        y = (acc - mu) * lax.rsqrt(var + EPS) * lg_ref[...] + lb_ref[...]
        o_ref[0, pl.ds(r0, rc), :] = y * jax.nn.sigmoid(y)
        return carry

    lax.fori_loop(0, s_tot // rc, chunk, 0)


def _conv_branch(cv, w_dw, b_dw, ln_g, ln_b, ctx_len):
    b, s_tot, _ = cv.shape
    w = jnp.concatenate([w_dw.reshape(CONV_K, CONV_WIDTH), jnp.zeros((1, CONV_WIDTH), F32)], axis=0)
    vec = lambda t: t.reshape(1, CONV_WIDTH)
    cst = lambda shape: pl.BlockSpec(shape, lambda i: (0,) * len(shape))
    return pl.pallas_call(
        functools.partial(_conv_kernel, ctx_len=ctx_len),
        grid=(b,),
        in_specs=[pl.BlockSpec((1, s_tot, 2 * CONV_WIDTH), lambda i: (i, 0, 0)),
                  cst((CONV_K + 1, CONV_WIDTH)), cst((1, CONV_WIDTH)),
                  cst((1, CONV_WIDTH)), cst((1, CONV_WIDTH))],
        out_specs=pl.BlockSpec((1, s_tot, CONV_WIDTH), lambda i: (i, 0, 0)),
        out_shape=jax.ShapeDtypeStruct((b, s_tot, CONV_WIDTH), F32),
        scratch_shapes=[pltpu.VMEM((s_tot + 3 * CONV_PAD, CONV_WIDTH), F32),
                        pltpu.VMEM((RET_CHUNK + 2 * CONV_PAD, CONV_WIDTH), F32),
                        pltpu.VMEM((8, RET_CHUNK + 2 * CONV_PAD - 8, CONV_WIDTH), F32)],
        compiler_params=_cparams(("arbitrary",)),
        name="conv_branch",
    )(cv, w, vec(b_dw), vec(ln_g), vec(ln_b))


N_TAB = 5


def _ret_tables(ctx_len, seq):
    n = RET_HEAD_DIM // 4
    inv_freq = ROPE_BASE ** (-np.arange(n, dtype=np.float64) / n)
    pos = np.arange(seq)
    ang_r = (pos // GRID_W)[:, None] * inv_freq
    ang_c = (pos % GRID_W)[:, None] * inv_freq
    cos = np.concatenate([np.cos(ang_r), np.cos(ang_r), np.cos(ang_c), np.cos(ang_c)], axis=-1)
    sin = np.concatenate([-np.sin(ang_r), np.sin(ang_r), -np.sin(ang_c), np.sin(ang_c)], axis=-1)
    cos = np.concatenate([np.ones((ctx_len, RET_HEAD_DIM)), cos], axis=0)
    sin = np.concatenate([np.zeros((ctx_len, RET_HEAD_DIM)), sin], axis=0)
    log_g_fwd = np.log1p(-np.exp2(-5.0 - np.arange(RET_HEADS, dtype=np.float64)))
    idx = np.arange(RET_CHUNK, dtype=np.float64)
    diff = idx[:, None] - idx[None, :]
    tabs = np.zeros((2, RET_HEADS, N_TAB, RET_CHUNK, RET_CHUNK))
    ones = np.ones((RET_CHUNK, RET_CHUNK))
    for d, log_g in enumerate((log_g_fwd, log_g_fwd[::-1])):
        for hd in range(RET_HEADS):
            lg = log_g[hd]
            if d == 0:
                mask = np.where(diff >= 0, np.exp(lg * np.maximum(diff, 0.0)), 0.0)
                dec = np.exp(lg * (idx + 1.0))
                zeta = np.exp(lg * (RET_CHUNK - 1 - idx))
            else:
                mask = np.where(diff <= 0, np.exp(lg * np.maximum(-diff, 0.0)), 0.0)
                dec = np.exp(lg * (RET_CHUNK - idx))
                zeta = np.exp(lg * idx)
            tabs[d, hd, 0] = mask
            tabs[d, hd, 1] = dec[:, None] * ones
            tabs[d, hd, 2] = zeta[:, None] * ones
            tabs[d, hd, 3] = np.exp(lg * RET_CHUNK) * ones
            tabs[d, hd, 4] = zeta[None, :] * ones
    return (jnp.asarray(cos, F32), jnp.asarray(sin, F32), jnp.asarray(tabs, F32))


def _ret_kernel(q_ref, k_ref, v_ref, g_ref, cos_ref, sin_ref, tab_ref, o_ref,
                qs_ref, qd_ref, kt_ref, kz_ref, out_ref, state_ref, *, ctx_len):
    s_tot = q_ref.shape[1]
    rc = RET_CHUNK
    nch = s_tot // rc
    nctx = ctx_len // rc
    k_scale = RET_HEAD_DIM ** -0.5
    lane = lax.broadcasted_iota(jnp.int32, (rc, RET_HEAD_DIM), 1)
    first_half = (lane % (RET_HEAD_DIM // 2)) < (RET_HEAD_DIM // 4)

    def rope(t, cs, sn):
        quarter = RET_HEAD_DIM // 4
        partner = jnp.where(first_half, pltpu.roll(t, RET_HEAD_DIM - quarter, 1), pltpu.roll(t, quarter, 1))
        return t * cs + partner * sn

    def prepare(c, carry):
        rows = pl.ds(pl.multiple_of(c * rc, rc), rc)
        cs = cos_ref[rows, :]
        sn = sin_ref[rows, :]
        q = rope(q_ref[0, rows, :].astype(F32), cs, sn)
        kt = (rope(k_ref[0, rows, :].astype(F32), cs, sn) * k_scale).T
        qs_ref[c] = q.astype(BF16)
        kt_ref[c] = kt.astype(BF16)
        for d in range(2):
            qd_ref[d, c] = (q * tab_ref[d, 0, 1]).astype(BF16)
            kz_ref[d, c] = (kt * tab_ref[d, 0, 4]).astype(BF16)
        return carry

    lax.fori_loop(0, nch, prepare, 0, unroll=2)
    state_ref[...] = jnp.zeros_like(state_ref)

    def chunk(c, d):
        rows = pl.ds(pl.multiple_of(c * rc, rc), rc)
        vb = v_ref[0, rows, :]
        state = state_ref[d]
        scores = jnp.dot(qs_ref[c], kt_ref[c], preferred_element_type=F32) * tab_ref[d, 0, 0]
        o = (jnp.dot(scores.astype(BF16), vb, preferred_element_type=F32)
             + jnp.dot(qd_ref[d, c], state.astype(BF16), preferred_element_type=F32))
        kv = jnp.dot(kz_ref[d, c], vb, preferred_element_type=F32)
        state_ref[d] = tab_ref[d, 0, 3] * state + kv
        out_ref[d, rows, :] = o

    def step(j, carry):
        chunk(j, 0)
        chunk(jnp.where(j < nctx, nctx - 1 - j, nctx + nch - 1 - j), 1)
        return carry

    lax.fori_loop(0, nch, step, 0, unroll=3)
    g = g_ref[0].astype(F32)
    o_ref[0] = (_rms(out_ref[0] + out_ref[1]) * (g * jax.nn.sigmoid(g))).astype(o_ref.dtype)


def _retention(qkvg, cos, sin, tabs, ctx_len):
    b, s_tot, _ = qkvg.shape
    hd = RET_HEAD_DIM
    nch = s_tot // RET_CHUNK
    part = lambda p: pl.BlockSpec((1, s_tot, hd), lambda i, h: (i, 0, p * RET_HEADS + h))
    return pl.pallas_call(
        functools.partial(_ret_kernel, ctx_len=ctx_len),
        grid=(b, RET_HEADS),
        in_specs=[part(0), part(1), part(2), part(3),
                  pl.BlockSpec((s_tot, hd), lambda i, h: (0, 0)),
                  pl.BlockSpec((s_tot, hd), lambda i, h: (0, 0)),
                  pl.BlockSpec((2, 1, N_TAB, RET_CHUNK, RET_CHUNK), lambda i, h: (0, h, 0, 0, 0))],
        out_specs=pl.BlockSpec((1, s_tot, hd), lambda i, h: (i, 0, h)),
        out_shape=jax.ShapeDtypeStruct((b, s_tot, RET_WIDTH), BF16),
        scratch_shapes=[pltpu.VMEM((nch, RET_CHUNK, hd), BF16),
                        pltpu.VMEM((2, nch, RET_CHUNK, hd), BF16),
                        pltpu.VMEM((nch, hd, RET_CHUNK), BF16),
                        pltpu.VMEM((2, nch, hd, RET_CHUNK), BF16),
                        pltpu.VMEM((2, s_tot, hd), F32),
                        pltpu.VMEM((2, hd, hd), F32)],
        compiler_params=_cparams(("arbitrary", "arbitrary")),
        name="retention",
    )(qkvg, qkvg, qkvg, qkvg, cos, sin, tabs)


def _merge_kernel(x_ref, yf_ref, yb_ref, hc_ref, ro_ref, gates_ref, mod_ref, g2_ref,
                  wglu_ref, bglu_ref, wpw_ref, bpw_ref, wo_ref, wout_ref, wr_ref, br_ref,
                  x1_ref, h2_ref, lg_ref):
    d = D_MODEL
    mod = mod_ref[0, 0]

    for half in range(ROW_TILE // HALF_TILE):
        rows = pl.ds(half * HALF_TILE, HALF_TILE)
        ys = jax.nn.gelu(yf_ref[0, 0, rows, :] + yb_ref[0, 0, rows, :]).astype(BF16)
        z = jnp.dot(ys, wglu_ref[...], preferred_element_type=F32) + bglu_ref[...]
        y_ssm = z[:, 0:d] * jax.nn.sigmoid(z[:, d:2 * d])
        y_conv = jnp.dot(hc_ref[0, rows, :].astype(BF16), wpw_ref[...],
                         preferred_element_type=F32) + bpw_ref[...]
        y_ret = jnp.dot(ro_ref[0, rows, :].astype(BF16), wo_ref[...], preferred_element_type=F32)
        m = (gates_ref[0, rows, 0:d].astype(F32) * y_ssm
             + gates_ref[0, rows, d:2 * d].astype(F32) * y_conv
             + gates_ref[0, rows, 2 * d:3 * d].astype(F32) * y_ret)
        y = jnp.dot(m.astype(BF16), wout_ref[...], preferred_element_type=F32)
        x1 = x_ref[0, rows, :] + mod[2:3] * y
        x1_ref[0, rows, :] = x1
        h2 = (_rms(x1) * g2_ref[...]) * (1.0 + mod[4:5]) + mod[3:4]
        h2_ref[0, 0, rows, :] = h2
        h2_ref[1, 0, rows, :] = jnp.zeros_like(h2)
        h_hi = h2.astype(BF16)
        h_lo = (h2 - h_hi.astype(F32)).astype(BF16)
        lg_ref[0, rows, :] = (jnp.dot(h_hi, wr_ref[0], preferred_element_type=F32)
                              + jnp.dot(h_lo, wr_ref[0], preferred_element_type=F32)
                              + jnp.dot(h_hi, wr_ref[1], preferred_element_type=F32)) + br_ref[...]


def _router_kernel(lg_ref, tril_ref, ti_ref, tg_ref, cnt_out_ref, cnt_ref):
    @pl.when(jnp.logical_and(pl.program_id(0) == 0, pl.program_id(1) == 0))
    def _():
        cnt_ref[...] = jnp.zeros_like(cnt_ref)

    seen = cnt_ref[0:1, :]
    for part in range(lg_ref.shape[1] // ROW_TILE):
        rows = pl.ds(part * ROW_TILE, ROW_TILE)
        logits = lg_ref[0, rows, :]
        lane = lax.broadcasted_iota(jnp.int32, logits.shape, 1)
        lane_f = lane.astype(F32)
        neg = jnp.float32(-jnp.inf)
        l = jnp.where(lane < N_EXPERTS, logits, neg)
        ti = jnp.zeros(logits.shape, F32)
        vals = []
        picks = []
        for k in range(TOP_K):
            top = jnp.max(l, axis=-1, keepdims=True)
            idx = jnp.min(jnp.where(l == top, lane_f, float(LOGIT_LANES)), axis=-1, keepdims=True)
            ti = jnp.where(lane == k, idx, ti)
            vals.append(top)
            picked = lane_f == idx
            picks.append(picked.astype(F32))
            l = jnp.where(picked, neg, l)
        es = [jnp.exp(v - vals[0]) for v in vals]
        tot = es[0]
        for e in es[1:]:
            tot = tot + e
        tg = jnp.zeros(logits.shape, F32)
        for k in range(TOP_K):
            tg = jnp.where(lane == k, es[k] / tot, tg)
        chosen = picks[0] + picks[1] + picks[2] + picks[3]
        before = seen + jnp.dot(tril_ref[...], chosen.astype(BF16), preferred_element_type=F32)
        for k in range(TOP_K):
            rank = jnp.sum(picks[k] * before, axis=-1, keepdims=True)
            ti = jnp.where(lane == TOP_K + k, rank, ti)
        seen = seen + jnp.sum(chosen, axis=0, keepdims=True)
        ti_ref[0, rows, :] = ti.astype(jnp.int32)
        tg_ref[0, rows, :] = tg
    cnt_ref[...] = jnp.broadcast_to(seen, cnt_ref.shape)
    cnt_out_ref[...] = jnp.broadcast_to(seen, cnt_out_ref.shape)


def _router(logits):
    gb, s_tot, _ = logits.shape
    tril = jnp.asarray(np.tril(np.ones((ROW_TILE, ROW_TILE), np.float32), -1), BF16)
    row = pl.BlockSpec((1, ROUTE_ROWS, LOGIT_LANES), lambda i, j: (i, j, 0))
    return pl.pallas_call(
        _router_kernel,
        grid=(gb, s_tot // ROUTE_ROWS),
        in_specs=[row, pl.BlockSpec((ROW_TILE, ROW_TILE), lambda i, j: (0, 0))],
        out_specs=[row, row, pl.BlockSpec((8, LOGIT_LANES), lambda i, j: (0, 0))],
        out_shape=[jax.ShapeDtypeStruct((gb, s_tot, LOGIT_LANES), jnp.int32),
                   jax.ShapeDtypeStruct((gb, s_tot, LOGIT_LANES), F32),
                   jax.ShapeDtypeStruct((8, LOGIT_LANES), F32)],
        scratch_shapes=[pltpu.VMEM((8, LOGIT_LANES), F32)],
        compiler_params=_cparams(("arbitrary", "arbitrary")),
        name="router",
    )(logits, tril)


def _positions_kernel(top_ref, starts_ref, pos_ref):
    top = top_ref[...]
    lane = lax.broadcasted_iota(jnp.int32, top.shape, 1)
    starts = starts_ref[0:1, :]
    pos = jnp.zeros(top.shape, F32)
    for k in range(TOP_K):
        mine = lane == top[:, k:k + 1]
        first = jnp.sum(jnp.where(mine, starts, 0.0), axis=-1, keepdims=True)
        pos = jnp.where(lane == k, first + top[:, TOP_K + k:TOP_K + k + 1].astype(F32), pos)
    pos_ref[...] = pos.astype(jnp.int32)


def _positions(top, starts_row):
    n = top.shape[0]
    tile = 1024
    return pl.pallas_call(
        _positions_kernel,
        grid=(n // tile,),
        in_specs=[pl.BlockSpec((tile, LOGIT_LANES), lambda i: (i, 0)),
                  pl.BlockSpec((8, LOGIT_LANES), lambda i: (0, 0))],
        out_specs=pl.BlockSpec((tile, LOGIT_LANES), lambda i: (i, 0)),
        out_shape=jax.ShapeDtypeStruct((n, LOGIT_LANES), jnp.int32),
        compiler_params=_cparams(("arbitrary",)),
        name="moe_positions",
    )(top, starts_row)


def _merge(x, y_s5, hc, ro, gates, modt, g2, wglu, bglu, wpw, bpw, wo, wout, wr, br, ctx_len, group):
    b, s_tot, d = x.shape
    gb = MOE_GROUP_BATCH
    b0 = group * gb
    nt = s_tot // ROW_TILE
    nctx = ctx_len // ROW_TILE
    seg = lambda j: jnp.where(j >= nctx, 1, 0)
    row = lambda w: pl.BlockSpec((1, ROW_TILE, w), lambda i, j: (i + b0, j, 0))
    own = lambda w: pl.BlockSpec((1, ROW_TILE, w), lambda i, j: (i, j, 0))
    cst = lambda shape: pl.BlockSpec(shape, lambda i, j: (0,) * len(shape))
    y2 = y_s5
    return pl.pallas_call(
        _merge_kernel,
        grid=(gb, nt),
        in_specs=[row(d),
                  pl.BlockSpec((1, 1, ROW_TILE, SSM_WIDTH), lambda i, j: (0, i + b0, j, 0)),
                  pl.BlockSpec((1, 1, ROW_TILE, SSM_WIDTH), lambda i, j: (1, i + b0, j, 0)),
                  row(CONV_WIDTH), row(RET_WIDTH), row(3 * d),
                  pl.BlockSpec((1, 1, 8, d), lambda i, j: (i + b0, seg(j), 0, 0)),
                  cst((1, d)),
                  cst((SSM_WIDTH, 2 * d)), cst((1, 2 * d)),
                  cst((CONV_WIDTH, d)), cst((1, d)),
                  cst((RET_WIDTH, d)), cst((d, d)),
                  cst((2, d, LOGIT_LANES)), cst((1, LOGIT_LANES))],
        out_specs=[row(d),
                   pl.BlockSpec((2, 1, ROW_TILE, d), lambda i, j: (0, i, j, 0)),
                   own(LOGIT_LANES)],
        out_shape=[jax.ShapeDtypeStruct((b, s_tot, d), F32),
                   jax.ShapeDtypeStruct((2, gb, s_tot, d), F32),
                   jax.ShapeDtypeStruct((gb, s_tot, LOGIT_LANES), F32)],
        input_output_aliases={0: 0},
        compiler_params=_cparams(("arbitrary", "arbitrary")),
        name="merge_logits",
    )(x, y2, y2, hc, ro, gates, modt, g2.reshape(1, d), wglu, bglu.reshape(1, 2 * d),
      wpw, bpw.reshape(1, d), wo, wout, wr, br)


PLAN_TILE, PLAN_EXPERT, PLAN_LO, PLAN_HI, PLAN_NEXT, PLAN_SLOT = range(6)
PLAN_LANES = 256


def _plan_kernel(cnt_ref, plan_ref, *, n_tiles):
    ne = N_EXPERTS
    rows_per = float(MOE_ROWS)
    e = lax.broadcasted_iota(jnp.int32, (ne, PLAN_LANES), 0)
    e_f = e.astype(F32)
    v = lax.broadcasted_iota(jnp.int32, (ne, PLAN_LANES), 1).astype(F32)

    def cumsum(t):
        for s in (1, 2, 4, 8, 16):
            t = t + jnp.where(e >= s, pltpu.roll(t, s, 0), 0.0)
        return t

    def lookup(table, onehot):
        return jnp.sum(onehot * table, axis=0, keepdims=True)

    c = jnp.broadcast_to(cnt_ref[...], (ne, PLAN_LANES))
    has = c > 0.0
    ends = cumsum(c)
    starts = ends - c
    first_tile = jnp.floor(starts / rows_per)
    tiles_per = jnp.where(has, jnp.floor((ends - 1.0) / rows_per) - first_tile + 1.0, 0.0)
    vend = cumsum(tiles_per)
    vstart = vend - tiles_per
    total = jnp.max(vend, axis=0, keepdims=True)
    slot = cumsum(has.astype(F32)) - 1.0
    slot = slot - 2.0 * jnp.floor(slot / 2.0)
    later = jnp.where(has, e_f, float(ne))
    nxt = jnp.where(e < ne - 1, pltpu.roll(later, ne - 1, 0), float(ne))
    for s in (1, 2, 4, 8, 16):
        nxt = jnp.minimum(nxt, jnp.where(e < ne - s, pltpu.roll(nxt, ne - s, 0), float(ne)))
    nxt = jnp.where(nxt >= float(ne), -1.0, nxt)
    last_with_rows = jnp.max(jnp.where(has, e_f, 0.0), axis=0, keepdims=True)

    valid = v[0:1] < total
    ve = jnp.sum((v >= vend).astype(F32), axis=0, keepdims=True)
    ve = jnp.where(valid, jnp.minimum(ve, float(ne - 1)), last_with_rows)
    onehot = (e_f == ve).astype(F32)
    vt = jnp.where(valid, lookup(first_tile, onehot) + v[0:1] - lookup(vstart, onehot),
                   float(n_tiles - 1))
    lo = jnp.where(valid, jnp.clip(lookup(starts, onehot) - vt * rows_per, 0.0, rows_per), 0.0)
    hi = jnp.where(valid, jnp.clip(lookup(ends, onehot) - vt * rows_per, 0.0, rows_per), 0.0)
    fields = {PLAN_TILE: vt, PLAN_EXPERT: ve, PLAN_LO: lo, PLAN_HI: hi,
              PLAN_NEXT: lookup(nxt, onehot), PLAN_SLOT: lookup(slot, onehot)}
    row = lax.broadcasted_iota(jnp.int32, (8, PLAN_LANES), 0)
    plan = jnp.zeros((8, PLAN_LANES), F32)
    for r, val in fields.items():
        plan = jnp.where(row == r, val, plan)
    plan_ref[...] = plan.astype(jnp.int32)


def _plan(counts_col, n_tiles):
    return pl.pallas_call(
        functools.partial(_plan_kernel, n_tiles=n_tiles),
        out_shape=jax.ShapeDtypeStruct((8, PLAN_LANES), jnp.int32),
        name="moe_plan",
    )(counts_col)


def _expert_kernel(plan_ref, x_ref, wgu_hbm, bgu_ref, wd_hbm, bd_ref, y_ref,
                   wgu_f32, wd_f32, wgu_bf, wd_bf, sem, *, layer):
    v = pl.program_id(0)
    prev = jnp.maximum(v - 1, 0)
    lo = plan_ref[PLAN_LO, v]
    hi = plan_ref[PLAN_HI, v]
    expert = plan_ref[PLAN_EXPERT, v]
    next_expert = plan_ref[PLAN_NEXT, v]
    active = hi > lo
    fresh_weights = jnp.logical_or(v == 0, expert != plan_ref[PLAN_EXPERT, prev])
    first_visit = jnp.logical_or(v == 0, plan_ref[PLAN_TILE, v] != plan_ref[PLAN_TILE, prev])

    def weight_copies(expert, slot):
        return (pltpu.make_async_copy(wgu_hbm.at[layer, expert], wgu_f32.at[slot], sem.at[0, slot]),
                pltpu.make_async_copy(wd_hbm.at[layer, expert], wd_f32.at[slot], sem.at[1, slot]))

    @pl.when(jnp.logical_and(fresh_weights, active))
    def _():
        slot = plan_ref[PLAN_SLOT, v]

        @pl.when(v == 0)
        def _():
            for c in weight_copies(expert, slot):
                c.start()

        for c in weight_copies(expert, slot):
            c.wait()

        @pl.when(next_expert >= 0)
        def _():
            for c in weight_copies(next_expert, 1 - slot):
                c.start()

        wgu_bf[...] = wgu_f32[slot].astype(BF16)
        wd_bf[...] = wd_f32[slot].astype(BF16)

    whole = jnp.logical_and(lo == 0, hi == MOE_ROWS)

    def ffn(rows):
        f = EXPERT_FF
        gu = jnp.dot(x_ref[rows, :].astype(BF16), wgu_bf[...],
                     preferred_element_type=F32) + bgu_ref[0, 0]
        gate = jnp.minimum(gu[:, 0:f], SWIGLU_LIMIT)
        up = jnp.clip(gu[:, f:2 * f], -SWIGLU_LIMIT, SWIGLU_LIMIT)
        act = (up + 1.0) * gate * jax.nn.sigmoid(gate * SWIGLU_ALPHA)
        y = jnp.dot(act.astype(BF16), wd_bf[...], preferred_element_type=F32) + bd_ref[0, 0]
        return y.astype(y_ref.dtype)

    parts = [pl.ds(p * MOE_PART, MOE_PART) for p in range(MOE_ROWS // MOE_PART)]

    @pl.when(whole)
    def _():
        for rows in parts:
            y_ref[rows, :] = ffn(rows)

    @pl.when(jnp.logical_and(first_visit, jnp.logical_not(whole)))
    def _():
        y_ref[...] = jnp.zeros_like(y_ref)

    for p in range(MOE_ROWS // MOE_EDGE):
        rows = pl.ds(p * MOE_EDGE, MOE_EDGE)
        touched = jnp.logical_and(lo < (p + 1) * MOE_EDGE, hi > p * MOE_EDGE)

        @pl.when(jnp.logical_and(touched, jnp.logical_not(whole)))
        def _():
            row = p * MOE_EDGE + lax.broadcasted_iota(jnp.int32, (MOE_EDGE, 1), 0)
            mine = jnp.logical_and(row >= lo, row < hi)
            y_ref[rows, :] = jnp.where(mine, ffn(rows), y_ref[rows, :])


def _experts(xs, plan, w_gu, b_gu, w_down, b_down, layer):
    n_rows, d = xs.shape
    n_visits = n_rows // MOE_ROWS + N_EXPERTS - 1
    assert n_visits <= PLAN_LANES
    f = EXPERT_FF
    wmap = lambda v, plan: (layer, plan[PLAN_EXPERT, v], 0, 0)
    tmap = lambda v, plan: (plan[PLAN_TILE, v], 0)
    grid_spec = pltpu.PrefetchScalarGridSpec(
        num_scalar_prefetch=1,
        grid=(n_visits,),
        in_specs=[pl.BlockSpec((MOE_ROWS, d), tmap),
                  pl.BlockSpec(memory_space=pl.ANY),
                  pl.BlockSpec((1, 1, 1, 2 * f), wmap),
                  pl.BlockSpec(memory_space=pl.ANY),
                  pl.BlockSpec((1, 1, 1, d), wmap)],
        out_specs=pl.BlockSpec((MOE_ROWS, d), tmap),
        scratch_shapes=[pltpu.VMEM((2, d, 2 * f), F32), pltpu.VMEM((2, f, d), F32),
                        pltpu.VMEM((d, 2 * f), BF16), pltpu.VMEM((f, d), BF16),
                        pltpu.SemaphoreType.DMA((2, 2))],
    )
    return pl.pallas_call(
        functools.partial(_expert_kernel, layer=layer),
        grid_spec=grid_spec,
        out_shape=jax.ShapeDtypeStruct((n_rows, d), BF16),
        compiler_params=_cparams(("arbitrary",)),
        name="experts",
    )(plan, xs, w_gu, b_gu.reshape(DEPTH, N_EXPERTS, 1, 2 * f),
      w_down, b_down.reshape(DEPTH, N_EXPERTS, 1, d))


def _combine_kernel(x_ref, y0_ref, y1_ref, y2_ref, y3_ref, gt_ref, mod_ref, fg_ref, o_ref, *, final):
    mod = mod_ref[0, 0]
    gt = gt_ref[0]
    y = gt[:, 0:1] * y0_ref[0].astype(F32)
    for k, y_ref in ((1, y1_ref), (2, y2_ref), (3, y3_ref)):
        y = y + gt[:, k:k + 1] * y_ref[0].astype(F32)
    x2 = x_ref[0] + mod[5:6] * y
    o_ref[0] = _rms(x2) * fg_ref[...] if final else x2


def _combine(x1, yg, gates, modt, final_g, ctx_len, final, gb, group):
    b, s_tot, d = x1.shape
    b0 = group * gb
    nt = s_tot // ROW_TILE
    nctx = ctx_len // ROW_TILE
    off = nctx if final else 0
    seg = lambda j: jnp.where(j + off >= nctx, 1, 0)
    choice = lambda k: pl.BlockSpec((1, ROW_TILE, d), lambda i, j: (k, i * nt + j + off, 0))
    if final:
        out_spec = pl.BlockSpec((1, ROW_TILE, d), lambda i, j: (i, j, 0))
        out_shape = jax.ShapeDtypeStruct((gb, s_tot - off * ROW_TILE, d), F32)
        aliases = {}
    else:
        out_spec = pl.BlockSpec((1, ROW_TILE, d), lambda i, j: (i + b0, j, 0))
        out_shape = jax.ShapeDtypeStruct((b, s_tot, d), F32)
        aliases = {0: 0}
    return pl.pallas_call(
        functools.partial(_combine_kernel, final=final),
        grid=(gb, nt - off),
        in_specs=[pl.BlockSpec((1, ROW_TILE, d), lambda i, j: (i + b0, j + off, 0)),
                  choice(0), choice(1), choice(2), choice(3),
                  pl.BlockSpec((1, ROW_TILE, LOGIT_LANES), lambda i, j: (i, j + off, 0)),
                  pl.BlockSpec((1, 1, 8, d), lambda i, j: (i + b0, seg(j), 0, 0)),
                  pl.BlockSpec((1, d), lambda i, j: (0, 0))],
        out_specs=out_spec,
        out_shape=out_shape,
        input_output_aliases=aliases,
        compiler_params=_cparams(("arbitrary", "arbitrary")),
        name="moe_combine",
    )(x1, yg, yg, yg, yg, gates, modt, final_g.reshape(1, d))


def _moe_group(x, h2, logits, modt, final_g, exp_gu_w, exp_gu_b, exp_down_w, exp_down_b,
               layer, ctx_len, final, group):
    _, gb, s_tot, d = h2.shape
    n_grp = gb * s_tot
    n_assign = n_grp * TOP_K
    top, top_gate, cnt = _router(logits)
    plan = _plan(cnt[0, :N_EXPERTS].reshape(N_EXPERTS, 1), n_assign // MOE_ROWS)
    pos = _positions(top.reshape(n_grp, LOGIT_LANES), jnp.cumsum(cnt, axis=1) - cnt)[:, :TOP_K]
    _, order = lax.sort((pos.reshape(-1), jnp.arange(n_assign, dtype=jnp.int32)), num_keys=1)
    xs = h2.reshape(2 * n_grp, d).at[order // TOP_K].get(mode="promise_in_bounds")
    ys = _experts(xs, plan, exp_gu_w, exp_gu_b, exp_down_w, exp_down_b, layer)
    yg = ys.at[pos.T.reshape(-1)].get(mode="promise_in_bounds").reshape(TOP_K, n_grp, d)
    return _combine(x, yg, top_gate, modt, final_g, ctx_len, final, gb, group)


def _layer(x, modt, i, ctx_len, tables, norm1_g, w_in_bf, s5p, ssm_d, conv_p, merge_p, moe_p, final_g,
           final):
    b, s_tot, d = x.shape
    u2, cv, qkvg, gates = _in_proj(x, modt, norm1_g, w_in_bf, ctx_len)
    bmat, cmat, lvec = s5p
    y_s5 = _s5_scan(u2, bmat, cmat, lvec, ssm_d, ctx_len)
    hc = _conv_branch(cv, *conv_p, ctx_len)
    cos, sin, tabs = tables
    ro = _retention(qkvg, cos, sin, tabs, ctx_len)
    groups = range(b // MOE_GROUP_BATCH)
    routed = []
    for g in groups:
        x, h2, logits = _merge(x, y_s5, hc, ro, gates, modt, *merge_p, ctx_len, g)
        routed.append((h2, logits))
    outs = []
    for g in groups:
        h2, logits = routed[g]
        out = _moe_group(x, h2, logits, modt, final_g, *moe_p, i, ctx_len, final, g)
        if final:
            outs.append(out)
        else:
            x = out
    return jnp.concatenate(outs, axis=0) if final else x


def kernel(x, c, ctx, c_ctx, ada_w, ada_b, norm1_g, w_in, ssm_lam_re, ssm_lam_im, ssm_log_dt, ssm_b_re, ssm_b_im, ssm_c_re, ssm_c_im, ssm_d, ssm_glu_w, ssm_glu_b, conv_dw_w, conv_dw_b, conv_ln_g, conv_ln_b, conv_pw_w, conv_pw_b, ret_w_o, w_out, norm2_g, router_w, router_b, exp_gu_w, exp_gu_b, exp_down_w, exp_down_b, final_g):
    b, seq, d = x.shape
    ctx_len = ctx.shape[1]
    depth = ada_w.shape[0]
    xs = jnp.concatenate([ctx, x], axis=1)
    s_in = jnp.concatenate([c, c_ctx[None], jnp.zeros((16 - b - 1, d), F32)], axis=0)
    mod = _ada_mod(s_in, ada_w, ada_b).reshape(depth, 16, N_MOD, d)
    mod = jnp.pad(mod, ((0, 0), (0, 0), (0, 8 - N_MOD), (0, 0)))
    modt = jnp.stack([jnp.broadcast_to(mod[:, b][:, None], (depth, b, 8, d)), mod[:, :b]], axis=2)
    tables = _ret_tables(ctx_len, seq)
    wr = jnp.pad(router_w, ((0, 0), (0, 0), (0, LOGIT_LANES - N_EXPERTS)))
    wr_hi = wr.astype(BF16)
    wr = jnp.stack([wr_hi, (wr - wr_hi.astype(F32)).astype(BF16)], axis=1)
    br = jnp.pad(router_b, ((0, 0), (0, LOGIT_LANES - N_EXPERTS))).reshape(depth, 1, LOGIT_LANES)
    for i in range(depth):
        s5p = _s5_params(ssm_lam_re[i], ssm_lam_im[i], ssm_log_dt[i], ssm_b_re[i], ssm_b_im[i],
                         ssm_c_re[i], ssm_c_im[i])
        conv_p = (conv_dw_w[i], conv_dw_b[i], conv_ln_g[i], conv_ln_b[i])
        merge_p = (norm2_g[i], ssm_glu_w[i].astype(BF16), ssm_glu_b[i], conv_pw_w[i].astype(BF16),
                   conv_pw_b[i], ret_w_o[i].astype(BF16), w_out[i].astype(BF16), wr[i], br[i])
        moe_p = (exp_gu_w, exp_gu_b, exp_down_w, exp_down_b)
        xs = _layer(xs, modt[i], i, ctx_len, tables, norm1_g[i], w_in[i].astype(BF16), s5p, ssm_d[i],
                    conv_p, merge_p, moe_p, final_g, final=(i == depth - 1))
    return xs
```

```python
import functools
import math

import numpy as np
import jax
import jax.numpy as jnp
from jax import lax
from jax.experimental import pallas as pl
from jax.experimental.pallas import tpu as pltpu

F32 = jnp.float32
BF16 = jnp.bfloat16

D_MODEL = 1024
DEPTH = 4
GRID_W = 64
N_MOD = 6
SSM_WIDTH = 256
SSM_GROUP = 16
SSM_GROUPS = SSM_WIDTH // SSM_GROUP
SSM_STATE = 64
SSM_LANES = SSM_GROUPS * SSM_STATE
CONV_WIDTH = 256
CONV_K = 31
CONV_PAD = 16
RET_HEADS = 4
RET_HEAD_DIM = 128
RET_WIDTH = RET_HEADS * RET_HEAD_DIM
RET_CHUNK = 128
ROPE_BASE = 10000.0
IN_WIDTH = SSM_WIDTH + 2 * CONV_WIDTH + 4 * RET_WIDTH + 3 * D_MODEL
COL_CV = SSM_WIDTH
COL_QKVG = COL_CV + 2 * CONV_WIDTH
COL_GATES = COL_QKVG + 4 * RET_WIDTH
N_EXPERTS = 32
TOP_K = 4
EXPERT_FF = D_MODEL
SWIGLU_LIMIT = 7.0
SWIGLU_ALPHA = 1.702
EPS = 1e-6

ROW_TILE = 256
HALF_TILE = 128
S5_STEPS = 64
S5_STRIP = 512
MOE_ROWS = 512
MOE_PART = 256
MOE_EDGE = 256
ROUTE_ROWS = 768
MOE_GROUP_BATCH = 4
LOGIT_LANES = 128
VMEM_LIMIT = 56 * 1024 * 1024


def _cparams(sem):
    return pltpu.CompilerParams(dimension_semantics=sem, vmem_limit_bytes=VMEM_LIMIT)


def _rms(x):
    return x * lax.rsqrt(jnp.mean(x * x, axis=-1, keepdims=True) + EPS)


def _ada_kernel(s_ref, w_ref, b_ref, o_ref):
    s = s_ref[...]
    s = s * jax.nn.sigmoid(s)
    o_ref[0] = jnp.dot(s, w_ref[0], preferred_element_type=F32,
                       precision=lax.Precision.HIGHEST) + b_ref[0]


def _ada_mod(s_in, ada_w, ada_b):
    depth, d, n = ada_w.shape
    tn = 1536
    return pl.pallas_call(
        _ada_kernel,
        grid=(depth, n // tn),
        in_specs=[pl.BlockSpec((16, d), lambda i, j: (0, 0)),
                  pl.BlockSpec((1, d, tn), lambda i, j: (i, 0, j)),
                  pl.BlockSpec((1, 1, tn), lambda i, j: (i, 0, j))],
        out_specs=pl.BlockSpec((1, 16, tn), lambda i, j: (i, 0, j)),
        out_shape=jax.ShapeDtypeStruct((depth, 16, n), F32),
        compiler_params=_cparams(("arbitrary", "arbitrary")),
        name="ada_mod",
    )(s_in, ada_w, ada_b.reshape(depth, 1, n))


def _inproj_kernel(x_ref, mod_ref, g_ref, w_ref, u_ref, cv_ref, qkvg_ref, gates_ref):
    x = x_ref[0]
    mod = mod_ref[0, 0]
    h = (_rms(x) * g_ref[...]) * (1.0 + mod[1:2]) + mod[0:1]
    hb = h.astype(BF16)
    for c0 in range(COL_GATES, IN_WIDTH, D_MODEL):
        gates = jnp.dot(hb, w_ref[:, c0:c0 + D_MODEL], preferred_element_type=F32)
        gates_ref[0, :, c0 - COL_GATES:c0 - COL_GATES + D_MODEL] = jax.nn.sigmoid(gates).astype(BF16)
    qkvg_ref[0] = jnp.dot(hb, w_ref[:, COL_QKVG:COL_GATES], preferred_element_type=F32).astype(BF16)
    cv_ref[0] = jnp.dot(hb, w_ref[:, COL_CV:COL_QKVG], preferred_element_type=F32)
    u_ref[0] = jnp.dot(hb, w_ref[:, 0:COL_CV], preferred_element_type=F32)


def _in_proj(x, modt, g1, w_in_bf, ctx_len):
    b, s_tot, d = x.shape
    nt = s_tot // ROW_TILE
    nctx = ctx_len // ROW_TILE
    seg = lambda j: jnp.where(j >= nctx, 1, 0)
    return pl.pallas_call(
        _inproj_kernel,
        grid=(b, nt),
        in_specs=[pl.BlockSpec((1, ROW_TILE, d), lambda i, j: (i, j, 0)),
                  pl.BlockSpec((1, 1, 8, d), lambda i, j: (i, seg(j), 0, 0)),
                  pl.BlockSpec((1, d), lambda i, j: (0, 0)),
                  pl.BlockSpec((d, IN_WIDTH), lambda i, j: (0, 0))],
        out_specs=[pl.BlockSpec((1, ROW_TILE, SSM_WIDTH), lambda i, j: (i, j, 0)),
                   pl.BlockSpec((1, ROW_TILE, 2 * CONV_WIDTH), lambda i, j: (i, j, 0)),
                   pl.BlockSpec((1, ROW_TILE, 4 * RET_WIDTH), lambda i, j: (i, j, 0)),
                   pl.BlockSpec((1, ROW_TILE, 3 * D_MODEL), lambda i, j: (i, j, 0))],
        out_shape=[jax.ShapeDtypeStruct((b, s_tot, SSM_WIDTH), F32),
                   jax.ShapeDtypeStruct((b, s_tot, 2 * CONV_WIDTH), F32),
                   jax.ShapeDtypeStruct((b, s_tot, 4 * RET_WIDTH), BF16),
                   jax.ShapeDtypeStruct((b, s_tot, 3 * D_MODEL), BF16)],
        compiler_params=_cparams(("arbitrary", "arbitrary")),
        name="in_proj",
    )(x, modt, g1.reshape(1, d), w_in_bf)


def _s5_params(lam_re, lam_im, log_dt, b_re, b_im, c_re, c_im):
    lam = lax.complex(jnp.minimum(lam_re.astype(F32), -1e-4), lam_im.astype(F32))
    lam_dt = lam * jnp.exp(log_dt.astype(F32))[..., None]
    lam_bar = jnp.exp(lam_dt)
    b = lax.complex(b_re.astype(F32), b_im.astype(F32))
    b_bar = ((lam_bar - 1.0) / lam)[..., None] * b
    eye = jnp.eye(SSM_GROUPS, dtype=F32)

    def in_mat(t):
        return jnp.einsum('dgpm,gh->dgmhp', t, eye).reshape(2, SSM_WIDTH, SSM_LANES)

    def out_mat(t):
        return jnp.einsum('dgmp,gh->dgphm', t, eye).reshape(2, SSM_LANES, SSM_WIDTH)

    bmat = jnp.concatenate([in_mat(jnp.real(b_bar)), in_mat(jnp.imag(b_bar))], axis=-1)
    cmat = jnp.concatenate([out_mat(c_re.astype(F32)), -out_mat(c_im.astype(F32))], axis=1)
    lvec = jnp.concatenate([jnp.real(lam_bar).reshape(2, 1, SSM_LANES),
                            jnp.imag(lam_bar).reshape(2, 1, SSM_LANES)], axis=-1)
    return bmat.astype(BF16), cmat.astype(BF16), lvec


def _s5_kernel(u_ref, perm_ref, permt_ref, bmat_ref, cmat_ref, lam_ref, dsk_ref, y_ref, st_ref, carry_ref):
    d = pl.program_id(0)
    c = pl.program_id(1)
    nb, steps, width = u_ref.shape

    @pl.when(c == 0)
    def _():
        carry_ref[...] = jnp.zeros_like(carry_ref)

    u = u_ref[...].reshape(nb * steps, width)
    u_tm = jnp.dot(perm_ref[...], u.astype(BF16), preferred_element_type=F32).astype(BF16)
    bu = jnp.dot(u_tm, bmat_ref[0], preferred_element_type=F32)
    st_ref[...] = bu.reshape(steps, nb, 2 * SSM_LANES)

    for k in range(SSM_LANES // S5_STRIP):
        re = pl.ds(k * S5_STRIP, S5_STRIP)
        im = pl.ds(SSM_LANES + k * S5_STRIP, S5_STRIP)
        lr = jnp.broadcast_to(lam_ref[0, :, re], (nb, S5_STRIP))
        li = jnp.broadcast_to(lam_ref[0, :, im], (nb, S5_STRIP))

        def step(t, s):
            sr, si = s
            tt = jnp.where(d == 1, steps - 1 - t, t)
            nr = lr * sr - li * si + st_ref[tt, :, re]
            ni = lr * si + li * sr + st_ref[tt, :, im]
            st_ref[tt, :, re] = nr
            st_ref[tt, :, im] = ni
            return nr, ni

        sr, si = lax.fori_loop(0, steps, step, (carry_ref[:, re], carry_ref[:, im]), unroll=4)
        carry_ref[:, re] = sr
        carry_ref[:, im] = si

    st = st_ref[...].reshape(steps * nb, 2 * SSM_LANES)
    y_tm = jnp.dot(st.astype(BF16), cmat_ref[0], preferred_element_type=F32)
    y = jnp.dot(permt_ref[...], y_tm.astype(BF16), preferred_element_type=F32)
    skip = jnp.where(d == 0, 1.0, 0.0) * dsk_ref[...]
    y_ref[0] = (y + u * skip).reshape(nb, steps, width)


def _s5_scan(u, bmat, cmat, lvec, d_skip, ctx_len):
    nb, s_tot, width = u.shape
    nch = s_tot // S5_STEPS
    nctx = ctx_len // S5_STEPS
    rows = nb * S5_STEPS
    r = np.arange(rows)
    perm = np.zeros((rows, rows), np.float32)
    perm[r, (r % nb) * S5_STEPS + r // nb] = 1.0
    perm_tm = jnp.asarray(perm, BF16)
    perm_bm = jnp.asarray(perm.T, BF16)

    def chunk(d, c):
        back = jnp.where(c < nctx, nctx - 1 - c, nctx + nch - 1 - c)
        return jnp.where(d == 1, back, c)

    return pl.pallas_call(
        _s5_kernel,
        grid=(2, nch),
        in_specs=[pl.BlockSpec((nb, S5_STEPS, width), lambda d, c: (0, chunk(d, c), 0)),
                  pl.BlockSpec((rows, rows), lambda d, c: (0, 0)),
                  pl.BlockSpec((rows, rows), lambda d, c: (0, 0)),
                  pl.BlockSpec((1, width, 2 * SSM_LANES), lambda d, c: (d, 0, 0)),
                  pl.BlockSpec((1, 2 * SSM_LANES, width), lambda d, c: (d, 0, 0)),
                  pl.BlockSpec((1, 1, 2 * SSM_LANES), lambda d, c: (d, 0, 0)),
                  pl.BlockSpec((1, width), lambda d, c: (0, 0))],
        out_specs=pl.BlockSpec((1, nb, S5_STEPS, width), lambda d, c: (d, 0, chunk(d, c), 0)),
        out_shape=jax.ShapeDtypeStruct((2, nb, s_tot, width), F32),
        scratch_shapes=[pltpu.VMEM((S5_STEPS, nb, 2 * SSM_LANES), F32),
                        pltpu.VMEM((nb, 2 * SSM_LANES), F32)],
        compiler_params=_cparams(("arbitrary", "arbitrary")),
        name="s5_scan",
    )(u, perm_tm, perm_bm, bmat, cmat, lvec, d_skip.reshape(1, width))


def _conv_kernel(cv_ref, w_ref, b_ref, lg_ref, lb_ref, o_ref, hp_ref, win_ref, shift_ref, *, ctx_len):
    s_tot = cv_ref.shape[1]
    lat = s_tot - ctx_len
    rc = RET_CHUNK
    zeros = jnp.zeros((CONV_PAD, CONV_WIDTH), F32)
    a = cv_ref[0, :, 0:CONV_WIDTH]
    g = cv_ref[0, :, CONV_WIDTH:2 * CONV_WIDTH]
    h = a * jax.nn.sigmoid(g)
    hp_ref[0:CONV_PAD] = zeros
    hp_ref[CONV_PAD:CONV_PAD + ctx_len] = h[0:ctx_len]
    hp_ref[CONV_PAD + ctx_len:2 * CONV_PAD + ctx_len] = zeros
    hp_ref[2 * CONV_PAD + ctx_len:2 * CONV_PAD + s_tot] = h[ctx_len:s_tot]
    hp_ref[2 * CONV_PAD + s_tot:3 * CONV_PAD + s_tot] = zeros
    del lat

    def chunk(c, carry):
        r0 = pl.multiple_of(c * rc, rc)
        wbase = pl.multiple_of(r0 + jnp.where(r0 >= ctx_len, CONV_PAD, 0), 8)
        win_ref[...] = hp_ref[pl.ds(wbase, rc + 2 * CONV_PAD), :]
        for r in range(1, 8):
            shift_ref[r] = win_ref[r:r + rc + 2 * CONV_PAD - 8, :]
        acc = jnp.zeros((rc, CONV_WIDTH), F32) + b_ref[...]
        for k in range(CONV_K):
            off = k + CONV_PAD - CONV_K // 2
            base = off - off % 8
            if off % 8 == 0:
                tap = win_ref[base:base + rc, :]
            else:
                tap = shift_ref[off % 8, base:base + rc, :]
            acc = acc + w_ref[k:k + 1, :] * tap
        mu = jnp.mean(acc, axis=-1, keepdims=True)
        var = jnp.mean(jnp.square(acc - mu), axis=-1, keepdims=True)
        y = (acc - mu) * lax.rsqrt(var + EPS) * lg_ref[...] + lb_ref[...]
        o_ref[0, pl.ds(r0, rc), :] = y * jax.nn.sigmoid(y)
        return carry

    lax.fori_loop(0, s_tot // rc, chunk, 0)


def _conv_branch(cv, w_dw, b_dw, ln_g, ln_b, ctx_len):
    b, s_tot, _ = cv.shape
    w = jnp.concatenate([w_dw.reshape(CONV_K, CONV_WIDTH), jnp.zeros((1, CONV_WIDTH), F32)], axis=0)
    vec = lambda t: t.reshape(1, CONV_WIDTH)
    cst = lambda shape: pl.BlockSpec(shape, lambda i: (0,) * len(shape))
    return pl.pallas_call(
        functools.partial(_conv_kernel, ctx_len=ctx_len),
        grid=(b,),
        in_specs=[pl.BlockSpec((1, s_tot, 2 * CONV_WIDTH), lambda i: (i, 0, 0)),
                  cst((CONV_K + 1, CONV_WIDTH)), cst((1, CONV_WIDTH)),
                  cst((1, CONV_WIDTH)), cst((1, CONV_WIDTH))],
        out_specs=pl.BlockSpec((1, s_tot, CONV_WIDTH), lambda i: (i, 0, 0)),
        out_shape=jax.ShapeDtypeStruct((b, s_tot, CONV_WIDTH), F32),
        scratch_shapes=[pltpu.VMEM((s_tot + 3 * CONV_PAD, CONV_WIDTH), F32),
                        pltpu.VMEM((RET_CHUNK + 2 * CONV_PAD, CONV_WIDTH), F32),
                        pltpu.VMEM((8, RET_CHUNK + 2 * CONV_PAD - 8, CONV_WIDTH), F32)],
        compiler_params=_cparams(("arbitrary",)),
        name="conv_branch",
    )(cv, w, vec(b_dw), vec(ln_g), vec(ln_b))


N_TAB = 5


def _ret_tables(ctx_len, seq):
    n = RET_HEAD_DIM // 4
    inv_freq = ROPE_BASE ** (-np.arange(n, dtype=np.float64) / n)
    pos = np.arange(seq)
    ang_r = (pos // GRID_W)[:, None] * inv_freq
    ang_c = (pos % GRID_W)[:, None] * inv_freq
    cos = np.concatenate([np.cos(ang_r), np.cos(ang_r), np.cos(ang_c), np.cos(ang_c)], axis=-1)
    sin = np.concatenate([-np.sin(ang_r), np.sin(ang_r), -np.sin(ang_c), np.sin(ang_c)], axis=-1)
    cos = np.concatenate([np.ones((ctx_len, RET_HEAD_DIM)), cos], axis=0)
    sin = np.concatenate([np.zeros((ctx_len, RET_HEAD_DIM)), sin], axis=0)
    log_g_fwd = np.log1p(-np.exp2(-5.0 - np.arange(RET_HEADS, dtype=np.float64)))
    idx = np.arange(RET_CHUNK, dtype=np.float64)
    diff = idx[:, None] - idx[None, :]
    tabs = np.zeros((2, RET_HEADS, N_TAB, RET_CHUNK, RET_CHUNK))
    ones = np.ones((RET_CHUNK, RET_CHUNK))
    for d, log_g in enumerate((log_g_fwd, log_g_fwd[::-1])):
        for hd in range(RET_HEADS):
            lg = log_g[hd]
            if d == 0:
                mask = np.where(diff >= 0, np.exp(lg * np.maximum(diff, 0.0)), 0.0)
                dec = np.exp(lg * (idx + 1.0))
                zeta = np.exp(lg * (RET_CHUNK - 1 - idx))
            else:
                mask = np.where(diff <= 0, np.exp(lg * np.maximum(-diff, 0.0)), 0.0)
                dec = np.exp(lg * (RET_CHUNK - idx))
                zeta = np.exp(lg * idx)
            tabs[d, hd, 0] = mask
            tabs[d, hd, 1] = dec[:, None] * ones
            tabs[d, hd, 2] = zeta[:, None] * ones
            tabs[d, hd, 3] = np.exp(lg * RET_CHUNK) * ones
            tabs[d, hd, 4] = zeta[None, :] * ones
    return (jnp.asarray(cos, F32), jnp.asarray(sin, F32), jnp.asarray(tabs, F32))


def _ret_kernel(q_ref, k_ref, v_ref, g_ref, cos_ref, sin_ref, tab_ref, o_ref,
                qs_ref, qd_ref, kt_ref, kz_ref, out_ref, state_ref, *, ctx_len):
    s_tot = q_ref.shape[1]
    rc = RET_CHUNK
    nch = s_tot // rc
    nctx = ctx_len // rc
    k_scale = RET_HEAD_DIM ** -0.5
    lane = lax.broadcasted_iota(jnp.int32, (rc, RET_HEAD_DIM), 1)
    first_half = (lane % (RET_HEAD_DIM // 2)) < (RET_HEAD_DIM // 4)

    def rope(t, cs, sn):
        quarter = RET_HEAD_DIM // 4
        partner = jnp.where(first_half, pltpu.roll(t, RET_HEAD_DIM - quarter, 1), pltpu.roll(t, quarter, 1))
        return t * cs + partner * sn

    def prepare(c, carry):
        rows = pl.ds(pl.multiple_of(c * rc, rc), rc)
        cs = cos_ref[rows, :]
        sn = sin_ref[rows, :]
        q = rope(q_ref[0, rows, :].astype(F32), cs, sn)
        kt = (rope(k_ref[0, rows, :].astype(F32), cs, sn) * k_scale).T
        qs_ref[c] = q.astype(BF16)
        kt_ref[c] = kt.astype(BF16)
        for d in range(2):
            qd_ref[d, c] = (q * tab_ref[d, 0, 1]).astype(BF16)
            kz_ref[d, c] = (kt * tab_ref[d, 0, 4]).astype(BF16)
        return carry

    lax.fori_loop(0, nch, prepare, 0, unroll=2)
    state_ref[...] = jnp.zeros_like(state_ref)

    def chunk(c, d):
        rows = pl.ds(pl.multiple_of(c * rc, rc), rc)
        vb = v_ref[0, rows, :]
        state = state_ref[d]
        scores = jnp.dot(qs_ref[c], kt_ref[c], preferred_element_type=F32) * tab_ref[d, 0, 0]
        o = (jnp.dot(scores.astype(BF16), vb, preferred_element_type=F32)
             + jnp.dot(qd_ref[d, c], state.astype(BF16), preferred_element_type=F32))
        kv = jnp.dot(kz_ref[d, c], vb, preferred_element_type=F32)
        state_ref[d] = tab_ref[d, 0, 3] * state + kv
        out_ref[d, rows, :] = o

    def step(j, carry):
        chunk(j, 0)
        chunk(jnp.where(j < nctx, nctx - 1 - j, nctx + nch - 1 - j), 1)
        return carry

    lax.fori_loop(0, nch, step, 0, unroll=3)
    g = g_ref[0].astype(F32)
    o_ref[0] = (_rms(out_ref[0] + out_ref[1]) * (g * jax.nn.sigmoid(g))).astype(o_ref.dtype)


def _retention(qkvg, cos, sin, tabs, ctx_len):
    b, s_tot, _ = qkvg.shape
    hd = RET_HEAD_DIM
    nch = s_tot // RET_CHUNK
    part = lambda p: pl.BlockSpec((1, s_tot, hd), lambda i, h: (i, 0, p * RET_HEADS + h))
    return pl.pallas_call(
        functools.partial(_ret_kernel, ctx_len=ctx_len),
        grid=(b, RET_HEADS),
        in_specs=[part(0), part(1), part(2), part(3),
                  pl.BlockSpec((s_tot, hd), lambda i, h: (0, 0)),
                  pl.BlockSpec((s_tot, hd), lambda i, h: (0, 0)),
                  pl.BlockSpec((2, 1, N_TAB, RET_CHUNK, RET_CHUNK), lambda i, h: (0, h, 0, 0, 0))],
        out_specs=pl.BlockSpec((1, s_tot, hd), lambda i, h: (i, 0, h)),
        out_shape=jax.ShapeDtypeStruct((b, s_tot, RET_WIDTH), BF16),
        scratch_shapes=[pltpu.VMEM((nch, RET_CHUNK, hd), BF16),
                        pltpu.VMEM((2, nch, RET_CHUNK, hd), BF16),
                        pltpu.VMEM((nch, hd, RET_CHUNK), BF16),
                        pltpu.VMEM((2, nch, hd, RET_CHUNK), BF16),
                        pltpu.VMEM((2, s_tot, hd), F32),
                        pltpu.VMEM((2, hd, hd), F32)],
        compiler_params=_cparams(("arbitrary", "arbitrary")),
        name="retention",
    )(qkvg, qkvg, qkvg, qkvg, cos, sin, tabs)


def _merge_kernel(x_ref, yf_ref, yb_ref, hc_ref, ro_ref, gates_ref, mod_ref, g2_ref,
                  wglu_ref, bglu_ref, wpw_ref, bpw_ref, wo_ref, wout_ref, wr_ref, br_ref,
                  x1_ref, h2_ref, lg_ref):
    d = D_MODEL
    mod = mod_ref[0, 0]

    for half in range(ROW_TILE // HALF_TILE):
        rows = pl.ds(half * HALF_TILE, HALF_TILE)
        ys = jax.nn.gelu(yf_ref[0, 0, rows, :] + yb_ref[0, 0, rows, :]).astype(BF16)
        z = jnp.dot(ys, wglu_ref[...], preferred_element_type=F32) + bglu_ref[...]
        y_ssm = z[:, 0:d] * jax.nn.sigmoid(z[:, d:2 * d])
        y_conv = jnp.dot(hc_ref[0, rows, :].astype(BF16), wpw_ref[...],
                         preferred_element_type=F32) + bpw_ref[...]
        y_ret = jnp.dot(ro_ref[0, rows, :].astype(BF16), wo_ref[...], preferred_element_type=F32)
        m = (gates_ref[0, rows, 0:d].astype(F32) * y_ssm
             + gates_ref[0, rows, d:2 * d].astype(F32) * y_conv
             + gates_ref[0, rows, 2 * d:3 * d].astype(F32) * y_ret)
        y = jnp.dot(m.astype(BF16), wout_ref[...], preferred_element_type=F32)
        x1 = x_ref[0, rows, :] + mod[2:3] * y
        x1_ref[0, rows, :] = x1
        h2 = (_rms(x1) * g2_ref[...]) * (1.0 + mod[4:5]) + mod[3:4]
        h2_ref[0, rows, :] = h2
        h_hi = h2.astype(BF16)
        h_lo = (h2 - h_hi.astype(F32)).astype(BF16)
        lg_ref[0, rows, :] = (jnp.dot(h_hi, wr_ref[0], preferred_element_type=F32)
                              + jnp.dot(h_lo, wr_ref[0], preferred_element_type=F32)
                              + jnp.dot(h_hi, wr_ref[1], preferred_element_type=F32)) + br_ref[...]


def _router_kernel(lg_ref, tril_ref, ti_ref, tg_ref, cnt_out_ref, cnt_ref):
    @pl.when(jnp.logical_and(pl.program_id(0) % MOE_GROUP_BATCH == 0, pl.program_id(1) == 0))
    def _():
        cnt_ref[...] = jnp.zeros_like(cnt_ref)

    seen = cnt_ref[0:1, :]
    for part in range(ROUTE_ROWS // ROW_TILE):
        rows = pl.ds(part * ROW_TILE, ROW_TILE)
        logits = lg_ref[0, rows, :]
        lane = lax.broadcasted_iota(jnp.int32, logits.shape, 1)
        lane_f = lane.astype(F32)
        neg = jnp.float32(-jnp.inf)
        l = jnp.where(lane < N_EXPERTS, logits, neg)
        ti = jnp.zeros(logits.shape, F32)
        vals = []
        picks = []
        for k in range(TOP_K):
            top = jnp.max(l, axis=-1, keepdims=True)
            idx = jnp.min(jnp.where(l == top, lane_f, float(LOGIT_LANES)), axis=-1, keepdims=True)
            ti = jnp.where(lane == k, idx, ti)
            vals.append(top)
            picked = lane_f == idx
            picks.append(picked.astype(F32))
            l = jnp.where(picked, neg, l)
        es = [jnp.exp(v - vals[0]) for v in vals]
        tot = es[0]
        for e in es[1:]:
            tot = tot + e
        tg = jnp.zeros(logits.shape, F32)
        for k in range(TOP_K):
            tg = jnp.where(lane == k, es[k] / tot, tg)
        chosen = picks[0] + picks[1] + picks[2] + picks[3]
        before = seen + jnp.dot(tril_ref[...], chosen.astype(BF16), preferred_element_type=F32)
        for k in range(TOP_K):
            rank = jnp.sum(picks[k] * before, axis=-1, keepdims=True)
            ti = jnp.where(lane == TOP_K + k, rank, ti)
        seen = seen + jnp.sum(chosen, axis=0, keepdims=True)
        ti_ref[0, rows, :] = ti.astype(jnp.int32)
        tg_ref[0, rows, :] = tg
    cnt_ref[...] = jnp.broadcast_to(seen, cnt_ref.shape)
    cnt_out_ref[0] = jnp.broadcast_to(seen, cnt_out_ref.shape[1:])


def _router(logits):
    b, s_tot, _ = logits.shape
    tril = jnp.asarray(np.tril(np.ones((ROW_TILE, ROW_TILE), np.float32), -1), BF16)
    row = pl.BlockSpec((1, ROUTE_ROWS, LOGIT_LANES), lambda i, j: (i, j, 0))
    return pl.pallas_call(
        _router_kernel,
        grid=(b, s_tot // ROUTE_ROWS),
        in_specs=[row, pl.BlockSpec((ROW_TILE, ROW_TILE), lambda i, j: (0, 0))],
        out_specs=[row, row,
                   pl.BlockSpec((1, 8, LOGIT_LANES), lambda i, j: (i // MOE_GROUP_BATCH, 0, 0))],
        out_shape=[jax.ShapeDtypeStruct((b, s_tot, LOGIT_LANES), jnp.int32),
                   jax.ShapeDtypeStruct((b, s_tot, LOGIT_LANES), F32),
                   jax.ShapeDtypeStruct((b // MOE_GROUP_BATCH, 8, LOGIT_LANES), F32)],
        scratch_shapes=[pltpu.VMEM((8, LOGIT_LANES), F32)],
        compiler_params=_cparams(("arbitrary", "arbitrary")),
        name="router",
    )(logits, tril)


def _merge(x, y_s5, hc, ro, gates, modt, g2, wglu, bglu, wpw, bpw, wo, wout, wr, br, ctx_len):
    b, s_tot, d = x.shape
    nt = s_tot // ROW_TILE
    nctx = ctx_len // ROW_TILE
    seg = lambda j: jnp.where(j >= nctx, 1, 0)
    row = lambda w: pl.BlockSpec((1, ROW_TILE, w), lambda i, j: (i, j, 0))
    cst = lambda shape: pl.BlockSpec(shape, lambda i, j: (0,) * len(shape))
    y2 = y_s5
    return pl.pallas_call(
        _merge_kernel,
        grid=(b, nt),
        in_specs=[row(d),
                  pl.BlockSpec((1, 1, ROW_TILE, SSM_WIDTH), lambda i, j: (0, i, j, 0)),
                  pl.BlockSpec((1, 1, ROW_TILE, SSM_WIDTH), lambda i, j: (1, i, j, 0)),
                  row(CONV_WIDTH), row(RET_WIDTH), row(3 * d),
                  pl.BlockSpec((1, 1, 8, d), lambda i, j: (i, seg(j), 0, 0)),
                  cst((1, d)),
                  cst((SSM_WIDTH, 2 * d)), cst((1, 2 * d)),
                  cst((CONV_WIDTH, d)), cst((1, d)),
                  cst((RET_WIDTH, d)), cst((d, d)),
                  cst((2, d, LOGIT_LANES)), cst((1, LOGIT_LANES))],
        out_specs=[row(d), row(d), row(LOGIT_LANES)],
        out_shape=[jax.ShapeDtypeStruct((b, s_tot, d), F32),
                   jax.ShapeDtypeStruct((b, s_tot, d), F32),
                   jax.ShapeDtypeStruct((b, s_tot, LOGIT_LANES), F32)],
        compiler_params=_cparams(("arbitrary", "arbitrary")),
        name="merge_logits",
    )(x, y2, y2, hc, ro, gates, modt, g2.reshape(1, d), wglu, bglu.reshape(1, 2 * d),
      wpw, bpw.reshape(1, d), wo, wout, wr, br)


PLAN_TILE, PLAN_EXPERT, PLAN_LO, PLAN_HI, PLAN_NEXT, PLAN_SLOT = range(6)
PLAN_LANES = 256


def _plan_kernel(cnt_ref, plan_ref, *, n_tiles):
    ne = N_EXPERTS
    rows_per = float(MOE_ROWS)
    e = lax.broadcasted_iota(jnp.int32, (ne, PLAN_LANES), 0)
    e_f = e.astype(F32)
    v = lax.broadcasted_iota(jnp.int32, (ne, PLAN_LANES), 1).astype(F32)

    def cumsum(t):
        for s in (1, 2, 4, 8, 16):
            t = t + jnp.where(e >= s, pltpu.roll(t, s, 0), 0.0)
        return t

    def lookup(table, onehot):
        return jnp.sum(onehot * table, axis=0, keepdims=True)

    c = jnp.broadcast_to(cnt_ref[...], (ne, PLAN_LANES))
    has = c > 0.0
    ends = cumsum(c)
    starts = ends - c
    first_tile = jnp.floor(starts / rows_per)
    tiles_per = jnp.where(has, jnp.floor((ends - 1.0) / rows_per) - first_tile + 1.0, 0.0)
    vend = cumsum(tiles_per)
    vstart = vend - tiles_per
    total = jnp.max(vend, axis=0, keepdims=True)
    slot = cumsum(has.astype(F32)) - 1.0
    slot = slot - 2.0 * jnp.floor(slot / 2.0)
    later = jnp.where(has, e_f, float(ne))
    nxt = jnp.where(e < ne - 1, pltpu.roll(later, ne - 1, 0), float(ne))
    for s in (1, 2, 4, 8, 16):
        nxt = jnp.minimum(nxt, jnp.where(e < ne - s, pltpu.roll(nxt, ne - s, 0), float(ne)))
    nxt = jnp.where(nxt >= float(ne), -1.0, nxt)
    last_with_rows = jnp.max(jnp.where(has, e_f, 0.0), axis=0, keepdims=True)

    valid = v[0:1] < total
    ve = jnp.sum((v >= vend).astype(F32), axis=0, keepdims=True)
    ve = jnp.where(valid, jnp.minimum(ve, float(ne - 1)), last_with_rows)
    onehot = (e_f == ve).astype(F32)
    vt = jnp.where(valid, lookup(first_tile, onehot) + v[0:1] - lookup(vstart, onehot),
                   float(n_tiles - 1))
    lo = jnp.where(valid, jnp.clip(lookup(starts, onehot) - vt * rows_per, 0.0, rows_per), 0.0)
    hi = jnp.where(valid, jnp.clip(lookup(ends, onehot) - vt * rows_per, 0.0, rows_per), 0.0)
    fields = {PLAN_TILE: vt, PLAN_EXPERT: ve, PLAN_LO: lo, PLAN_HI: hi,
              PLAN_NEXT: lookup(nxt, onehot), PLAN_SLOT: lookup(slot, onehot)}
    row = lax.broadcasted_iota(jnp.int32, (8, PLAN_LANES), 0)
    plan = jnp.zeros((8, PLAN_LANES), F32)
    for r, val in fields.items():
        plan = jnp.where(row == r, val, plan)
    plan_ref[...] = plan.astype(jnp.int32)


def _plan(counts_col, n_tiles):
    return pl.pallas_call(
        functools.partial(_plan_kernel, n_tiles=n_tiles),
        out_shape=jax.ShapeDtypeStruct((8, PLAN_LANES), jnp.int32),
        name="moe_plan",
    )(counts_col)


def _expert_kernel(plan_ref, x_ref, wgu_hbm, bgu_ref, wd_hbm, bd_ref, y_ref,
                   wgu_f32, wd_f32, wgu_bf, wd_bf, sem, *, layer):
    v = pl.program_id(0)
    prev = jnp.maximum(v - 1, 0)
    lo = plan_ref[PLAN_LO, v]
    hi = plan_ref[PLAN_HI, v]
    expert = plan_ref[PLAN_EXPERT, v]
    next_expert = plan_ref[PLAN_NEXT, v]
    active = hi > lo
    fresh_weights = jnp.logical_or(v == 0, expert != plan_ref[PLAN_EXPERT, prev])
    first_visit = jnp.logical_or(v == 0, plan_ref[PLAN_TILE, v] != plan_ref[PLAN_TILE, prev])

    def weight_copies(expert, slot):
        return (pltpu.make_async_copy(wgu_hbm.at[layer, expert], wgu_f32.at[slot], sem.at[0, slot]),
                pltpu.make_async_copy(wd_hbm.at[layer, expert], wd_f32.at[slot], sem.at[1, slot]))

    @pl.when(jnp.logical_and(fresh_weights, active))
    def _():
        slot = plan_ref[PLAN_SLOT, v]

        @pl.when(v == 0)
        def _():
            for c in weight_copies(expert, slot):
                c.start()

        for c in weight_copies(expert, slot):
            c.wait()

        @pl.when(next_expert >= 0)
        def _():
            for c in weight_copies(next_expert, 1 - slot):
                c.start()

        wgu_bf[...] = wgu_f32[slot].astype(BF16)
        wd_bf[...] = wd_f32[slot].astype(BF16)

    whole = jnp.logical_and(lo == 0, hi == MOE_ROWS)

    def ffn(rows):
        f = EXPERT_FF
        gu = jnp.dot(x_ref[rows, :].astype(BF16), wgu_bf[...],
                     preferred_element_type=F32) + bgu_ref[0, 0]
        gate = jnp.minimum(gu[:, 0:f], SWIGLU_LIMIT)
        up = jnp.clip(gu[:, f:2 * f], -SWIGLU_LIMIT, SWIGLU_LIMIT)
        act = (up + 1.0) * gate * jax.nn.sigmoid(gate * SWIGLU_ALPHA)
        y = jnp.dot(act.astype(BF16), wd_bf[...], preferred_element_type=F32) + bd_ref[0, 0]
        return y.astype(y_ref.dtype)

    parts = [pl.ds(p * MOE_PART, MOE_PART) for p in range(MOE_ROWS // MOE_PART)]

    @pl.when(whole)
    def _():
        for rows in parts:
            y_ref[rows, :] = ffn(rows)

    @pl.when(jnp.logical_and(first_visit, jnp.logical_not(whole)))
    def _():
        y_ref[...] = jnp.zeros_like(y_ref)

    for p in range(MOE_ROWS // MOE_EDGE):
        rows = pl.ds(p * MOE_EDGE, MOE_EDGE)
        touched = jnp.logical_and(lo < (p + 1) * MOE_EDGE, hi > p * MOE_EDGE)

        @pl.when(jnp.logical_and(touched, jnp.logical_not(whole)))
        def _():
            row = p * MOE_EDGE + lax.broadcasted_iota(jnp.int32, (MOE_EDGE, 1), 0)
            mine = jnp.logical_and(row >= lo, row < hi)
            y_ref[rows, :] = jnp.where(mine, ffn(rows), y_ref[rows, :])


def _experts(xs, plan, w_gu, b_gu, w_down, b_down, layer):
    n_rows, d = xs.shape
    n_visits = n_rows // MOE_ROWS + N_EXPERTS - 1
    assert n_visits <= PLAN_LANES
    f = EXPERT_FF
    wmap = lambda v, plan: (layer, plan[PLAN_EXPERT, v], 0, 0)
    tmap = lambda v, plan: (plan[PLAN_TILE, v], 0)
    grid_spec = pltpu.PrefetchScalarGridSpec(
        num_scalar_prefetch=1,
        grid=(n_visits,),
        in_specs=[pl.BlockSpec((MOE_ROWS, d), tmap),
                  pl.BlockSpec(memory_space=pl.ANY),
                  pl.BlockSpec((1, 1, 1, 2 * f), wmap),
                  pl.BlockSpec(memory_space=pl.ANY),
                  pl.BlockSpec((1, 1, 1, d), wmap)],
        out_specs=pl.BlockSpec((MOE_ROWS, d), tmap),
        scratch_shapes=[pltpu.VMEM((2, d, 2 * f), F32), pltpu.VMEM((2, f, d), F32),
                        pltpu.VMEM((d, 2 * f), BF16), pltpu.VMEM((f, d), BF16),
                        pltpu.SemaphoreType.DMA((2, 2))],
    )
    return pl.pallas_call(
        functools.partial(_expert_kernel, layer=layer),
        grid_spec=grid_spec,
        out_shape=jax.ShapeDtypeStruct((n_rows, d), BF16),
        compiler_params=_cparams(("arbitrary",)),
        name="experts",
    )(plan, xs, w_gu, b_gu.reshape(DEPTH, N_EXPERTS, 1, 2 * f),
      w_down, b_down.reshape(DEPTH, N_EXPERTS, 1, d))


def _combine_kernel(x_ref, y0_ref, y1_ref, y2_ref, y3_ref, gt_ref, mod_ref, fg_ref, o_ref, *, final):
    mod = mod_ref[0, 0]
    gt = gt_ref[0]
    y = gt[:, 0:1] * y0_ref[0].astype(F32)
    for k, y_ref in ((1, y1_ref), (2, y2_ref), (3, y3_ref)):
        y = y + gt[:, k:k + 1] * y_ref[0].astype(F32)
    x2 = x_ref[0] + mod[5:6] * y
    o_ref[0] = _rms(x2) * fg_ref[...] if final else x2


def _combine(x1, yg, gates, modt, final_g, ctx_len, final, group):
    b, s_tot, d = x1.shape
    gb = MOE_GROUP_BATCH
    b0 = group * gb
    nt = s_tot // ROW_TILE
    nctx = ctx_len // ROW_TILE
    off = nctx if final else 0
    seg = lambda j: jnp.where(j + off >= nctx, 1, 0)
    choice = lambda k: pl.BlockSpec((1, ROW_TILE, d), lambda i, j: (k, i * nt + j + off, 0))
    if final:
        out_spec = pl.BlockSpec((1, ROW_TILE, d), lambda i, j: (i, j, 0))
        out_shape = jax.ShapeDtypeStruct((gb, s_tot - off * ROW_TILE, d), F32)
        aliases = {}
    else:
        out_spec = pl.BlockSpec((1, ROW_TILE, d), lambda i, j: (i + b0, j, 0))
        out_shape = jax.ShapeDtypeStruct((b, s_tot, d), F32)
        aliases = {0: 0}
    return pl.pallas_call(
        functools.partial(_combine_kernel, final=final),
        grid=(gb, nt - off),
        in_specs=[pl.BlockSpec((1, ROW_TILE, d), lambda i, j: (i + b0, j + off, 0)),
                  choice(0), choice(1), choice(2), choice(3),
                  pl.BlockSpec((1, ROW_TILE, LOGIT_LANES), lambda i, j: (i + b0, j + off, 0)),
                  pl.BlockSpec((1, 1, 8, d), lambda i, j: (i + b0, seg(j), 0, 0)),
                  pl.BlockSpec((1, d), lambda i, j: (0, 0))],
        out_specs=out_spec,
        out_shape=out_shape,
        input_output_aliases=aliases,
        compiler_params=_cparams(("arbitrary", "arbitrary")),
        name="moe_combine",
    )(x1, yg, yg, yg, yg, gates, modt, final_g.reshape(1, d))


def _moe(x1, h2, top, top_gate, counts, modt, final_g, exp_gu_w, exp_gu_b, exp_down_w, exp_down_b,
         layer, ctx_len, final):
    b, s_tot, d = x1.shape
    n_grp = MOE_GROUP_BATCH * s_tot
    n_assign = n_grp * TOP_K
    top = top.reshape(b * s_tot, LOGIT_LANES)
    h2 = h2.reshape(b * s_tot, d)
    x = x1
    finals = []
    for g in range(b // MOE_GROUP_BATCH):
        top_g = top[g * n_grp:(g + 1) * n_grp]
        cnt = counts[g, 0, :N_EXPERTS]
        plan = _plan(cnt.reshape(N_EXPERTS, 1), n_assign // MOE_ROWS)
        starts = (jnp.cumsum(cnt) - cnt).astype(jnp.int32)
        inv = jnp.take(starts, top_g[:, :TOP_K].reshape(-1)) + top_g[:, TOP_K:2 * TOP_K].reshape(-1)
        _, order = lax.sort((inv, jnp.arange(n_assign, dtype=jnp.int32)), num_keys=1)
        rows = order // TOP_K + g * n_grp
        xs = h2.at[rows].get(mode="promise_in_bounds")
        ys = _experts(xs, plan, exp_gu_w, exp_gu_b, exp_down_w, exp_down_b, layer)
        by_choice = inv.reshape(n_grp, TOP_K).T.reshape(-1)
        yg = ys.at[by_choice].get(mode="promise_in_bounds").reshape(TOP_K, n_grp, d)
        out = _combine(x, yg, top_gate, modt, final_g, ctx_len, final, g)
        if final:
            finals.append(out)
        else:
            x = out
    return jnp.concatenate(finals, axis=0) if final else x


def _mixer_layer(x, modt, i, ctx_len, tables, norm1_g, w_in_bf, s5p, ssm_d, conv_p, merge_p):
    b, s_tot, d = x.shape
    u2, cv, qkvg, gates = _in_proj(x, modt, norm1_g, w_in_bf, ctx_len)
    bmat, cmat, lvec = s5p
    y_s5 = _s5_scan(u2, bmat, cmat, lvec, ssm_d, ctx_len)
    hc = _conv_branch(cv, *conv_p, ctx_len)
    cos, sin, tabs = tables
    ro = _retention(qkvg, cos, sin, tabs, ctx_len)
    x1, h2, logits = _merge(x, y_s5, hc, ro, gates, modt, *merge_p, ctx_len)
    return (x1, h2) + tuple(_router(logits))


def kernel(x, c, ctx, c_ctx, ada_w, ada_b, norm1_g, w_in, ssm_lam_re, ssm_lam_im, ssm_log_dt, ssm_b_re, ssm_b_im, ssm_c_re, ssm_c_im, ssm_d, ssm_glu_w, ssm_glu_b, conv_dw_w, conv_dw_b, conv_ln_g, conv_ln_b, conv_pw_w, conv_pw_b, ret_w_o, w_out, norm2_g, router_w, router_b, exp_gu_w, exp_gu_b, exp_down_w, exp_down_b, final_g):
    b, seq, d = x.shape
    ctx_len = ctx.shape[1]
    depth = ada_w.shape[0]
    xs = jnp.concatenate([ctx, x], axis=1)
    s_in = jnp.concatenate([c, c_ctx[None], jnp.zeros((16 - b - 1, d), F32)], axis=0)
    mod = _ada_mod(s_in, ada_w, ada_b).reshape(depth, 16, N_MOD, d)
    mod = jnp.pad(mod, ((0, 0), (0, 0), (0, 8 - N_MOD), (0, 0)))
    modt = jnp.stack([jnp.broadcast_to(mod[:, b][:, None], (depth, b, 8, d)), mod[:, :b]], axis=2)
    tables = _ret_tables(ctx_len, seq)
    wr = jnp.pad(router_w, ((0, 0), (0, 0), (0, LOGIT_LANES - N_EXPERTS)))
    wr_hi = wr.astype(BF16)
    wr = jnp.stack([wr_hi, (wr - wr_hi.astype(F32)).astype(BF16)], axis=1)
    br = jnp.pad(router_b, ((0, 0), (0, LOGIT_LANES - N_EXPERTS))).reshape(depth, 1, LOGIT_LANES)
    for i in range(depth):
        s5p = _s5_params(ssm_lam_re[i], ssm_lam_im[i], ssm_log_dt[i], ssm_b_re[i], ssm_b_im[i],
                         ssm_c_re[i], ssm_c_im[i])
        conv_p = (conv_dw_w[i], conv_dw_b[i], conv_ln_g[i], conv_ln_b[i])
        merge_p = (norm2_g[i], ssm_glu_w[i].astype(BF16), ssm_glu_b[i], conv_pw_w[i].astype(BF16),
                   conv_pw_b[i], ret_w_o[i].astype(BF16), w_out[i].astype(BF16), wr[i], br[i])
        x1, h2, top, top_gate, counts = _mixer_layer(xs, modt[i], i, ctx_len, tables, norm1_g[i],
                                                     w_in[i].astype(BF16), s5p, ssm_d[i], conv_p, merge_p)
        xs = _moe(x1, h2, top, top_gate, counts, modt[i], final_g, exp_gu_w, exp_gu_b, exp_down_w,
                  exp_down_b, i, ctx_len, final=(i == depth - 1))
    return xs
```

```python
import functools
import math

import numpy as np
import jax
import jax.numpy as jnp
from jax import lax
from jax.experimental import pallas as pl
from jax.experimental.pallas import tpu as pltpu

F32 = jnp.float32
BF16 = jnp.bfloat16

D_MODEL = 1024
DEPTH = 4
GRID_W = 64
N_MOD = 6
SSM_WIDTH = 256
SSM_GROUP = 16
SSM_GROUPS = SSM_WIDTH // SSM_GROUP
SSM_STATE = 64
SSM_LANES = SSM_GROUPS * SSM_STATE
CONV_WIDTH = 256
CONV_K = 31
CONV_PAD = 16
RET_HEADS = 4
RET_HEAD_DIM = 128
RET_WIDTH = RET_HEADS * RET_HEAD_DIM
RET_CHUNK = 128
ROPE_BASE = 10000.0
IN_WIDTH = SSM_WIDTH + 2 * CONV_WIDTH + 4 * RET_WIDTH + 3 * D_MODEL
COL_CV = SSM_WIDTH
COL_QKVG = COL_CV + 2 * CONV_WIDTH
COL_GATES = COL_QKVG + 4 * RET_WIDTH
N_EXPERTS = 32
TOP_K = 4
EXPERT_FF = D_MODEL
SWIGLU_LIMIT = 7.0
SWIGLU_ALPHA = 1.702
EPS = 1e-6

ROW_TILE = 256
HALF_TILE = 128
S5_STEPS = 64
S5_STRIP = 512
MOE_ROWS = 512
MOE_PART = 256
MOE_EDGE = 256
ROUTE_ROWS = 768
MOE_GROUP_BATCH = 4
LOGIT_LANES = 128
VMEM_LIMIT = 56 * 1024 * 1024


def _cparams(sem):
    return pltpu.CompilerParams(dimension_semantics=sem, vmem_limit_bytes=VMEM_LIMIT)


def _rms(x):
    return x * lax.rsqrt(jnp.mean(x * x, axis=-1, keepdims=True) + EPS)


def _ada_kernel(s_ref, w_ref, b_ref, o_ref):
    s = s_ref[...]
    s = s * jax.nn.sigmoid(s)
    o_ref[0] = jnp.dot(s, w_ref[0], preferred_element_type=F32,
                       precision=lax.Precision.HIGHEST) + b_ref[0]


def _ada_mod(s_in, ada_w, ada_b):
    depth, d, n = ada_w.shape
    tn = 1536
    return pl.pallas_call(
        _ada_kernel,
        grid=(depth, n // tn),
        in_specs=[pl.BlockSpec((16, d), lambda i, j: (0, 0)),
                  pl.BlockSpec((1, d, tn), lambda i, j: (i, 0, j)),
                  pl.BlockSpec((1, 1, tn), lambda i, j: (i, 0, j))],
        out_specs=pl.BlockSpec((1, 16, tn), lambda i, j: (i, 0, j)),
        out_shape=jax.ShapeDtypeStruct((depth, 16, n), F32),
        compiler_params=_cparams(("arbitrary", "arbitrary")),
        name="ada_mod",
    )(s_in, ada_w, ada_b.reshape(depth, 1, n))


def _inproj_kernel(x_ref, mod_ref, g_ref, w_ref, u_ref, cv_ref, qkvg_ref, gates_ref):
    x = x_ref[0]
    mod = mod_ref[0, 0]
    h = (_rms(x) * g_ref[...]) * (1.0 + mod[1:2]) + mod[0:1]
    hb = h.astype(BF16)
    for c0 in range(COL_GATES, IN_WIDTH, D_MODEL):
        gates = jnp.dot(hb, w_ref[:, c0:c0 + D_MODEL], preferred_element_type=F32)
        gates_ref[0, :, c0 - COL_GATES:c0 - COL_GATES + D_MODEL] = jax.nn.sigmoid(gates).astype(BF16)
    qkvg_ref[0] = jnp.dot(hb, w_ref[:, COL_QKVG:COL_GATES], preferred_element_type=F32).astype(BF16)
    cv_ref[0] = jnp.dot(hb, w_ref[:, COL_CV:COL_QKVG], preferred_element_type=F32)
    u_ref[0] = jnp.dot(hb, w_ref[:, 0:COL_CV], preferred_element_type=F32)


def _in_proj(x, modt, g1, w_in_bf, ctx_len):
    b, s_tot, d = x.shape
    nt = s_tot // ROW_TILE
    nctx = ctx_len // ROW_TILE
    seg = lambda j: jnp.where(j >= nctx, 1, 0)
    return pl.pallas_call(
        _inproj_kernel,
        grid=(b, nt),
        in_specs=[pl.BlockSpec((1, ROW_TILE, d), lambda i, j: (i, j, 0)),
                  pl.BlockSpec((1, 1, 8, d), lambda i, j: (i, seg(j), 0, 0)),
                  pl.BlockSpec((1, d), lambda i, j: (0, 0)),
                  pl.BlockSpec((d, IN_WIDTH), lambda i, j: (0, 0))],
        out_specs=[pl.BlockSpec((1, ROW_TILE, SSM_WIDTH), lambda i, j: (i, j, 0)),
                   pl.BlockSpec((1, ROW_TILE, 2 * CONV_WIDTH), lambda i, j: (i, j, 0)),
                   pl.BlockSpec((1, ROW_TILE, 4 * RET_WIDTH), lambda i, j: (i, j, 0)),
                   pl.BlockSpec((1, ROW_TILE, 3 * D_MODEL), lambda i, j: (i, j, 0))],
        out_shape=[jax.ShapeDtypeStruct((b, s_tot, SSM_WIDTH), F32),
                   jax.ShapeDtypeStruct((b, s_tot, 2 * CONV_WIDTH), F32),
                   jax.ShapeDtypeStruct((b, s_tot, 4 * RET_WIDTH), BF16),
                   jax.ShapeDtypeStruct((b, s_tot, 3 * D_MODEL), BF16)],
        compiler_params=_cparams(("arbitrary", "arbitrary")),
        name="in_proj",
    )(x, modt, g1.reshape(1, d), w_in_bf)


def _s5_params(lam_re, lam_im, log_dt, b_re, b_im, c_re, c_im):
    lam = lax.complex(jnp.minimum(lam_re.astype(F32), -1e-4), lam_im.astype(F32))
    lam_dt = lam * jnp.exp(log_dt.astype(F32))[..., None]
    lam_bar = jnp.exp(lam_dt)
    b = lax.complex(b_re.astype(F32), b_im.astype(F32))
    b_bar = ((lam_bar - 1.0) / lam)[..., None] * b
    eye = jnp.eye(SSM_GROUPS, dtype=F32)

    def in_mat(t):
        return jnp.einsum('dgpm,gh->dgmhp', t, eye).reshape(2, SSM_WIDTH, SSM_LANES)

    def out_mat(t):
        return jnp.einsum('dgmp,gh->dgphm', t, eye).reshape(2, SSM_LANES, SSM_WIDTH)

    bmat = jnp.concatenate([in_mat(jnp.real(b_bar)), in_mat(jnp.imag(b_bar))], axis=-1)
    cmat = jnp.concatenate([out_mat(c_re.astype(F32)), -out_mat(c_im.astype(F32))], axis=1)
    lvec = jnp.concatenate([jnp.real(lam_bar).reshape(2, 1, SSM_LANES),
                            jnp.imag(lam_bar).reshape(2, 1, SSM_LANES)], axis=-1)
    return bmat.astype(BF16), cmat.astype(BF16), lvec


def _s5_kernel(u_ref, perm_ref, permt_ref, bmat_ref, cmat_ref, lam_ref, dsk_ref, y_ref, st_ref, carry_ref):
    d = pl.program_id(0)
    c = pl.program_id(1)
    nb, steps, width = u_ref.shape

    @pl.when(c == 0)
    def _():
        carry_ref[...] = jnp.zeros_like(carry_ref)

    u = u_ref[...].reshape(nb * steps, width)
    u_tm = jnp.dot(perm_ref[...], u.astype(BF16), preferred_element_type=F32).astype(BF16)
    bu = jnp.dot(u_tm, bmat_ref[0], preferred_element_type=F32)
    st_ref[...] = bu.reshape(steps, nb, 2 * SSM_LANES)

    for k in range(SSM_LANES // S5_STRIP):
        re = pl.ds(k * S5_STRIP, S5_STRIP)
        im = pl.ds(SSM_LANES + k * S5_STRIP, S5_STRIP)
        lr = jnp.broadcast_to(lam_ref[0, :, re], (nb, S5_STRIP))
        li = jnp.broadcast_to(lam_ref[0, :, im], (nb, S5_STRIP))

        def step(t, s):
            sr, si = s
            tt = jnp.where(d == 1, steps - 1 - t, t)
            nr = lr * sr - li * si + st_ref[tt, :, re]
            ni = lr * si + li * sr + st_ref[tt, :, im]
            st_ref[tt, :, re] = nr
            st_ref[tt, :, im] = ni
            return nr, ni

        sr, si = lax.fori_loop(0, steps, step, (carry_ref[:, re], carry_ref[:, im]), unroll=4)
        carry_ref[:, re] = sr
        carry_ref[:, im] = si

    st = st_ref[...].reshape(steps * nb, 2 * SSM_LANES)
    y_tm = jnp.dot(st.astype(BF16), cmat_ref[0], preferred_element_type=F32)
    y = jnp.dot(permt_ref[...], y_tm.astype(BF16), preferred_element_type=F32)
    skip = jnp.where(d == 0, 1.0, 0.0) * dsk_ref[...]
    y_ref[0] = (y + u * skip).reshape(nb, steps, width)


def _s5_scan(u, bmat, cmat, lvec, d_skip, ctx_len):
    nb, s_tot, width = u.shape
    nch = s_tot // S5_STEPS
    nctx = ctx_len // S5_STEPS
    rows = nb * S5_STEPS
    r = np.arange(rows)
    perm = np.zeros((rows, rows), np.float32)
    perm[r, (r % nb) * S5_STEPS + r // nb] = 1.0
    perm_tm = jnp.asarray(perm, BF16)
    perm_bm = jnp.asarray(perm.T, BF16)

    def chunk(d, c):
        back = jnp.where(c < nctx, nctx - 1 - c, nctx + nch - 1 - c)
        return jnp.where(d == 1, back, c)

    return pl.pallas_call(
        _s5_kernel,
        grid=(2, nch),
        in_specs=[pl.BlockSpec((nb, S5_STEPS, width), lambda d, c: (0, chunk(d, c), 0)),
                  pl.BlockSpec((rows, rows), lambda d, c: (0, 0)),
                  pl.BlockSpec((rows, rows), lambda d, c: (0, 0)),
                  pl.BlockSpec((1, width, 2 * SSM_LANES), lambda d, c: (d, 0, 0)),
                  pl.BlockSpec((1, 2 * SSM_LANES, width), lambda d, c: (d, 0, 0)),
                  pl.BlockSpec((1, 1, 2 * SSM_LANES), lambda d, c: (d, 0, 0)),
                  pl.BlockSpec((1, width), lambda d, c: (0, 0))],
        out_specs=pl.BlockSpec((1, nb, S5_STEPS, width), lambda d, c: (d, 0, chunk(d, c), 0)),
        out_shape=jax.ShapeDtypeStruct((2, nb, s_tot, width), F32),
        scratch_shapes=[pltpu.VMEM((S5_STEPS, nb, 2 * SSM_LANES), F32),
                        pltpu.VMEM((nb, 2 * SSM_LANES), F32)],
        compiler_params=_cparams(("arbitrary", "arbitrary")),
        name="s5_scan",
    )(u, perm_tm, perm_bm, bmat, cmat, lvec, d_skip.reshape(1, width))


def _conv_kernel(cv_ref, w_ref, b_ref, lg_ref, lb_ref, o_ref, hp_ref, win_ref, shift_ref, *, ctx_len):
    s_tot = cv_ref.shape[1]
    lat = s_tot - ctx_len
    rc = RET_CHUNK
    zeros = jnp.zeros((CONV_PAD, CONV_WIDTH), F32)
    a = cv_ref[0, :, 0:CONV_WIDTH]
    g = cv_ref[0, :, CONV_WIDTH:2 * CONV_WIDTH]
    h = a * jax.nn.sigmoid(g)
    hp_ref[0:CONV_PAD] = zeros
    hp_ref[CONV_PAD:CONV_PAD + ctx_len] = h[0:ctx_len]
    hp_ref[CONV_PAD + ctx_len:2 * CONV_PAD + ctx_len] = zeros
    hp_ref[2 * CONV_PAD + ctx_len:2 * CONV_PAD + s_tot] = h[ctx_len:s_tot]
    hp_ref[2 * CONV_PAD + s_tot:3 * CONV_PAD + s_tot] = zeros
    del lat

    def chunk(c, carry):
        r0 = pl.multiple_of(c * rc, rc)
        wbase = pl.multiple_of(r0 + jnp.where(r0 >= ctx_len, CONV_PAD, 0), 8)
        win_ref[...] = hp_ref[pl.ds(wbase, rc + 2 * CONV_PAD), :]
        for r in range(1, 8):
            shift_ref[r] = win_ref[r:r + rc + 2 * CONV_PAD - 8, :]
        acc = jnp.zeros((rc, CONV_WIDTH), F32) + b_ref[...]
        for k in range(CONV_K):
            off = k + CONV_PAD - CONV_K // 2
            base = off - off % 8
            if off % 8 == 0:
                tap = win_ref[base:base + rc, :]
            else:
                tap = shift_ref[off % 8, base:base + rc, :]
            acc = acc + w_ref[k:k + 1, :] * tap
        mu = jnp.mean(acc, axis=-1, keepdims=True)
        var = jnp.mean(jnp.square(acc - mu), axis=-1, keepdims=True)
        y = (acc - mu) * lax.rsqrt(var + EPS) * lg_ref[...] + lb_ref[...]
        o_ref[0, pl.ds(r0, rc), :] = y * jax.nn.sigmoid(y)
        return carry

    lax.fori_loop(0, s_tot // rc, chunk, 0)


def _conv_branch(cv, w_dw, b_dw, ln_g, ln_b, ctx_len):
    b, s_tot, _ = cv.shape
    w = jnp.concatenate([w_dw.reshape(CONV_K, CONV_WIDTH), jnp.zeros((1, CONV_WIDTH), F32)], axis=0)
    vec = lambda t: t.reshape(1, CONV_WIDTH)
    cst = lambda shape: pl.BlockSpec(shape, lambda i: (0,) * len(shape))
    return pl.pallas_call(
        functools.partial(_conv_kernel, ctx_len=ctx_len),
        grid=(b,),
        in_specs=[pl.BlockSpec((1, s_tot, 2 * CONV_WIDTH), lambda i: (i, 0, 0)),
                  cst((CONV_K + 1, CONV_WIDTH)), cst((1, CONV_WIDTH)),
                  cst((1, CONV_WIDTH)), cst((1, CONV_WIDTH))],
        out_specs=pl.BlockSpec((1, s_tot, CONV_WIDTH), lambda i: (i, 0, 0)),
        out_shape=jax.ShapeDtypeStruct((b, s_tot, CONV_WIDTH), F32),
        scratch_shapes=[pltpu.VMEM((s_tot + 3 * CONV_PAD, CONV_WIDTH), F32),
                        pltpu.VMEM((RET_CHUNK + 2 * CONV_PAD, CONV_WIDTH), F32),
                        pltpu.VMEM((8, RET_CHUNK + 2 * CONV_PAD - 8, CONV_WIDTH), F32)],
        compiler_params=_cparams(("arbitrary",)),
        name="conv_branch",
    )(cv, w, vec(b_dw), vec(ln_g), vec(ln_b))


N_TAB = 5


def _ret_tables(ctx_len, seq):
    n = RET_HEAD_DIM // 4
    inv_freq = ROPE_BASE ** (-np.arange(n, dtype=np.float64) / n)
    pos = np.arange(seq)
    ang_r = (pos // GRID_W)[:, None] * inv_freq
    ang_c = (pos % GRID_W)[:, None] * inv_freq
    cos = np.concatenate([np.cos(ang_r), np.cos(ang_r), np.cos(ang_c), np.cos(ang_c)], axis=-1)
    sin = np.concatenate([-np.sin(ang_r), np.sin(ang_r), -np.sin(ang_c), np.sin(ang_c)], axis=-1)
    cos = np.concatenate([np.ones((ctx_len, RET_HEAD_DIM)), cos], axis=0)
    sin = np.concatenate([np.zeros((ctx_len, RET_HEAD_DIM)), sin], axis=0)
    log_g_fwd = np.log1p(-np.exp2(-5.0 - np.arange(RET_HEADS, dtype=np.float64)))
    idx = np.arange(RET_CHUNK, dtype=np.float64)
    diff = idx[:, None] - idx[None, :]
    tabs = np.zeros((2, RET_HEADS, N_TAB, RET_CHUNK, RET_CHUNK))
    ones = np.ones((RET_CHUNK, RET_CHUNK))
    for d, log_g in enumerate((log_g_fwd, log_g_fwd[::-1])):
        for hd in range(RET_HEADS):
            lg = log_g[hd]
            if d == 0:
                mask = np.where(diff >= 0, np.exp(lg * np.maximum(diff, 0.0)), 0.0)
                dec = np.exp(lg * (idx + 1.0))
                zeta = np.exp(lg * (RET_CHUNK - 1 - idx))
            else:
                mask = np.where(diff <= 0, np.exp(lg * np.maximum(-diff, 0.0)), 0.0)
                dec = np.exp(lg * (RET_CHUNK - idx))
                zeta = np.exp(lg * idx)
            tabs[d, hd, 0] = mask
            tabs[d, hd, 1] = dec[:, None] * ones
            tabs[d, hd, 2] = zeta[:, None] * ones
            tabs[d, hd, 3] = np.exp(lg * RET_CHUNK) * ones
            tabs[d, hd, 4] = zeta[None, :] * ones
    return (jnp.asarray(cos, F32), jnp.asarray(sin, F32), jnp.asarray(tabs, F32))


def _ret_kernel(q_ref, k_ref, v_ref, g_ref, cos_ref, sin_ref, tab_ref, o_ref,
                qs_ref, qd_ref, kt_ref, kz_ref, out_ref, state_ref, *, ctx_len):
    s_tot = q_ref.shape[1]
    rc = RET_CHUNK
    nch = s_tot // rc
    nctx = ctx_len // rc
    k_scale = RET_HEAD_DIM ** -0.5
    lane = lax.broadcasted_iota(jnp.int32, (rc, RET_HEAD_DIM), 1)
    first_half = (lane % (RET_HEAD_DIM // 2)) < (RET_HEAD_DIM // 4)

    def rope(t, cs, sn):
        quarter = RET_HEAD_DIM // 4
        partner = jnp.where(first_half, pltpu.roll(t, RET_HEAD_DIM - quarter, 1), pltpu.roll(t, quarter, 1))
        return t * cs + partner * sn

    def prepare(c, carry):
        rows = pl.ds(pl.multiple_of(c * rc, rc), rc)
        cs = cos_ref[rows, :]
        sn = sin_ref[rows, :]
        q = rope(q_ref[0, rows, :].astype(F32), cs, sn)
        kt = (rope(k_ref[0, rows, :].astype(F32), cs, sn) * k_scale).T
        qs_ref[c] = q.astype(BF16)
        kt_ref[c] = kt.astype(BF16)
        for d in range(2):
            qd_ref[d, c] = (q * tab_ref[d, 0, 1]).astype(BF16)
            kz_ref[d, c] = (kt * tab_ref[d, 0, 4]).astype(BF16)
        return carry

    lax.fori_loop(0, nch, prepare, 0, unroll=2)
    state_ref[...] = jnp.zeros_like(state_ref)

    def chunk(c, d):
        rows = pl.ds(pl.multiple_of(c * rc, rc), rc)
        vb = v_ref[0, rows, :]
        state = state_ref[d]
        scores = jnp.dot(qs_ref[c], kt_ref[c], preferred_element_type=F32) * tab_ref[d, 0, 0]
        o = (jnp.dot(scores.astype(BF16), vb, preferred_element_type=F32)
             + jnp.dot(qd_ref[d, c], state.astype(BF16), preferred_element_type=F32))
        kv = jnp.dot(kz_ref[d, c], vb, preferred_element_type=F32)
        state_ref[d] = tab_ref[d, 0, 3] * state + kv
        out_ref[d, rows, :] = o

    def step(j, carry):
        chunk(j, 0)
        chunk(jnp.where(j < nctx, nctx - 1 - j, nctx + nch - 1 - j), 1)
        return carry

    lax.fori_loop(0, nch, step, 0, unroll=3)
    g = g_ref[0].astype(F32)
    o_ref[0] = (_rms(out_ref[0] + out_ref[1]) * (g * jax.nn.sigmoid(g))).astype(o_ref.dtype)


def _retention(qkvg, cos, sin, tabs, ctx_len):
    b, s_tot, _ = qkvg.shape
    hd = RET_HEAD_DIM
    nch = s_tot // RET_CHUNK
    part = lambda p: pl.BlockSpec((1, s_tot, hd), lambda i, h: (i, 0, p * RET_HEADS + h))
    return pl.pallas_call(
        functools.partial(_ret_kernel, ctx_len=ctx_len),
        grid=(b, RET_HEADS),
        in_specs=[part(0), part(1), part(2), part(3),
                  pl.BlockSpec((s_tot, hd), lambda i, h: (0, 0)),
                  pl.BlockSpec((s_tot, hd), lambda i, h: (0, 0)),
                  pl.BlockSpec((2, 1, N_TAB, RET_CHUNK, RET_CHUNK), lambda i, h: (0, h, 0, 0, 0))],
        out_specs=pl.BlockSpec((1, s_tot, hd), lambda i, h: (i, 0, h)),
        out_shape=jax.ShapeDtypeStruct((b, s_tot, RET_WIDTH), BF16),
        scratch_shapes=[pltpu.VMEM((nch, RET_CHUNK, hd), BF16),
                        pltpu.VMEM((2, nch, RET_CHUNK, hd), BF16),
                        pltpu.VMEM((nch, hd, RET_CHUNK), BF16),
                        pltpu.VMEM((2, nch, hd, RET_CHUNK), BF16),
                        pltpu.VMEM((2, s_tot, hd), F32),
                        pltpu.VMEM((2, hd, hd), F32)],
        compiler_params=_cparams(("arbitrary", "arbitrary")),
        name="retention",
    )(qkvg, qkvg, qkvg, qkvg, cos, sin, tabs)


def _merge_kernel(x_ref, yf_ref, yb_ref, hc_ref, ro_ref, gates_ref, mod_ref, g2_ref,
                  wglu_ref, bglu_ref, wpw_ref, bpw_ref, wo_ref, wout_ref, wr_ref, br_ref,
                  x1_ref, h2_ref, lg_ref):
    d = D_MODEL
    mod = mod_ref[0, 0]

    for half in range(ROW_TILE // HALF_TILE):
        rows = pl.ds(half * HALF_TILE, HALF_TILE)
        ys = jax.nn.gelu(yf_ref[0, 0, rows, :] + yb_ref[0, 0, rows, :]).astype(BF16)
        z = jnp.dot(ys, wglu_ref[...], preferred_element_type=F32) + bglu_ref[...]
        y_ssm = z[:, 0:d] * jax.nn.sigmoid(z[:, d:2 * d])
        y_conv = jnp.dot(hc_ref[0, rows, :].astype(BF16), wpw_ref[...],
                         preferred_element_type=F32) + bpw_ref[...]
        y_ret = jnp.dot(ro_ref[0, rows, :].astype(BF16), wo_ref[...], preferred_element_type=F32)
        m = (gates_ref[0, rows, 0:d].astype(F32) * y_ssm
             + gates_ref[0, rows, d:2 * d].astype(F32) * y_conv
             + gates_ref[0, rows, 2 * d:3 * d].astype(F32) * y_ret)
        y = jnp.dot(m.astype(BF16), wout_ref[...], preferred_element_type=F32)
        x1 = x_ref[0, rows, :] + mod[2:3] * y
        x1_ref[0, rows, :] = x1
        h2 = (_rms(x1) * g2_ref[...]) * (1.0 + mod[4:5]) + mod[3:4]
        h2_ref[0, rows, :] = h2
        h_hi = h2.astype(BF16)
        h_lo = (h2 - h_hi.astype(F32)).astype(BF16)
        lg_ref[0, rows, :] = (jnp.dot(h_hi, wr_ref[0], preferred_element_type=F32)
                              + jnp.dot(h_lo, wr_ref[0], preferred_element_type=F32)
                              + jnp.dot(h_hi, wr_ref[1], preferred_element_type=F32)) + br_ref[...]


def _router_kernel(lg_ref, tril_ref, ti_ref, tg_ref, cnt_out_ref, cnt_ref):
    @pl.when(jnp.logical_and(pl.program_id(0) % MOE_GROUP_BATCH == 0, pl.program_id(1) == 0))
    def _():
        cnt_ref[...] = jnp.zeros_like(cnt_ref)

    seen = cnt_ref[0:1, :]
    for part in range(ROUTE_ROWS // ROW_TILE):
        rows = pl.ds(part * ROW_TILE, ROW_TILE)
        logits = lg_ref[0, rows, :]
        lane = lax.broadcasted_iota(jnp.int32, logits.shape, 1)
        lane_f = lane.astype(F32)
        neg = jnp.float32(-jnp.inf)
        l = jnp.where(lane < N_EXPERTS, logits, neg)
        ti = jnp.zeros(logits.shape, F32)
        vals = []
        picks = []
        for k in range(TOP_K):
            top = jnp.max(l, axis=-1, keepdims=True)
            idx = jnp.min(jnp.where(l == top, lane_f, float(LOGIT_LANES)), axis=-1, keepdims=True)
            ti = jnp.where(lane == k, idx, ti)
            vals.append(top)
            picked = lane_f == idx
            picks.append(picked.astype(F32))
            l = jnp.where(picked, neg, l)
        es = [jnp.exp(v - vals[0]) for v in vals]
        tot = es[0]
        for e in es[1:]:
            tot = tot + e
        tg = jnp.zeros(logits.shape, F32)
        for k in range(TOP_K):
            tg = jnp.where(lane == k, es[k] / tot, tg)
        chosen = picks[0] + picks[1] + picks[2] + picks[3]
        before = seen + jnp.dot(tril_ref[...], chosen.astype(BF16), preferred_element_type=F32)
        for k in range(TOP_K):
            rank = jnp.sum(picks[k] * before, axis=-1, keepdims=True)
            ti = jnp.where(lane == TOP_K + k, rank, ti)
        seen = seen + jnp.sum(chosen, axis=0, keepdims=True)
        ti_ref[0, rows, :] = ti.astype(jnp.int32)
        tg_ref[0, rows, :] = tg
    cnt_ref[...] = jnp.broadcast_to(seen, cnt_ref.shape)
    cnt_out_ref[0] = jnp.broadcast_to(seen, cnt_out_ref.shape[1:])


def _router(logits):
    b, s_tot, _ = logits.shape
    tril = jnp.asarray(np.tril(np.ones((ROW_TILE, ROW_TILE), np.float32), -1), BF16)
    row = pl.BlockSpec((1, ROUTE_ROWS, LOGIT_LANES), lambda i, j: (i, j, 0))
    return pl.pallas_call(
        _router_kernel,
        grid=(b, s_tot // ROUTE_ROWS),
        in_specs=[row, pl.BlockSpec((ROW_TILE, ROW_TILE), lambda i, j: (0, 0))],
        out_specs=[row, row,
                   pl.BlockSpec((1, 8, LOGIT_LANES), lambda i, j: (i // MOE_GROUP_BATCH, 0, 0))],
        out_shape=[jax.ShapeDtypeStruct((b, s_tot, LOGIT_LANES), jnp.int32),
                   jax.ShapeDtypeStruct((b, s_tot, LOGIT_LANES), F32),
                   jax.ShapeDtypeStruct((b // MOE_GROUP_BATCH, 8, LOGIT_LANES), F32)],
        scratch_shapes=[pltpu.VMEM((8, LOGIT_LANES), F32)],
        compiler_params=_cparams(("arbitrary", "arbitrary")),
        name="router",
    )(logits, tril)


def _positions_kernel(top_ref, starts_ref, pos_ref):
    top = top_ref[...]
    lane = lax.broadcasted_iota(jnp.int32, top.shape, 1)
    starts = starts_ref[0:1, :]
    pos = jnp.zeros(top.shape, F32)
    for k in range(TOP_K):
        mine = lane == top[:, k:k + 1]
        first = jnp.sum(jnp.where(mine, starts, 0.0), axis=-1, keepdims=True)
        pos = jnp.where(lane == k, first + top[:, TOP_K + k:TOP_K + k + 1].astype(F32), pos)
    pos_ref[...] = pos.astype(jnp.int32)


def _positions(top, starts_row):
    n = top.shape[0]
    tile = 1024
    return pl.pallas_call(
        _positions_kernel,
        grid=(n // tile,),
        in_specs=[pl.BlockSpec((tile, LOGIT_LANES), lambda i: (i, 0)),
                  pl.BlockSpec((8, LOGIT_LANES), lambda i: (0, 0))],
        out_specs=pl.BlockSpec((tile, LOGIT_LANES), lambda i: (i, 0)),
        out_shape=jax.ShapeDtypeStruct((n, LOGIT_LANES), jnp.int32),
        compiler_params=_cparams(("arbitrary",)),
        name="moe_positions",
    )(top, starts_row)


def _merge(x, y_s5, hc, ro, gates, modt, g2, wglu, bglu, wpw, bpw, wo, wout, wr, br, ctx_len):
    b, s_tot, d = x.shape
    nt = s_tot // ROW_TILE
    nctx = ctx_len // ROW_TILE
    seg = lambda j: jnp.where(j >= nctx, 1, 0)
    row = lambda w: pl.BlockSpec((1, ROW_TILE, w), lambda i, j: (i, j, 0))
    cst = lambda shape: pl.BlockSpec(shape, lambda i, j: (0,) * len(shape))
    y2 = y_s5
    return pl.pallas_call(
        _merge_kernel,
        grid=(b, nt),
        in_specs=[row(d),
                  pl.BlockSpec((1, 1, ROW_TILE, SSM_WIDTH), lambda i, j: (0, i, j, 0)),
                  pl.BlockSpec((1, 1, ROW_TILE, SSM_WIDTH), lambda i, j: (1, i, j, 0)),
                  row(CONV_WIDTH), row(RET_WIDTH), row(3 * d),
                  pl.BlockSpec((1, 1, 8, d), lambda i, j: (i, seg(j), 0, 0)),
                  cst((1, d)),
                  cst((SSM_WIDTH, 2 * d)), cst((1, 2 * d)),
                  cst((CONV_WIDTH, d)), cst((1, d)),
                  cst((RET_WIDTH, d)), cst((d, d)),
                  cst((2, d, LOGIT_LANES)), cst((1, LOGIT_LANES))],
        out_specs=[row(d), row(d), row(LOGIT_LANES)],
        out_shape=[jax.ShapeDtypeStruct((b, s_tot, d), F32),
                   jax.ShapeDtypeStruct((b, s_tot, d), F32),
                   jax.ShapeDtypeStruct((b, s_tot, LOGIT_LANES), F32)],
        compiler_params=_cparams(("arbitrary", "arbitrary")),
        name="merge_logits",
    )(x, y2, y2, hc, ro, gates, modt, g2.reshape(1, d), wglu, bglu.reshape(1, 2 * d),
      wpw, bpw.reshape(1, d), wo, wout, wr, br)


PLAN_TILE, PLAN_EXPERT, PLAN_LO, PLAN_HI, PLAN_NEXT, PLAN_SLOT = range(6)
PLAN_LANES = 256


def _plan_kernel(cnt_ref, plan_ref, *, n_tiles):
    ne = N_EXPERTS
    rows_per = float(MOE_ROWS)
    e = lax.broadcasted_iota(jnp.int32, (ne, PLAN_LANES), 0)
    e_f = e.astype(F32)
    v = lax.broadcasted_iota(jnp.int32, (ne, PLAN_LANES), 1).astype(F32)

    def cumsum(t):
        for s in (1, 2, 4, 8, 16):
            t = t + jnp.where(e >= s, pltpu.roll(t, s, 0), 0.0)
        return t

    def lookup(table, onehot):
        return jnp.sum(onehot * table, axis=0, keepdims=True)

    c = jnp.broadcast_to(cnt_ref[...], (ne, PLAN_LANES))
    has = c > 0.0
    ends = cumsum(c)
    starts = ends - c
    first_tile = jnp.floor(starts / rows_per)
    tiles_per = jnp.where(has, jnp.floor((ends - 1.0) / rows_per) - first_tile + 1.0, 0.0)
    vend = cumsum(tiles_per)
    vstart = vend - tiles_per
    total = jnp.max(vend, axis=0, keepdims=True)
    slot = cumsum(has.astype(F32)) - 1.0
    slot = slot - 2.0 * jnp.floor(slot / 2.0)
    later = jnp.where(has, e_f, float(ne))
    nxt = jnp.where(e < ne - 1, pltpu.roll(later, ne - 1, 0), float(ne))
    for s in (1, 2, 4, 8, 16):
        nxt = jnp.minimum(nxt, jnp.where(e < ne - s, pltpu.roll(nxt, ne - s, 0), float(ne)))
    nxt = jnp.where(nxt >= float(ne), -1.0, nxt)
    last_with_rows = jnp.max(jnp.where(has, e_f, 0.0), axis=0, keepdims=True)

    valid = v[0:1] < total
    ve = jnp.sum((v >= vend).astype(F32), axis=0, keepdims=True)
    ve = jnp.where(valid, jnp.minimum(ve, float(ne - 1)), last_with_rows)
    onehot = (e_f == ve).astype(F32)
    vt = jnp.where(valid, lookup(first_tile, onehot) + v[0:1] - lookup(vstart, onehot),
                   float(n_tiles - 1))
    lo = jnp.where(valid, jnp.clip(lookup(starts, onehot) - vt * rows_per, 0.0, rows_per), 0.0)
    hi = jnp.where(valid, jnp.clip(lookup(ends, onehot) - vt * rows_per, 0.0, rows_per), 0.0)
    fields = {PLAN_TILE: vt, PLAN_EXPERT: ve, PLAN_LO: lo, PLAN_HI: hi,
              PLAN_NEXT: lookup(nxt, onehot), PLAN_SLOT: lookup(slot, onehot)}
    row = lax.broadcasted_iota(jnp.int32, (8, PLAN_LANES), 0)
    plan = jnp.zeros((8, PLAN_LANES), F32)
    for r, val in fields.items():
        plan = jnp.where(row == r, val, plan)
    plan_ref[...] = plan.astype(jnp.int32)


def _plan(counts_col, n_tiles):
    return pl.pallas_call(
        functools.partial(_plan_kernel, n_tiles=n_tiles),
        out_shape=jax.ShapeDtypeStruct((8, PLAN_LANES), jnp.int32),
        name="moe_plan",
    )(counts_col)


def _expert_kernel(plan_ref, x_ref, wgu_hbm, bgu_ref, wd_hbm, bd_ref, y_ref,
                   wgu_f32, wd_f32, wgu_bf, wd_bf, sem, *, layer):
    v = pl.program_id(0)
    prev = jnp.maximum(v - 1, 0)
    lo = plan_ref[PLAN_LO, v]
    hi = plan_ref[PLAN_HI, v]
    expert = plan_ref[PLAN_EXPERT, v]
    next_expert = plan_ref[PLAN_NEXT, v]
    active = hi > lo
    fresh_weights = jnp.logical_or(v == 0, expert != plan_ref[PLAN_EXPERT, prev])
    first_visit = jnp.logical_or(v == 0, plan_ref[PLAN_TILE, v] != plan_ref[PLAN_TILE, prev])

    def weight_copies(expert, slot):
        return (pltpu.make_async_copy(wgu_hbm.at[layer, expert], wgu_f32.at[slot], sem.at[0, slot]),
                pltpu.make_async_copy(wd_hbm.at[layer, expert], wd_f32.at[slot], sem.at[1, slot]))

    @pl.when(jnp.logical_and(fresh_weights, active))
    def _():
        slot = plan_ref[PLAN_SLOT, v]

        @pl.when(v == 0)
        def _():
            for c in weight_copies(expert, slot):
                c.start()

        for c in weight_copies(expert, slot):
            c.wait()

        @pl.when(next_expert >= 0)
        def _():
            for c in weight_copies(next_expert, 1 - slot):
                c.start()

        wgu_bf[...] = wgu_f32[slot].astype(BF16)
        wd_bf[...] = wd_f32[slot].astype(BF16)

    whole = jnp.logical_and(lo == 0, hi == MOE_ROWS)

    def ffn(rows):
        f = EXPERT_FF
        gu = jnp.dot(x_ref[rows, :].astype(BF16), wgu_bf[...],
                     preferred_element_type=F32) + bgu_ref[0, 0]
        gate = jnp.minimum(gu[:, 0:f], SWIGLU_LIMIT)
        up = jnp.clip(gu[:, f:2 * f], -SWIGLU_LIMIT, SWIGLU_LIMIT)
        act = (up + 1.0) * gate * jax.nn.sigmoid(gate * SWIGLU_ALPHA)
        y = jnp.dot(act.astype(BF16), wd_bf[...], preferred_element_type=F32) + bd_ref[0, 0]
        return y.astype(y_ref.dtype)

    parts = [pl.ds(p * MOE_PART, MOE_PART) for p in range(MOE_ROWS // MOE_PART)]

    @pl.when(whole)
    def _():
        for rows in parts:
            y_ref[rows, :] = ffn(rows)

    @pl.when(jnp.logical_and(first_visit, jnp.logical_not(whole)))
    def _():
        y_ref[...] = jnp.zeros_like(y_ref)

    for p in range(MOE_ROWS // MOE_EDGE):
        rows = pl.ds(p * MOE_EDGE, MOE_EDGE)
        touched = jnp.logical_and(lo < (p + 1) * MOE_EDGE, hi > p * MOE_EDGE)

        @pl.when(jnp.logical_and(touched, jnp.logical_not(whole)))
        def _():
            row = p * MOE_EDGE + lax.broadcasted_iota(jnp.int32, (MOE_EDGE, 1), 0)
            mine = jnp.logical_and(row >= lo, row < hi)
            y_ref[rows, :] = jnp.where(mine, ffn(rows), y_ref[rows, :])


def _experts(xs, plan, w_gu, b_gu, w_down, b_down, layer):
    n_rows, d = xs.shape
    n_visits = n_rows // MOE_ROWS + N_EXPERTS - 1
    assert n_visits <= PLAN_LANES
    f = EXPERT_FF
    wmap = lambda v, plan: (layer, plan[PLAN_EXPERT, v], 0, 0)
    tmap = lambda v, plan: (plan[PLAN_TILE, v], 0)
    grid_spec = pltpu.PrefetchScalarGridSpec(
        num_scalar_prefetch=1,
        grid=(n_visits,),
        in_specs=[pl.BlockSpec((MOE_ROWS, d), tmap),
                  pl.BlockSpec(memory_space=pl.ANY),
                  pl.BlockSpec((1, 1, 1, 2 * f), wmap),
                  pl.BlockSpec(memory_space=pl.ANY),
                  pl.BlockSpec((1, 1, 1, d), wmap)],
        out_specs=pl.BlockSpec((MOE_ROWS, d), tmap),
        scratch_shapes=[pltpu.VMEM((2, d, 2 * f), F32), pltpu.VMEM((2, f, d), F32),
                        pltpu.VMEM((d, 2 * f), BF16), pltpu.VMEM((f, d), BF16),
                        pltpu.SemaphoreType.DMA((2, 2))],
    )
    return pl.pallas_call(
        functools.partial(_expert_kernel, layer=layer),
        grid_spec=grid_spec,
        out_shape=jax.ShapeDtypeStruct((n_rows, d), BF16),
        compiler_params=_cparams(("arbitrary",)),
        name="experts",
    )(plan, xs, w_gu, b_gu.reshape(DEPTH, N_EXPERTS, 1, 2 * f),
      w_down, b_down.reshape(DEPTH, N_EXPERTS, 1, d))


def _combine_kernel(x_ref, y0_ref, y1_ref, y2_ref, y3_ref, gt_ref, mod_ref, fg_ref, o_ref, *, final):
    mod = mod_ref[0, 0]
    gt = gt_ref[0]
    y = gt[:, 0:1] * y0_ref[0].astype(F32)
    for k, y_ref in ((1, y1_ref), (2, y2_ref), (3, y3_ref)):
        y = y + gt[:, k:k + 1] * y_ref[0].astype(F32)
    x2 = x_ref[0] + mod[5:6] * y
    o_ref[0] = _rms(x2) * fg_ref[...] if final else x2


def _combine(x1, yg, gates, modt, final_g, ctx_len, final, group):
    b, s_tot, d = x1.shape
    gb = MOE_GROUP_BATCH
    b0 = group * gb
    nt = s_tot // ROW_TILE
    nctx = ctx_len // ROW_TILE
    off = nctx if final else 0
    seg = lambda j: jnp.where(j + off >= nctx, 1, 0)
    choice = lambda k: pl.BlockSpec((1, ROW_TILE, d), lambda i, j: (k, i * nt + j + off, 0))
    if final:
        out_spec = pl.BlockSpec((1, ROW_TILE, d), lambda i, j: (i, j, 0))
        out_shape = jax.ShapeDtypeStruct((gb, s_tot - off * ROW_TILE, d), F32)
        aliases = {}
    else:
        out_spec = pl.BlockSpec((1, ROW_TILE, d), lambda i, j: (i + b0, j, 0))
        out_shape = jax.ShapeDtypeStruct((b, s_tot, d), F32)
        aliases = {0: 0}
    return pl.pallas_call(
        functools.partial(_combine_kernel, final=final),
        grid=(gb, nt - off),
        in_specs=[pl.BlockSpec((1, ROW_TILE, d), lambda i, j: (i + b0, j + off, 0)),
                  choice(0), choice(1), choice(2), choice(3),
                  pl.BlockSpec((1, ROW_TILE, LOGIT_LANES), lambda i, j: (i + b0, j + off, 0)),
                  pl.BlockSpec((1, 1, 8, d), lambda i, j: (i + b0, seg(j), 0, 0)),
                  pl.BlockSpec((1, d), lambda i, j: (0, 0))],
        out_specs=out_spec,
        out_shape=out_shape,
        input_output_aliases=aliases,
        compiler_params=_cparams(("arbitrary", "arbitrary")),
        name="moe_combine",
    )(x1, yg, yg, yg, yg, gates, modt, final_g.reshape(1, d))


def _moe(x1, h2, top, top_gate, counts, modt, final_g, exp_gu_w, exp_gu_b, exp_down_w, exp_down_b,
         layer, ctx_len, final):
    b, s_tot, d = x1.shape
    n_grp = MOE_GROUP_BATCH * s_tot
    n_assign = n_grp * TOP_K
    top = top.reshape(b * s_tot, LOGIT_LANES)
    h2 = h2.reshape(b * s_tot, d)
    x = x1
    finals = []
    for g in range(b // MOE_GROUP_BATCH):
        top_g = top[g * n_grp:(g + 1) * n_grp]
        cnt = counts[g]
        plan = _plan(cnt[0, :N_EXPERTS].reshape(N_EXPERTS, 1), n_assign // MOE_ROWS)
        pos = _positions(top_g, jnp.cumsum(cnt, axis=1) - cnt)[:, :TOP_K]
        inv = pos.reshape(-1)
        _, order = lax.sort((inv, jnp.arange(n_assign, dtype=jnp.int32)), num_keys=1)
        rows = order // TOP_K + g * n_grp
        xs = h2.at[rows].get(mode="promise_in_bounds")
        ys = _experts(xs, plan, exp_gu_w, exp_gu_b, exp_down_w, exp_down_b, layer)
        by_choice = inv.reshape(n_grp, TOP_K).T.reshape(-1)
        yg = ys.at[by_choice].get(mode="promise_in_bounds").reshape(TOP_K, n_grp, d)
        out = _combine(x, yg, top_gate, modt, final_g, ctx_len, final, g)
        if final:
            finals.append(out)
        else:
            x = out
    return jnp.concatenate(finals, axis=0) if final else x


def _mixer_layer(x, modt, i, ctx_len, tables, norm1_g, w_in_bf, s5p, ssm_d, conv_p, merge_p):
    b, s_tot, d = x.shape
    u2, cv, qkvg, gates = _in_proj(x, modt, norm1_g, w_in_bf, ctx_len)
    bmat, cmat, lvec = s5p
    y_s5 = _s5_scan(u2, bmat, cmat, lvec, ssm_d, ctx_len)
    hc = _conv_branch(cv, *conv_p, ctx_len)
    cos, sin, tabs = tables
    ro = _retention(qkvg, cos, sin, tabs, ctx_len)
    x1, h2, logits = _merge(x, y_s5, hc, ro, gates, modt, *merge_p, ctx_len)
    return (x1, h2) + tuple(_router(logits))


def kernel(x, c, ctx, c_ctx, ada_w, ada_b, norm1_g, w_in, ssm_lam_re, ssm_lam_im, ssm_log_dt, ssm_b_re, ssm_b_im, ssm_c_re, ssm_c_im, ssm_d, ssm_glu_w, ssm_glu_b, conv_dw_w, conv_dw_b, conv_ln_g, conv_ln_b, conv_pw_w, conv_pw_b, ret_w_o, w_out, norm2_g, router_w, router_b, exp_gu_w, exp_gu_b, exp_down_w, exp_down_b, final_g):
    b, seq, d = x.shape
    ctx_len = ctx.shape[1]
    depth = ada_w.shape[0]
    xs = jnp.concatenate([ctx, x], axis=1)
    s_in = jnp.concatenate([c, c_ctx[None], jnp.zeros((16 - b - 1, d), F32)], axis=0)
    mod = _ada_mod(s_in, ada_w, ada_b).reshape(depth, 16, N_MOD, d)
    mod = jnp.pad(mod, ((0, 0), (0, 0), (0, 8 - N_MOD), (0, 0)))
    modt = jnp.stack([jnp.broadcast_to(mod[:, b][:, None], (depth, b, 8, d)), mod[:, :b]], axis=2)
    tables = _ret_tables(ctx_len, seq)
    wr = jnp.pad(router_w, ((0, 0), (0, 0), (0, LOGIT_LANES - N_EXPERTS)))
    wr_hi = wr.astype(BF16)
    wr = jnp.stack([wr_hi, (wr - wr_hi.astype(F32)).astype(BF16)], axis=1)
    br = jnp.pad(router_b, ((0, 0), (0, LOGIT_LANES - N_EXPERTS))).reshape(depth, 1, LOGIT_LANES)
    for i in range(depth):
        s5p = _s5_params(ssm_lam_re[i], ssm_lam_im[i], ssm_log_dt[i], ssm_b_re[i], ssm_b_im[i],
                         ssm_c_re[i], ssm_c_im[i])
        conv_p = (conv_dw_w[i], conv_dw_b[i], conv_ln_g[i], conv_ln_b[i])
        merge_p = (norm2_g[i], ssm_glu_w[i].astype(BF16), ssm_glu_b[i], conv_pw_w[i].astype(BF16),
                   conv_pw_b[i], ret_w_o[i].astype(BF16), w_out[i].astype(BF16), wr[i], br[i])
        x1, h2, top, top_gate, counts = _mixer_layer(xs, modt[i], i, ctx_len, tables, norm1_g[i],
                                                     w_in[i].astype(BF16), s5p, ssm_d[i], conv_p, merge_p)
        xs = _moe(x1, h2, top, top_gate, counts, modt[i], final_g, exp_gu_w, exp_gu_b, exp_down_w,
                  exp_down_b, i, ctx_len, final=(i == depth - 1))
    return xs
```

```python
import functools
import math

import numpy as np
import jax
import jax.numpy as jnp
from jax import lax
from jax.experimental import pallas as pl
from jax.experimental.pallas import tpu as pltpu

F32 = jnp.float32
BF16 = jnp.bfloat16

D_MODEL = 1024
DEPTH = 4
GRID_W = 64
N_MOD = 6
SSM_WIDTH = 256
SSM_GROUP = 16
SSM_GROUPS = SSM_WIDTH // SSM_GROUP
SSM_STATE = 64
SSM_LANES = SSM_GROUPS * SSM_STATE
CONV_WIDTH = 256
CONV_K = 31
CONV_PAD = 16
RET_HEADS = 4
RET_HEAD_DIM = 128
RET_WIDTH = RET_HEADS * RET_HEAD_DIM
RET_CHUNK = 128
ROPE_BASE = 10000.0
IN_WIDTH = SSM_WIDTH + 2 * CONV_WIDTH + 4 * RET_WIDTH + 3 * D_MODEL
COL_CV = SSM_WIDTH
COL_QKVG = COL_CV + 2 * CONV_WIDTH
COL_GATES = COL_QKVG + 4 * RET_WIDTH
N_EXPERTS = 32
TOP_K = 4
EXPERT_FF = D_MODEL
SWIGLU_LIMIT = 7.0
SWIGLU_ALPHA = 1.702
EPS = 1e-6

ROW_TILE = 256
HALF_TILE = 128
S5_STEPS = 64
S5_STRIP = 512
MOE_ROWS = 512
MOE_PART = 256
MOE_EDGE = 256
ROUTE_ROWS = 768
MOE_GROUP_BATCH = 4
LOGIT_LANES = 128
VMEM_LIMIT = 56 * 1024 * 1024


def _cparams(sem):
    return pltpu.CompilerParams(dimension_semantics=sem, vmem_limit_bytes=VMEM_LIMIT)


def _rms(x):
    return x * lax.rsqrt(jnp.mean(x * x, axis=-1, keepdims=True) + EPS)


def _ada_kernel(s_ref, w_ref, b_ref, o_ref):
    s = s_ref[...]
    s = s * jax.nn.sigmoid(s)
    o_ref[0] = jnp.dot(s, w_ref[0], preferred_element_type=F32,
                       precision=lax.Precision.HIGHEST) + b_ref[0]


def _ada_mod(s_in, ada_w, ada_b):
    depth, d, n = ada_w.shape
    tn = 1536
    return pl.pallas_call(
        _ada_kernel,
        grid=(depth, n // tn),
        in_specs=[pl.BlockSpec((16, d), lambda i, j: (0, 0)),
                  pl.BlockSpec((1, d, tn), lambda i, j: (i, 0, j)),
                  pl.BlockSpec((1, 1, tn), lambda i, j: (i, 0, j))],
        out_specs=pl.BlockSpec((1, 16, tn), lambda i, j: (i, 0, j)),
        out_shape=jax.ShapeDtypeStruct((depth, 16, n), F32),
        compiler_params=_cparams(("arbitrary", "arbitrary")),
        name="ada_mod",
    )(s_in, ada_w, ada_b.reshape(depth, 1, n))


def _inproj_kernel(x_ref, mod_ref, g_ref, w_ref, u_ref, cv_ref, qkvg_ref, gates_ref):
    x = x_ref[0]
    mod = mod_ref[0, 0]
    h = (_rms(x) * g_ref[...]) * (1.0 + mod[1:2]) + mod[0:1]
    hb = h.astype(BF16)
    for c0 in range(COL_GATES, IN_WIDTH, D_MODEL):
        gates = jnp.dot(hb, w_ref[:, c0:c0 + D_MODEL], preferred_element_type=F32)
        gates_ref[0, :, c0 - COL_GATES:c0 - COL_GATES + D_MODEL] = jax.nn.sigmoid(gates).astype(BF16)
    qkvg_ref[0] = jnp.dot(hb, w_ref[:, COL_QKVG:COL_GATES], preferred_element_type=F32).astype(BF16)
    cv_ref[0] = jnp.dot(hb, w_ref[:, COL_CV:COL_QKVG], preferred_element_type=F32)
    u_ref[0] = jnp.dot(hb, w_ref[:, 0:COL_CV], preferred_element_type=F32)


def _in_proj(x, modt, g1, w_in_bf, ctx_len):
    b, s_tot, d = x.shape
    nt = s_tot // ROW_TILE
    nctx = ctx_len // ROW_TILE
    seg = lambda j: jnp.where(j >= nctx, 1, 0)
    return pl.pallas_call(
        _inproj_kernel,
        grid=(b, nt),
        in_specs=[pl.BlockSpec((1, ROW_TILE, d), lambda i, j: (i, j, 0)),
                  pl.BlockSpec((1, 1, 8, d), lambda i, j: (i, seg(j), 0, 0)),
                  pl.BlockSpec((1, d), lambda i, j: (0, 0)),
                  pl.BlockSpec((d, IN_WIDTH), lambda i, j: (0, 0))],
        out_specs=[pl.BlockSpec((1, ROW_TILE, SSM_WIDTH), lambda i, j: (i, j, 0)),
                   pl.BlockSpec((1, ROW_TILE, 2 * CONV_WIDTH), lambda i, j: (i, j, 0)),
                   pl.BlockSpec((1, ROW_TILE, 4 * RET_WIDTH), lambda i, j: (i, j, 0)),
                   pl.BlockSpec((1, ROW_TILE, 3 * D_MODEL), lambda i, j: (i, j, 0))],
        out_shape=[jax.ShapeDtypeStruct((b, s_tot, SSM_WIDTH), F32),
                   jax.ShapeDtypeStruct((b, s_tot, 2 * CONV_WIDTH), F32),
                   jax.ShapeDtypeStruct((b, s_tot, 4 * RET_WIDTH), BF16),
                   jax.ShapeDtypeStruct((b, s_tot, 3 * D_MODEL), BF16)],
        compiler_params=_cparams(("arbitrary", "arbitrary")),
        name="in_proj",
    )(x, modt, g1.reshape(1, d), w_in_bf)


def _s5_params(lam_re, lam_im, log_dt, b_re, b_im, c_re, c_im):
    lam = lax.complex(jnp.minimum(lam_re.astype(F32), -1e-4), lam_im.astype(F32))
    lam_dt = lam * jnp.exp(log_dt.astype(F32))[..., None]
    lam_bar = jnp.exp(lam_dt)
    b = lax.complex(b_re.astype(F32), b_im.astype(F32))
    b_bar = ((lam_bar - 1.0) / lam)[..., None] * b
    eye = jnp.eye(SSM_GROUPS, dtype=F32)

    def in_mat(t):
        return jnp.einsum('dgpm,gh->dgmhp', t, eye).reshape(2, SSM_WIDTH, SSM_LANES)

    def out_mat(t):
        return jnp.einsum('dgmp,gh->dgphm', t, eye).reshape(2, SSM_LANES, SSM_WIDTH)

    bmat = jnp.concatenate([in_mat(jnp.real(b_bar)), in_mat(jnp.imag(b_bar))], axis=-1)
    cmat = jnp.concatenate([out_mat(c_re.astype(F32)), -out_mat(c_im.astype(F32))], axis=1)
    lvec = jnp.concatenate([jnp.real(lam_bar).reshape(2, 1, SSM_LANES),
                            jnp.imag(lam_bar).reshape(2, 1, SSM_LANES)], axis=-1)
    return bmat.astype(BF16), cmat.astype(BF16), lvec


def _s5_kernel(u_ref, perm_ref, permt_ref, bmat_ref, cmat_ref, lam_ref, dsk_ref, y_ref, st_ref, carry_ref):
    d = pl.program_id(0)
    c = pl.program_id(1)
    nb, steps, width = u_ref.shape

    @pl.when(c == 0)
    def _():
        carry_ref[...] = jnp.zeros_like(carry_ref)

    u = u_ref[...].reshape(nb * steps, width)
    u_tm = jnp.dot(perm_ref[...], u.astype(BF16), preferred_element_type=F32).astype(BF16)
    bu = jnp.dot(u_tm, bmat_ref[0], preferred_element_type=F32)
    st_ref[...] = bu.reshape(steps, nb, 2 * SSM_LANES)

    for k in range(SSM_LANES // S5_STRIP):
        re = pl.ds(k * S5_STRIP, S5_STRIP)
        im = pl.ds(SSM_LANES + k * S5_STRIP, S5_STRIP)
        lr = jnp.broadcast_to(lam_ref[0, :, re], (nb, S5_STRIP))
        li = jnp.broadcast_to(lam_ref[0, :, im], (nb, S5_STRIP))

        def step(t, s):
            sr, si = s
            tt = jnp.where(d == 1, steps - 1 - t, t)
            nr = lr * sr - li * si + st_ref[tt, :, re]
            ni = lr * si + li * sr + st_ref[tt, :, im]
            st_ref[tt, :, re] = nr
            st_ref[tt, :, im] = ni
            return nr, ni

        sr, si = lax.fori_loop(0, steps, step, (carry_ref[:, re], carry_ref[:, im]), unroll=4)
        carry_ref[:, re] = sr
        carry_ref[:, im] = si

    st = st_ref[...].reshape(steps * nb, 2 * SSM_LANES)
    y_tm = jnp.dot(st.astype(BF16), cmat_ref[0], preferred_element_type=F32)
    y = jnp.dot(permt_ref[...], y_tm.astype(BF16), preferred_element_type=F32)
    skip = jnp.where(d == 0, 1.0, 0.0) * dsk_ref[...]
    y_ref[0] = (y + u * skip).reshape(nb, steps, width)


def _s5_scan(u, bmat, cmat, lvec, d_skip, ctx_len):
    nb, s_tot, width = u.shape
    nch = s_tot // S5_STEPS
    nctx = ctx_len // S5_STEPS
    rows = nb * S5_STEPS
    r = np.arange(rows)
    perm = np.zeros((rows, rows), np.float32)
    perm[r, (r % nb) * S5_STEPS + r // nb] = 1.0
    perm_tm = jnp.asarray(perm, BF16)
    perm_bm = jnp.asarray(perm.T, BF16)

    def chunk(d, c):
        back = jnp.where(c < nctx, nctx - 1 - c, nctx + nch - 1 - c)
        return jnp.where(d == 1, back, c)

    return pl.pallas_call(
        _s5_kernel,
        grid=(2, nch),
        in_specs=[pl.BlockSpec((nb, S5_STEPS, width), lambda d, c: (0, chunk(d, c), 0)),
                  pl.BlockSpec((rows, rows), lambda d, c: (0, 0)),
                  pl.BlockSpec((rows, rows), lambda d, c: (0, 0)),
                  pl.BlockSpec((1, width, 2 * SSM_LANES), lambda d, c: (d, 0, 0)),
                  pl.BlockSpec((1, 2 * SSM_LANES, width), lambda d, c: (d, 0, 0)),
                  pl.BlockSpec((1, 1, 2 * SSM_LANES), lambda d, c: (d, 0, 0)),
                  pl.BlockSpec((1, width), lambda d, c: (0, 0))],
        out_specs=pl.BlockSpec((1, nb, S5_STEPS, width), lambda d, c: (d, 0, chunk(d, c), 0)),
        out_shape=jax.ShapeDtypeStruct((2, nb, s_tot, width), F32),
        scratch_shapes=[pltpu.VMEM((S5_STEPS, nb, 2 * SSM_LANES), F32),
                        pltpu.VMEM((nb, 2 * SSM_LANES), F32)],
        compiler_params=_cparams(("arbitrary", "arbitrary")),
        name="s5_scan",
    )(u, perm_tm, perm_bm, bmat, cmat, lvec, d_skip.reshape(1, width))


def _conv_kernel(cv_ref, w_ref, b_ref, lg_ref, lb_ref, o_ref, hp_ref, win_ref, shift_ref, *, ctx_len):
    s_tot = cv_ref.shape[1]
    lat = s_tot - ctx_len
    rc = RET_CHUNK
    zeros = jnp.zeros((CONV_PAD, CONV_WIDTH), F32)
    a = cv_ref[0, :, 0:CONV_WIDTH]
    g = cv_ref[0, :, CONV_WIDTH:2 * CONV_WIDTH]
    h = a * jax.nn.sigmoid(g)
    hp_ref[0:CONV_PAD] = zeros
    hp_ref[CONV_PAD:CONV_PAD + ctx_len] = h[0:ctx_len]
    hp_ref[CONV_PAD + ctx_len:2 * CONV_PAD + ctx_len] = zeros
    hp_ref[2 * CONV_PAD + ctx_len:2 * CONV_PAD + s_tot] = h[ctx_len:s_tot]
    hp_ref[2 * CONV_PAD + s_tot:3 * CONV_PAD + s_tot] = zeros
    del lat

    def chunk(c, carry):
        r0 = pl.multiple_of(c * rc, rc)
        wbase = pl.multiple_of(r0 + jnp.where(r0 >= ctx_len, CONV_PAD, 0), 8)
        win_ref[...] = hp_ref[pl.ds(wbase, rc + 2 * CONV_PAD), :]
        for r in range(1, 8):
            shift_ref[r] = win_ref[r:r + rc + 2 * CONV_PAD - 8, :]
        acc = jnp.zeros((rc, CONV_WIDTH), F32) + b_ref[...]
        for k in range(CONV_K):
            off = k + CONV_PAD - CONV_K // 2
            base = off - off % 8
            if off % 8 == 0:
                tap = win_ref[base:base + rc, :]
            else:
                tap = shift_ref[off % 8, base:base + rc, :]
            acc = acc + w_ref[k:k + 1, :] * tap
        mu = jnp.mean(acc, axis=-1, keepdims=True)
        var = jnp.mean(jnp.square(acc - mu), axis=-1, keepdims=True)
        y = (acc - mu) * lax.rsqrt(var + EPS) * lg_ref[...] + lb_ref[...]
        o_ref[0, pl.ds(r0, rc), :] = y * jax.nn.sigmoid(y)
        return carry

    lax.fori_loop(0, s_tot // rc, chunk, 0)


def _conv_branch(cv, w_dw, b_dw, ln_g, ln_b, ctx_len):
    b, s_tot, _ = cv.shape
    w = jnp.concatenate([w_dw.reshape(CONV_K, CONV_WIDTH), jnp.zeros((1, CONV_WIDTH), F32)], axis=0)
    vec = lambda t: t.reshape(1, CONV_WIDTH)
    cst = lambda shape: pl.BlockSpec(shape, lambda i: (0,) * len(shape))
    return pl.pallas_call(
        functools.partial(_conv_kernel, ctx_len=ctx_len),
        grid=(b,),
        in_specs=[pl.BlockSpec((1, s_tot, 2 * CONV_WIDTH), lambda i: (i, 0, 0)),
                  cst((CONV_K + 1, CONV_WIDTH)), cst((1, CONV_WIDTH)),
                  cst((1, CONV_WIDTH)), cst((1, CONV_WIDTH))],
        out_specs=pl.BlockSpec((1, s_tot, CONV_WIDTH), lambda i: (i, 0, 0)),
        out_shape=jax.ShapeDtypeStruct((b, s_tot, CONV_WIDTH), F32),
        scratch_shapes=[pltpu.VMEM((s_tot + 3 * CONV_PAD, CONV_WIDTH), F32),
                        pltpu.VMEM((RET_CHUNK + 2 * CONV_PAD, CONV_WIDTH), F32),
                        pltpu.VMEM((8, RET_CHUNK + 2 * CONV_PAD - 8, CONV_WIDTH), F32)],
        compiler_params=_cparams(("arbitrary",)),
        name="conv_branch",
    )(cv, w, vec(b_dw), vec(ln_g), vec(ln_b))


N_TAB = 5


def _ret_tables(ctx_len, seq):
    n = RET_HEAD_DIM // 4
    inv_freq = ROPE_BASE ** (-np.arange(n, dtype=np.float64) / n)
    pos = np.arange(seq)
    ang_r = (pos // GRID_W)[:, None] * inv_freq
    ang_c = (pos % GRID_W)[:, None] * inv_freq
    cos = np.concatenate([np.cos(ang_r), np.cos(ang_r), np.cos(ang_c), np.cos(ang_c)], axis=-1)
    sin = np.concatenate([-np.sin(ang_r), np.sin(ang_r), -np.sin(ang_c), np.sin(ang_c)], axis=-1)
    cos = np.concatenate([np.ones((ctx_len, RET_HEAD_DIM)), cos], axis=0)
    sin = np.concatenate([np.zeros((ctx_len, RET_HEAD_DIM)), sin], axis=0)
    log_g_fwd = np.log1p(-np.exp2(-5.0 - np.arange(RET_HEADS, dtype=np.float64)))
    idx = np.arange(RET_CHUNK, dtype=np.float64)
    diff = idx[:, None] - idx[None, :]
    tabs = np.zeros((2, RET_HEADS, N_TAB, RET_CHUNK, RET_CHUNK))
    ones = np.ones((RET_CHUNK, RET_CHUNK))
    for d, log_g in enumerate((log_g_fwd, log_g_fwd[::-1])):
        for hd in range(RET_HEADS):
            lg = log_g[hd]
            if d == 0:
                mask = np.where(diff >= 0, np.exp(lg * np.maximum(diff, 0.0)), 0.0)
                dec = np.exp(lg * (idx + 1.0))
                zeta = np.exp(lg * (RET_CHUNK - 1 - idx))
            else:
                mask = np.where(diff <= 0, np.exp(lg * np.maximum(-diff, 0.0)), 0.0)
                dec = np.exp(lg * (RET_CHUNK - idx))
                zeta = np.exp(lg * idx)
            tabs[d, hd, 0] = mask
            tabs[d, hd, 1] = dec[:, None] * ones
            tabs[d, hd, 2] = zeta[:, None] * ones
            tabs[d, hd, 3] = np.exp(lg * RET_CHUNK) * ones
            tabs[d, hd, 4] = zeta[None, :] * ones
    return (jnp.asarray(cos, F32), jnp.asarray(sin, F32), jnp.asarray(tabs, F32))


def _ret_kernel(q_ref, k_ref, v_ref, g_ref, cos_ref, sin_ref, tab_ref, o_ref,
                qs_ref, qd_ref, kt_ref, kz_ref, out_ref, state_ref, *, ctx_len):
    s_tot = q_ref.shape[1]
    rc = RET_CHUNK
    nch = s_tot // rc
    nctx = ctx_len // rc
    k_scale = RET_HEAD_DIM ** -0.5
    lane = lax.broadcasted_iota(jnp.int32, (rc, RET_HEAD_DIM), 1)
    first_half = (lane % (RET_HEAD_DIM // 2)) < (RET_HEAD_DIM // 4)

    def rope(t, cs, sn):
        quarter = RET_HEAD_DIM // 4
        partner = jnp.where(first_half, pltpu.roll(t, RET_HEAD_DIM - quarter, 1), pltpu.roll(t, quarter, 1))
        return t * cs + partner * sn

    heads = [pl.ds(hd * RET_HEAD_DIM, RET_HEAD_DIM) for hd in range(RET_HEADS)]

    def prepare(c, carry):
        rows = pl.ds(pl.multiple_of(c * rc, rc), rc)
        cs = cos_ref[rows, :]
        sn = sin_ref[rows, :]
        for hd, cols in enumerate(heads):
            q = rope(q_ref[0, rows, cols].astype(F32), cs, sn)
            kt = (rope(k_ref[0, rows, cols].astype(F32), cs, sn) * k_scale).T
            qs_ref[hd, c] = q.astype(BF16)
            kt_ref[hd, c] = kt.astype(BF16)
            for d in range(2):
                qd_ref[hd, d, c] = (q * tab_ref[d, hd, 1]).astype(BF16)
                kz_ref[hd, d, c] = (kt * tab_ref[d, hd, 4]).astype(BF16)
        return carry

    lax.fori_loop(0, nch, prepare, 0)
    state_ref[...] = jnp.zeros_like(state_ref)

    def chunk(c, d, hd):
        rows = pl.ds(pl.multiple_of(c * rc, rc), rc)
        vb = v_ref[0, rows, heads[hd]]
        state = state_ref[hd, d]
        scores = jnp.dot(qs_ref[hd, c], kt_ref[hd, c], preferred_element_type=F32) * tab_ref[d, hd, 0]
        o = (jnp.dot(scores.astype(BF16), vb, preferred_element_type=F32)
             + jnp.dot(qd_ref[hd, d, c], state.astype(BF16), preferred_element_type=F32))
        kv = jnp.dot(kz_ref[hd, d, c], vb, preferred_element_type=F32)
        state_ref[hd, d] = tab_ref[d, hd, 3] * state + kv
        out_ref[hd, d, rows, :] = o

    def step(j, carry):
        back = jnp.where(j < nctx, nctx - 1 - j, nctx + nch - 1 - j)
        for hd in range(RET_HEADS):
            chunk(j, 0, hd)
            chunk(back, 1, hd)
        return carry

    lax.fori_loop(0, nch, step, 0)
    for hd, cols in enumerate(heads):
        g = g_ref[0, :, cols].astype(F32)
        o_ref[0, :, cols] = (_rms(out_ref[hd, 0] + out_ref[hd, 1])
                             * (g * jax.nn.sigmoid(g))).astype(o_ref.dtype)


def _retention(qkvg, cos, sin, tabs, ctx_len):
    b, s_tot, _ = qkvg.shape
    hd = RET_HEAD_DIM
    nch = s_tot // RET_CHUNK
    nh = RET_HEADS
    part = lambda p: pl.BlockSpec((1, s_tot, RET_WIDTH), lambda i: (i, 0, p))
    return pl.pallas_call(
        functools.partial(_ret_kernel, ctx_len=ctx_len),
        grid=(b,),
        in_specs=[part(0), part(1), part(2), part(3),
                  pl.BlockSpec((s_tot, hd), lambda i: (0, 0)),
                  pl.BlockSpec((s_tot, hd), lambda i: (0, 0)),
                  pl.BlockSpec((2, nh, N_TAB, RET_CHUNK, RET_CHUNK), lambda i: (0, 0, 0, 0, 0))],
        out_specs=pl.BlockSpec((1, s_tot, RET_WIDTH), lambda i: (i, 0, 0)),
        out_shape=jax.ShapeDtypeStruct((b, s_tot, RET_WIDTH), BF16),
        scratch_shapes=[pltpu.VMEM((nh, nch, RET_CHUNK, hd), BF16),
                        pltpu.VMEM((nh, 2, nch, RET_CHUNK, hd), BF16),
                        pltpu.VMEM((nh, nch, hd, RET_CHUNK), BF16),
                        pltpu.VMEM((nh, 2, nch, hd, RET_CHUNK), BF16),
                        pltpu.VMEM((nh, 2, s_tot, hd), F32),
                        pltpu.VMEM((nh, 2, hd, hd), F32)],
        compiler_params=_cparams(("arbitrary",)),
        name="retention",
    )(qkvg, qkvg, qkvg, qkvg, cos, sin, tabs)


def _merge_kernel(x_ref, yf_ref, yb_ref, hc_ref, ro_ref, gates_ref, mod_ref, g2_ref,
                  wglu_ref, bglu_ref, wpw_ref, bpw_ref, wo_ref, wout_ref, wr_ref, br_ref,
                  x1_ref, h2_ref, lg_ref):
    d = D_MODEL
    mod = mod_ref[0, 0]

    for half in range(ROW_TILE // HALF_TILE):
        rows = pl.ds(half * HALF_TILE, HALF_TILE)
        ys = jax.nn.gelu(yf_ref[0, 0, rows, :] + yb_ref[0, 0, rows, :]).astype(BF16)
        z = jnp.dot(ys, wglu_ref[...], preferred_element_type=F32) + bglu_ref[...]
        y_ssm = z[:, 0:d] * jax.nn.sigmoid(z[:, d:2 * d])
        y_conv = jnp.dot(hc_ref[0, rows, :].astype(BF16), wpw_ref[...],
                         preferred_element_type=F32) + bpw_ref[...]
        y_ret = jnp.dot(ro_ref[0, rows, :].astype(BF16), wo_ref[...], preferred_element_type=F32)
        m = (gates_ref[0, rows, 0:d].astype(F32) * y_ssm
             + gates_ref[0, rows, d:2 * d].astype(F32) * y_conv
             + gates_ref[0, rows, 2 * d:3 * d].astype(F32) * y_ret)
        y = jnp.dot(m.astype(BF16), wout_ref[...], preferred_element_type=F32)
        x1 = x_ref[0, rows, :] + mod[2:3] * y
        x1_ref[0, rows, :] = x1
        h2 = (_rms(x1) * g2_ref[...]) * (1.0 + mod[4:5]) + mod[3:4]
        h2_ref[0, rows, :] = h2
        h_hi = h2.astype(BF16)
        h_lo = (h2 - h_hi.astype(F32)).astype(BF16)
        lg_ref[0, rows, :] = (jnp.dot(h_hi, wr_ref[0], preferred_element_type=F32)
                              + jnp.dot(h_lo, wr_ref[0], preferred_element_type=F32)
                              + jnp.dot(h_hi, wr_ref[1], preferred_element_type=F32)) + br_ref[...]


def _router_kernel(lg_ref, tril_ref, ti_ref, tg_ref, cnt_out_ref, cnt_ref):
    @pl.when(jnp.logical_and(pl.program_id(0) % MOE_GROUP_BATCH == 0, pl.program_id(1) == 0))
    def _():
        cnt_ref[...] = jnp.zeros_like(cnt_ref)

    seen = cnt_ref[0:1, :]
    for part in range(ROUTE_ROWS // ROW_TILE):
        rows = pl.ds(part * ROW_TILE, ROW_TILE)
        logits = lg_ref[0, rows, :]
        lane = lax.broadcasted_iota(jnp.int32, logits.shape, 1)
        lane_f = lane.astype(F32)
        neg = jnp.float32(-jnp.inf)
        l = jnp.where(lane < N_EXPERTS, logits, neg)
        ti = jnp.zeros(logits.shape, F32)
        vals = []
        picks = []
        for k in range(TOP_K):
            top = jnp.max(l, axis=-1, keepdims=True)
            idx = jnp.min(jnp.where(l == top, lane_f, float(LOGIT_LANES)), axis=-1, keepdims=True)
            ti = jnp.where(lane == k, idx, ti)
            vals.append(top)
            picked = lane_f == idx
            picks.append(picked.astype(F32))
            l = jnp.where(picked, neg, l)
        es = [jnp.exp(v - vals[0]) for v in vals]
        tot = es[0]
        for e in es[1:]:
            tot = tot + e
        tg = jnp.zeros(logits.shape, F32)
        for k in range(TOP_K):
            tg = jnp.where(lane == k, es[k] / tot, tg)
        chosen = picks[0] + picks[1] + picks[2] + picks[3]
        before = seen + jnp.dot(tril_ref[...], chosen.astype(BF16), preferred_element_type=F32)
        for k in range(TOP_K):
            rank = jnp.sum(picks[k] * before, axis=-1, keepdims=True)
            ti = jnp.where(lane == TOP_K + k, rank, ti)
        seen = seen + jnp.sum(chosen, axis=0, keepdims=True)
        ti_ref[0, rows, :] = ti.astype(jnp.int32)
        tg_ref[0, rows, :] = tg
    cnt_ref[...] = jnp.broadcast_to(seen, cnt_ref.shape)
    cnt_out_ref[0] = jnp.broadcast_to(seen, cnt_out_ref.shape[1:])


def _router(logits):
    b, s_tot, _ = logits.shape
    tril = jnp.asarray(np.tril(np.ones((ROW_TILE, ROW_TILE), np.float32), -1), BF16)
    row = pl.BlockSpec((1, ROUTE_ROWS, LOGIT_LANES), lambda i, j: (i, j, 0))
    return pl.pallas_call(
        _router_kernel,
        grid=(b, s_tot // ROUTE_ROWS),
        in_specs=[row, pl.BlockSpec((ROW_TILE, ROW_TILE), lambda i, j: (0, 0))],
        out_specs=[row, row,
                   pl.BlockSpec((1, 8, LOGIT_LANES), lambda i, j: (i // MOE_GROUP_BATCH, 0, 0))],
        out_shape=[jax.ShapeDtypeStruct((b, s_tot, LOGIT_LANES), jnp.int32),
                   jax.ShapeDtypeStruct((b, s_tot, LOGIT_LANES), F32),
                   jax.ShapeDtypeStruct((b // MOE_GROUP_BATCH, 8, LOGIT_LANES), F32)],
        scratch_shapes=[pltpu.VMEM((8, LOGIT_LANES), F32)],
        compiler_params=_cparams(("arbitrary", "arbitrary")),
        name="router",
    )(logits, tril)


def _positions_kernel(top_ref, starts_ref, pos_ref):
    top = top_ref[...]
    lane = lax.broadcasted_iota(jnp.int32, top.shape, 1)
    starts = starts_ref[0:1, :]
    pos = jnp.zeros(top.shape, F32)
    for k in range(TOP_K):
        mine = lane == top[:, k:k + 1]
        first = jnp.sum(jnp.where(mine, starts, 0.0), axis=-1, keepdims=True)
        pos = jnp.where(lane == k, first + top[:, TOP_K + k:TOP_K + k + 1].astype(F32), pos)
    pos_ref[...] = pos.astype(jnp.int32)


def _positions(top, starts_row):
    n = top.shape[0]
    tile = 1024
    return pl.pallas_call(
        _positions_kernel,
        grid=(n // tile,),
        in_specs=[pl.BlockSpec((tile, LOGIT_LANES), lambda i: (i, 0)),
                  pl.BlockSpec((8, LOGIT_LANES), lambda i: (0, 0))],
        out_specs=pl.BlockSpec((tile, LOGIT_LANES), lambda i: (i, 0)),
        out_shape=jax.ShapeDtypeStruct((n, LOGIT_LANES), jnp.int32),
        compiler_params=_cparams(("arbitrary",)),
        name="moe_positions",
    )(top, starts_row)


def _merge(x, y_s5, hc, ro, gates, modt, g2, wglu, bglu, wpw, bpw, wo, wout, wr, br, ctx_len):
    b, s_tot, d = x.shape
    nt = s_tot // ROW_TILE
    nctx = ctx_len // ROW_TILE
    seg = lambda j: jnp.where(j >= nctx, 1, 0)
    row = lambda w: pl.BlockSpec((1, ROW_TILE, w), lambda i, j: (i, j, 0))
    cst = lambda shape: pl.BlockSpec(shape, lambda i, j: (0,) * len(shape))
    y2 = y_s5
    return pl.pallas_call(
        _merge_kernel,
        grid=(b, nt),
        in_specs=[row(d),
                  pl.BlockSpec((1, 1, ROW_TILE, SSM_WIDTH), lambda i, j: (0, i, j, 0)),
                  pl.BlockSpec((1, 1, ROW_TILE, SSM_WIDTH), lambda i, j: (1, i, j, 0)),
                  row(CONV_WIDTH), row(RET_WIDTH), row(3 * d),
                  pl.BlockSpec((1, 1, 8, d), lambda i, j: (i, seg(j), 0, 0)),
                  cst((1, d)),
                  cst((SSM_WIDTH, 2 * d)), cst((1, 2 * d)),
                  cst((CONV_WIDTH, d)), cst((1, d)),
                  cst((RET_WIDTH, d)), cst((d, d)),
                  cst((2, d, LOGIT_LANES)), cst((1, LOGIT_LANES))],
        out_specs=[row(d), row(d), row(LOGIT_LANES)],
        out_shape=[jax.ShapeDtypeStruct((b, s_tot, d), F32),
                   jax.ShapeDtypeStruct((b, s_tot, d), F32),
                   jax.ShapeDtypeStruct((b, s_tot, LOGIT_LANES), F32)],
        compiler_params=_cparams(("arbitrary", "arbitrary")),
        name="merge_logits",
    )(x, y2, y2, hc, ro, gates, modt, g2.reshape(1, d), wglu, bglu.reshape(1, 2 * d),
      wpw, bpw.reshape(1, d), wo, wout, wr, br)


PLAN_TILE, PLAN_EXPERT, PLAN_LO, PLAN_HI, PLAN_NEXT, PLAN_SLOT = range(6)
PLAN_LANES = 256


def _plan_kernel(cnt_ref, plan_ref, *, n_tiles):
    ne = N_EXPERTS
    rows_per = float(MOE_ROWS)
    e = lax.broadcasted_iota(jnp.int32, (ne, PLAN_LANES), 0)
    e_f = e.astype(F32)
    v = lax.broadcasted_iota(jnp.int32, (ne, PLAN_LANES), 1).astype(F32)

    def cumsum(t):
        for s in (1, 2, 4, 8, 16):
            t = t + jnp.where(e >= s, pltpu.roll(t, s, 0), 0.0)
        return t

    def lookup(table, onehot):
        return jnp.sum(onehot * table, axis=0, keepdims=True)

    c = jnp.broadcast_to(cnt_ref[...], (ne, PLAN_LANES))
    has = c > 0.0
    ends = cumsum(c)
    starts = ends - c
    first_tile = jnp.floor(starts / rows_per)
    tiles_per = jnp.where(has, jnp.floor((ends - 1.0) / rows_per) - first_tile + 1.0, 0.0)
    vend = cumsum(tiles_per)
    vstart = vend - tiles_per
    total = jnp.max(vend, axis=0, keepdims=True)
    slot = cumsum(has.astype(F32)) - 1.0
    slot = slot - 2.0 * jnp.floor(slot / 2.0)
    later = jnp.where(has, e_f, float(ne))
    nxt = jnp.where(e < ne - 1, pltpu.roll(later, ne - 1, 0), float(ne))
    for s in (1, 2, 4, 8, 16):
        nxt = jnp.minimum(nxt, jnp.where(e < ne - s, pltpu.roll(nxt, ne - s, 0), float(ne)))
    nxt = jnp.where(nxt >= float(ne), -1.0, nxt)
    last_with_rows = jnp.max(jnp.where(has, e_f, 0.0), axis=0, keepdims=True)

    valid = v[0:1] < total
    ve = jnp.sum((v >= vend).astype(F32), axis=0, keepdims=True)
    ve = jnp.where(valid, jnp.minimum(ve, float(ne - 1)), last_with_rows)
    onehot = (e_f == ve).astype(F32)
    vt = jnp.where(valid, lookup(first_tile, onehot) + v[0:1] - lookup(vstart, onehot),
                   float(n_tiles - 1))
    lo = jnp.where(valid, jnp.clip(lookup(starts, onehot) - vt * rows_per, 0.0, rows_per), 0.0)
    hi = jnp.where(valid, jnp.clip(lookup(ends, onehot) - vt * rows_per, 0.0, rows_per), 0.0)
    fields = {PLAN_TILE: vt, PLAN_EXPERT: ve, PLAN_LO: lo, PLAN_HI: hi,
              PLAN_NEXT: lookup(nxt, onehot), PLAN_SLOT: lookup(slot, onehot)}
    row = lax.broadcasted_iota(jnp.int32, (8, PLAN_LANES), 0)
    plan = jnp.zeros((8, PLAN_LANES), F32)
    for r, val in fields.items():
        plan = jnp.where(row == r, val, plan)
    plan_ref[...] = plan.astype(jnp.int32)


def _plan(counts_col, n_tiles):
    return pl.pallas_call(
        functools.partial(_plan_kernel, n_tiles=n_tiles),
        out_shape=jax.ShapeDtypeStruct((8, PLAN_LANES), jnp.int32),
        name="moe_plan",
    )(counts_col)


def _expert_kernel(plan_ref, x_ref, wgu_hbm, bgu_ref, wd_hbm, bd_ref, y_ref,
                   wgu_f32, wd_f32, wgu_bf, wd_bf, sem, *, layer):
    v = pl.program_id(0)
    prev = jnp.maximum(v - 1, 0)
    lo = plan_ref[PLAN_LO, v]
    hi = plan_ref[PLAN_HI, v]
    expert = plan_ref[PLAN_EXPERT, v]
    next_expert = plan_ref[PLAN_NEXT, v]
    active = hi > lo
    fresh_weights = jnp.logical_or(v == 0, expert != plan_ref[PLAN_EXPERT, prev])
    first_visit = jnp.logical_or(v == 0, plan_ref[PLAN_TILE, v] != plan_ref[PLAN_TILE, prev])

    def weight_copies(expert, slot):
        return (pltpu.make_async_copy(wgu_hbm.at[layer, expert], wgu_f32.at[slot], sem.at[0, slot]),
                pltpu.make_async_copy(wd_hbm.at[layer, expert], wd_f32.at[slot], sem.at[1, slot]))

    @pl.when(jnp.logical_and(fresh_weights, active))
    def _():
        slot = plan_ref[PLAN_SLOT, v]

        @pl.when(v == 0)
        def _():
            for c in weight_copies(expert, slot):
                c.start()

        for c in weight_copies(expert, slot):
            c.wait()

        @pl.when(next_expert >= 0)
        def _():
            for c in weight_copies(next_expert, 1 - slot):
                c.start()

        wgu_bf[...] = wgu_f32[slot].astype(BF16)
        wd_bf[...] = wd_f32[slot].astype(BF16)

    whole = jnp.logical_and(lo == 0, hi == MOE_ROWS)

    def ffn(rows):
        f = EXPERT_FF
        gu = jnp.dot(x_ref[rows, :].astype(BF16), wgu_bf[...],
                     preferred_element_type=F32) + bgu_ref[0, 0]
        gate = jnp.minimum(gu[:, 0:f], SWIGLU_LIMIT)
        up = jnp.clip(gu[:, f:2 * f], -SWIGLU_LIMIT, SWIGLU_LIMIT)
        act = (up + 1.0) * gate * jax.nn.sigmoid(gate * SWIGLU_ALPHA)
        y = jnp.dot(act.astype(BF16), wd_bf[...], preferred_element_type=F32) + bd_ref[0, 0]
        return y.astype(y_ref.dtype)

    parts = [pl.ds(p * MOE_PART, MOE_PART) for p in range(MOE_ROWS // MOE_PART)]

    @pl.when(whole)
    def _():
        for rows in parts:
            y_ref[rows, :] = ffn(rows)

    @pl.when(jnp.logical_and(first_visit, jnp.logical_not(whole)))
    def _():
        y_ref[...] = jnp.zeros_like(y_ref)

    for p in range(MOE_ROWS // MOE_EDGE):
        rows = pl.ds(p * MOE_EDGE, MOE_EDGE)
        touched = jnp.logical_and(lo < (p + 1) * MOE_EDGE, hi > p * MOE_EDGE)

        @pl.when(jnp.logical_and(touched, jnp.logical_not(whole)))
        def _():
            row = p * MOE_EDGE + lax.broadcasted_iota(jnp.int32, (MOE_EDGE, 1), 0)
            mine = jnp.logical_and(row >= lo, row < hi)
            y_ref[rows, :] = jnp.where(mine, ffn(rows), y_ref[rows, :])


def _experts(xs, plan, w_gu, b_gu, w_down, b_down, layer):
    n_rows, d = xs.shape
    n_visits = n_rows // MOE_ROWS + N_EXPERTS - 1
    assert n_visits <= PLAN_LANES
    f = EXPERT_FF
    wmap = lambda v, plan: (layer, plan[PLAN_EXPERT, v], 0, 0)
    tmap = lambda v, plan: (plan[PLAN_TILE, v], 0)
    grid_spec = pltpu.PrefetchScalarGridSpec(
        num_scalar_prefetch=1,
        grid=(n_visits,),
        in_specs=[pl.BlockSpec((MOE_ROWS, d), tmap),
                  pl.BlockSpec(memory_space=pl.ANY),
                  pl.BlockSpec((1, 1, 1, 2 * f), wmap),
                  pl.BlockSpec(memory_space=pl.ANY),
                  pl.BlockSpec((1, 1, 1, d), wmap)],
        out_specs=pl.BlockSpec((MOE_ROWS, d), tmap),
        scratch_shapes=[pltpu.VMEM((2, d, 2 * f), F32), pltpu.VMEM((2, f, d), F32),
                        pltpu.VMEM((d, 2 * f), BF16), pltpu.VMEM((f, d), BF16),
                        pltpu.SemaphoreType.DMA((2, 2))],
    )
    return pl.pallas_call(
        functools.partial(_expert_kernel, layer=layer),
        grid_spec=grid_spec,
        out_shape=jax.ShapeDtypeStruct((n_rows, d), BF16),
        compiler_params=_cparams(("arbitrary",)),
        name="experts",
    )(plan, xs, w_gu, b_gu.reshape(DEPTH, N_EXPERTS, 1, 2 * f),
      w_down, b_down.reshape(DEPTH, N_EXPERTS, 1, d))


def _combine_kernel(x_ref, y0_ref, y1_ref, y2_ref, y3_ref, gt_ref, mod_ref, fg_ref, o_ref, *, final):
    mod = mod_ref[0, 0]
    gt = gt_ref[0]
    y = gt[:, 0:1] * y0_ref[0].astype(F32)
    for k, y_ref in ((1, y1_ref), (2, y2_ref), (3, y3_ref)):
        y = y + gt[:, k:k + 1] * y_ref[0].astype(F32)
    x2 = x_ref[0] + mod[5:6] * y
    o_ref[0] = _rms(x2) * fg_ref[...] if final else x2


def _combine(x1, yg, gates, modt, final_g, ctx_len, final, group):
    b, s_tot, d = x1.shape
    gb = MOE_GROUP_BATCH
    b0 = group * gb
    nt = s_tot // ROW_TILE
    nctx = ctx_len // ROW_TILE
    off = nctx if final else 0
    seg = lambda j: jnp.where(j + off >= nctx, 1, 0)
    choice = lambda k: pl.BlockSpec((1, ROW_TILE, d), lambda i, j: (k, i * nt + j + off, 0))
    if final:
        out_spec = pl.BlockSpec((1, ROW_TILE, d), lambda i, j: (i, j, 0))
        out_shape = jax.ShapeDtypeStruct((gb, s_tot - off * ROW_TILE, d), F32)
        aliases = {}
    else:
        out_spec = pl.BlockSpec((1, ROW_TILE, d), lambda i, j: (i + b0, j, 0))
        out_shape = jax.ShapeDtypeStruct((b, s_tot, d), F32)
        aliases = {0: 0}
    return pl.pallas_call(
        functools.partial(_combine_kernel, final=final),
        grid=(gb, nt - off),
        in_specs=[pl.BlockSpec((1, ROW_TILE, d), lambda i, j: (i + b0, j + off, 0)),
                  choice(0), choice(1), choice(2), choice(3),
                  pl.BlockSpec((1, ROW_TILE, LOGIT_LANES), lambda i, j: (i + b0, j + off, 0)),
                  pl.BlockSpec((1, 1, 8, d), lambda i, j: (i + b0, seg(j), 0, 0)),
                  pl.BlockSpec((1, d), lambda i, j: (0, 0))],
        out_specs=out_spec,
        out_shape=out_shape,
        input_output_aliases=aliases,
        compiler_params=_cparams(("arbitrary", "arbitrary")),
        name="moe_combine",
    )(x1, yg, yg, yg, yg, gates, modt, final_g.reshape(1, d))


def _moe(x1, h2, top, top_gate, counts, modt, final_g, exp_gu_w, exp_gu_b, exp_down_w, exp_down_b,
         layer, ctx_len, final):
    b, s_tot, d = x1.shape
    n_grp = MOE_GROUP_BATCH * s_tot
    n_assign = n_grp * TOP_K
    top = top.reshape(b * s_tot, LOGIT_LANES)
    h2 = h2.reshape(b * s_tot, d)
    x = x1
    finals = []
    for g in range(b // MOE_GROUP_BATCH):
        top_g = top[g * n_grp:(g + 1) * n_grp]
        cnt = counts[g]
        plan = _plan(cnt[0, :N_EXPERTS].reshape(N_EXPERTS, 1), n_assign // MOE_ROWS)
        pos = _positions(top_g, jnp.cumsum(cnt, axis=1) - cnt)[:, :TOP_K]
        inv = pos.reshape(-1)
        _, order = lax.sort((inv, jnp.arange(n_assign, dtype=jnp.int32)), num_keys=1)
        rows = order // TOP_K + g * n_grp
        xs = h2.at[rows].get(mode="promise_in_bounds")
        ys = _experts(xs, plan, exp_gu_w, exp_gu_b, exp_down_w, exp_down_b, layer)
        by_choice = inv.reshape(n_grp, TOP_K).T.reshape(-1)
        yg = ys.at[by_choice].get(mode="promise_in_bounds").reshape(TOP_K, n_grp, d)
        out = _combine(x, yg, top_gate, modt, final_g, ctx_len, final, g)
        if final:
            finals.append(out)
        else:
            x = out
    return jnp.concatenate(finals, axis=0) if final else x


def _mixer_layer(x, modt, i, ctx_len, tables, norm1_g, w_in_bf, s5p, ssm_d, conv_p, merge_p):
    b, s_tot, d = x.shape
    u2, cv, qkvg, gates = _in_proj(x, modt, norm1_g, w_in_bf, ctx_len)
    bmat, cmat, lvec = s5p
    y_s5 = _s5_scan(u2, bmat, cmat, lvec, ssm_d, ctx_len)
    hc = _conv_branch(cv, *conv_p, ctx_len)
    cos, sin, tabs = tables
    ro = _retention(qkvg, cos, sin, tabs, ctx_len)
    x1, h2, logits = _merge(x, y_s5, hc, ro, gates, modt, *merge_p, ctx_len)
    return (x1, h2) + tuple(_router(logits))


def kernel(x, c, ctx, c_ctx, ada_w, ada_b, norm1_g, w_in, ssm_lam_re, ssm_lam_im, ssm_log_dt, ssm_b_re, ssm_b_im, ssm_c_re, ssm_c_im, ssm_d, ssm_glu_w, ssm_glu_b, conv_dw_w, conv_dw_b, conv_ln_g, conv_ln_b, conv_pw_w, conv_pw_b, ret_w_o, w_out, norm2_g, router_w, router_b, exp_gu_w, exp_gu_b, exp_down_w, exp_down_b, final_g):
    b, seq, d = x.shape
    ctx_len = ctx.shape[1]
    depth = ada_w.shape[0]
    xs = jnp.concatenate([ctx, x], axis=1)
    s_in = jnp.concatenate([c, c_ctx[None], jnp.zeros((16 - b - 1, d), F32)], axis=0)
    mod = _ada_mod(s_in, ada_w, ada_b).reshape(depth, 16, N_MOD, d)
    mod = jnp.pad(mod, ((0, 0), (0, 0), (0, 8 - N_MOD), (0, 0)))
    modt = jnp.stack([jnp.broadcast_to(mod[:, b][:, None], (depth, b, 8, d)), mod[:, :b]], axis=2)
    tables = _ret_tables(ctx_len, seq)
    wr = jnp.pad(router_w, ((0, 0), (0, 0), (0, LOGIT_LANES - N_EXPERTS)))
    wr_hi = wr.astype(BF16)
    wr = jnp.stack([wr_hi, (wr - wr_hi.astype(F32)).astype(BF16)], axis=1)
    br = jnp.pad(router_b, ((0, 0), (0, LOGIT_LANES - N_EXPERTS))).reshape(depth, 1, LOGIT_LANES)
    for i in range(depth):
        s5p = _s5_params(ssm_lam_re[i], ssm_lam_im[i], ssm_log_dt[i], ssm_b_re[i], ssm_b_im[i],
                         ssm_c_re[i], ssm_c_im[i])
        conv_p = (conv_dw_w[i], conv_dw_b[i], conv_ln_g[i], conv_ln_b[i])
        merge_p = (norm2_g[i], ssm_glu_w[i].astype(BF16), ssm_glu_b[i], conv_pw_w[i].astype(BF16),
                   conv_pw_b[i], ret_w_o[i].astype(BF16), w_out[i].astype(BF16), wr[i], br[i])
        x1, h2, top, top_gate, counts = _mixer_layer(xs, modt[i], i, ctx_len, tables, norm1_g[i],
                                                     w_in[i].astype(BF16), s5p, ssm_d[i], conv_p, merge_p)
        xs = _moe(x1, h2, top, top_gate, counts, modt[i], final_g, exp_gu_w, exp_gu_b, exp_down_w,
                  exp_down_b, i, ctx_len, final=(i == depth - 1))
    return xs
```
